```python
import jax, jax.numpy as jnp
from jax import lax
import numpy as np

D_MODEL = 1024
BATCH = 1
SEQ = 16384
DEPTH = 2

GRID_W = 64
CTX_LEN = 256
N_MIXERS = 2
N_ATTN_LAYERS = (DEPTH + N_MIXERS - 1) // N_MIXERS
N_CONV_LAYERS = DEPTH // N_MIXERS

N_HEADS = 16
N_KV_HEADS = 2
HEAD_DIM = 64
GROUP = N_HEADS // N_KV_HEADS
QKV_DIM = (N_HEADS + 2 * N_KV_HEADS) * HEAD_DIM
WINDOW = 128
Q_BLOCK = 128
N_SIDE = WINDOW // Q_BLOCK
BAND = (2 * N_SIDE + 1) * Q_BLOCK
ROPE_BASE = 10000.0

CONV_WIDTH = 31
CONV_PAD = CONV_WIDTH // 2

N_EXPERTS = 32
TOP_K = 4
D_FF = D_MODEL
SWIGLU_LIMIT = 7.0
SWIGLU_ALPHA = 1.702
EXPERT_BLOCK = 256

EPS = 1e-6

kernel_name = "hybrid_swa_conformer_moe_dit"


def rmsnorm(x, g):
    xf = x.astype(jnp.float32)
    y = xf * lax.rsqrt(jnp.mean(xf * xf, axis=-1, keepdims=True) + EPS)
    return y.astype(x.dtype) * g


def layernorm(x, g, b):
    xf = x.astype(jnp.float32)
    mu = jnp.mean(xf, axis=-1, keepdims=True)
    var = jnp.mean(jnp.square(xf - mu), axis=-1, keepdims=True)
    return ((xf - mu) * lax.rsqrt(var + EPS)).astype(x.dtype) * g + b


def modulate(h, shift, scale):
    return h * (1 + scale) + shift


def axial_rope(t, row, col):
    half = HEAD_DIM // 2
    quarter = HEAD_DIM // 4
    inv = ROPE_BASE ** (-jnp.arange(quarter, dtype=jnp.float32) / quarter)

    def rot(tp, pos):
        ang = pos.astype(jnp.float32)[:, None] * inv[None, :]
        cos = jnp.cos(ang)[None, :, None, :]
        sin = jnp.sin(ang)[None, :, None, :]
        tf = tp.astype(jnp.float32)
        t1, t2 = tf[..., :quarter], tf[..., quarter:]
        return jnp.concatenate([t1 * cos - t2 * sin, t2 * cos + t1 * sin], axis=-1)

    out = jnp.concatenate([rot(t[..., :half], row), rot(t[..., half:], col)], axis=-1)
    return out.astype(t.dtype)


def qkv_project(t, w_qkv, b_qkv, q_g, k_g):
    B, L, _ = t.shape
    qkv = t @ w_qkv + b_qkv
    q = qkv[..., :N_HEADS * HEAD_DIM].reshape(B, L, N_HEADS, HEAD_DIM)
    k = qkv[..., N_HEADS * HEAD_DIM:(N_HEADS + N_KV_HEADS) * HEAD_DIM].reshape(B, L, N_KV_HEADS, HEAD_DIM)
    v = qkv[..., (N_HEADS + N_KV_HEADS) * HEAD_DIM:].reshape(B, L, N_KV_HEADS, HEAD_DIM)
    return rmsnorm(q, q_g), rmsnorm(k, k_g), v


def window_attention(h, hc, w_qkv, b_qkv, w_o, b_o, q_g, k_g, sinks, with_ctx_out):
    B, S, _ = h.shape
    C = hc.shape[1]
    nb = S // Q_BLOCK
    rows = S // GRID_W
    row = jnp.repeat(jnp.arange(rows, dtype=jnp.int32), GRID_W)
    col = jnp.tile(jnp.arange(GRID_W, dtype=jnp.int32), rows)
    scale = HEAD_DIM ** -0.5

    q, k, v = qkv_project(h, w_qkv, b_qkv, q_g, k_g)
    q = axial_rope(q, row, col)
    k = axial_rope(k, row, col)
    qc, kc, vc = qkv_project(hc, w_qkv, b_qkv, q_g, k_g)
    sink = sinks.astype(jnp.float32).reshape(N_KV_HEADS, GROUP)[None, :, :, None, None]

    qb = jnp.moveaxis(q.reshape(B, nb, Q_BLOCK, N_KV_HEADS, GROUP, HEAD_DIM), 1, 0)

    def band(t):
        pad = N_SIDE * Q_BLOCK
        tp = jnp.pad(t, ((0, 0), (pad, pad), (0, 0), (0, 0)))
        tb = tp.reshape(B, nb + 2 * N_SIDE, Q_BLOCK, N_KV_HEADS, HEAD_DIM)
        bnd = jnp.concatenate([tb[:, j:j + nb] for j in range(2 * N_SIDE + 1)], axis=2)
        return jnp.moveaxis(bnd, 1, 0)

    kband = band(k)
    vband = band(v)

    def step(args):
        qblk, kblk, vblk, bi = args
        s_lat = jnp.einsum('bqkgd,bnkd->bkgqn', qblk, kblk).astype(jnp.float32) * scale
        qpos = bi * Q_BLOCK + jnp.arange(Q_BLOCK)
        kpos = bi * Q_BLOCK - N_SIDE * Q_BLOCK + jnp.arange(BAND)
        valid = (jnp.abs(qpos[:, None] - kpos[None, :]) <= WINDOW) & (kpos >= 0)[None, :] & (kpos < S)[None, :]
        s_lat = jnp.where(valid, s_lat, jnp.finfo(jnp.float32).min)
        s_ctx = jnp.einsum('bqkgd,bckd->bkgqc', qblk, kc).astype(jnp.float32) * scale
        s_sink = jnp.broadcast_to(sink, s_ctx.shape[:-1] + (1,))
        p = jax.nn.softmax(jnp.concatenate([s_lat, s_ctx, s_sink], axis=-1), axis=-1)
        p_lat = p[..., :BAND].astype(vblk.dtype)
        p_ctx = p[..., BAND:BAND + C].astype(vc.dtype)
        return (jnp.einsum('bkgqn,bnkd->bqkgd', p_lat, vblk)
                + jnp.einsum('bkgqc,bckd->bqkgd', p_ctx, vc))

    o = lax.map(step, (qb, kband, vband, jnp.arange(nb, dtype=jnp.int32)))
    o = jnp.moveaxis(o, 0, 1).reshape(B, S, N_HEADS * HEAD_DIM)
    y = o @ w_o + b_o

    yc = None
    if with_ctx_out:
        qcg = qc.reshape(B, C, N_KV_HEADS, GROUP, HEAD_DIM)
        s = jnp.einsum('bqkgd,bckd->bkgqc', qcg, kc).astype(jnp.float32) * scale
        s_sink = jnp.broadcast_to(sink, s.shape[:-1] + (1,))
        p = jax.nn.softmax(jnp.concatenate([s, s_sink], axis=-1), axis=-1)[..., :C].astype(vc.dtype)
        oc = jnp.einsum('bkgqc,bckd->bqkgd', p, vc).reshape(B, C, N_HEADS * HEAD_DIM)
        yc = oc @ w_o + b_o
    return y, yc


def conformer_conv(h, w_pw1, b_pw1, w_dw, b_dw, ln_g, ln_b, w_pw2, b_pw2):
    D = h.shape[-1]
    u = h @ w_pw1 + b_pw1
    u = u[..., :D] * jax.nn.sigmoid(u[..., D:])
    u = lax.conv_general_dilated(u, w_dw[:, None, :].astype(u.dtype), window_strides=(1,),
                                 padding=[(CONV_PAD, CONV_PAD)],
                                 dimension_numbers=('NWC', 'WIO', 'NWC'),
                                 feature_group_count=D) + b_dw
    u = jax.nn.silu(layernorm(u, ln_g, ln_b))
    return u @ w_pw2 + b_pw2


def moe(h, w_router, b_router, w1, b1, w2, b2):
    Bsz, L, D = h.shape
    xt = h.reshape(-1, D)
    T = xt.shape[0]
    logits = (xt @ w_router + b_router).astype(jnp.float32)
    top_val, top_idx = lax.top_k(logits, TOP_K)
    gates = jax.nn.softmax(top_val, axis=-1).astype(h.dtype)

    A = T * TOP_K
    e_flat = top_idx.reshape(-1).astype(jnp.int32)
    g_flat = gates.reshape(-1)
    tok_flat = jnp.repeat(jnp.arange(T, dtype=jnp.int32), TOP_K)
    order = jnp.argsort(e_flat)
    e_sorted = e_flat[order]
    counts = jnp.zeros(N_EXPERTS, jnp.int32).at[e_flat].add(1)
    starts = jnp.cumsum(counts) - counts
    padded = (counts + EXPERT_BLOCK - 1) // EXPERT_BLOCK * EXPERT_BLOCK
    pad_ends = jnp.cumsum(padded)
    pad_starts = pad_ends - padded
    rank = jnp.arange(A, dtype=jnp.int32) - starts[e_sorted]
    dest = pad_starts[e_sorted] + rank
    n_blocks = (A + N_EXPERTS * (EXPERT_BLOCK - 1)) // EXPERT_BLOCK
    P = n_blocks * EXPERT_BLOCK
    slot_tok = jnp.zeros(P, jnp.int32).at[dest].set(tok_flat[order])
    slot_gate = jnp.zeros(P, h.dtype).at[dest].set(g_flat[order])
    block_start = jnp.arange(n_blocks, dtype=jnp.int32) * EXPERT_BLOCK
    block_exp = jnp.minimum(jnp.searchsorted(pad_ends, block_start, side='right'), N_EXPERTS - 1)

    def expert_block(args):
        tok, e = args
        xb = xt[tok]
        gu = xb @ w1[e] + b1[e]
        gate = jnp.minimum(gu[:, :D_FF], SWIGLU_LIMIT)
        up = jnp.clip(gu[:, D_FF:], -SWIGLU_LIMIT, SWIGLU_LIMIT)
        act = gate * jax.nn.sigmoid(SWIGLU_ALPHA * gate) * (up + 1)
        return act @ w2[e] + b2[e]

    out = lax.map(expert_block, (slot_tok.reshape(n_blocks, EXPERT_BLOCK), block_exp))
    y = jnp.zeros_like(xt).at[slot_tok].add(out.reshape(P, D) * slot_gate[:, None])
    return y.reshape(Bsz, L, D)


def setup_inputs(seed: int = 0) -> dict:
    key = jax.random.key(seed)
    ks = jax.random.split(key, 32)
    f = jnp.float32
    D = D_MODEL

    def nrm(k, shape, std):
        return jax.random.normal(k, shape, f) * std

    return {
        "x": nrm(ks[0], (BATCH, SEQ, D), 1.0),
        "c": nrm(ks[1], (BATCH, D), 1.0),
        "ctx": nrm(ks[2], (BATCH, CTX_LEN, D), 1.0),
        "c_ctx": nrm(ks[3], (D,), 1.0),
        "w_mod": nrm(ks[4], (DEPTH, D, 6 * D), 0.5 * D ** -0.5),
        "b_mod": nrm(ks[5], (DEPTH, 6 * D), 0.02),
        "norm1_g": 1.0 + nrm(ks[6], (DEPTH, D), 0.02),
        "norm2_g": 1.0 + nrm(ks[7], (DEPTH, D), 0.02),
        "attn_w_qkv": nrm(ks[8], (N_ATTN_LAYERS, D, QKV_DIM), D ** -0.5),
        "attn_b_qkv": nrm(ks[9], (N_ATTN_LAYERS, QKV_DIM), 0.02),
        "attn_w_o": nrm(ks[10], (N_ATTN_LAYERS, N_HEADS * HEAD_DIM, D), (N_HEADS * HEAD_DIM) ** -0.5),
        "attn_b_o": nrm(ks[11], (N_ATTN_LAYERS, D), 0.02),
        "attn_q_norm": 1.0 + nrm(ks[12], (N_ATTN_LAYERS, HEAD_DIM), 0.02),
        "attn_k_norm": 1.0 + nrm(ks[13], (N_ATTN_LAYERS, HEAD_DIM), 0.02),
        "attn_sinks": nrm(ks[14], (N_ATTN_LAYERS, N_HEADS), 1.0),
        "conv_w_pw1": nrm(ks[15], (N_CONV_LAYERS, D, 2 * D), D ** -0.5),
        "conv_b_pw1": nrm(ks[16], (N_CONV_LAYERS, 2 * D), 0.02),
        "conv_w_dw": nrm(ks[17], (N_CONV_LAYERS, CONV_WIDTH, D), CONV_WIDTH ** -0.5),
        "conv_b_dw": nrm(ks[18], (N_CONV_LAYERS, D), 0.02),
        "conv_ln_g": 1.0 + nrm(ks[19], (N_CONV_LAYERS, D), 0.02),
        "conv_ln_b": nrm(ks[20], (N_CONV_LAYERS, D), 0.02),
        "conv_w_pw2": nrm(ks[21], (N_CONV_LAYERS, D, D), D ** -0.5),
        "conv_b_pw2": nrm(ks[22], (N_CONV_LAYERS, D), 0.02),
        "moe_w_router": nrm(ks[23], (DEPTH, D, N_EXPERTS), D ** -0.5),
        "moe_b_router": nrm(ks[24], (DEPTH, N_EXPERTS), 0.01),
        "moe_w1": nrm(ks[25], (DEPTH, N_EXPERTS, D, 2 * D_FF), D ** -0.5),
        "moe_b1": nrm(ks[26], (DEPTH, N_EXPERTS, 2 * D_FF), 0.02),
        "moe_w2": nrm(ks[27], (DEPTH, N_EXPERTS, D_FF, D), D_FF ** -0.5),
        "moe_b2": nrm(ks[28], (DEPTH, N_EXPERTS, D), 0.02),
    }


def reference(x, c, ctx, c_ctx, w_mod, b_mod, norm1_g, norm2_g,
              attn_w_qkv, attn_b_qkv, attn_w_o, attn_b_o, attn_q_norm, attn_k_norm, attn_sinks,
              conv_w_pw1, conv_b_pw1, conv_w_dw, conv_b_dw, conv_ln_g, conv_ln_b, conv_w_pw2, conv_b_pw2,
              moe_w_router, moe_b_router, moe_w1, moe_b1, moe_w2, moe_b2):
    ctx_stream = ctx
    for i in range(DEPTH):
        mixer = i % N_MIXERS
        ctx_later = any(j % N_MIXERS == 0 for j in range(i + 1, DEPTH))
        mod_lat = (jax.nn.silu(c) @ w_mod[i] + b_mod[i])[:, None, :]
        mod_ctx = (jax.nn.silu(c_ctx) @ w_mod[i] + b_mod[i])[None, None, :]
        sh1, sc1, g1, sh2, sc2, g2 = jnp.split(mod_lat, 6, axis=-1)
        csh1, csc1, cg1, csh2, csc2, cg2 = jnp.split(mod_ctx, 6, axis=-1)

        h = modulate(rmsnorm(x, norm1_g[i]), sh1, sc1)
        if mixer == 0:
            a = i // N_MIXERS
            hc = modulate(rmsnorm(ctx_stream, norm1_g[i]), csh1, csc1)
            y, yc = window_attention(h, hc, attn_w_qkv[a], attn_b_qkv[a], attn_w_o[a], attn_b_o[a],
                                     attn_q_norm[a], attn_k_norm[a], attn_sinks[a], ctx_later)
        else:
            a = i // N_MIXERS
            conv_args = (conv_w_pw1[a], conv_b_pw1[a], conv_w_dw[a], conv_b_dw[a],
                         conv_ln_g[a], conv_ln_b[a], conv_w_pw2[a], conv_b_pw2[a])
            y = conformer_conv(h, *conv_args)
            yc = None
            if ctx_later:
                hc = modulate(rmsnorm(ctx_stream, norm1_g[i]), csh1, csc1)
                yc = conformer_conv(hc, *conv_args)
        x = x + g1 * y
        h2 = modulate(rmsnorm(x, norm2_g[i]), sh2, sc2)
        x = x + g2 * moe(h2, moe_w_router[i], moe_b_router[i], moe_w1[i], moe_b1[i], moe_w2[i], moe_b2[i])

        if ctx_later:
            ctx_stream = ctx_stream + cg1 * yc
            hc2 = modulate(rmsnorm(ctx_stream, norm2_g[i]), csh2, csc2)
            ctx_stream = ctx_stream + cg2 * moe(hc2, moe_w_router[i], moe_b_router[i], moe_w1[i],
                                                moe_b1[i], moe_w2[i], moe_b2[i])
    return x
```

```python
import functools

import jax
import jax.numpy as jnp
from jax import lax
from jax.experimental import pallas as pl
from jax.experimental.pallas import tpu as pltpu

F32 = jnp.float32
BF16 = jnp.bfloat16

N_HEADS = 16
N_KV_HEADS = 2
HEAD_DIM = 64
GRID_W = 64
WINDOW = 128
Q_BLOCK = 128
ROPE_BASE = 10000.0
CONV_WIDTH = 31
CONV_PAD = CONV_WIDTH // 2
N_EXPERTS = 32
TOP_K = 4
SWIGLU_LIMIT = 7.0
SWIGLU_ALPHA = 1.702
EXPERT_BLOCK = 256
EPS = 1e-6

LANES = 128
TOKEN_TILE = 512
CONV_HALO = 16
NEG_BIG = -1e30
VMEM_LIMIT = 56 * 1024 * 1024


def _cparams(n_axes=1, vmem=VMEM_LIMIT):
    return pltpu.CompilerParams(dimension_semantics=("arbitrary",) * n_axes, vmem_limit_bytes=vmem)


def _full(shape):
    nd = len(shape)
    return pl.BlockSpec(shape, lambda *_: (0,) * nd)


def _silu(v):
    return v * jax.nn.sigmoid(v)


def _rms_mod(xv, g, shift, scale):
    ms = jnp.mean(xv * xv, axis=-1, keepdims=True)
    return (xv * lax.rsqrt(ms + EPS)) * g * (1.0 + scale) + shift


def _bdot(a, b):
    return jnp.dot(a.astype(BF16), b.astype(BF16), preferred_element_type=F32)


def _split_dot(a, b_bf16):
    hi = a.astype(BF16)
    lo = (a - hi.astype(F32)).astype(BF16)
    return (jnp.dot(hi, b_bf16, preferred_element_type=F32)
            + jnp.dot(lo, b_bf16, preferred_element_type=F32))


def _router_tail(x_new, g2n, sh2, sc2, wr, br, h2_ref, idx_ref, gate_ref):
    h2 = _rms_mod(x_new, g2n, sh2, sc2)
    h2_ref[...] = h2
    logits = jnp.dot(h2, wr, precision=lax.Precision.HIGHEST, preferred_element_type=F32) + br
    lane = lax.broadcasted_iota(jnp.int32, logits.shape, 1)
    idx_out = jnp.zeros(logits.shape, jnp.int32)
    val_out = jnp.full(logits.shape, NEG_BIG, F32)
    cur = logits
    for k in range(TOP_K):
        m = jnp.max(cur, axis=-1, keepdims=True)
        sel = jnp.min(jnp.where(cur == m, lane, LANES), axis=-1, keepdims=True)
        idx_out = jnp.where(lane == k, sel, idx_out)
        val_out = jnp.where(lane == k, m, val_out)
        cur = jnp.where(lane == sel, NEG_BIG * 2.0, cur)
    vmax = jnp.max(val_out, axis=-1, keepdims=True)
    ev = jnp.where(lane < TOP_K, jnp.exp(val_out - vmax), 0.0)
    gate_ref[...] = ev / jnp.sum(ev, axis=-1, keepdims=True)
    idx_ref[...] = idx_out


def _mod_kernel(cc_ref, w_ref, b_ref, o_ref):
    a = _silu(cc_ref[...])
    o_ref[0] = jnp.dot(a, w_ref[0], precision=lax.Precision.HIGHEST,
                       preferred_element_type=F32) + b_ref[0]


def _mod_vectors(c, c_ctx, w_mod, b_mod):
    depth, d, d6 = w_mod.shape
    cc = jnp.zeros((8, d), F32).at[0].set(c[0]).at[1].set(c_ctx)
    ncol = 4
    cw = d6 // ncol
    return pl.pallas_call(
        _mod_kernel,
        grid=(depth, ncol),
        in_specs=[pl.BlockSpec((8, d), lambda l, j: (0, 0)),
                  pl.BlockSpec((1, d, cw), lambda l, j: (l, 0, j)),
                  pl.BlockSpec((1, 1, cw), lambda l, j: (l, 0, j))],
        out_specs=pl.BlockSpec((1, 8, cw), lambda l, j: (l, 0, j)),
        out_shape=jax.ShapeDtypeStruct((depth, 8, d6), F32),
        compiler_params=_cparams(2),
        name="mod_vectors",
    )(cc, w_mod, b_mod.reshape(depth, 1, d6))


def _qkv_kernel(mod_row, x_ref, mod_ref, g_ref, w_ref, b_ref, qg_ref, kg_ref, cos_ref, sa_ref, sb_ref,
                bd_ref, q_ref, k_ref, v_ref, wb_ref):
    d = x_ref.shape[1]
    nq = N_HEADS * HEAD_DIM
    nk = N_KV_HEADS * HEAD_DIM

    @pl.when(pl.program_id(0) == 0)
    def _():
        wb_ref[...] = w_ref[...].astype(BF16)

    sh = mod_ref[mod_row:mod_row + 1, 0:d]
    sc = mod_ref[mod_row:mod_row + 1, d:2 * d]
    h = _rms_mod(x_ref[...], g_ref[...], sh, sc)
    qkv = jnp.dot(h.astype(BF16), wb_ref[...], preferred_element_type=F32) + b_ref[...]
    q = qkv[:, :nq]
    k = qkv[:, nq:nq + nk]
    v = qkv[:, nq + nk:]
    bd = bd_ref[...]

    def head_norm(t, gain):
        tt = t * t
        w = bd.shape[0]
        if t.shape[1] >= w:
            ss = jnp.concatenate([_split_dot(tt[:, c:c + w], bd) for c in range(0, t.shape[1], w)], axis=1)
        else:
            ss = _split_dot(tt, bd[:t.shape[1], :t.shape[1]])
        return t * lax.rsqrt(ss * (1.0 / HEAD_DIM) + EPS) * gain

    cosv, sav, sbv = cos_ref[...], sa_ref[...], sb_ref[...]

    def rope(t):
        n = t.shape[1]
        reps = n // LANES
        ct = jnp.concatenate([cosv] * reps, axis=1) if reps > 1 else cosv
        at = jnp.concatenate([sav] * reps, axis=1) if reps > 1 else sav
        bt = jnp.concatenate([sbv] * reps, axis=1) if reps > 1 else sbv
        up = pltpu.roll(t, n - HEAD_DIM // 4, 1)
        dn = pltpu.roll(t, HEAD_DIM // 4, 1)
        return t * ct + up * at + dn * bt

    qn = rope(head_norm(q, qg_ref[...])) * (HEAD_DIM ** -0.5)
    kn = rope(head_norm(k, kg_ref[...]))
    q_ref[...] = qn.astype(BF16)

    lane = lax.broadcasted_iota(jnp.int32, kn.shape, 1)
    low = lane < HEAD_DIM

    def variants(t):
        sw = pltpu.roll(t, HEAD_DIM, 1)
        z = jnp.zeros_like(t)
        return jnp.concatenate([jnp.where(low, t, z), jnp.where(low, z, sw),
                                jnp.where(low, sw, z), jnp.where(low, z, t)], axis=1)

    k_ref[...] = variants(kn).astype(BF16)
    v_ref[...] = variants(v).astype(BF16)


def _qkv_project(xt, mod_l, mod_row, g1n, w_qkv, b_qkv, qg, kg, cos_t, sa_t, sb_t, bd, tile):
    t, d = xt.shape
    nqkv = w_qkv.shape[1]
    nq = N_HEADS * HEAD_DIM
    tok = lambda i: (i, 0)
    return pl.pallas_call(
        functools.partial(_qkv_kernel, mod_row),
        grid=(t // tile,),
        in_specs=[pl.BlockSpec((tile, d), tok), _full(mod_l.shape), _full((1, d)),
                  _full((d, nqkv)), _full((1, nqkv)), _full((1, nq)), _full((1, LANES)),
                  pl.BlockSpec((tile, LANES), tok), pl.BlockSpec((tile, LANES), tok),
                  pl.BlockSpec((tile, LANES), tok), _full(bd.shape)],
        out_specs=[pl.BlockSpec((tile, nq), tok), pl.BlockSpec((tile, 4 * LANES), tok),
                   pl.BlockSpec((tile, 4 * LANES), tok)],
        out_shape=[jax.ShapeDtypeStruct((t, nq), BF16), jax.ShapeDtypeStruct((t, 4 * LANES), BF16),
                   jax.ShapeDtypeStruct((t, 4 * LANES), BF16)],
        scratch_shapes=[pltpu.VMEM((d, nqkv), BF16)],
        compiler_params=_cparams(1),
        name="qkv_project",
    )(xt, mod_l, g1n, w_qkv, b_qkv, qg, kg, cos_t, sa_t, sb_t, bd)


def _attn_kernel(seq_len, sink_ref, q_ref, kp_ref, kc_ref, kn_ref, vp_ref, vc_ref, vn_ref, kx_ref, vx_ref,
                 o_ref):
    i = pl.program_id(0)
    qb = q_ref.shape[0]
    band = 3 * qb
    pairs_per_group = (N_HEADS // N_KV_HEADS) // 2
    rows = pairs_per_group * qb

    r = lax.broadcasted_iota(jnp.int32, (rows, band), 0) % qb
    cpos = lax.broadcasted_iota(jnp.int32, (rows, band), 1) - qb
    kabs = i * qb + cpos
    valid = (jnp.abs(r - cpos) <= WINDOW) & (kabs >= 0) & (kabs < seq_len)

    kband = jnp.concatenate([kp_ref[...], kc_ref[...], kn_ref[...]], axis=0)
    vband = jnp.concatenate([vp_ref[...], vc_ref[...], vn_ref[...]], axis=0)
    kctx = kx_ref[...]
    vctx = vx_ref[...]
    dn_t = (((1,), (1,)), ((), ()))
    rsub = lax.broadcasted_iota(jnp.int32, (rows, 1), 0) // qb

    for g in range(N_KV_HEADS):
        qg = jnp.concatenate(
            [q_ref[:, (g * pairs_per_group + j) * LANES:(g * pairs_per_group + j + 1) * LANES]
             for j in range(pairs_per_group)], axis=0)
        acc = jnp.zeros((rows, LANES), F32)
        for par in range(2):
            col = (2 * g + par) * LANES
            s_b = lax.dot_general(qg, kband[:, col:col + LANES], dn_t, preferred_element_type=F32)
            s_c = lax.dot_general(qg, kctx[:, col:col + LANES], dn_t, preferred_element_type=F32)
            s_b = jnp.where(valid, s_b, jnp.finfo(F32).min)
            sink = jnp.zeros((rows, 1), F32)
            for j in range(pairs_per_group):
                hd = 2 * (g * pairs_per_group + j) + par
                sink = jnp.where(rsub == j, sink_ref[hd], sink)
            m = jnp.maximum(jnp.maximum(jnp.max(s_b, axis=-1, keepdims=True),
                                        jnp.max(s_c, axis=-1, keepdims=True)), sink)
            p_b = jnp.exp(s_b - m)
            p_c = jnp.exp(s_c - m)
            l = (jnp.sum(p_b, axis=-1, keepdims=True) + jnp.sum(p_c, axis=-1, keepdims=True)
                 + jnp.exp(sink - m))
            o = (jnp.dot(p_b.astype(BF16), vband[:, col:col + LANES], preferred_element_type=F32)
                 + jnp.dot(p_c.astype(BF16), vctx[:, col:col + LANES], preferred_element_type=F32))
            acc = acc + o / l
        for j in range(pairs_per_group):
            pcol = (g * pairs_per_group + j) * LANES
            o_ref[:, pcol:pcol + LANES] = acc[j * qb:(j + 1) * qb].astype(BF16)


def _attention(q, kk, vv, kkc, vvc, sinks):
    t, nq = q.shape
    qb = Q_BLOCK
    nb = t // qb
    c = kkc.shape[0]
    w = kk.shape[1]
    cur = lambda i, s: (i, 0)
    prv = lambda i, s: (jnp.maximum(i - 1, 0), 0)
    nxt = lambda i, s: (jnp.minimum(i + 1, nb - 1), 0)
    zero = lambda i, s: (0, 0)
    gs = pltpu.PrefetchScalarGridSpec(
        num_scalar_prefetch=1,
        grid=(nb,),
        in_specs=[pl.BlockSpec((qb, nq), cur),
                  pl.BlockSpec((qb, w), prv), pl.BlockSpec((qb, w), cur), pl.BlockSpec((qb, w), nxt),
                  pl.BlockSpec((qb, w), prv), pl.BlockSpec((qb, w), cur), pl.BlockSpec((qb, w), nxt),
                  pl.BlockSpec((c, w), zero), pl.BlockSpec((c, w), zero)],
        out_specs=pl.BlockSpec((qb, nq), cur),
    )
    return pl.pallas_call(
        functools.partial(_attn_kernel, t),
        grid_spec=gs,
        out_shape=jax.ShapeDtypeStruct((t, nq), BF16),
        compiler_params=_cparams(1),
        name="window_attention",
    )(sinks, q, kk, kk, kk, vv, vv, vv, kkc, vvc)


def _oproj_kernel(x_ref, o_ref, mod_ref, wo_ref, bo_ref, g2_ref, wr_ref, br_ref,
                  x1_ref, h2_ref, idx_ref, gate_ref, wb_ref):
    d = x_ref.shape[1]

    @pl.when(pl.program_id(0) == 0)
    def _():
        wb_ref[...] = wo_ref[...].astype(BF16)

    y = jnp.dot(o_ref[...], wb_ref[...], preferred_element_type=F32) + bo_ref[...]
    x1 = x_ref[...] + mod_ref[0:1, 2 * d:3 * d] * y
    x1_ref[...] = x1
    _router_tail(x1, g2_ref[...], mod_ref[0:1, 3 * d:4 * d], mod_ref[0:1, 4 * d:5 * d],
                 wr_ref[...], br_ref[...], h2_ref, idx_ref, gate_ref)


def _oproj_router(xt, o, mod_l, w_o, b_o, g2n, wr, br, tile):
    t, d = xt.shape
    tok = lambda i: (i, 0)
    return pl.pallas_call(
        _oproj_kernel,
        grid=(t // tile,),
        in_specs=[pl.BlockSpec((tile, d), tok), pl.BlockSpec((tile, o.shape[1]), tok), _full(mod_l.shape),
                  _full(w_o.shape), _full((1, d)), _full((1, d)), _full(wr.shape), _full(br.shape)],
        out_specs=[pl.BlockSpec((tile, d), tok), pl.BlockSpec((tile, d), tok),
                   pl.BlockSpec((tile, LANES), tok), pl.BlockSpec((tile, LANES), tok)],
        out_shape=[jax.ShapeDtypeStruct((t, d), F32), jax.ShapeDtypeStruct((t, d), F32),
                   jax.ShapeDtypeStruct((t, LANES), jnp.int32), jax.ShapeDtypeStruct((t, LANES), F32)],
        scratch_shapes=[pltpu.VMEM(w_o.shape, BF16)],
        compiler_params=_cparams(1),
        name="oproj_router",
    )(xt, o, mod_l, w_o, b_o, g2n, wr, br)


TOK_BITS = 14


DMA_GROUP = 8


def _moe_kernel(bexp_ref, nv_ref, slot_ref, gate_ref, h2_hbm, w1_ref, b1_ref, w2_ref, b2_ref, y4_hbm,
                xbuf, obuf, w1b, w2b, gsem, ssem):
    b = pl.program_id(0)
    nb = pl.num_programs(0)
    eb = EXPERT_BLOCK
    dff = w2_ref.shape[1]
    slot = b % 2
    grp = DMA_GROUP

    def gather_copy(blk, s, rr):
        tok = slot_ref[blk * eb + rr] & ((1 << TOK_BITS) - 1)
        return pltpu.make_async_copy(h2_hbm.at[pl.ds(tok, 1)], xbuf.at[s, pl.ds(rr, 1)], gsem.at[s])

    def scatter_copy(blk, s, rr):
        dst = slot_ref[blk * eb + rr] >> TOK_BITS
        return pltpu.make_async_copy(obuf.at[s, pl.ds(rr, 1)], y4_hbm.at[pl.ds(dst, 1)], ssem.at[s])

    def start_rows(copy_fn, blk, s, n):
        full = n // grp

        def group(it, carry):
            for u in range(grp):
                copy_fn(blk, s, it * grp + u).start()
            return carry

        def single(rr, carry):
            copy_fn(blk, s, rr).start()
            return carry

        lax.fori_loop(0, full, group, 0)
        lax.fori_loop(full * grp, n, single, 0)

    def wait_rows(group_copy, row_copy, n):
        full = n // grp

        def group(it, carry):
            group_copy.wait()
            return carry

        def single(rr, carry):
            row_copy.wait()
            return carry

        lax.fori_loop(0, full, group, 0)
        lax.fori_loop(full * grp, n, single, 0)

    def wait_gather(s, n):
        wait_rows(pltpu.make_async_copy(h2_hbm.at[pl.ds(0, grp)], xbuf.at[s, pl.ds(0, grp)], gsem.at[s]),
                  pltpu.make_async_copy(h2_hbm.at[pl.ds(0, 1)], xbuf.at[s, pl.ds(0, 1)], gsem.at[s]), n)

    def wait_scatter(s, n):
        wait_rows(pltpu.make_async_copy(obuf.at[s, pl.ds(0, grp)], y4_hbm.at[pl.ds(0, grp)], ssem.at[s]),
                  pltpu.make_async_copy(obuf.at[s, pl.ds(0, 1)], y4_hbm.at[pl.ds(0, 1)], ssem.at[s]), n)

    nv = nv_ref[b]

    @pl.when(b == 0)
    def _():
        xbuf[...] = jnp.zeros(xbuf.shape, F32)
        start_rows(gather_copy, 0, 0, nv)

    @pl.when(b + 1 < nb)
    def _():
        start_rows(gather_copy, b + 1, 1 - slot, nv_ref[jnp.minimum(b + 1, nb - 1)])

    @pl.when(b >= 2)
    def _():
        wait_scatter(slot, nv_ref[jnp.maximum(b - 2, 0)])

    @pl.when(nv > 0)
    def _():
        changed = jnp.logical_or(b == 0, bexp_ref[b] != bexp_ref[jnp.maximum(b - 1, 0)])

        @pl.when(changed)
        def _():
            w1b[...] = w1_ref[0].astype(BF16)
            w2b[...] = w2_ref[0].astype(BF16)

        wait_gather(slot, nv)
        xb = xbuf[slot].astype(BF16)
        gu = jnp.dot(xb, w1b[...], preferred_element_type=F32) + b1_ref[0]
        gt = jnp.minimum(gu[:, :dff], SWIGLU_LIMIT)
        up = jnp.clip(gu[:, dff:], -SWIGLU_LIMIT, SWIGLU_LIMIT)
        act = gt * jax.nn.sigmoid(SWIGLU_ALPHA * gt) * (up + 1.0)
        out = jnp.dot(act.astype(BF16), w2b[...], preferred_element_type=F32) + b2_ref[0]
        obuf[slot] = out * gate_ref[...]
        start_rows(scatter_copy, b, slot, nv)

    @pl.when(b == nb - 1)
    def _():
        wait_scatter(slot, nv)

        @pl.when(nb >= 2)
        def _():
            wait_scatter(1 - slot, nv_ref[jnp.maximum(b - 1, 0)])


def _moe_experts(h2, block_exp, n_valid, slot_word, slot_gate, w1, b1, w2, b2):
    t, d = h2.shape
    n_blocks = block_exp.shape[0]
    dff = w2.shape[1]
    eb = EXPERT_BLOCK
    gs = pltpu.PrefetchScalarGridSpec(
        num_scalar_prefetch=3,
        grid=(n_blocks,),
        in_specs=[pl.BlockSpec((eb, 1), lambda b, be, na, sw: (b, 0)),
                  pl.BlockSpec(memory_space=pl.ANY),
                  pl.BlockSpec((1, d, 2 * dff), lambda b, be, na, sw: (be[b], 0, 0)),
                  pl.BlockSpec((1, 1, 2 * dff), lambda b, be, na, sw: (be[b], 0, 0)),
                  pl.BlockSpec((1, dff, d), lambda b, be, na, sw: (be[b], 0, 0)),
                  pl.BlockSpec((1, 1, d), lambda b, be, na, sw: (be[b], 0, 0))],
        out_specs=pl.BlockSpec(memory_space=pl.ANY),
        scratch_shapes=[pltpu.VMEM((2, eb, d), F32), pltpu.VMEM((2, eb, d), F32),
                        pltpu.VMEM((d, 2 * dff), BF16), pltpu.VMEM((dff, d), BF16),
                        pltpu.SemaphoreType.DMA((2,)), pltpu.SemaphoreType.DMA((2,))],
    )
    n_exp = w1.shape[0]
    return pl.pallas_call(
        _moe_kernel,
        grid_spec=gs,
        out_shape=jax.ShapeDtypeStruct((t * TOP_K, d), F32),
        compiler_params=_cparams(1),
        name="moe_experts",
    )(block_exp, n_valid, slot_word, slot_gate, h2, w1, b1.reshape(n_exp, 1, 2 * dff),
      w2, b2.reshape(n_exp, 1, d))


def _routing_tables(idx, gates, t):
    a = t * TOP_K
    eb = EXPERT_BLOCK
    e_flat = idx.reshape(-1)
    g_flat = gates.reshape(-1)
    order = jnp.argsort(e_flat).astype(jnp.int32)
    e_sorted = e_flat[order]
    counts = jnp.zeros(N_EXPERTS, jnp.int32).at[e_flat].add(1)
    starts = jnp.cumsum(counts) - counts
    padded = (counts + eb - 1) // eb * eb
    pad_ends = jnp.cumsum(padded)
    pad_starts = pad_ends - padded
    rank = jnp.arange(a, dtype=jnp.int32) - starts[e_sorted]
    dest = pad_starts[e_sorted] + rank
    n_blocks = (a + N_EXPERTS * (eb - 1)) // eb
    p = n_blocks * eb
    tok_sorted = order // TOP_K
    row_sorted = (order % TOP_K) * t + tok_sorted
    slot_word = jnp.zeros(p, jnp.int32).at[dest].set(tok_sorted | (row_sorted << TOK_BITS))
    slot_gate = jnp.zeros(p, F32).at[dest].set(g_flat[order])
    block_start = jnp.arange(n_blocks, dtype=jnp.int32) * eb
    raw_exp = jnp.searchsorted(pad_ends, block_start, side='right')
    block_exp = jnp.minimum(raw_exp, N_EXPERTS - 1).astype(jnp.int32)
    in_expert = block_start - pad_starts[block_exp]
    n_valid = jnp.where(raw_exp < N_EXPERTS, jnp.clip(counts[block_exp] - in_expert, 0, eb), 0)
    return block_exp, n_valid.astype(jnp.int32), slot_word, slot_gate.reshape(p, 1)


def _combine_kernel(x_ref, y0, y1, y2, y3, mod_ref, o_ref):
    d = x_ref.shape[1]
    y = (y0[...] + y1[...]) + (y2[...] + y3[...])
    o_ref[...] = x_ref[...] + mod_ref[0:1, 5 * d:6 * d] * y


def _combine_glu_kernel(x_ref, y0, y1, y2, y3, modp_ref, modn_ref, g_ref, w_ref, b_ref, o_ref, u_ref, wb_ref):
    d = x_ref.shape[1]

    @pl.when(pl.program_id(0) == 0)
    def _():
        wb_ref[...] = w_ref[...].astype(BF16)

    y = (y0[...] + y1[...]) + (y2[...] + y3[...])
    x2 = x_ref[...] + modp_ref[0:1, 5 * d:6 * d] * y
    o_ref[...] = x2
    h = _rms_mod(x2, g_ref[...], modn_ref[0:1, 0:d], modn_ref[0:1, d:2 * d])
    u = jnp.dot(h.astype(BF16), wb_ref[...], preferred_element_type=F32) + b_ref[...]
    u_ref[...] = u[:, :d] * jax.nn.sigmoid(u[:, d:])


def _y4_specs(tile, d, nt):
    return [pl.BlockSpec((tile, d), functools.partial(lambda k, i: (k * nt + i, 0), k)) for k in range(TOP_K)]


def _combine(x1, y4, mod_l, tile):
    t, d = x1.shape
    nt = t // tile
    tok = lambda i: (i, 0)
    return pl.pallas_call(
        _combine_kernel,
        grid=(nt,),
        in_specs=[pl.BlockSpec((tile, d), tok)] + _y4_specs(tile, d, nt) + [_full(mod_l.shape)],
        out_specs=pl.BlockSpec((tile, d), tok),
        out_shape=jax.ShapeDtypeStruct((t, d), F32),
        compiler_params=_cparams(1),
        name="moe_combine",
    )(x1, y4, y4, y4, y4, mod_l)


def _combine_glu(x1, y4, mod_prev, mod_next, g1n, w_pw1, b_pw1, tile):
    t, d = x1.shape
    nt = t // tile
    tok = lambda i: (i, 0)
    return pl.pallas_call(
        _combine_glu_kernel,
        grid=(nt,),
        in_specs=[pl.BlockSpec((tile, d), tok)] + _y4_specs(tile, d, nt)
                 + [_full(mod_prev.shape), _full(mod_next.shape), _full((1, d)),
                    _full(w_pw1.shape), _full((1, 2 * d))],
        out_specs=[pl.BlockSpec((tile, d), tok), pl.BlockSpec((tile, d), tok)],
        out_shape=[jax.ShapeDtypeStruct((t, d), F32), jax.ShapeDtypeStruct((t, d), F32)],
        scratch_shapes=[pltpu.VMEM(w_pw1.shape, BF16)],
        compiler_params=_cparams(1),
        name="combine_pw1_glu",
    )(x1, y4, y4, y4, y4, mod_prev, mod_next, g1n, w_pw1, b_pw1)


def _conv_kernel(x_ref, up_ref, uc_ref, un_ref, mod_ref, wdw_ref, bdw_ref, lg_ref, lb_ref, w2_ref, b2_ref,
                 g2_ref, wr_ref, br_ref, x1_ref, h2_ref, idx_ref, gate_ref, wb_ref, ubuf):
    i = pl.program_id(0)
    n = pl.num_programs(0)
    d = x_ref.shape[1]
    tile = x_ref.shape[0]
    halo = up_ref.shape[0]

    @pl.when(i == 0)
    def _():
        wb_ref[...] = w2_ref[...].astype(BF16)

    ubuf[0:halo, :] = jnp.where(i > 0, up_ref[...], 0.0)
    ubuf[halo:halo + tile, :] = uc_ref[...]
    ubuf[halo + tile:, :] = jnp.where(i < n - 1, un_ref[...], 0.0)

    acc = jnp.zeros((tile, d), F32) + bdw_ref[...]
    for j in range(CONV_WIDTH):
        off = halo + j - CONV_PAD
        acc = acc + ubuf[off:off + tile, :] * wdw_ref[j:j + 1, :]

    mu = jnp.mean(acc, axis=-1, keepdims=True)
    cen = acc - mu
    var = jnp.mean(cen * cen, axis=-1, keepdims=True)
    z = _silu(cen * lax.rsqrt(var + EPS) * lg_ref[...] + lb_ref[...])
    y = jnp.dot(z.astype(BF16), wb_ref[...], preferred_element_type=F32) + b2_ref[...]
    x1 = x_ref[...] + mod_ref[0:1, 2 * d:3 * d] * y
    x1_ref[...] = x1
    _router_tail(x1, g2_ref[...], mod_ref[0:1, 3 * d:4 * d], mod_ref[0:1, 4 * d:5 * d],
                 wr_ref[...], br_ref[...], h2_ref, idx_ref, gate_ref)


def _conv_router(x2, u, mod_l, w_dw, b_dw, ln_g, ln_b, w_pw2, b_pw2, g2n, wr, br, tile):
    t, d = x2.shape
    nt = t // tile
    hb = tile // CONV_HALO
    nh = t // CONV_HALO
    tok = lambda i: (i, 0)
    prv = lambda i: (jnp.maximum(i * hb - 1, 0), 0)
    nxt = lambda i: (jnp.minimum((i + 1) * hb, nh - 1), 0)
    wpad = jnp.zeros((32, d), F32).at[:CONV_WIDTH].set(w_dw)
    return pl.pallas_call(
        _conv_kernel,
        grid=(nt,),
        in_specs=[pl.BlockSpec((tile, d), tok), pl.BlockSpec((CONV_HALO, d), prv),
                  pl.BlockSpec((tile, d), tok), pl.BlockSpec((CONV_HALO, d), nxt),
                  _full(mod_l.shape), _full((32, d)), _full((1, d)), _full((1, d)), _full((1, d)),
                  _full(w_pw2.shape), _full((1, d)), _full((1, d)), _full(wr.shape), _full(br.shape)],
        out_specs=[pl.BlockSpec((tile, d), tok), pl.BlockSpec((tile, d), tok),
                   pl.BlockSpec((tile, LANES), tok), pl.BlockSpec((tile, LANES), tok)],
        out_shape=[jax.ShapeDtypeStruct((t, d), F32), jax.ShapeDtypeStruct((t, d), F32),
                   jax.ShapeDtypeStruct((t, LANES), jnp.int32), jax.ShapeDtypeStruct((t, LANES), F32)],
        scratch_shapes=[pltpu.VMEM(w_pw2.shape, BF16), pltpu.VMEM((tile + 2 * CONV_HALO, d), F32)],
        compiler_params=_cparams(1),
        name="conv_router",
    )(x2, u, u, u, mod_l, wpad, b_dw, ln_g, ln_b, w_pw2, b_pw2, g2n, wr, br)


def _rope_tables(t):
    quarter = HEAD_DIM // 4
    inv = ROPE_BASE ** (-jnp.arange(quarter, dtype=F32) / quarter)
    pos = jnp.arange(t, dtype=jnp.int32)
    row = (pos // GRID_W).astype(F32)[:, None] * inv[None, :]
    col = (pos % GRID_W).astype(F32)[:, None] * inv[None, :]
    z = jnp.zeros((t, quarter), F32)
    cos_h = jnp.concatenate([jnp.cos(row), jnp.cos(row), jnp.cos(col), jnp.cos(col)], axis=1)
    sa_h = jnp.concatenate([-jnp.sin(row), z, -jnp.sin(col), z], axis=1)
    sb_h = jnp.concatenate([z, jnp.sin(row), z, jnp.sin(col)], axis=1)
    rep = lambda v: jnp.concatenate([v, v], axis=1)
    return rep(cos_h), rep(sa_h), rep(sb_h)


def _router_params(w_router, b_router):
    d, e = w_router.shape
    wr = jnp.zeros((d, LANES), F32).at[:, :e].set(w_router)
    br = jnp.full((1, LANES), NEG_BIG, F32).at[0, :e].set(b_router)
    return wr, br


def _moe_layer(x1, h2, idx_l, gate_l, w1, b1, w2, b2):
    t = x1.shape[0]
    idx = idx_l[:, :TOP_K]
    gates = gate_l[:, :TOP_K]
    block_exp, n_valid, slot_word, slot_gate = _routing_tables(idx, gates, t)
    return _moe_experts(h2, block_exp, n_valid, slot_word, slot_gate, w1, b1, w2, b2)


def kernel(x, c, ctx, c_ctx, w_mod, b_mod, norm1_g, norm2_g, attn_w_qkv, attn_b_qkv, attn_w_o, attn_b_o,
           attn_q_norm, attn_k_norm, attn_sinks, conv_w_pw1, conv_b_pw1, conv_w_dw, conv_b_dw, conv_ln_g,
           conv_ln_b, conv_w_pw2, conv_b_pw2, moe_w_router, moe_b_router, moe_w1, moe_b1, moe_w2, moe_b2):
    bsz, t, d = x.shape
    assert bsz == 1 and w_mod.shape[0] == 2
    n_ctx = ctx.shape[1]
    tile = min(TOKEN_TILE, t)
    assert t % tile == 0 and t % Q_BLOCK == 0 and t < (1 << TOK_BITS) + 1
    xt = x[0]
    row = lambda v: v.reshape(1, -1)

    mod = _mod_vectors(c, c_ctx, w_mod, b_mod)
    mod0, mod1 = mod[0], mod[1]

    nq = N_HEADS * HEAD_DIM
    bd = (jnp.arange(2 * LANES)[:, None] // HEAD_DIM == jnp.arange(2 * LANES)[None, :] // HEAD_DIM).astype(BF16)
    qg = row(jnp.tile(attn_q_norm[0], N_HEADS))
    kg = row(jnp.tile(attn_k_norm[0], N_KV_HEADS))
    cos_t, sa_t, sb_t = _rope_tables(t)
    ones_c = jnp.ones((n_ctx, LANES), F32)
    zeros_c = jnp.zeros((n_ctx, LANES), F32)
    g1n = row(norm1_g[0])
    bq = row(attn_b_qkv[0])
    q, kk, vv = _qkv_project(xt, mod0, 0, g1n, attn_w_qkv[0], bq, qg, kg, cos_t, sa_t, sb_t, bd, tile)
    _, kkc, vvc = _qkv_project(ctx[0], mod0, 1, g1n, attn_w_qkv[0], bq, qg, kg, ones_c, zeros_c, zeros_c,
                               bd, n_ctx)
    o = _attention(q, kk, vv, kkc, vvc, attn_sinks[0])
    wr0, br0 = _router_params(moe_w_router[0], moe_b_router[0])
    x1, h2, idx_l, gate_l = _oproj_router(xt, o, mod0, attn_w_o[0], row(attn_b_o[0]), row(norm2_g[0]),
                                          wr0, br0, tile)
    y4 = _moe_layer(x1, h2, idx_l, gate_l, moe_w1[0], moe_b1[0], moe_w2[0], moe_b2[0])

    x2, u = _combine_glu(x1, y4, mod0, mod1, row(norm1_g[1]), conv_w_pw1[0], row(conv_b_pw1[0]), tile)
    wr1, br1 = _router_params(moe_w_router[1], moe_b_router[1])
    x3, h2b, idx_l, gate_l = _conv_router(x2, u, mod1, conv_w_dw[0], row(conv_b_dw[0]), row(conv_ln_g[0]),
                                          row(conv_ln_b[0]), conv_w_pw2[0], row(conv_b_pw2[0]),
                                          row(norm2_g[1]), wr1, br1, tile)
    y4b = _moe_layer(x3, h2b, idx_l, gate_l, moe_w1[1], moe_b1[1], moe_w2[1], moe_b2[1])
    out = _combine(x3, y4b, mod1, tile)
    return out[None]
```

```python
import functools

import jax
import jax.numpy as jnp
from jax import lax
from jax.experimental import pallas as pl
from jax.experimental.pallas import tpu as pltpu

F32 = jnp.float32
BF16 = jnp.bfloat16
I32 = jnp.int32

N_HEADS = 16
N_KV_HEADS = 2
HEAD_DIM = 64
GRID_W = 64
WINDOW = 128
Q_BLOCK = 128
ROPE_BASE = 10000.0
CONV_WIDTH = 31
CONV_PAD = CONV_WIDTH // 2
N_EXPERTS = 32
TOP_K = 4
SWIGLU_LIMIT = 7.0
SWIGLU_ALPHA = 1.702
EPS = 1e-6

LANES = 128
SUBLANES = 8
TOKEN_TILE = 512
MOE_TILE = 256
GRANULE = SUBLANES
MOE_REGION = -(-(MOE_TILE * TOP_K + N_EXPERTS * (GRANULE - 1)) // 256) * 256
EXPERT_ROWS = 256
GRANULES_PER_BLOCK = EXPERT_ROWS // GRANULE
CONV_HALO = 16
NEG_BIG = -1e30
VMEM_LIMIT = 56 * 1024 * 1024


def _cparams(n_axes=1, vmem=VMEM_LIMIT):
    return pltpu.CompilerParams(dimension_semantics=("arbitrary",) * n_axes, vmem_limit_bytes=vmem)


def _full(shape):
    nd = len(shape)
    return pl.BlockSpec(shape, lambda *_: (0,) * nd)


def _silu(v):
    return v * jax.nn.sigmoid(v)


def _rms_mod(xv, g, shift, scale):
    ms = jnp.mean(xv * xv, axis=-1, keepdims=True)
    return (xv * lax.rsqrt(ms + EPS)) * g * (1.0 + scale) + shift


def _split_dot(a, b_bf16):
    hi = a.astype(BF16)
    lo = (a - hi.astype(F32)).astype(BF16)
    return (jnp.dot(hi, b_bf16, preferred_element_type=F32)
            + jnp.dot(lo, b_bf16, preferred_element_type=F32))


def _router_dispatch(x_new, g2n, sh2, sc2, wr, br, xs_ref, pos_ref, gate_ref, cnt_ref):
    tt = x_new.shape[0]
    rt = xs_ref.shape[0]
    h2 = _rms_mod(x_new, g2n, sh2, sc2)
    logits = jnp.dot(h2, wr, precision=lax.Precision.HIGHEST, preferred_element_type=F32) + br
    lane = lax.broadcasted_iota(I32, logits.shape, 1)
    val_out = jnp.full(logits.shape, NEG_BIG, F32)
    onehot = jnp.zeros(logits.shape, F32)
    sels = []
    cur = logits
    for k in range(TOP_K):
        m = jnp.max(cur, axis=-1, keepdims=True)
        sel = jnp.min(jnp.where(cur == m, lane, LANES), axis=-1, keepdims=True)
        hit = lane == sel
        sels.append(sel)
        val_out = jnp.where(lane == k, m, val_out)
        onehot = jnp.where(hit, 1.0, onehot)
        cur = jnp.where(hit, NEG_BIG * 2.0, cur)
    vmax = jnp.max(val_out, axis=-1, keepdims=True)
    ev = jnp.where(lane < TOP_K, jnp.exp(val_out - vmax), 0.0)
    gate_ref[...] = ev / jnp.sum(ev, axis=-1, keepdims=True)

    ri = lax.broadcasted_iota(I32, (tt, tt), 0)
    ci = lax.broadcasted_iota(I32, (tt, tt), 1)
    earlier = jnp.where(ci < ri, 1.0, 0.0).astype(BF16)
    rank = jnp.dot(earlier, onehot.astype(BF16), preferred_element_type=F32)
    cnt = jnp.sum(onehot, axis=0, keepdims=True).astype(I32)
    c8 = jnp.bitwise_and(cnt + (GRANULE - 1), -GRANULE)
    ue = lax.broadcasted_iota(I32, (LANES, LANES), 0)
    ve = lax.broadcasted_iota(I32, (LANES, LANES), 1)
    c8_rows = jnp.broadcast_to(c8.astype(F32), (SUBLANES, LANES)).astype(BF16)
    before = jnp.where(ue < ve, 1.0, 0.0).astype(BF16)
    off = jnp.dot(c8_rows, before, preferred_element_type=F32)[0:1]
    cnt_ref[0] = jnp.broadcast_to(c8, (SUBLANES, LANES))

    slot_of = off + rank
    pos_out = jnp.zeros(logits.shape, I32)
    r_iota = lax.broadcasted_iota(I32, (tt, rt), 1)
    perm_t = jnp.zeros((tt, rt), F32)
    for k in range(TOP_K):
        pk = jnp.sum(jnp.where(lane == sels[k], slot_of, 0.0), axis=-1, keepdims=True).astype(I32)
        pos_out = jnp.where(lane == k, pk, pos_out)
        perm_t = jnp.where(r_iota == pk, 1.0, perm_t)
    pos_ref[...] = pos_out
    perm = perm_t.T.astype(BF16)
    xs_ref[...] = jnp.dot(perm, h2.astype(BF16), preferred_element_type=F32)


def _moe_combine(ys_ref, pos_ref, gate_ref):
    tt = pos_ref.shape[0]
    rt = ys_ref.shape[0]
    r_iota = lax.broadcasted_iota(I32, (tt, rt), 1)
    g = jnp.zeros((tt, rt), F32)
    for k in range(TOP_K):
        g = jnp.where(r_iota == pos_ref[:, k:k + 1], gate_ref[:, k:k + 1], g)
    return jnp.dot(g.astype(BF16), ys_ref[...].astype(BF16), preferred_element_type=F32)


def _mod_kernel(cc_ref, w_ref, b_ref, o_ref):
    a = _silu(cc_ref[...])
    o_ref[0] = jnp.dot(a, w_ref[0], precision=lax.Precision.HIGHEST,
                       preferred_element_type=F32) + b_ref[0]


def _mod_vectors(c, c_ctx, w_mod, b_mod):
    depth, d, d6 = w_mod.shape
    cc = jnp.zeros((SUBLANES, d), F32).at[0].set(c[0]).at[1].set(c_ctx)
    ncol = 4
    cw = d6 // ncol
    return pl.pallas_call(
        _mod_kernel,
        grid=(depth, ncol),
        in_specs=[pl.BlockSpec((SUBLANES, d), lambda l, j: (0, 0)),
                  pl.BlockSpec((1, d, cw), lambda l, j: (l, 0, j)),
                  pl.BlockSpec((1, 1, cw), lambda l, j: (l, 0, j))],
        out_specs=pl.BlockSpec((1, SUBLANES, cw), lambda l, j: (l, 0, j)),
        out_shape=jax.ShapeDtypeStruct((depth, SUBLANES, d6), F32),
        compiler_params=_cparams(2),
        name="mod_vectors",
    )(cc, w_mod, b_mod.reshape(depth, 1, d6))


def _qkv_kernel(mod_row, use_rope, x_ref, mod_ref, g_ref, w_ref, b_ref, qg_ref, kg_ref, rowt_ref, colt_ref,
                bd_ref, q_ref, k_ref, v_ref, wb_ref):
    d = x_ref.shape[1]
    tile = x_ref.shape[0]
    nq = N_HEADS * HEAD_DIM
    nk = N_KV_HEADS * HEAD_DIM

    @pl.when(pl.program_id(0) == 0)
    def _():
        wb_ref[...] = w_ref[...].astype(BF16)

    sh = mod_ref[mod_row:mod_row + 1, 0:d]
    sc = mod_ref[mod_row:mod_row + 1, d:2 * d]
    h = _rms_mod(x_ref[...], g_ref[...], sh, sc)
    qkv = jnp.dot(h.astype(BF16), wb_ref[...], preferred_element_type=F32) + b_ref[...]
    q = qkv[:, :nq]
    k = qkv[:, nq:nq + nk]
    v = qkv[:, nq + nk:]
    bd = bd_ref[...]

    def head_norm(t, gain):
        tt = t * t
        w = bd.shape[0]
        if t.shape[1] >= w:
            ss = jnp.concatenate([_split_dot(tt[:, c:c + w], bd) for c in range(0, t.shape[1], w)], axis=1)
        else:
            ss = _split_dot(tt, bd[:t.shape[1], :t.shape[1]])
        return t * lax.rsqrt(ss * (1.0 / HEAD_DIM) + EPS) * gain

    qn = head_norm(q, qg_ref[...])
    kn = head_norm(k, kg_ref[...])

    if use_rope:
        lane = lax.broadcasted_iota(I32, (tile, LANES), 1)
        is_row = (lane % HEAD_DIM) < HEAD_DIM // 2
        nrow = tile // GRID_W

        def table(idx):
            rt = jnp.concatenate([jnp.broadcast_to(rowt_ref[idx, r:r + 1, :], (GRID_W, LANES))
                                  for r in range(nrow)], axis=0)
            ct = jnp.concatenate([colt_ref[idx]] * nrow, axis=0)
            return jnp.where(is_row, rt, ct)

        cosv, sav, sbv = table(0), table(1), table(2)

        def rope(t):
            n = t.shape[1]
            reps = n // LANES
            ct = jnp.concatenate([cosv] * reps, axis=1) if reps > 1 else cosv
            at = jnp.concatenate([sav] * reps, axis=1) if reps > 1 else sav
            bt = jnp.concatenate([sbv] * reps, axis=1) if reps > 1 else sbv
            up = pltpu.roll(t, n - HEAD_DIM // 4, 1)
            dn = pltpu.roll(t, HEAD_DIM // 4, 1)
            return t * ct + up * at + dn * bt

        qn = rope(qn)
        kn = rope(kn)

    q_ref[...] = (qn * (HEAD_DIM ** -0.5)).astype(BF16)
    klane = lax.broadcasted_iota(I32, kn.shape, 1)
    low = klane < HEAD_DIM

    def variants(t):
        sw = pltpu.roll(t, HEAD_DIM, 1)
        z = jnp.zeros_like(t)
        return jnp.concatenate([jnp.where(low, t, z), jnp.where(low, z, sw),
                                jnp.where(low, sw, z), jnp.where(low, z, t)], axis=1)

    k_ref[...] = variants(kn).astype(BF16)
    v_ref[...] = variants(v).astype(BF16)


def _qkv_project(xt, mod_l, mod_row, use_rope, g1n, w_qkv, b_qkv, qg, kg, rowt, colt, bd, tile):
    t, d = xt.shape
    nqkv = w_qkv.shape[1]
    nq = N_HEADS * HEAD_DIM
    tok = lambda i: (i, 0)
    nrow = max(tile // GRID_W, 1)
    return pl.pallas_call(
        functools.partial(_qkv_kernel, mod_row, use_rope),
        grid=(t // tile,),
        in_specs=[pl.BlockSpec((tile, d), tok), _full(mod_l.shape), _full((1, d)),
                  _full((d, nqkv)), _full((1, nqkv)), _full((1, nq)), _full((1, LANES)),
                  pl.BlockSpec((3, nrow, LANES), lambda i: (0, i, 0)), _full(colt.shape), _full(bd.shape)],
        out_specs=[pl.BlockSpec((tile, nq), tok), pl.BlockSpec((tile, 4 * LANES), tok),
                   pl.BlockSpec((tile, 4 * LANES), tok)],
        out_shape=[jax.ShapeDtypeStruct((t, nq), BF16), jax.ShapeDtypeStruct((t, 4 * LANES), BF16),
                   jax.ShapeDtypeStruct((t, 4 * LANES), BF16)],
        scratch_shapes=[pltpu.VMEM((d, nqkv), BF16)],
        compiler_params=_cparams(1),
        name="qkv_project",
    )(xt, mod_l, g1n, w_qkv, b_qkv, qg, kg, rowt, colt, bd)


def _attn_kernel(seq_len, sink_ref, q_ref, kp_ref, kc_ref, kn_ref, vp_ref, vc_ref, vn_ref, kx_ref, vx_ref,
                 o_ref):
    i = pl.program_id(0)
    qb = q_ref.shape[0]
    band = 3 * qb
    pairs_per_group = (N_HEADS // N_KV_HEADS) // 2
    rows = pairs_per_group * qb

    r = lax.broadcasted_iota(I32, (rows, band), 0) % qb
    cpos = lax.broadcasted_iota(I32, (rows, band), 1) - qb
    kabs = i * qb + cpos
    valid = (jnp.abs(r - cpos) <= WINDOW) & (kabs >= 0) & (kabs < seq_len)

    kband = jnp.concatenate([kp_ref[...], kc_ref[...], kn_ref[...]], axis=0)
    vband = jnp.concatenate([vp_ref[...], vc_ref[...], vn_ref[...]], axis=0)
    kctx = kx_ref[...]
    vctx = vx_ref[...]
    dn_t = (((1,), (1,)), ((), ()))
    rsub = lax.broadcasted_iota(I32, (rows, 1), 0) // qb

    for g in range(N_KV_HEADS):
        qg = jnp.concatenate(
            [q_ref[:, (g * pairs_per_group + j) * LANES:(g * pairs_per_group + j + 1) * LANES]
             for j in range(pairs_per_group)], axis=0)
        acc = jnp.zeros((rows, LANES), F32)
        for par in range(2):
            col = (2 * g + par) * LANES
            s_b = lax.dot_general(qg, kband[:, col:col + LANES], dn_t, preferred_element_type=F32)
            s_c = lax.dot_general(qg, kctx[:, col:col + LANES], dn_t, preferred_element_type=F32)
            s_b = jnp.where(valid, s_b, jnp.finfo(F32).min)
            sink = jnp.zeros((rows, 1), F32)
            for j in range(pairs_per_group):
                hd = 2 * (g * pairs_per_group + j) + par
                sink = jnp.where(rsub == j, sink_ref[hd], sink)
            m = jnp.maximum(jnp.maximum(jnp.max(s_b, axis=-1, keepdims=True),
                                        jnp.max(s_c, axis=-1, keepdims=True)), sink)
            p_b = jnp.exp(s_b - m)
            p_c = jnp.exp(s_c - m)
            l = (jnp.sum(p_b, axis=-1, keepdims=True) + jnp.sum(p_c, axis=-1, keepdims=True)
                 + jnp.exp(sink - m))
            o = (jnp.dot(p_b.astype(BF16), vband[:, col:col + LANES], preferred_element_type=F32)
                 + jnp.dot(p_c.astype(BF16), vctx[:, col:col + LANES], preferred_element_type=F32))
            acc = acc + o / l
        for j in range(pairs_per_group):
            pcol = (g * pairs_per_group + j) * LANES
            o_ref[:, pcol:pcol + LANES] = acc[j * qb:(j + 1) * qb].astype(BF16)


def _attention(q, kk, vv, kkc, vvc, sinks):
    t, nq = q.shape
    qb = Q_BLOCK
    nb = t // qb
    c = kkc.shape[0]
    w = kk.shape[1]
    cur = lambda i, s: (i, 0)
    prv = lambda i, s: (jnp.maximum(i - 1, 0), 0)
    nxt = lambda i, s: (jnp.minimum(i + 1, nb - 1), 0)
    zero = lambda i, s: (0, 0)
    gs = pltpu.PrefetchScalarGridSpec(
        num_scalar_prefetch=1,
        grid=(nb,),
        in_specs=[pl.BlockSpec((qb, nq), cur),
                  pl.BlockSpec((qb, w), prv), pl.BlockSpec((qb, w), cur), pl.BlockSpec((qb, w), nxt),
                  pl.BlockSpec((qb, w), prv), pl.BlockSpec((qb, w), cur), pl.BlockSpec((qb, w), nxt),
                  pl.BlockSpec((c, w), zero), pl.BlockSpec((c, w), zero)],
        out_specs=pl.BlockSpec((qb, nq), cur),
    )
    return pl.pallas_call(
        functools.partial(_attn_kernel, t),
        grid_spec=gs,
        out_shape=jax.ShapeDtypeStruct((t, nq), BF16),
        compiler_params=_cparams(1),
        name="window_attention",
    )(sinks, q, kk, kk, kk, vv, vv, vv, kkc, vvc)


def _dispatch_out(t, d, tile):
    nt = t // tile
    tok = lambda i: (i, 0)
    specs = [pl.BlockSpec((tile, d), tok), pl.BlockSpec((MOE_REGION, d), tok),
             pl.BlockSpec((tile, LANES), tok), pl.BlockSpec((tile, LANES), tok),
             pl.BlockSpec((1, SUBLANES, LANES), lambda i: (i, 0, 0))]
    shapes = [jax.ShapeDtypeStruct((t, d), F32), jax.ShapeDtypeStruct((nt * MOE_REGION, d), F32),
              jax.ShapeDtypeStruct((t, LANES), I32), jax.ShapeDtypeStruct((t, LANES), F32),
              jax.ShapeDtypeStruct((nt, SUBLANES, LANES), I32)]
    return specs, shapes


def _oproj_kernel(x_ref, o_ref, mod_ref, wo_ref, bo_ref, g2_ref, wr_ref, br_ref,
                  x1_ref, xs_ref, pos_ref, gate_ref, cnt_ref, wb_ref):
    d = x_ref.shape[1]

    @pl.when(pl.program_id(0) == 0)
    def _():
        wb_ref[...] = wo_ref[...].astype(BF16)

    y = jnp.dot(o_ref[...], wb_ref[...], preferred_element_type=F32) + bo_ref[...]
    x1 = x_ref[...] + mod_ref[0:1, 2 * d:3 * d] * y
    x1_ref[...] = x1
    _router_dispatch(x1, g2_ref[...], mod_ref[0:1, 3 * d:4 * d], mod_ref[0:1, 4 * d:5 * d],
                     wr_ref[...], br_ref[...], xs_ref, pos_ref, gate_ref, cnt_ref)


def _oproj_router(xt, o, mod_l, w_o, b_o, g2n, wr, br):
    t, d = xt.shape
    tile = MOE_TILE
    tok = lambda i: (i, 0)
    out_specs, out_shape = _dispatch_out(t, d, tile)
    return pl.pallas_call(
        _oproj_kernel,
        grid=(t // tile,),
        in_specs=[pl.BlockSpec((tile, d), tok), pl.BlockSpec((tile, o.shape[1]), tok), _full(mod_l.shape),
                  _full(w_o.shape), _full((1, d)), _full((1, d)), _full(wr.shape), _full(br.shape)],
        out_specs=out_specs,
        out_shape=out_shape,
        scratch_shapes=[pltpu.VMEM(w_o.shape, BF16)],
        compiler_params=_cparams(1),
        name="oproj_router",
    )(xt, o, mod_l, w_o, b_o, g2n, wr, br)


def _moe_kernel(bexp_ref, nvg_ref, gsrc_ref, xs_hbm, w1_ref, b1_ref, w2_ref, b2_ref, ys_hbm,
                xbuf, obuf, w1b, w2b, gsem, ssem):
    b = pl.program_id(0)
    nb = pl.num_programs(0)
    gpb = GRANULES_PER_BLOCK
    dff = w2_ref.shape[1]
    slot = b % 2

    def granule_rows(blk, gi):
        return pl.ds(pl.multiple_of(gsrc_ref[blk * gpb + gi] * GRANULE, GRANULE), GRANULE)

    def local_rows(gi):
        return pl.ds(pl.multiple_of(gi * GRANULE, GRANULE), GRANULE)

    def gather_copy(blk, s, gi):
        return pltpu.make_async_copy(xs_hbm.at[granule_rows(blk, gi)], xbuf.at[s, local_rows(gi)], gsem.at[s])

    def scatter_copy(blk, s, gi):
        return pltpu.make_async_copy(obuf.at[s, local_rows(gi)], ys_hbm.at[granule_rows(blk, gi)], ssem.at[s])

    def for_granules(n, fn):
        def body(gi, carry):
            fn(gi)
            return carry
        lax.fori_loop(0, n, body, 0)

    nv = nvg_ref[b]

    @pl.when(b == 0)
    def _():
        xbuf[...] = jnp.zeros(xbuf.shape, F32)
        for_granules(nv, lambda gi: gather_copy(0, 0, gi).start())

    @pl.when(b + 1 < nb)
    def _():
        nxt = jnp.minimum(b + 1, nb - 1)
        for_granules(nvg_ref[nxt], lambda gi: gather_copy(nxt, 1 - slot, gi).start())

    @pl.when(b >= 2)
    def _():
        old = jnp.maximum(b - 2, 0)
        for_granules(nvg_ref[old], lambda gi: scatter_copy(old, slot, gi).wait())

    @pl.when(nv > 0)
    def _():
        changed = jnp.logical_or(b == 0, bexp_ref[b] != bexp_ref[jnp.maximum(b - 1, 0)])

        @pl.when(changed)
        def _():
            w1b[...] = w1_ref[0].astype(BF16)
            w2b[...] = w2_ref[0].astype(BF16)

        for_granules(nv, lambda gi: gather_copy(b, slot, gi).wait())
        xb = xbuf[slot].astype(BF16)
        gu = jnp.dot(xb, w1b[...], preferred_element_type=F32) + b1_ref[0]
        gt = jnp.minimum(gu[:, :dff], SWIGLU_LIMIT)
        up = jnp.clip(gu[:, dff:], -SWIGLU_LIMIT, SWIGLU_LIMIT)
        act = gt * jax.nn.sigmoid(SWIGLU_ALPHA * gt) * (up + 1.0)
        obuf[slot] = jnp.dot(act.astype(BF16), w2b[...], preferred_element_type=F32) + b2_ref[0]
        for_granules(nv, lambda gi: scatter_copy(b, slot, gi).start())

    @pl.when(b == nb - 1)
    def _():
        for_granules(nv, lambda gi: scatter_copy(b, slot, gi).wait())

        @pl.when(nb >= 2)
        def _():
            prev = jnp.maximum(b - 1, 0)
            for_granules(nvg_ref[prev], lambda gi: scatter_copy(prev, 1 - slot, gi).wait())


def _moe_experts(xs, block_exp, n_granules, granule_src, w1, b1, w2, b2):
    d = xs.shape[1]
    n_blocks = block_exp.shape[0]
    dff = w2.shape[1]
    n_exp = w1.shape[0]
    by_expert = lambda b, be, ng, gs: (be[b], 0, 0)
    gs = pltpu.PrefetchScalarGridSpec(
        num_scalar_prefetch=3,
        grid=(n_blocks,),
        in_specs=[pl.BlockSpec(memory_space=pl.ANY),
                  pl.BlockSpec((1, d, 2 * dff), by_expert), pl.BlockSpec((1, 1, 2 * dff), by_expert),
                  pl.BlockSpec((1, dff, d), by_expert), pl.BlockSpec((1, 1, d), by_expert)],
        out_specs=pl.BlockSpec(memory_space=pl.ANY),
        scratch_shapes=[pltpu.VMEM((2, EXPERT_ROWS, d), F32), pltpu.VMEM((2, EXPERT_ROWS, d), F32),
                        pltpu.VMEM((d, 2 * dff), BF16), pltpu.VMEM((dff, d), BF16),
                        pltpu.SemaphoreType.DMA((2,)), pltpu.SemaphoreType.DMA((2,))],
    )
    return pl.pallas_call(
        _moe_kernel,
        grid_spec=gs,
        out_shape=jax.ShapeDtypeStruct(xs.shape, F32),
        input_output_aliases={3: 0},
        compiler_params=_cparams(1),
        name="moe_experts",
    )(block_exp, n_granules, granule_src, xs, w1, b1.reshape(n_exp, 1, 2 * dff), w2, b2.reshape(n_exp, 1, d))


def _block_tables(cnt):
    nt = cnt.shape[0]
    gpb = GRANULES_PER_BLOCK
    c8 = cnt[:, 0, :N_EXPERTS]
    off = jnp.cumsum(c8, axis=1) - c8
    src_start = (jnp.arange(nt, dtype=I32)[:, None] * MOE_REGION + off) // GRANULE
    glen = (c8 // GRANULE).T
    n_g = jnp.sum(glen, axis=1)
    blocks_e = (n_g + gpb - 1) // gpb
    bend = jnp.cumsum(blocks_e)
    bstart = bend - blocks_e
    seg_dst = (bstart[:, None] * gpb + jnp.cumsum(glen, axis=1) - glen).reshape(-1)
    seg_len = glen.reshape(-1)
    seg_src = src_start.T.reshape(-1)
    n_blocks = (nt * (MOE_TILE * TOP_K + N_EXPERTS * (GRANULE - 1))) // EXPERT_ROWS + N_EXPERTS
    q = jnp.arange(n_blocks * gpb, dtype=I32)
    s = jnp.maximum(jnp.searchsorted(seg_dst, q, side='right').astype(I32) - 1, 0)
    valid = q < seg_dst[s] + seg_len[s]
    granule_src = jnp.where(valid, seg_src[s] + q - seg_dst[s], 0).astype(I32)
    bidx = jnp.arange(n_blocks, dtype=I32)
    raw = jnp.searchsorted(bend, bidx, side='right').astype(I32)
    block_exp = jnp.minimum(raw, N_EXPERTS - 1)
    n_gran = jnp.where(raw < N_EXPERTS, jnp.clip(n_g[block_exp] - (bidx - bstart[block_exp]) * gpb, 0, gpb), 0)
    return block_exp, n_gran.astype(I32), granule_src


def _combine_kernel(x_ref, ys_ref, pos_ref, gate_ref, mod_ref, o_ref):
    d = x_ref.shape[1]
    o_ref[...] = x_ref[...] + mod_ref[0:1, 5 * d:6 * d] * _moe_combine(ys_ref, pos_ref, gate_ref)


def _combine_glu_kernel(x_ref, ys_ref, pos_ref, gate_ref, modp_ref, modn_ref, g_ref, w_ref, b_ref,
                        o_ref, u_ref, wb_ref):
    d = x_ref.shape[1]

    @pl.when(pl.program_id(0) == 0)
    def _():
        wb_ref[...] = w_ref[...].astype(BF16)

    x2 = x_ref[...] + modp_ref[0:1, 5 * d:6 * d] * _moe_combine(ys_ref, pos_ref, gate_ref)
    o_ref[...] = x2
    h = _rms_mod(x2, g_ref[...], modn_ref[0:1, 0:d], modn_ref[0:1, d:2 * d])
    u = jnp.dot(h.astype(BF16), wb_ref[...], preferred_element_type=F32) + b_ref[...]
    u_ref[...] = u[:, :d] * jax.nn.sigmoid(u[:, d:])


def _combine_in_specs(d, tile):
    tok = lambda i: (i, 0)
    return [pl.BlockSpec((tile, d), tok), pl.BlockSpec((MOE_REGION, d), tok),
            pl.BlockSpec((tile, LANES), tok), pl.BlockSpec((tile, LANES), tok)]


def _combine(x1, ys, pos, gates, mod_l):
    t, d = x1.shape
    tile = MOE_TILE
    return pl.pallas_call(
        _combine_kernel,
        grid=(t // tile,),
        in_specs=_combine_in_specs(d, tile) + [_full(mod_l.shape)],
        out_specs=pl.BlockSpec((tile, d), lambda i: (i, 0)),
        out_shape=jax.ShapeDtypeStruct((t, d), F32),
        compiler_params=_cparams(1),
        name="moe_combine",
    )(x1, ys, pos, gates, mod_l)


def _combine_glu(x1, ys, pos, gates, mod_prev, mod_next, g1n, w_pw1, b_pw1):
    t, d = x1.shape
    tile = MOE_TILE
    tok = lambda i: (i, 0)
    return pl.pallas_call(
        _combine_glu_kernel,
        grid=(t // tile,),
        in_specs=_combine_in_specs(d, tile)
                 + [_full(mod_prev.shape), _full(mod_next.shape), _full((1, d)),
                    _full(w_pw1.shape), _full((1, 2 * d))],
        out_specs=[pl.BlockSpec((tile, d), tok), pl.BlockSpec((tile, d), tok)],
        out_shape=[jax.ShapeDtypeStruct((t, d), F32), jax.ShapeDtypeStruct((t, d), F32)],
        scratch_shapes=[pltpu.VMEM(w_pw1.shape, BF16)],
        compiler_params=_cparams(1),
        name="combine_pw1_glu",
    )(x1, ys, pos, gates, mod_prev, mod_next, g1n, w_pw1, b_pw1)


def _conv_kernel(x_ref, up_ref, uc_ref, un_ref, mod_ref, wdw_ref, bdw_ref, lg_ref, lb_ref, w2_ref, b2_ref,
                 g2_ref, wr_ref, br_ref, x1_ref, xs_ref, pos_ref, gate_ref, cnt_ref, wb_ref, ubuf):
    i = pl.program_id(0)
    n = pl.num_programs(0)
    d = x_ref.shape[1]
    tile = x_ref.shape[0]
    halo = up_ref.shape[0]

    @pl.when(i == 0)
    def _():
        wb_ref[...] = w2_ref[...].astype(BF16)

    ubuf[0:halo, :] = jnp.where(i > 0, up_ref[...], 0.0)
    ubuf[halo:halo + tile, :] = uc_ref[...]
    ubuf[halo + tile:, :] = jnp.where(i < n - 1, un_ref[...], 0.0)

    acc = jnp.zeros((tile, d), F32) + bdw_ref[...]
    for j in range(CONV_WIDTH):
        off = halo + j - CONV_PAD
        acc = acc + ubuf[off:off + tile, :] * wdw_ref[j:j + 1, :]

    mu = jnp.mean(acc, axis=-1, keepdims=True)
    cen = acc - mu
    var = jnp.mean(cen * cen, axis=-1, keepdims=True)
    z = _silu(cen * lax.rsqrt(var + EPS) * lg_ref[...] + lb_ref[...])
    y = jnp.dot(z.astype(BF16), wb_ref[...], preferred_element_type=F32) + b2_ref[...]
    x1 = x_ref[...] + mod_ref[0:1, 2 * d:3 * d] * y
    x1_ref[...] = x1
    _router_dispatch(x1, g2_ref[...], mod_ref[0:1, 3 * d:4 * d], mod_ref[0:1, 4 * d:5 * d],
                     wr_ref[...], br_ref[...], xs_ref, pos_ref, gate_ref, cnt_ref)


def _conv_router(x2, u, mod_l, w_dw, b_dw, ln_g, ln_b, w_pw2, b_pw2, g2n, wr, br):
    t, d = x2.shape
    tile = MOE_TILE
    nt = t // tile
    hb = tile // CONV_HALO
    nh = t // CONV_HALO
    tok = lambda i: (i, 0)
    prv = lambda i: (jnp.maximum(i * hb - 1, 0), 0)
    nxt = lambda i: (jnp.minimum((i + 1) * hb, nh - 1), 0)
    wpad = jnp.zeros((32, d), F32).at[:CONV_WIDTH].set(w_dw)
    out_specs, out_shape = _dispatch_out(t, d, tile)
    return pl.pallas_call(
        _conv_kernel,
        grid=(nt,),
        in_specs=[pl.BlockSpec((tile, d), tok), pl.BlockSpec((CONV_HALO, d), prv),
                  pl.BlockSpec((tile, d), tok), pl.BlockSpec((CONV_HALO, d), nxt),
                  _full(mod_l.shape), _full((32, d)), _full((1, d)), _full((1, d)), _full((1, d)),
                  _full(w_pw2.shape), _full((1, d)), _full((1, d)), _full(wr.shape), _full(br.shape)],
        out_specs=out_specs,
        out_shape=out_shape,
        scratch_shapes=[pltpu.VMEM(w_pw2.shape, BF16), pltpu.VMEM((tile + 2 * CONV_HALO, d), F32)],
        compiler_params=_cparams(1),
        name="conv_router",
    )(x2, u, u, u, mod_l, wpad, b_dw, ln_g, ln_b, w_pw2, b_pw2, g2n, wr, br)


def _rope_tables(n_rows):
    quarter = HEAD_DIM // 4
    inv = ROPE_BASE ** (-jnp.arange(quarter, dtype=F32) / quarter)

    def tables(npos):
        ang = jnp.arange(npos, dtype=I32).astype(F32)[:, None] * inv[None, :]
        z = jnp.zeros_like(ang)
        c, s = jnp.cos(ang), jnp.sin(ang)
        half = lambda a, b: jnp.concatenate([a, b], axis=1)
        full = lambda h: jnp.concatenate([h, h, h, h], axis=1)
        return jnp.stack([full(half(c, c)), full(half(-s, z)), full(half(z, s))])

    return tables(n_rows), tables(GRID_W)


def _router_params(w_router, b_router):
    d, e = w_router.shape
    wr = jnp.zeros((d, LANES), F32).at[:, :e].set(w_router)
    br = jnp.full((1, LANES), NEG_BIG, F32).at[0, :e].set(b_router)
    return wr, br


def kernel(x, c, ctx, c_ctx, w_mod, b_mod, norm1_g, norm2_g, attn_w_qkv, attn_b_qkv, attn_w_o, attn_b_o,
           attn_q_norm, attn_k_norm, attn_sinks, conv_w_pw1, conv_b_pw1, conv_w_dw, conv_b_dw, conv_ln_g,
           conv_ln_b, conv_w_pw2, conv_b_pw2, moe_w_router, moe_b_router, moe_w1, moe_b1, moe_w2, moe_b2):
    bsz, t, d = x.shape
    assert bsz == 1 and w_mod.shape[0] == 2
    n_ctx = ctx.shape[1]
    tile = min(TOKEN_TILE, t)
    assert t % tile == 0 and t % Q_BLOCK == 0 and t % MOE_TILE == 0 and tile % GRID_W == 0
    xt = x.reshape(t, d)
    row = lambda v: v.reshape(1, -1)

    mod = _mod_vectors(c, c_ctx, w_mod, b_mod)
    mod0, mod1 = mod[0], mod[1]

    bd = (jnp.arange(2 * LANES)[:, None] // HEAD_DIM == jnp.arange(2 * LANES)[None, :] // HEAD_DIM).astype(BF16)
    qg = row(jnp.tile(attn_q_norm[0], N_HEADS))
    kg = row(jnp.tile(attn_k_norm[0], N_KV_HEADS))
    rowt, colt = _rope_tables(t // GRID_W)
    g1n = row(norm1_g[0])
    bq = row(attn_b_qkv[0])
    q, kk, vv = _qkv_project(xt, mod0, 0, True, g1n, attn_w_qkv[0], bq, qg, kg, rowt, colt, bd, tile)
    _, kkc, vvc = _qkv_project(ctx.reshape(n_ctx, d), mod0, 1, False, g1n, attn_w_qkv[0], bq, qg, kg,
                               rowt[:, :max(n_ctx // GRID_W, 1)], colt, bd, n_ctx)
    o = _attention(q, kk, vv, kkc, vvc, attn_sinks[0])
    wr0, br0 = _router_params(moe_w_router[0], moe_b_router[0])
    x1, xs, pos, gates, cnt = _oproj_router(xt, o, mod0, attn_w_o[0], row(attn_b_o[0]), row(norm2_g[0]),
                                            wr0, br0)
    ys = _moe_experts(xs, *_block_tables(cnt), moe_w1[0], moe_b1[0], moe_w2[0], moe_b2[0])

    x2, u = _combine_glu(x1, ys, pos, gates, mod0, mod1, row(norm1_g[1]), conv_w_pw1[0], row(conv_b_pw1[0]))
    wr1, br1 = _router_params(moe_w_router[1], moe_b_router[1])
    x3, xs, pos, gates, cnt = _conv_router(x2, u, mod1, conv_w_dw[0], row(conv_b_dw[0]), row(conv_ln_g[0]),
                                           row(conv_ln_b[0]), conv_w_pw2[0], row(conv_b_pw2[0]),
                                           row(norm2_g[1]), wr1, br1)
    ys = _moe_experts(xs, *_block_tables(cnt), moe_w1[1], moe_b1[1], moe_w2[1], moe_b2[1])
    out = _combine(x3, ys, pos, gates, mod1)
    return out.reshape(1, t, d)
```

```python
import functools

import jax
import jax.numpy as jnp
from jax import lax
from jax.experimental import pallas as pl
from jax.experimental.pallas import tpu as pltpu

F32 = jnp.float32
BF16 = jnp.bfloat16
I32 = jnp.int32

N_HEADS = 16
N_KV_HEADS = 2
HEAD_DIM = 64
GRID_W = 64
WINDOW = 128
Q_BLOCK = 128
ROPE_BASE = 10000.0
CONV_WIDTH = 31
CONV_PAD = CONV_WIDTH // 2
N_EXPERTS = 32
TOP_K = 4
SWIGLU_LIMIT = 7.0
SWIGLU_ALPHA = 1.702
EPS = 1e-6

LANES = 128
SUBLANES = 8
TOKEN_TILE = 512
MOE_TILE = 256
GRANULE = SUBLANES
MOE_REGION = -(-(MOE_TILE * TOP_K + N_EXPERTS * (GRANULE - 1)) // 256) * 256
EXPERT_ROWS = 256
GRANULES_PER_BLOCK = EXPERT_ROWS // GRANULE
CONV_HALO = 16
NEG_BIG = -1e30
VMEM_LIMIT = 56 * 1024 * 1024


def _cparams(n_axes=1, vmem=VMEM_LIMIT):
    return pltpu.CompilerParams(dimension_semantics=("arbitrary",) * n_axes, vmem_limit_bytes=vmem)


def _full(shape):
    nd = len(shape)
    return pl.BlockSpec(shape, lambda *_: (0,) * nd)


def _silu(v):
    return v * jax.nn.sigmoid(v)


def _rms_mod(xv, g, shift, scale):
    ms = jnp.mean(xv * xv, axis=-1, keepdims=True)
    return (xv * lax.rsqrt(ms + EPS)) * g * (1.0 + scale) + shift


def _split_dot(a, b_bf16):
    hi = a.astype(BF16)
    lo = (a - hi.astype(F32)).astype(BF16)
    return (jnp.dot(hi, b_bf16, preferred_element_type=F32)
            + jnp.dot(lo, b_bf16, preferred_element_type=F32))


def _router_dispatch(x_new, g2n, sh2, sc2, wr, br, xs_ref, pos_ref, gate_ref, cnt_ref):
    tt = x_new.shape[0]
    rt = xs_ref.shape[0]
    h2 = _rms_mod(x_new, g2n, sh2, sc2)
    logits = jnp.dot(h2, wr, precision=lax.Precision.HIGHEST, preferred_element_type=F32) + br
    lane = lax.broadcasted_iota(I32, logits.shape, 1)
    val_out = jnp.full(logits.shape, NEG_BIG, F32)
    onehot = jnp.zeros(logits.shape, F32)
    sels = []
    cur = logits
    for k in range(TOP_K):
        m = jnp.max(cur, axis=-1, keepdims=True)
        sel = jnp.min(jnp.where(cur == m, lane, LANES), axis=-1, keepdims=True)
        hit = lane == sel
        sels.append(sel)
        val_out = jnp.where(lane == k, m, val_out)
        onehot = jnp.where(hit, 1.0, onehot)
        cur = jnp.where(hit, NEG_BIG * 2.0, cur)
    vmax = jnp.max(val_out, axis=-1, keepdims=True)
    ev = jnp.where(lane < TOP_K, jnp.exp(val_out - vmax), 0.0)
    gate_ref[...] = ev / jnp.sum(ev, axis=-1, keepdims=True)

    ri = lax.broadcasted_iota(I32, (tt, tt), 0)
    ci = lax.broadcasted_iota(I32, (tt, tt), 1)
    earlier = jnp.where(ci < ri, 1.0, 0.0).astype(BF16)
    rank = jnp.dot(earlier, onehot.astype(BF16), preferred_element_type=F32)
    cnt = jnp.sum(onehot, axis=0, keepdims=True).astype(I32)
    c8 = jnp.bitwise_and(cnt + (GRANULE - 1), -GRANULE)
    ue = lax.broadcasted_iota(I32, (LANES, LANES), 0)
    ve = lax.broadcasted_iota(I32, (LANES, LANES), 1)
    c8_rows = jnp.broadcast_to(c8.astype(F32), (SUBLANES, LANES)).astype(BF16)
    before = jnp.where(ue < ve, 1.0, 0.0).astype(BF16)
    off = jnp.dot(c8_rows, before, preferred_element_type=F32)[0:1]
    cnt_ref[0] = jnp.broadcast_to(c8, (SUBLANES, LANES))

    slot_of = off + rank
    pos_out = jnp.zeros(logits.shape, I32)
    r_iota = lax.broadcasted_iota(I32, (tt, rt), 1)
    perm_t = jnp.zeros((tt, rt), F32)
    for k in range(TOP_K):
        pk = jnp.sum(jnp.where(lane == sels[k], slot_of, 0.0), axis=-1, keepdims=True).astype(I32)
        pos_out = jnp.where(lane == k, pk, pos_out)
        perm_t = jnp.where(r_iota == pk, 1.0, perm_t)
    pos_ref[...] = pos_out
    perm = perm_t.T.astype(BF16)
    xs_ref[...] = jnp.dot(perm, h2.astype(BF16), preferred_element_type=F32)


def _moe_combine(ys_ref, pos_ref, gate_ref):
    tt = pos_ref.shape[0]
    rt = ys_ref.shape[0]
    r_iota = lax.broadcasted_iota(I32, (tt, rt), 1)
    g = jnp.zeros((tt, rt), F32)
    for k in range(TOP_K):
        g = jnp.where(r_iota == pos_ref[:, k:k + 1], gate_ref[:, k:k + 1], g)
    return jnp.dot(g.astype(BF16), ys_ref[...].astype(BF16), preferred_element_type=F32)


def _mod_kernel(cc_ref, w_ref, b_ref, o_ref):
    a = _silu(cc_ref[...])
    o_ref[0] = jnp.dot(a, w_ref[0], precision=lax.Precision.HIGHEST,
                       preferred_element_type=F32) + b_ref[0]


def _mod_vectors(c, c_ctx, w_mod, b_mod):
    depth, d, d6 = w_mod.shape
    cc = jnp.zeros((SUBLANES, d), F32).at[0].set(c[0]).at[1].set(c_ctx)
    ncol = 4
    cw = d6 // ncol
    return pl.pallas_call(
        _mod_kernel,
        grid=(depth, ncol),
        in_specs=[pl.BlockSpec((SUBLANES, d), lambda l, j: (0, 0)),
                  pl.BlockSpec((1, d, cw), lambda l, j: (l, 0, j)),
                  pl.BlockSpec((1, 1, cw), lambda l, j: (l, 0, j))],
        out_specs=pl.BlockSpec((1, SUBLANES, cw), lambda l, j: (l, 0, j)),
        out_shape=jax.ShapeDtypeStruct((depth, SUBLANES, d6), F32),
        compiler_params=_cparams(2),
        name="mod_vectors",
    )(cc, w_mod, b_mod.reshape(depth, 1, d6))


def _qkv_kernel(mod_row, use_rope, x_ref, mod_ref, g_ref, w_ref, b_ref, qg_ref, kg_ref, rowt_ref, colt_ref,
                bd_ref, q_ref, k_ref, v_ref, wb_ref):
    d = x_ref.shape[1]
    tile = x_ref.shape[0]
    nq = N_HEADS * HEAD_DIM
    nk = N_KV_HEADS * HEAD_DIM

    @pl.when(pl.program_id(0) == 0)
    def _():
        wb_ref[...] = w_ref[...].astype(BF16)

    sh = mod_ref[mod_row:mod_row + 1, 0:d]
    sc = mod_ref[mod_row:mod_row + 1, d:2 * d]
    h = _rms_mod(x_ref[...], g_ref[...], sh, sc)
    qkv = jnp.dot(h.astype(BF16), wb_ref[...], preferred_element_type=F32) + b_ref[...]
    q = qkv[:, :nq]
    k = qkv[:, nq:nq + nk]
    v = qkv[:, nq + nk:]
    bd = bd_ref[...]

    def head_norm(t, gain):
        tt = t * t
        w = bd.shape[0]
        if t.shape[1] >= w:
            ss = jnp.concatenate([_split_dot(tt[:, c:c + w], bd) for c in range(0, t.shape[1], w)], axis=1)
        else:
            ss = _split_dot(tt, bd[:t.shape[1], :t.shape[1]])
        return t * lax.rsqrt(ss * (1.0 / HEAD_DIM) + EPS) * gain

    qn = head_norm(q, qg_ref[...])
    kn = head_norm(k, kg_ref[...])

    if use_rope:
        lane = lax.broadcasted_iota(I32, (tile, LANES), 1)
        is_row = (lane % HEAD_DIM) < HEAD_DIM // 2
        nrow = tile // GRID_W

        def table(idx):
            rt = jnp.concatenate([jnp.broadcast_to(rowt_ref[idx, r:r + 1, :], (GRID_W, LANES))
                                  for r in range(nrow)], axis=0)
            ct = jnp.concatenate([colt_ref[idx]] * nrow, axis=0)
            return jnp.where(is_row, rt, ct)

        cosv, sav, sbv = table(0), table(1), table(2)

        def rope(t):
            n = t.shape[1]
            reps = n // LANES
            ct = jnp.concatenate([cosv] * reps, axis=1) if reps > 1 else cosv
            at = jnp.concatenate([sav] * reps, axis=1) if reps > 1 else sav
            bt = jnp.concatenate([sbv] * reps, axis=1) if reps > 1 else sbv
            up = pltpu.roll(t, n - HEAD_DIM // 4, 1)
            dn = pltpu.roll(t, HEAD_DIM // 4, 1)
            return t * ct + up * at + dn * bt

        qn = rope(qn)
        kn = rope(kn)

    q_ref[...] = (qn * (HEAD_DIM ** -0.5)).astype(BF16)
    klane = lax.broadcasted_iota(I32, kn.shape, 1)
    low = klane < HEAD_DIM

    def variants(t):
        sw = pltpu.roll(t, HEAD_DIM, 1)
        z = jnp.zeros_like(t)
        return jnp.concatenate([jnp.where(low, t, z), jnp.where(low, z, sw),
                                jnp.where(low, sw, z), jnp.where(low, z, t)], axis=1)

    k_ref[...] = variants(kn).astype(BF16)
    v_ref[...] = variants(v).astype(BF16)


def _qkv_project(xt, mod_l, mod_row, use_rope, g1n, w_qkv, b_qkv, qg, kg, rowt, colt, bd, tile):
    t, d = xt.shape
    nqkv = w_qkv.shape[1]
    nq = N_HEADS * HEAD_DIM
    tok = lambda i: (i, 0)
    nrow = max(tile // GRID_W, 1)
    return pl.pallas_call(
        functools.partial(_qkv_kernel, mod_row, use_rope),
        grid=(t // tile,),
        in_specs=[pl.BlockSpec((tile, d), tok), _full(mod_l.shape), _full((1, d)),
                  _full((d, nqkv)), _full((1, nqkv)), _full((1, nq)), _full((1, LANES)),
                  pl.BlockSpec((3, nrow, LANES), lambda i: (0, i, 0)), _full(colt.shape), _full(bd.shape)],
        out_specs=[pl.BlockSpec((tile, nq), tok), pl.BlockSpec((tile, 4 * LANES), tok),
                   pl.BlockSpec((tile, 4 * LANES), tok)],
        out_shape=[jax.ShapeDtypeStruct((t, nq), BF16), jax.ShapeDtypeStruct((t, 4 * LANES), BF16),
                   jax.ShapeDtypeStruct((t, 4 * LANES), BF16)],
        scratch_shapes=[pltpu.VMEM((d, nqkv), BF16)],
        compiler_params=_cparams(1),
        name="qkv_project",
    )(xt, mod_l, g1n, w_qkv, b_qkv, qg, kg, rowt, colt, bd)


def _attn_kernel(seq_len, sink_ref, q_ref, kp_ref, kc_ref, kn_ref, vp_ref, vc_ref, vn_ref, kx_ref, vx_ref,
                 o_ref):
    i = pl.program_id(0)
    qb = q_ref.shape[0]
    band = 3 * qb
    pairs_per_group = (N_HEADS // N_KV_HEADS) // 2
    rows = pairs_per_group * qb

    r = lax.broadcasted_iota(I32, (rows, band), 0) % qb
    cpos = lax.broadcasted_iota(I32, (rows, band), 1) - qb
    kabs = i * qb + cpos
    valid = (jnp.abs(r - cpos) <= WINDOW) & (kabs >= 0) & (kabs < seq_len)

    kband = jnp.concatenate([kp_ref[...], kc_ref[...], kn_ref[...]], axis=0)
    vband = jnp.concatenate([vp_ref[...], vc_ref[...], vn_ref[...]], axis=0)
    kctx = kx_ref[...]
    vctx = vx_ref[...]
    dn_t = (((1,), (1,)), ((), ()))
    rsub = lax.broadcasted_iota(I32, (rows, 1), 0) // qb

    for g in range(N_KV_HEADS):
        qg = jnp.concatenate(
            [q_ref[:, (g * pairs_per_group + j) * LANES:(g * pairs_per_group + j + 1) * LANES]
             for j in range(pairs_per_group)], axis=0)
        acc = jnp.zeros((rows, LANES), F32)
        for par in range(2):
            col = (2 * g + par) * LANES
            s_b = lax.dot_general(qg, kband[:, col:col + LANES], dn_t, preferred_element_type=F32)
            s_c = lax.dot_general(qg, kctx[:, col:col + LANES], dn_t, preferred_element_type=F32)
            s_b = jnp.where(valid, s_b, jnp.finfo(F32).min)
            sink = jnp.zeros((rows, 1), F32)
            for j in range(pairs_per_group):
                hd = 2 * (g * pairs_per_group + j) + par
                sink = jnp.where(rsub == j, sink_ref[hd], sink)
            m = jnp.maximum(jnp.maximum(jnp.max(s_b, axis=-1, keepdims=True),
                                        jnp.max(s_c, axis=-1, keepdims=True)), sink)
            p_b = jnp.exp(s_b - m)
            p_c = jnp.exp(s_c - m)
            l = (jnp.sum(p_b, axis=-1, keepdims=True) + jnp.sum(p_c, axis=-1, keepdims=True)
                 + jnp.exp(sink - m))
            o = (jnp.dot(p_b.astype(BF16), vband[:, col:col + LANES], preferred_element_type=F32)
                 + jnp.dot(p_c.astype(BF16), vctx[:, col:col + LANES], preferred_element_type=F32))
            acc = acc + o / l
        for j in range(pairs_per_group):
            pcol = (g * pairs_per_group + j) * LANES
            o_ref[:, pcol:pcol + LANES] = acc[j * qb:(j + 1) * qb].astype(BF16)


def _attention(q, kk, vv, kkc, vvc, sinks):
    t, nq = q.shape
    qb = Q_BLOCK
    nb = t // qb
    c = kkc.shape[0]
    w = kk.shape[1]
    cur = lambda i, s: (i, 0)
    prv = lambda i, s: (jnp.maximum(i - 1, 0), 0)
    nxt = lambda i, s: (jnp.minimum(i + 1, nb - 1), 0)
    zero = lambda i, s: (0, 0)
    gs = pltpu.PrefetchScalarGridSpec(
        num_scalar_prefetch=1,
        grid=(nb,),
        in_specs=[pl.BlockSpec((qb, nq), cur),
                  pl.BlockSpec((qb, w), prv), pl.BlockSpec((qb, w), cur), pl.BlockSpec((qb, w), nxt),
                  pl.BlockSpec((qb, w), prv), pl.BlockSpec((qb, w), cur), pl.BlockSpec((qb, w), nxt),
                  pl.BlockSpec((c, w), zero), pl.BlockSpec((c, w), zero)],
        out_specs=pl.BlockSpec((qb, nq), cur),
    )
    return pl.pallas_call(
        functools.partial(_attn_kernel, t),
        grid_spec=gs,
        out_shape=jax.ShapeDtypeStruct((t, nq), BF16),
        compiler_params=_cparams(1),
        name="window_attention",
    )(sinks, q, kk, kk, kk, vv, vv, vv, kkc, vvc)


def _dispatch_out(t, d, tile):
    nt = t // tile
    tok = lambda i: (i, 0)
    specs = [pl.BlockSpec((tile, d), tok), pl.BlockSpec((MOE_REGION, d), tok),
             pl.BlockSpec((tile, LANES), tok), pl.BlockSpec((tile, LANES), tok),
             pl.BlockSpec((1, SUBLANES, LANES), lambda i: (i, 0, 0))]
    shapes = [jax.ShapeDtypeStruct((t, d), F32), jax.ShapeDtypeStruct((nt * MOE_REGION, d), F32),
              jax.ShapeDtypeStruct((t, LANES), I32), jax.ShapeDtypeStruct((t, LANES), F32),
              jax.ShapeDtypeStruct((nt, SUBLANES, LANES), I32)]
    return specs, shapes


def _oproj_kernel(x_ref, o_ref, mod_ref, wo_ref, bo_ref, g2_ref, wr_ref, br_ref,
                  x1_ref, xs_ref, pos_ref, gate_ref, cnt_ref, wb_ref):
    d = x_ref.shape[1]

    @pl.when(pl.program_id(0) == 0)
    def _():
        wb_ref[...] = wo_ref[...].astype(BF16)

    y = jnp.dot(o_ref[...], wb_ref[...], preferred_element_type=F32) + bo_ref[...]
    x1 = x_ref[...] + mod_ref[0:1, 2 * d:3 * d] * y
    x1_ref[...] = x1
    _router_dispatch(x1, g2_ref[...], mod_ref[0:1, 3 * d:4 * d], mod_ref[0:1, 4 * d:5 * d],
                     wr_ref[...], br_ref[...], xs_ref, pos_ref, gate_ref, cnt_ref)


def _oproj_router(xt, o, mod_l, w_o, b_o, g2n, wr, br):
    t, d = xt.shape
    tile = MOE_TILE
    tok = lambda i: (i, 0)
    out_specs, out_shape = _dispatch_out(t, d, tile)
    return pl.pallas_call(
        _oproj_kernel,
        grid=(t // tile,),
        in_specs=[pl.BlockSpec((tile, d), tok), pl.BlockSpec((tile, o.shape[1]), tok), _full(mod_l.shape),
                  _full(w_o.shape), _full((1, d)), _full((1, d)), _full(wr.shape), _full(br.shape)],
        out_specs=out_specs,
        out_shape=out_shape,
        scratch_shapes=[pltpu.VMEM(w_o.shape, BF16)],
        compiler_params=_cparams(1),
        name="oproj_router",
    )(xt, o, mod_l, w_o, b_o, g2n, wr, br)


def _moe_kernel(bexp_ref, nvg_ref, gsrc_ref, xs_hbm, w1_ref, b1_ref, w2_ref, b2_ref, ys_hbm,
                xbuf, obuf, w1b, w2b, gsem, ssem):
    b = pl.program_id(0)
    nb = pl.num_programs(0)
    gpb = GRANULES_PER_BLOCK
    dff = w2_ref.shape[1]
    slot = b % 2

    def granule_rows(blk, gi):
        return pl.ds(pl.multiple_of(gsrc_ref[blk * gpb + gi] * GRANULE, GRANULE), GRANULE)

    def local_rows(gi):
        return pl.ds(pl.multiple_of(gi * GRANULE, GRANULE), GRANULE)

    def gather_copy(blk, s, gi):
        return pltpu.make_async_copy(xs_hbm.at[granule_rows(blk, gi)], xbuf.at[s, local_rows(gi)], gsem.at[s])

    def scatter_copy(blk, s, gi):
        return pltpu.make_async_copy(obuf.at[s, local_rows(gi)], ys_hbm.at[granule_rows(blk, gi)], ssem.at[s])

    def for_granules(n, fn):
        def body(gi, carry):
            fn(gi)
            return carry
        lax.fori_loop(0, n, body, 0)

    nv = nvg_ref[b]

    @pl.when(b == 0)
    def _():
        xbuf[...] = jnp.zeros(xbuf.shape, F32)
        for_granules(nv, lambda gi: gather_copy(0, 0, gi).start())

    @pl.when(b + 1 < nb)
    def _():
        nxt = jnp.minimum(b + 1, nb - 1)
        for_granules(nvg_ref[nxt], lambda gi: gather_copy(nxt, 1 - slot, gi).start())

    @pl.when(b >= 2)
    def _():
        old = jnp.maximum(b - 2, 0)
        for_granules(nvg_ref[old], lambda gi: scatter_copy(old, slot, gi).wait())

    @pl.when(nv > 0)
    def _():
        changed = jnp.logical_or(b == 0, bexp_ref[b] != bexp_ref[jnp.maximum(b - 1, 0)])

        @pl.when(changed)
        def _():
            w1b[...] = w1_ref[0].astype(BF16)
            w2b[...] = w2_ref[0].astype(BF16)

        for_granules(nv, lambda gi: gather_copy(b, slot, gi).wait())
        xb = xbuf[slot].astype(BF16)
        gu = jnp.dot(xb, w1b[...], preferred_element_type=F32) + b1_ref[0]
        gt = jnp.minimum(gu[:, :dff], SWIGLU_LIMIT)
        up = jnp.clip(gu[:, dff:], -SWIGLU_LIMIT, SWIGLU_LIMIT)
        act = gt * jax.nn.sigmoid(SWIGLU_ALPHA * gt) * (up + 1.0)
        obuf[slot] = jnp.dot(act.astype(BF16), w2b[...], preferred_element_type=F32) + b2_ref[0]
        for_granules(nv, lambda gi: scatter_copy(b, slot, gi).start())

    @pl.when(b == nb - 1)
    def _():
        for_granules(nv, lambda gi: scatter_copy(b, slot, gi).wait())

        @pl.when(nb >= 2)
        def _():
            prev = jnp.maximum(b - 1, 0)
            for_granules(nvg_ref[prev], lambda gi: scatter_copy(prev, 1 - slot, gi).wait())


def _moe_experts(xs, block_exp, n_granules, granule_src, layer, w1, b1, w2, b2):
    d = xs.shape[1]
    n_blocks = block_exp.shape[0]
    dff = w2.shape[2]
    n_layers, n_exp = w1.shape[:2]
    by_expert = lambda b, be, ng, gs: (layer, be[b], 0, 0)
    gs = pltpu.PrefetchScalarGridSpec(
        num_scalar_prefetch=3,
        grid=(n_blocks,),
        in_specs=[pl.BlockSpec(memory_space=pl.ANY),
                  pl.BlockSpec((None, 1, d, 2 * dff), by_expert), pl.BlockSpec((None, 1, 1, 2 * dff), by_expert),
                  pl.BlockSpec((None, 1, dff, d), by_expert), pl.BlockSpec((None, 1, 1, d), by_expert)],
        out_specs=pl.BlockSpec(memory_space=pl.ANY),
        scratch_shapes=[pltpu.VMEM((2, EXPERT_ROWS, d), F32), pltpu.VMEM((2, EXPERT_ROWS, d), F32),
                        pltpu.VMEM((d, 2 * dff), BF16), pltpu.VMEM((dff, d), BF16),
                        pltpu.SemaphoreType.DMA((2,)), pltpu.SemaphoreType.DMA((2,))],
    )
    return pl.pallas_call(
        _moe_kernel,
        grid_spec=gs,
        out_shape=jax.ShapeDtypeStruct(xs.shape, F32),
        input_output_aliases={3: 0},
        compiler_params=_cparams(1),
        name="moe_experts",
    )(block_exp, n_granules, granule_src, xs, w1, b1.reshape(n_layers, n_exp, 1, 2 * dff),
      w2, b2.reshape(n_layers, n_exp, 1, d))


def _tables_kernel(cnt_ref, bexp_ref, ngran_ref, gsrc_ref, off_ref):
    nt = cnt_ref.shape[0]
    nbk = bexp_ref.shape[0]
    gpb = GRANULES_PER_BLOCK
    log_g = GRANULE.bit_length() - 1
    log_b = gpb.bit_length() - 1

    def loop(lo, hi, fn, init=0):
        return lax.fori_loop(lo, hi, fn, init)

    def init_block(i, c):
        bexp_ref[i] = N_EXPERTS - 1
        ngran_ref[i] = 0
        return c

    def init_granule(i, c):
        gsrc_ref[i] = 0
        return c

    def init_off(j, c):
        off_ref[j] = 0
        return c

    loop(0, nbk, init_block)
    loop(0, nbk * gpb, init_granule)
    loop(0, nt, init_off)

    def expert_body(e, blk0):
        q0 = blk0 * gpb

        def tile_body(j, q):
            c = cnt_ref[j, e]
            o = off_ref[j]
            src0 = (j * MOE_REGION + o) >> log_g
            ng = c >> log_g

            def granule_body(i, cc):
                gsrc_ref[q + i] = src0 + i
                return cc

            loop(0, ng, granule_body)
            off_ref[j] = o + c
            return q + ng

        n_g = loop(0, nt, tile_body, q0) - q0
        nblk = (n_g + gpb - 1) >> log_b

        def block_body(m, cc):
            bexp_ref[blk0 + m] = e
            ngran_ref[blk0 + m] = jnp.minimum(gpb, n_g - m * gpb)
            return cc

        loop(0, nblk, block_body)
        return blk0 + nblk

    loop(0, N_EXPERTS, expert_body, 0)


def _block_tables(cnt):
    nt = cnt.shape[0]
    n_blocks = (nt * (MOE_TILE * TOP_K + N_EXPERTS * (GRANULE - 1))) // EXPERT_ROWS + N_EXPERTS
    smem = pl.BlockSpec(memory_space=pltpu.SMEM)
    return pl.pallas_call(
        _tables_kernel,
        in_specs=[smem],
        out_specs=[smem, smem, smem],
        out_shape=[jax.ShapeDtypeStruct((n_blocks,), I32), jax.ShapeDtypeStruct((n_blocks,), I32),
                   jax.ShapeDtypeStruct((n_blocks * GRANULES_PER_BLOCK,), I32)],
        scratch_shapes=[pltpu.SMEM((nt,), I32)],
        name="moe_block_tables",
    )(cnt[:, 0, :])


def _combine_kernel(x_ref, ys_ref, pos_ref, gate_ref, mod_ref, o_ref):
    d = x_ref.shape[1]
    o_ref[...] = x_ref[...] + mod_ref[0:1, 5 * d:6 * d] * _moe_combine(ys_ref, pos_ref, gate_ref)


def _combine_glu_kernel(x_ref, ys_ref, pos_ref, gate_ref, modp_ref, modn_ref, g_ref, w_ref, b_ref,
                        o_ref, u_ref, wb_ref):
    d = x_ref.shape[1]

    @pl.when(pl.program_id(0) == 0)
    def _():
        wb_ref[...] = w_ref[...].astype(BF16)

    x2 = x_ref[...] + modp_ref[0:1, 5 * d:6 * d] * _moe_combine(ys_ref, pos_ref, gate_ref)
    o_ref[...] = x2
    h = _rms_mod(x2, g_ref[...], modn_ref[0:1, 0:d], modn_ref[0:1, d:2 * d])
    u = jnp.dot(h.astype(BF16), wb_ref[...], preferred_element_type=F32) + b_ref[...]
    u_ref[...] = u[:, :d] * jax.nn.sigmoid(u[:, d:])


def _combine_in_specs(d, tile):
    tok = lambda i: (i, 0)
    return [pl.BlockSpec((tile, d), tok), pl.BlockSpec((MOE_REGION, d), tok),
            pl.BlockSpec((tile, LANES), tok), pl.BlockSpec((tile, LANES), tok)]


def _combine(x1, ys, pos, gates, mod_l):
    t, d = x1.shape
    tile = MOE_TILE
    return pl.pallas_call(
        _combine_kernel,
        grid=(t // tile,),
        in_specs=_combine_in_specs(d, tile) + [_full(mod_l.shape)],
        out_specs=pl.BlockSpec((tile, d), lambda i: (i, 0)),
        out_shape=jax.ShapeDtypeStruct((t, d), F32),
        compiler_params=_cparams(1),
        name="moe_combine",
    )(x1, ys, pos, gates, mod_l)


def _combine_glu(x1, ys, pos, gates, mod_prev, mod_next, g1n, w_pw1, b_pw1):
    t, d = x1.shape
    tile = MOE_TILE
    tok = lambda i: (i, 0)
    return pl.pallas_call(
        _combine_glu_kernel,
        grid=(t // tile,),
        in_specs=_combine_in_specs(d, tile)
                 + [_full(mod_prev.shape), _full(mod_next.shape), _full((1, d)),
                    _full(w_pw1.shape), _full((1, 2 * d))],
        out_specs=[pl.BlockSpec((tile, d), tok), pl.BlockSpec((tile, d), tok)],
        out_shape=[jax.ShapeDtypeStruct((t, d), F32), jax.ShapeDtypeStruct((t, d), F32)],
        scratch_shapes=[pltpu.VMEM(w_pw1.shape, BF16)],
        compiler_params=_cparams(1),
        name="combine_pw1_glu",
    )(x1, ys, pos, gates, mod_prev, mod_next, g1n, w_pw1, b_pw1)


def _conv_kernel(x_ref, up_ref, uc_ref, un_ref, mod_ref, wdw_ref, bdw_ref, lg_ref, lb_ref, w2_ref, b2_ref,
                 g2_ref, wr_ref, br_ref, x1_ref, xs_ref, pos_ref, gate_ref, cnt_ref, wb_ref, ubuf):
    i = pl.program_id(0)
    n = pl.num_programs(0)
    d = x_ref.shape[1]
    tile = x_ref.shape[0]
    halo = up_ref.shape[0]

    @pl.when(i == 0)
    def _():
        wb_ref[...] = w2_ref[...].astype(BF16)

    ubuf[0:halo, :] = jnp.where(i > 0, up_ref[...], 0.0)
    ubuf[halo:halo + tile, :] = uc_ref[...]
    ubuf[halo + tile:, :] = jnp.where(i < n - 1, un_ref[...], 0.0)

    acc = jnp.zeros((tile, d), F32) + bdw_ref[...]
    for j in range(CONV_WIDTH):
        off = halo + j - CONV_PAD
        acc = acc + ubuf[off:off + tile, :] * wdw_ref[j:j + 1, :]

    mu = jnp.mean(acc, axis=-1, keepdims=True)
    cen = acc - mu
    var = jnp.mean(cen * cen, axis=-1, keepdims=True)
    z = _silu(cen * lax.rsqrt(var + EPS) * lg_ref[...] + lb_ref[...])
    y = jnp.dot(z.astype(BF16), wb_ref[...], preferred_element_type=F32) + b2_ref[...]
    x1 = x_ref[...] + mod_ref[0:1, 2 * d:3 * d] * y
    x1_ref[...] = x1
    _router_dispatch(x1, g2_ref[...], mod_ref[0:1, 3 * d:4 * d], mod_ref[0:1, 4 * d:5 * d],
                     wr_ref[...], br_ref[...], xs_ref, pos_ref, gate_ref, cnt_ref)


def _conv_router(x2, u, mod_l, w_dw, b_dw, ln_g, ln_b, w_pw2, b_pw2, g2n, wr, br):
    t, d = x2.shape
    tile = MOE_TILE
    nt = t // tile
    hb = tile // CONV_HALO
    nh = t // CONV_HALO
    tok = lambda i: (i, 0)
    prv = lambda i: (jnp.maximum(i * hb - 1, 0), 0)
    nxt = lambda i: (jnp.minimum((i + 1) * hb, nh - 1), 0)
    wpad = jnp.zeros((32, d), F32).at[:CONV_WIDTH].set(w_dw)
    out_specs, out_shape = _dispatch_out(t, d, tile)
    return pl.pallas_call(
        _conv_kernel,
        grid=(nt,),
        in_specs=[pl.BlockSpec((tile, d), tok), pl.BlockSpec((CONV_HALO, d), prv),
                  pl.BlockSpec((tile, d), tok), pl.BlockSpec((CONV_HALO, d), nxt),
                  _full(mod_l.shape), _full((32, d)), _full((1, d)), _full((1, d)), _full((1, d)),
                  _full(w_pw2.shape), _full((1, d)), _full((1, d)), _full(wr.shape), _full(br.shape)],
        out_specs=out_specs,
        out_shape=out_shape,
        scratch_shapes=[pltpu.VMEM(w_pw2.shape, BF16), pltpu.VMEM((tile + 2 * CONV_HALO, d), F32)],
        compiler_params=_cparams(1),
        name="conv_router",
    )(x2, u, u, u, mod_l, wpad, b_dw, ln_g, ln_b, w_pw2, b_pw2, g2n, wr, br)


def _rope_tables(n_rows):
    quarter = HEAD_DIM // 4
    inv = ROPE_BASE ** (-jnp.arange(quarter, dtype=F32) / quarter)

    def tables(npos):
        ang = jnp.arange(npos, dtype=I32).astype(F32)[:, None] * inv[None, :]
        z = jnp.zeros_like(ang)
        c, s = jnp.cos(ang), jnp.sin(ang)
        half = lambda a, b: jnp.concatenate([a, b], axis=1)
        full = lambda h: jnp.concatenate([h, h, h, h], axis=1)
        return jnp.stack([full(half(c, c)), full(half(-s, z)), full(half(z, s))])

    return tables(n_rows), tables(GRID_W)


def _router_params(w_router, b_router):
    d, e = w_router.shape
    wr = jnp.zeros((d, LANES), F32).at[:, :e].set(w_router)
    br = jnp.full((1, LANES), NEG_BIG, F32).at[0, :e].set(b_router)
    return wr, br


def kernel(x, c, ctx, c_ctx, w_mod, b_mod, norm1_g, norm2_g, attn_w_qkv, attn_b_qkv, attn_w_o, attn_b_o,
           attn_q_norm, attn_k_norm, attn_sinks, conv_w_pw1, conv_b_pw1, conv_w_dw, conv_b_dw, conv_ln_g,
           conv_ln_b, conv_w_pw2, conv_b_pw2, moe_w_router, moe_b_router, moe_w1, moe_b1, moe_w2, moe_b2):
    bsz, t, d = x.shape
    assert bsz == 1 and w_mod.shape[0] == 2
    n_ctx = ctx.shape[1]
    tile = min(TOKEN_TILE, t)
    assert t % tile == 0 and t % Q_BLOCK == 0 and t % MOE_TILE == 0 and tile % GRID_W == 0
    xt = x.reshape(t, d)
    row = lambda v: v.reshape(1, -1)

    mod = _mod_vectors(c, c_ctx, w_mod, b_mod)
    mod0, mod1 = mod[0], mod[1]

    bd = (jnp.arange(2 * LANES)[:, None] // HEAD_DIM == jnp.arange(2 * LANES)[None, :] // HEAD_DIM).astype(BF16)
    qg = row(jnp.tile(attn_q_norm[0], N_HEADS))
    kg = row(jnp.tile(attn_k_norm[0], N_KV_HEADS))
    rowt, colt = _rope_tables(t // GRID_W)
    g1n = row(norm1_g[0])
    bq = row(attn_b_qkv[0])
    q, kk, vv = _qkv_project(xt, mod0, 0, True, g1n, attn_w_qkv[0], bq, qg, kg, rowt, colt, bd, tile)
    _, kkc, vvc = _qkv_project(ctx.reshape(n_ctx, d), mod0, 1, False, g1n, attn_w_qkv[0], bq, qg, kg,
                               rowt[:, :max(n_ctx // GRID_W, 1)], colt, bd, n_ctx)
    o = _attention(q, kk, vv, kkc, vvc, attn_sinks[0])
    wr0, br0 = _router_params(moe_w_router[0], moe_b_router[0])
    x1, xs, pos, gates, cnt = _oproj_router(xt, o, mod0, attn_w_o[0], row(attn_b_o[0]), row(norm2_g[0]),
                                            wr0, br0)
    ys = _moe_experts(xs, *_block_tables(cnt), 0, moe_w1, moe_b1, moe_w2, moe_b2)

    x2, u = _combine_glu(x1, ys, pos, gates, mod0, mod1, row(norm1_g[1]), conv_w_pw1[0], row(conv_b_pw1[0]))
    wr1, br1 = _router_params(moe_w_router[1], moe_b_router[1])
    x3, xs, pos, gates, cnt = _conv_router(x2, u, mod1, conv_w_dw[0], row(conv_b_dw[0]), row(conv_ln_g[0]),
                                           row(conv_ln_b[0]), conv_w_pw2[0], row(conv_b_pw2[0]),
                                           row(norm2_g[1]), wr1, br1)
    ys = _moe_experts(xs, *_block_tables(cnt), 1, moe_w1, moe_b1, moe_w2, moe_b2)
    out = _combine(x3, ys, pos, gates, mod1)
    return out.reshape(1, t, d)
```

```python
import functools

import jax
import jax.numpy as jnp
from jax import lax
from jax.experimental import pallas as pl
from jax.experimental.pallas import tpu as pltpu

F32 = jnp.float32
BF16 = jnp.bfloat16
I32 = jnp.int32

N_HEADS = 16
N_KV_HEADS = 2
HEAD_DIM = 64
GRID_W = 64
WINDOW = 128
Q_BLOCK = 128
ROPE_BASE = 10000.0
CONV_WIDTH = 31
CONV_PAD = CONV_WIDTH // 2
N_EXPERTS = 32
TOP_K = 4
SWIGLU_LIMIT = 7.0
SWIGLU_ALPHA = 1.702
EPS = 1e-6

LANES = 128
SUBLANES = 8
TOKEN_TILE = 512
MOE_TILE = 256
GRANULE = SUBLANES
MOE_REGION = -(-(MOE_TILE * TOP_K + N_EXPERTS * (GRANULE - 1)) // 256) * 256
EXPERT_ROWS = 512
GRANULES_PER_BLOCK = EXPERT_ROWS // GRANULE
DMA_GROUP = 4
CONV_HALO = 16
CONV_CHUNK = 32
NEG_BIG = -1e30
VMEM_LIMIT = 56 * 1024 * 1024


def _cparams(n_axes=1, vmem=VMEM_LIMIT):
    return pltpu.CompilerParams(dimension_semantics=("arbitrary",) * n_axes, vmem_limit_bytes=vmem)


def _full(shape):
    nd = len(shape)
    return pl.BlockSpec(shape, lambda *_: (0,) * nd)


def _silu(v):
    return v * jax.nn.sigmoid(v)


def _rms_mod(xv, g, shift, scale):
    ms = jnp.mean(xv * xv, axis=-1, keepdims=True)
    return (xv * lax.rsqrt(ms + EPS)) * g * (1.0 + scale) + shift


def _split_dot(a, b_bf16):
    hi = a.astype(BF16)
    lo = (a - hi.astype(F32)).astype(BF16)
    return (jnp.dot(hi, b_bf16, preferred_element_type=F32)
            + jnp.dot(lo, b_bf16, preferred_element_type=F32))


def _router_dispatch(x_new, g2n, sh2, sc2, wr, br, xs_ref, pos_ref, gate_ref, cnt_ref):
    tt = x_new.shape[0]
    rt = xs_ref.shape[0]
    h2 = _rms_mod(x_new, g2n, sh2, sc2)
    h_hi = h2.astype(BF16)
    h_lo = (h2 - h_hi.astype(F32)).astype(BF16)
    logits = (jnp.dot(h_hi, wr[0], preferred_element_type=F32)
              + (jnp.dot(h_lo, wr[0], preferred_element_type=F32)
                 + jnp.dot(h_hi, wr[1], preferred_element_type=F32))) + br
    lane = lax.broadcasted_iota(I32, logits.shape, 1)
    val_out = jnp.full(logits.shape, NEG_BIG, F32)
    onehot = jnp.zeros(logits.shape, F32)
    sels = []
    cur = logits
    for k in range(TOP_K):
        m = jnp.max(cur, axis=-1, keepdims=True)
        sel = jnp.min(jnp.where(cur == m, lane, LANES), axis=-1, keepdims=True)
        hit = lane == sel
        sels.append(sel)
        val_out = jnp.where(lane == k, m, val_out)
        onehot = jnp.where(hit, 1.0, onehot)
        cur = jnp.where(hit, NEG_BIG * 2.0, cur)
    vmax = jnp.max(val_out, axis=-1, keepdims=True)
    ev = jnp.where(lane < TOP_K, jnp.exp(val_out - vmax), 0.0)
    gate_ref[...] = ev / jnp.sum(ev, axis=-1, keepdims=True)

    ri = lax.broadcasted_iota(I32, (tt, tt), 0)
    ci = lax.broadcasted_iota(I32, (tt, tt), 1)
    earlier = jnp.where(ci < ri, 1.0, 0.0).astype(BF16)
    rank = jnp.dot(earlier, onehot.astype(BF16), preferred_element_type=F32)
    cnt = jnp.sum(onehot, axis=0, keepdims=True).astype(I32)
    c8 = jnp.bitwise_and(cnt + (GRANULE - 1), -GRANULE)
    ue = lax.broadcasted_iota(I32, (LANES, LANES), 0)
    ve = lax.broadcasted_iota(I32, (LANES, LANES), 1)
    c8_rows = jnp.broadcast_to(c8.astype(F32), (SUBLANES, LANES)).astype(BF16)
    before = jnp.where(ue < ve, 1.0, 0.0).astype(BF16)
    off = jnp.dot(c8_rows, before, preferred_element_type=F32)[0:1]
    cnt_ref[0] = jnp.broadcast_to(c8, (SUBLANES, LANES))

    slot_of = off + rank
    pos_out = jnp.zeros(logits.shape, I32)
    r_iota = lax.broadcasted_iota(I32, (tt, rt), 1)
    perm_t = jnp.zeros((tt, rt), F32)
    for k in range(TOP_K):
        pk = jnp.sum(jnp.where(lane == sels[k], slot_of, 0.0), axis=-1, keepdims=True).astype(I32)
        pos_out = jnp.where(lane == k, pk, pos_out)
        perm_t = jnp.where(r_iota == pk, 1.0, perm_t)
    pos_ref[...] = pos_out
    perm = perm_t.T.astype(BF16)
    xs_ref[...] = jnp.dot(perm, h_hi, preferred_element_type=F32)


def _moe_combine(ys_ref, pos_ref, gate_ref):
    tt = pos_ref.shape[0]
    rt = ys_ref.shape[0]
    r_iota = lax.broadcasted_iota(I32, (tt, rt), 1)
    g = jnp.zeros((tt, rt), F32)
    for k in range(TOP_K):
        g = jnp.where(r_iota == pos_ref[:, k:k + 1], gate_ref[:, k:k + 1], g)
    return jnp.dot(g.astype(BF16), ys_ref[...].astype(BF16), preferred_element_type=F32)


def _mod_kernel(cc_ref, w_ref, b_ref, o_ref):
    a = _silu(cc_ref[...])
    o_ref[0] = jnp.dot(a, w_ref[0], precision=lax.Precision.HIGHEST,
                       preferred_element_type=F32) + b_ref[0]


def _mod_vectors(c, c_ctx, w_mod, b_mod):
    depth, d, d6 = w_mod.shape
    cc = jnp.zeros((SUBLANES, d), F32).at[0].set(c[0]).at[1].set(c_ctx)
    ncol = 4
    cw = d6 // ncol
    return pl.pallas_call(
        _mod_kernel,
        grid=(depth, ncol),
        in_specs=[pl.BlockSpec((SUBLANES, d), lambda l, j: (0, 0)),
                  pl.BlockSpec((1, d, cw), lambda l, j: (l, 0, j)),
                  pl.BlockSpec((1, 1, cw), lambda l, j: (l, 0, j))],
        out_specs=pl.BlockSpec((1, SUBLANES, cw), lambda l, j: (l, 0, j)),
        out_shape=jax.ShapeDtypeStruct((depth, SUBLANES, d6), F32),
        compiler_params=_cparams(2),
        name="mod_vectors",
    )(cc, w_mod, b_mod.reshape(depth, 1, d6))


def _qkv_kernel(mod_row, use_rope, x_ref, mod_ref, g_ref, w_ref, b_ref, qg_ref, kg_ref, rowt_ref, colt_ref,
                bd_ref, q_ref, k_ref, v_ref, wb_ref):
    d = x_ref.shape[1]
    tile = x_ref.shape[0]
    nq = N_HEADS * HEAD_DIM
    nk = N_KV_HEADS * HEAD_DIM

    @pl.when(pl.program_id(0) == 0)
    def _():
        wb_ref[...] = w_ref[...].astype(BF16)

    sh = mod_ref[mod_row:mod_row + 1, 0:d]
    sc = mod_ref[mod_row:mod_row + 1, d:2 * d]
    h = _rms_mod(x_ref[...], g_ref[...], sh, sc)
    qkv = jnp.dot(h.astype(BF16), wb_ref[...], preferred_element_type=F32) + b_ref[...]
    q = qkv[:, :nq]
    k = qkv[:, nq:nq + nk]
    v = qkv[:, nq + nk:]
    bd = bd_ref[...]

    def head_norm(t, gain):
        tt = t * t
        w = bd.shape[0]
        if t.shape[1] >= w:
            ss = jnp.concatenate([_split_dot(tt[:, c:c + w], bd) for c in range(0, t.shape[1], w)], axis=1)
        else:
            ss = _split_dot(tt, bd[:t.shape[1], :t.shape[1]])
        return t * lax.rsqrt(ss * (1.0 / HEAD_DIM) + EPS) * gain

    qn = head_norm(q, qg_ref[...])
    kn = head_norm(k, kg_ref[...])

    if use_rope:
        lane = lax.broadcasted_iota(I32, (tile, LANES), 1)
        is_row = (lane % HEAD_DIM) < HEAD_DIM // 2
        nrow = tile // GRID_W

        def table(idx):
            rt = jnp.concatenate([jnp.broadcast_to(rowt_ref[idx, r:r + 1, :], (GRID_W, LANES))
                                  for r in range(nrow)], axis=0)
            ct = jnp.concatenate([colt_ref[idx]] * nrow, axis=0)
            return jnp.where(is_row, rt, ct)

        cosv, sav, sbv = table(0), table(1), table(2)

        def rope(t):
            n = t.shape[1]
            reps = n // LANES
            ct = jnp.concatenate([cosv] * reps, axis=1) if reps > 1 else cosv
            at = jnp.concatenate([sav] * reps, axis=1) if reps > 1 else sav
            bt = jnp.concatenate([sbv] * reps, axis=1) if reps > 1 else sbv
            up = pltpu.roll(t, n - HEAD_DIM // 4, 1)
            dn = pltpu.roll(t, HEAD_DIM // 4, 1)
            return t * ct + up * at + dn * bt

        qn = rope(qn)
        kn = rope(kn)

    q_ref[...] = (qn * (HEAD_DIM ** -0.5)).astype(BF16)
    klane = lax.broadcasted_iota(I32, kn.shape, 1)
    low = klane < HEAD_DIM

    def variants(t):
        sw = pltpu.roll(t, HEAD_DIM, 1)
        z = jnp.zeros_like(t)
        return jnp.concatenate([jnp.where(low, t, z), jnp.where(low, z, sw),
                                jnp.where(low, sw, z), jnp.where(low, z, t)], axis=1)

    k_ref[...] = variants(kn).astype(BF16)
    v_ref[...] = variants(v).astype(BF16)


def _qkv_project(xt, mod_l, mod_row, use_rope, g1n, w_qkv, b_qkv, qg, kg, rowt, colt, bd, tile):
    t, d = xt.shape
    nqkv = w_qkv.shape[1]
    nq = N_HEADS * HEAD_DIM
    tok = lambda i: (i, 0)
    nrow = max(tile // GRID_W, 1)
    return pl.pallas_call(
        functools.partial(_qkv_kernel, mod_row, use_rope),
        grid=(t // tile,),
        in_specs=[pl.BlockSpec((tile, d), tok), _full(mod_l.shape), _full((1, d)),
                  _full((d, nqkv)), _full((1, nqkv)), _full((1, nq)), _full((1, LANES)),
                  pl.BlockSpec((3, nrow, LANES), lambda i: (0, i, 0)), _full(colt.shape), _full(bd.shape)],
        out_specs=[pl.BlockSpec((tile, nq), tok), pl.BlockSpec((tile, 4 * LANES), tok),
                   pl.BlockSpec((tile, 4 * LANES), tok)],
        out_shape=[jax.ShapeDtypeStruct((t, nq), BF16), jax.ShapeDtypeStruct((t, 4 * LANES), BF16),
                   jax.ShapeDtypeStruct((t, 4 * LANES), BF16)],
        scratch_shapes=[pltpu.VMEM((d, nqkv), BF16)],
        compiler_params=_cparams(1),
        name="qkv_project",
    )(xt, mod_l, g1n, w_qkv, b_qkv, qg, kg, rowt, colt, bd)


def _attn_kernel(seq_len, sink_ref, q_ref, kp_ref, kc_ref, kn_ref, vp_ref, vc_ref, vn_ref, kx_ref, vx_ref,
                 o_ref):
    i = pl.program_id(0)
    qb = q_ref.shape[0]
    band = 3 * qb
    pairs_per_group = (N_HEADS // N_KV_HEADS) // 2
    rows = pairs_per_group * qb

    r = lax.broadcasted_iota(I32, (rows, band), 0) % qb
    cpos = lax.broadcasted_iota(I32, (rows, band), 1) - qb
    kabs = i * qb + cpos
    valid = (jnp.abs(r - cpos) <= WINDOW) & (kabs >= 0) & (kabs < seq_len)

    kband = jnp.concatenate([kp_ref[...], kc_ref[...], kn_ref[...]], axis=0)
    vband = jnp.concatenate([vp_ref[...], vc_ref[...], vn_ref[...]], axis=0)
    kctx = kx_ref[...]
    vctx = vx_ref[...]
    dn_t = (((1,), (1,)), ((), ()))
    rsub = lax.broadcasted_iota(I32, (rows, 1), 0) // qb

    for g in range(N_KV_HEADS):
        qg = jnp.concatenate(
            [q_ref[:, (g * pairs_per_group + j) * LANES:(g * pairs_per_group + j + 1) * LANES]
             for j in range(pairs_per_group)], axis=0)
        acc = jnp.zeros((rows, LANES), F32)
        for par in range(2):
            col = (2 * g + par) * LANES
            s_b = lax.dot_general(qg, kband[:, col:col + LANES], dn_t, preferred_element_type=F32)
            s_c = lax.dot_general(qg, kctx[:, col:col + LANES], dn_t, preferred_element_type=F32)
            s_b = jnp.where(valid, s_b, jnp.finfo(F32).min)
            sink = jnp.zeros((rows, 1), F32)
            for j in range(pairs_per_group):
                hd = 2 * (g * pairs_per_group + j) + par
                sink = jnp.where(rsub == j, sink_ref[hd], sink)
            m = jnp.maximum(jnp.maximum(jnp.max(s_b, axis=-1, keepdims=True),
                                        jnp.max(s_c, axis=-1, keepdims=True)), sink)
            p_b = jnp.exp(s_b - m)
            p_c = jnp.exp(s_c - m)
            l = (jnp.sum(p_b, axis=-1, keepdims=True) + jnp.sum(p_c, axis=-1, keepdims=True)
                 + jnp.exp(sink - m))
            o = (jnp.dot(p_b.astype(BF16), vband[:, col:col + LANES], preferred_element_type=F32)
                 + jnp.dot(p_c.astype(BF16), vctx[:, col:col + LANES], preferred_element_type=F32))
            acc = acc + o / l
        for j in range(pairs_per_group):
            pcol = (g * pairs_per_group + j) * LANES
            o_ref[:, pcol:pcol + LANES] = acc[j * qb:(j + 1) * qb].astype(BF16)


def _attention(q, kk, vv, kkc, vvc, sinks):
    t, nq = q.shape
    qb = Q_BLOCK
    nb = t // qb
    c = kkc.shape[0]
    w = kk.shape[1]
    cur = lambda i, s: (i, 0)
    prv = lambda i, s: (jnp.maximum(i - 1, 0), 0)
    nxt = lambda i, s: (jnp.minimum(i + 1, nb - 1), 0)
    zero = lambda i, s: (0, 0)
    gs = pltpu.PrefetchScalarGridSpec(
        num_scalar_prefetch=1,
        grid=(nb,),
        in_specs=[pl.BlockSpec((qb, nq), cur),
                  pl.BlockSpec((qb, w), prv), pl.BlockSpec((qb, w), cur), pl.BlockSpec((qb, w), nxt),
                  pl.BlockSpec((qb, w), prv), pl.BlockSpec((qb, w), cur), pl.BlockSpec((qb, w), nxt),
                  pl.BlockSpec((c, w), zero), pl.BlockSpec((c, w), zero)],
        out_specs=pl.BlockSpec((qb, nq), cur),
    )
    return pl.pallas_call(
        functools.partial(_attn_kernel, t),
        grid_spec=gs,
        out_shape=jax.ShapeDtypeStruct((t, nq), BF16),
        compiler_params=_cparams(1),
        name="window_attention",
    )(sinks, q, kk, kk, kk, vv, vv, vv, kkc, vvc)


def _dispatch_out(t, d, tile):
    nt = t // tile
    tok = lambda i: (i, 0)
    specs = [pl.BlockSpec((tile, d), tok), pl.BlockSpec((MOE_REGION, d), tok),
             pl.BlockSpec((tile, LANES), tok), pl.BlockSpec((tile, LANES), tok),
             pl.BlockSpec((1, SUBLANES, LANES), lambda i: (i, 0, 0))]
    shapes = [jax.ShapeDtypeStruct((t, d), F32), jax.ShapeDtypeStruct((nt * MOE_REGION, d), F32),
              jax.ShapeDtypeStruct((t, LANES), I32), jax.ShapeDtypeStruct((t, LANES), F32),
              jax.ShapeDtypeStruct((nt, SUBLANES, LANES), I32)]
    return specs, shapes


def _oproj_kernel(x_ref, o_ref, mod_ref, wo_ref, bo_ref, g2_ref, wr_ref, br_ref,
                  x1_ref, xs_ref, pos_ref, gate_ref, cnt_ref, wb_ref):
    d = x_ref.shape[1]

    @pl.when(pl.program_id(0) == 0)
    def _():
        wb_ref[...] = wo_ref[...].astype(BF16)

    y = jnp.dot(o_ref[...], wb_ref[...], preferred_element_type=F32) + bo_ref[...]
    x1 = x_ref[...] + mod_ref[0:1, 2 * d:3 * d] * y
    x1_ref[...] = x1
    _router_dispatch(x1, g2_ref[...], mod_ref[0:1, 3 * d:4 * d], mod_ref[0:1, 4 * d:5 * d],
                     wr_ref[...], br_ref[...], xs_ref, pos_ref, gate_ref, cnt_ref)


def _oproj_router(xt, o, mod_l, w_o, b_o, g2n, wr, br):
    t, d = xt.shape
    tile = MOE_TILE
    tok = lambda i: (i, 0)
    out_specs, out_shape = _dispatch_out(t, d, tile)
    return pl.pallas_call(
        _oproj_kernel,
        grid=(t // tile,),
        in_specs=[pl.BlockSpec((tile, d), tok), pl.BlockSpec((tile, o.shape[1]), tok), _full(mod_l.shape),
                  _full(w_o.shape), _full((1, d)), _full((1, d)), _full(wr.shape), _full(br.shape)],
        out_specs=out_specs,
        out_shape=out_shape,
        scratch_shapes=[pltpu.VMEM(w_o.shape, BF16)],
        compiler_params=_cparams(1),
        name="oproj_router",
    )(xt, o, mod_l, w_o, b_o, g2n, wr, br)


def _moe_kernel(bexp_ref, nvg_ref, gsrc_ref, xs_hbm, w1_ref, b1_ref, w2_ref, b2_ref, ys_hbm,
                xbuf, obuf, w1b, w2b, gsem, ssem):
    b = pl.program_id(0)
    nb = pl.num_programs(0)
    gpb = GRANULES_PER_BLOCK
    dff = w2_ref.shape[1]
    slot = b % 2

    def granule_rows(blk, gi):
        return pl.ds(pl.multiple_of(gsrc_ref[blk * gpb + gi] * GRANULE, GRANULE), GRANULE)

    def local_rows(gi):
        return pl.ds(pl.multiple_of(gi * GRANULE, GRANULE), GRANULE)

    def gather_copy(blk, s, gi):
        return pltpu.make_async_copy(xs_hbm.at[granule_rows(blk, gi)], xbuf.at[s, local_rows(gi)], gsem.at[s])

    def scatter_copy(blk, s, gi):
        return pltpu.make_async_copy(obuf.at[s, local_rows(gi)], ys_hbm.at[granule_rows(blk, gi)], ssem.at[s])

    grp = DMA_GROUP
    log_grp = grp.bit_length() - 1

    def start_granules(n, copy_fn):
        groups = n >> log_grp

        def group(it, carry):
            for u in range(grp):
                copy_fn(it * grp + u).start()
            return carry

        def single(gi, carry):
            copy_fn(gi).start()
            return carry

        lax.fori_loop(0, groups, group, 0)
        lax.fori_loop(groups * grp, n, single, 0)

    def wait_granules(n, group_copy, one_copy):
        groups = n >> log_grp

        def group(it, carry):
            group_copy.wait()
            return carry

        def single(gi, carry):
            one_copy.wait()
            return carry

        lax.fori_loop(0, groups, group, 0)
        lax.fori_loop(groups * grp, n, single, 0)

    def wait_gather(s, n):
        rows = pl.ds(0, grp * GRANULE)
        wait_granules(n, pltpu.make_async_copy(xs_hbm.at[rows], xbuf.at[s, rows], gsem.at[s]),
                      gather_copy(0, s, 0))

    def wait_scatter(s, n):
        rows = pl.ds(0, grp * GRANULE)
        wait_granules(n, pltpu.make_async_copy(obuf.at[s, rows], ys_hbm.at[rows], ssem.at[s]),
                      scatter_copy(0, s, 0))

    nv = nvg_ref[b]

    @pl.when(b == 0)
    def _():
        xbuf[...] = jnp.zeros(xbuf.shape, F32)
        start_granules(nv, lambda gi: gather_copy(0, 0, gi))

    @pl.when(b + 1 < nb)
    def _():
        nxt = jnp.minimum(b + 1, nb - 1)
        start_granules(nvg_ref[nxt], lambda gi: gather_copy(nxt, 1 - slot, gi))

    @pl.when(b >= 2)
    def _():
        wait_scatter(slot, nvg_ref[jnp.maximum(b - 2, 0)])

    @pl.when(nv > 0)
    def _():
        changed = jnp.logical_or(b == 0, bexp_ref[b] != bexp_ref[jnp.maximum(b - 1, 0)])

        @pl.when(changed)
        def _():
            w1b[...] = w1_ref[0].astype(BF16)
            w2b[...] = w2_ref[0].astype(BF16)

        wait_gather(slot, nv)

        def expert_mlp(rows):
            xb = xbuf[slot, 0:rows, :].astype(BF16)
            gu = jnp.dot(xb, w1b[...], preferred_element_type=F32) + b1_ref[0]
            gt = jnp.minimum(gu[:, :dff], SWIGLU_LIMIT)
            up = jnp.clip(gu[:, dff:], -SWIGLU_LIMIT, SWIGLU_LIMIT)
            act = gt * jax.nn.sigmoid(SWIGLU_ALPHA * gt) * (up + 1.0)
            obuf[slot, 0:rows, :] = jnp.dot(act.astype(BF16), w2b[...], preferred_element_type=F32) + b2_ref[0]

        half = EXPERT_ROWS // 2

        @pl.when(nv * GRANULE > half)
        def _():
            expert_mlp(EXPERT_ROWS)

        @pl.when(nv * GRANULE <= half)
        def _():
            expert_mlp(half)

        start_granules(nv, lambda gi: scatter_copy(b, slot, gi))

    @pl.when(b == nb - 1)
    def _():
        wait_scatter(slot, nv)

        @pl.when(nb >= 2)
        def _():
            wait_scatter(1 - slot, nvg_ref[jnp.maximum(b - 1, 0)])


def _moe_experts(xs, block_exp, n_granules, granule_src, layer, w1, b1, w2, b2):
    d = xs.shape[1]
    n_blocks = block_exp.shape[0]
    dff = w2.shape[2]
    n_layers, n_exp = w1.shape[:2]
    by_expert = lambda b, be, ng, gs: (layer, be[b], 0, 0)
    gs = pltpu.PrefetchScalarGridSpec(
        num_scalar_prefetch=3,
        grid=(n_blocks,),
        in_specs=[pl.BlockSpec(memory_space=pl.ANY),
                  pl.BlockSpec((None, 1, d, 2 * dff), by_expert), pl.BlockSpec((None, 1, 1, 2 * dff), by_expert),
                  pl.BlockSpec((None, 1, dff, d), by_expert), pl.BlockSpec((None, 1, 1, d), by_expert)],
        out_specs=pl.BlockSpec(memory_space=pl.ANY),
        scratch_shapes=[pltpu.VMEM((2, EXPERT_ROWS, d), F32), pltpu.VMEM((2, EXPERT_ROWS, d), F32),
                        pltpu.VMEM((d, 2 * dff), BF16), pltpu.VMEM((dff, d), BF16),
                        pltpu.SemaphoreType.DMA((2,)), pltpu.SemaphoreType.DMA((2,))],
    )
    return pl.pallas_call(
        _moe_kernel,
        grid_spec=gs,
        out_shape=jax.ShapeDtypeStruct(xs.shape, F32),
        input_output_aliases={3: 0},
        compiler_params=_cparams(1),
        name="moe_experts",
    )(block_exp, n_granules, granule_src, xs, w1, b1.reshape(n_layers, n_exp, 1, 2 * dff),
      w2, b2.reshape(n_layers, n_exp, 1, d))


def _tables_kernel(cnt_ref, bexp_ref, ngran_ref, gsrc_ref, off_ref):
    nt = cnt_ref.shape[0]
    nbk = bexp_ref.shape[0]
    gpb = GRANULES_PER_BLOCK
    log_g = GRANULE.bit_length() - 1
    log_b = gpb.bit_length() - 1

    def loop(lo, hi, fn, init=0):
        return lax.fori_loop(lo, hi, fn, init)

    def init_block(i, c):
        bexp_ref[i] = N_EXPERTS - 1
        ngran_ref[i] = 0
        return c

    def init_granule(i, c):
        gsrc_ref[i] = 0
        return c

    def init_off(j, c):
        off_ref[j] = 0
        return c

    loop(0, nbk, init_block)
    loop(0, nbk * gpb, init_granule)
    loop(0, nt, init_off)

    def expert_body(e, blk0):
        q0 = blk0 * gpb

        def tile_body(j, q):
            c = cnt_ref[j, e]
            o = off_ref[j]
            src0 = (j * MOE_REGION + o) >> log_g
            ng = c >> log_g

            def granule_body(i, cc):
                gsrc_ref[q + i] = src0 + i
                return cc

            loop(0, ng, granule_body)
            off_ref[j] = o + c
            return q + ng

        n_g = loop(0, nt, tile_body, q0) - q0
        nblk = (n_g + gpb - 1) >> log_b

        def block_body(m, cc):
            bexp_ref[blk0 + m] = e
            ngran_ref[blk0 + m] = jnp.minimum(gpb, n_g - m * gpb)
            return cc

        loop(0, nblk, block_body)
        return blk0 + nblk

    loop(0, N_EXPERTS, expert_body, 0)


def _block_tables(cnt):
    nt = cnt.shape[0]
    n_blocks = (nt * (MOE_TILE * TOP_K + N_EXPERTS * (GRANULE - 1))) // EXPERT_ROWS + N_EXPERTS
    smem = pl.BlockSpec(memory_space=pltpu.SMEM)
    return pl.pallas_call(
        _tables_kernel,
        in_specs=[smem],
        out_specs=[smem, smem, smem],
        out_shape=[jax.ShapeDtypeStruct((n_blocks,), I32), jax.ShapeDtypeStruct((n_blocks,), I32),
                   jax.ShapeDtypeStruct((n_blocks * GRANULES_PER_BLOCK,), I32)],
        scratch_shapes=[pltpu.SMEM((nt,), I32)],
        name="moe_block_tables",
    )(cnt[:, 0, :])


def _combine_kernel(x_ref, ys_ref, pos_ref, gate_ref, mod_ref, o_ref):
    d = x_ref.shape[1]
    o_ref[...] = x_ref[...] + mod_ref[0:1, 5 * d:6 * d] * _moe_combine(ys_ref, pos_ref, gate_ref)


def _combine_glu_kernel(x_ref, ys_ref, pos_ref, gate_ref, modp_ref, modn_ref, g_ref, w_ref, b_ref,
                        o_ref, u_ref, wb_ref):
    d = x_ref.shape[1]

    @pl.when(pl.program_id(0) == 0)
    def _():
        wb_ref[...] = w_ref[...].astype(BF16)

    x2 = x_ref[...] + modp_ref[0:1, 5 * d:6 * d] * _moe_combine(ys_ref, pos_ref, gate_ref)
    o_ref[...] = x2
    h = _rms_mod(x2, g_ref[...], modn_ref[0:1, 0:d], modn_ref[0:1, d:2 * d])
    u = jnp.dot(h.astype(BF16), wb_ref[...], preferred_element_type=F32) + b_ref[...]
    u_ref[...] = u[:, :d] * jax.nn.sigmoid(u[:, d:])


def _combine_in_specs(d, tile):
    tok = lambda i: (i, 0)
    return [pl.BlockSpec((tile, d), tok), pl.BlockSpec((MOE_REGION, d), tok),
            pl.BlockSpec((tile, LANES), tok), pl.BlockSpec((tile, LANES), tok)]


def _combine(x1, ys, pos, gates, mod_l):
    t, d = x1.shape
    tile = MOE_TILE
    return pl.pallas_call(
        _combine_kernel,
        grid=(t // tile,),
        in_specs=_combine_in_specs(d, tile) + [_full(mod_l.shape)],
        out_specs=pl.BlockSpec((tile, d), lambda i: (i, 0)),
        out_shape=jax.ShapeDtypeStruct((t, d), F32),
        compiler_params=_cparams(1),
        name="moe_combine",
    )(x1, ys, pos, gates, mod_l)


def _combine_glu(x1, ys, pos, gates, mod_prev, mod_next, g1n, w_pw1, b_pw1):
    t, d = x1.shape
    tile = MOE_TILE
    tok = lambda i: (i, 0)
    return pl.pallas_call(
        _combine_glu_kernel,
        grid=(t // tile,),
        in_specs=_combine_in_specs(d, tile)
                 + [_full(mod_prev.shape), _full(mod_next.shape), _full((1, d)),
                    _full(w_pw1.shape), _full((1, 2 * d))],
        out_specs=[pl.BlockSpec((tile, d), tok), pl.BlockSpec((tile, d), tok)],
        out_shape=[jax.ShapeDtypeStruct((t, d), F32), jax.ShapeDtypeStruct((t, d), F32)],
        scratch_shapes=[pltpu.VMEM(w_pw1.shape, BF16)],
        compiler_params=_cparams(1),
        name="combine_pw1_glu",
    )(x1, ys, pos, gates, mod_prev, mod_next, g1n, w_pw1, b_pw1)


def _conv_kernel(x_ref, up_ref, uc_ref, un_ref, mod_ref, wdw_ref, bdw_ref, lg_ref, lb_ref, w2_ref, b2_ref,
                 g2_ref, wr_ref, br_ref, x1_ref, xs_ref, pos_ref, gate_ref, cnt_ref, wb_ref, ubuf, shbuf, cbuf):
    i = pl.program_id(0)
    n = pl.num_programs(0)
    d = x_ref.shape[1]
    tile = x_ref.shape[0]
    halo = up_ref.shape[0]

    @pl.when(i == 0)
    def _():
        wb_ref[...] = w2_ref[...].astype(BF16)

    ubuf[0:halo, :] = jnp.where(i > 0, up_ref[...], 0.0)
    ubuf[halo:halo + tile, :] = uc_ref[...]
    ubuf[halo + tile:, :] = jnp.where(i < n - 1, un_ref[...], 0.0)

    span = tile + 2 * halo - SUBLANES
    for r in range(SUBLANES):
        shbuf[r, 0:span, :] = ubuf[r:r + span, :]
    for c0 in range(0, tile, CONV_CHUNK):
        part = jnp.zeros((CONV_CHUNK, d), F32) + bdw_ref[...]
        for j in range(CONV_WIDTH):
            a, r = divmod(halo + j - CONV_PAD, SUBLANES)
            lo = c0 + a * SUBLANES
            part = part + shbuf[r, lo:lo + CONV_CHUNK, :] * wdw_ref[j:j + 1, :]
        cbuf[c0:c0 + CONV_CHUNK, :] = part
    acc = cbuf[...]

    mu = jnp.mean(acc, axis=-1, keepdims=True)
    cen = acc - mu
    var = jnp.mean(cen * cen, axis=-1, keepdims=True)
    z = _silu(cen * lax.rsqrt(var + EPS) * lg_ref[...] + lb_ref[...])
    y = jnp.dot(z.astype(BF16), wb_ref[...], preferred_element_type=F32) + b2_ref[...]
    x1 = x_ref[...] + mod_ref[0:1, 2 * d:3 * d] * y
    x1_ref[...] = x1
    _router_dispatch(x1, g2_ref[...], mod_ref[0:1, 3 * d:4 * d], mod_ref[0:1, 4 * d:5 * d],
                     wr_ref[...], br_ref[...], xs_ref, pos_ref, gate_ref, cnt_ref)


def _conv_router(x2, u, mod_l, w_dw, b_dw, ln_g, ln_b, w_pw2, b_pw2, g2n, wr, br):
    t, d = x2.shape
    tile = MOE_TILE
    nt = t // tile
    hb = tile // CONV_HALO
    nh = t // CONV_HALO
    tok = lambda i: (i, 0)
    prv = lambda i: (jnp.maximum(i * hb - 1, 0), 0)
    nxt = lambda i: (jnp.minimum((i + 1) * hb, nh - 1), 0)
    wpad = jnp.zeros((32, d), F32).at[:CONV_WIDTH].set(w_dw)
    out_specs, out_shape = _dispatch_out(t, d, tile)
    return pl.pallas_call(
        _conv_kernel,
        grid=(nt,),
        in_specs=[pl.BlockSpec((tile, d), tok), pl.BlockSpec((CONV_HALO, d), prv),
                  pl.BlockSpec((tile, d), tok), pl.BlockSpec((CONV_HALO, d), nxt),
                  _full(mod_l.shape), _full((32, d)), _full((1, d)), _full((1, d)), _full((1, d)),
                  _full(w_pw2.shape), _full((1, d)), _full((1, d)), _full(wr.shape), _full(br.shape)],
        out_specs=out_specs,
        out_shape=out_shape,
        scratch_shapes=[pltpu.VMEM(w_pw2.shape, BF16), pltpu.VMEM((tile + 2 * CONV_HALO, d), F32),
                        pltpu.VMEM((SUBLANES, tile + 2 * CONV_HALO, d), F32), pltpu.VMEM((tile, d), F32)],
        compiler_params=_cparams(1),
        name="conv_router",
    )(x2, u, u, u, mod_l, wpad, b_dw, ln_g, ln_b, w_pw2, b_pw2, g2n, wr, br)


def _rope_tables(n_rows):
    quarter = HEAD_DIM // 4
    inv = ROPE_BASE ** (-jnp.arange(quarter, dtype=F32) / quarter)

    def tables(npos):
        ang = jnp.arange(npos, dtype=I32).astype(F32)[:, None] * inv[None, :]
        z = jnp.zeros_like(ang)
        c, s = jnp.cos(ang), jnp.sin(ang)
        half = lambda a, b: jnp.concatenate([a, b], axis=1)
        full = lambda h: jnp.concatenate([h, h, h, h], axis=1)
        return jnp.stack([full(half(c, c)), full(half(-s, z)), full(half(z, s))])

    return tables(n_rows), tables(GRID_W)


def _router_params(w_router, b_router):
    d, e = w_router.shape
    wr = jnp.zeros((d, LANES), F32).at[:, :e].set(w_router)
    br = jnp.full((1, LANES), NEG_BIG, F32).at[0, :e].set(b_router)
    hi = wr.astype(BF16)
    lo = (wr - hi.astype(F32)).astype(BF16)
    return jnp.stack([hi, lo]), br


def kernel(x, c, ctx, c_ctx, w_mod, b_mod, norm1_g, norm2_g, attn_w_qkv, attn_b_qkv, attn_w_o, attn_b_o,
           attn_q_norm, attn_k_norm, attn_sinks, conv_w_pw1, conv_b_pw1, conv_w_dw, conv_b_dw, conv_ln_g,
           conv_ln_b, conv_w_pw2, conv_b_pw2, moe_w_router, moe_b_router, moe_w1, moe_b1, moe_w2, moe_b2):
    bsz, t, d = x.shape
    assert bsz == 1 and w_mod.shape[0] == 2
    n_ctx = ctx.shape[1]
    tile = min(TOKEN_TILE, t)
    assert t % tile == 0 and t % Q_BLOCK == 0 and t % MOE_TILE == 0 and tile % GRID_W == 0
    xt = x.reshape(t, d)
    row = lambda v: v.reshape(1, -1)

    mod = _mod_vectors(c, c_ctx, w_mod, b_mod)
    mod0, mod1 = mod[0], mod[1]

    bd = (jnp.arange(2 * LANES)[:, None] // HEAD_DIM == jnp.arange(2 * LANES)[None, :] // HEAD_DIM).astype(BF16)
    qg = row(jnp.tile(attn_q_norm[0], N_HEADS))
    kg = row(jnp.tile(attn_k_norm[0], N_KV_HEADS))
    rowt, colt = _rope_tables(t // GRID_W)
    g1n = row(norm1_g[0])
    bq = row(attn_b_qkv[0])
    q, kk, vv = _qkv_project(xt, mod0, 0, True, g1n, attn_w_qkv[0], bq, qg, kg, rowt, colt, bd, tile)
    _, kkc, vvc = _qkv_project(ctx.reshape(n_ctx, d), mod0, 1, False, g1n, attn_w_qkv[0], bq, qg, kg,
                               rowt[:, :max(n_ctx // GRID_W, 1)], colt, bd, n_ctx)
    o = _attention(q, kk, vv, kkc, vvc, attn_sinks[0])
    wr0, br0 = _router_params(moe_w_router[0], moe_b_router[0])
    x1, xs, pos, gates, cnt = _oproj_router(xt, o, mod0, attn_w_o[0], row(attn_b_o[0]), row(norm2_g[0]),
                                            wr0, br0)
    ys = _moe_experts(xs, *_block_tables(cnt), 0, moe_w1, moe_b1, moe_w2, moe_b2)

    x2, u = _combine_glu(x1, ys, pos, gates, mod0, mod1, row(norm1_g[1]), conv_w_pw1[0], row(conv_b_pw1[0]))
    wr1, br1 = _router_params(moe_w_router[1], moe_b_router[1])
    x3, xs, pos, gates, cnt = _conv_router(x2, u, mod1, conv_w_dw[0], row(conv_b_dw[0]), row(conv_ln_g[0]),
                                           row(conv_ln_b[0]), conv_w_pw2[0], row(conv_b_pw2[0]),
                                           row(norm2_g[1]), wr1, br1)
    ys = _moe_experts(xs, *_block_tables(cnt), 1, moe_w1, moe_b1, moe_w2, moe_b2)
    out = _combine(x3, ys, pos, gates, mod1)
    return out.reshape(1, t, d)
```

```python
import functools

import jax
import jax.numpy as jnp
from jax import lax
from jax.experimental import pallas as pl
from jax.experimental.pallas import tpu as pltpu

F32 = jnp.float32
BF16 = jnp.bfloat16
I32 = jnp.int32

N_HEADS = 16
N_KV_HEADS = 2
HEAD_DIM = 64
GRID_W = 64
WINDOW = 128
Q_BLOCK = 128
ROPE_BASE = 10000.0
CONV_WIDTH = 31
CONV_PAD = CONV_WIDTH // 2
N_EXPERTS = 32
TOP_K = 4
SWIGLU_LIMIT = 7.0
SWIGLU_ALPHA = 1.702
EPS = 1e-6

LANES = 128
SUBLANES = 8
TOKEN_TILE = 512
MOE_TILE = 256
GRANULE = SUBLANES
MOE_REGION = -(-(MOE_TILE * TOP_K + N_EXPERTS * (GRANULE - 1)) // 256) * 256
EXPERT_ROWS = 512
GRANULES_PER_BLOCK = EXPERT_ROWS // GRANULE
DMA_GROUP = 4
CONV_HALO = 16
CONV_CHUNK = 32
NEG_BIG = -1e30
VMEM_LIMIT = 56 * 1024 * 1024


def _cparams(n_axes=1, vmem=VMEM_LIMIT):
    return pltpu.CompilerParams(dimension_semantics=("arbitrary",) * n_axes, vmem_limit_bytes=vmem)


def _full(shape):
    nd = len(shape)
    return pl.BlockSpec(shape, lambda *_: (0,) * nd)


def _silu(v):
    return v * jax.nn.sigmoid(v)


def _rms_mod(xv, g, shift, scale):
    ms = jnp.mean(xv * xv, axis=-1, keepdims=True)
    return (xv * lax.rsqrt(ms + EPS)) * g * (1.0 + scale) + shift


def _split_dot(a, b_bf16):
    hi = a.astype(BF16)
    lo = (a - hi.astype(F32)).astype(BF16)
    return (jnp.dot(hi, b_bf16, preferred_element_type=F32)
            + jnp.dot(lo, b_bf16, preferred_element_type=F32))


def _router_dispatch(x_new, g2n, sh2, sc2, wr, br, xs_ref, pos_ref, gate_ref, cnt_ref):
    tt = x_new.shape[0]
    rt = xs_ref.shape[0]
    h2 = _rms_mod(x_new, g2n, sh2, sc2)
    h_hi = h2.astype(BF16)
    h_lo = (h2 - h_hi.astype(F32)).astype(BF16)
    logits = (jnp.dot(h_hi, wr[0], preferred_element_type=F32)
              + (jnp.dot(h_lo, wr[0], preferred_element_type=F32)
                 + jnp.dot(h_hi, wr[1], preferred_element_type=F32))) + br
    lane = lax.broadcasted_iota(I32, logits.shape, 1)
    val_out = jnp.full(logits.shape, NEG_BIG, F32)
    onehot = jnp.zeros(logits.shape, F32)
    sels = []
    cur = logits
    for k in range(TOP_K):
        m = jnp.max(cur, axis=-1, keepdims=True)
        sel = jnp.min(jnp.where(cur == m, lane, LANES), axis=-1, keepdims=True)
        hit = lane == sel
        sels.append(sel)
        val_out = jnp.where(lane == k, m, val_out)
        onehot = jnp.where(hit, 1.0, onehot)
        cur = jnp.where(hit, NEG_BIG * 2.0, cur)
    vmax = jnp.max(val_out, axis=-1, keepdims=True)
    ev = jnp.where(lane < TOP_K, jnp.exp(val_out - vmax), 0.0)
    gate_ref[...] = ev / jnp.sum(ev, axis=-1, keepdims=True)

    ri = lax.broadcasted_iota(I32, (tt, tt), 0)
    ci = lax.broadcasted_iota(I32, (tt, tt), 1)
    earlier = jnp.where(ci < ri, 1.0, 0.0).astype(BF16)
    rank = jnp.dot(earlier, onehot.astype(BF16), preferred_element_type=F32)
    cnt = jnp.sum(onehot, axis=0, keepdims=True).astype(I32)
    c8 = jnp.bitwise_and(cnt + (GRANULE - 1), -GRANULE)
    ue = lax.broadcasted_iota(I32, (LANES, LANES), 0)
    ve = lax.broadcasted_iota(I32, (LANES, LANES), 1)
    c8_rows = jnp.broadcast_to(c8.astype(F32), (SUBLANES, LANES)).astype(BF16)
    before = jnp.where(ue < ve, 1.0, 0.0).astype(BF16)
    off = jnp.dot(c8_rows, before, preferred_element_type=F32)[0:1]
    cnt_ref[0] = jnp.broadcast_to(c8, (SUBLANES, LANES))

    slot_of = off + rank
    pos_out = jnp.zeros(logits.shape, I32)
    r_iota = lax.broadcasted_iota(I32, (tt, rt), 1)
    perm_t = jnp.zeros((tt, rt), F32)
    for k in range(TOP_K):
        pk = jnp.sum(jnp.where(lane == sels[k], slot_of, 0.0), axis=-1, keepdims=True).astype(I32)
        pos_out = jnp.where(lane == k, pk, pos_out)
        perm_t = jnp.where(r_iota == pk, 1.0, perm_t)
    pos_ref[...] = pos_out
    perm = perm_t.T.astype(BF16)
    xs_ref[...] = jnp.dot(perm, h_hi, preferred_element_type=F32)


def _moe_combine(ys_ref, pos_ref, gate_ref):
    tt = pos_ref.shape[0]
    rt = ys_ref.shape[0]
    r_iota = lax.broadcasted_iota(I32, (tt, rt), 1)
    g = jnp.zeros((tt, rt), F32)
    for k in range(TOP_K):
        g = jnp.where(r_iota == pos_ref[:, k:k + 1], gate_ref[:, k:k + 1], g)
    return jnp.dot(g.astype(BF16), ys_ref[...].astype(BF16), preferred_element_type=F32)


def _mod_kernel(cc_ref, w_ref, b_ref, o_ref):
    a = _silu(cc_ref[...])
    o_ref[0] = jnp.dot(a, w_ref[0], precision=lax.Precision.HIGHEST,
                       preferred_element_type=F32) + b_ref[0]


def _mod_vectors(c, c_ctx, w_mod, b_mod):
    depth, d, d6 = w_mod.shape
    cc = jnp.zeros((SUBLANES, d), F32).at[0].set(c[0]).at[1].set(c_ctx)
    ncol = 4
    cw = d6 // ncol
    return pl.pallas_call(
        _mod_kernel,
        grid=(depth, ncol),
        in_specs=[pl.BlockSpec((SUBLANES, d), lambda l, j: (0, 0)),
                  pl.BlockSpec((1, d, cw), lambda l, j: (l, 0, j)),
                  pl.BlockSpec((1, 1, cw), lambda l, j: (l, 0, j))],
        out_specs=pl.BlockSpec((1, SUBLANES, cw), lambda l, j: (l, 0, j)),
        out_shape=jax.ShapeDtypeStruct((depth, SUBLANES, d6), F32),
        compiler_params=_cparams(2),
        name="mod_vectors",
    )(cc, w_mod, b_mod.reshape(depth, 1, d6))


def _qkv_kernel(mod_row, use_rope, x_ref, mod_ref, g_ref, w_ref, b_ref, qg_ref, kg_ref, rowt_ref, colt_ref,
                bd_ref, q_ref, k_ref, v_ref, wb_ref):
    d = x_ref.shape[1]
    tile = x_ref.shape[0]
    nq = N_HEADS * HEAD_DIM
    nk = N_KV_HEADS * HEAD_DIM

    @pl.when(pl.program_id(0) == 0)
    def _():
        wb_ref[...] = w_ref[...].astype(BF16)

    sh = mod_ref[mod_row:mod_row + 1, 0:d]
    sc = mod_ref[mod_row:mod_row + 1, d:2 * d]
    h = _rms_mod(x_ref[...], g_ref[...], sh, sc)
    qkv = jnp.dot(h.astype(BF16), wb_ref[...], preferred_element_type=F32) + b_ref[...]
    q = qkv[:, :nq]
    k = qkv[:, nq:nq + nk]
    v = qkv[:, nq + nk:]
    bd = bd_ref[...]

    def head_norm(t, gain):
        tt = t * t
        w = bd.shape[0]
        if t.shape[1] >= w:
            ss = jnp.concatenate([_split_dot(tt[:, c:c + w], bd) for c in range(0, t.shape[1], w)], axis=1)
        else:
            ss = _split_dot(tt, bd[:t.shape[1], :t.shape[1]])
        return t * lax.rsqrt(ss * (1.0 / HEAD_DIM) + EPS) * gain

    qn = head_norm(q, qg_ref[...])
    kn = head_norm(k, kg_ref[...])

    if use_rope:
        lane = lax.broadcasted_iota(I32, (tile, LANES), 1)
        is_row = (lane % HEAD_DIM) < HEAD_DIM // 2
        nrow = tile // GRID_W

        def table(idx):
            rt = jnp.concatenate([jnp.broadcast_to(rowt_ref[idx, r:r + 1, :], (GRID_W, LANES))
                                  for r in range(nrow)], axis=0)
            ct = jnp.concatenate([colt_ref[idx]] * nrow, axis=0)
            return jnp.where(is_row, rt, ct)

        cosv, sav, sbv = table(0), table(1), table(2)

        def rope(t):
            n = t.shape[1]
            reps = n // LANES
            ct = jnp.concatenate([cosv] * reps, axis=1) if reps > 1 else cosv
            at = jnp.concatenate([sav] * reps, axis=1) if reps > 1 else sav
            bt = jnp.concatenate([sbv] * reps, axis=1) if reps > 1 else sbv
            up = pltpu.roll(t, n - HEAD_DIM // 4, 1)
            dn = pltpu.roll(t, HEAD_DIM // 4, 1)
            return t * ct + up * at + dn * bt

        qn = rope(qn)
        kn = rope(kn)

    q_ref[...] = (qn * (HEAD_DIM ** -0.5)).astype(BF16)
    klane = lax.broadcasted_iota(I32, kn.shape, 1)
    low = klane < HEAD_DIM

    def variants(t):
        sw = pltpu.roll(t, HEAD_DIM, 1)
        z = jnp.zeros_like(t)
        return jnp.concatenate([jnp.where(low, t, z), jnp.where(low, z, sw),
                                jnp.where(low, sw, z), jnp.where(low, z, t)], axis=1)

    k_ref[...] = variants(kn).astype(BF16)
    v_ref[...] = variants(v).astype(BF16)


def _qkv_project(xt, mod_l, mod_row, use_rope, g1n, w_qkv, b_qkv, qg, kg, rowt, colt, bd, tile):
    t, d = xt.shape
    nqkv = w_qkv.shape[1]
    nq = N_HEADS * HEAD_DIM
    tok = lambda i: (i, 0)
    nrow = max(tile // GRID_W, 1)
    return pl.pallas_call(
        functools.partial(_qkv_kernel, mod_row, use_rope),
        grid=(t // tile,),
        in_specs=[pl.BlockSpec((tile, d), tok), _full(mod_l.shape), _full((1, d)),
                  _full((d, nqkv)), _full((1, nqkv)), _full((1, nq)), _full((1, LANES)),
                  pl.BlockSpec((3, nrow, LANES), lambda i: (0, i, 0)), _full(colt.shape), _full(bd.shape)],
        out_specs=[pl.BlockSpec((tile, nq), tok), pl.BlockSpec((tile, 4 * LANES), tok),
                   pl.BlockSpec((tile, 4 * LANES), tok)],
        out_shape=[jax.ShapeDtypeStruct((t, nq), BF16), jax.ShapeDtypeStruct((t, 4 * LANES), BF16),
                   jax.ShapeDtypeStruct((t, 4 * LANES), BF16)],
        scratch_shapes=[pltpu.VMEM((d, nqkv), BF16)],
        compiler_params=_cparams(1),
        name="qkv_project",
    )(xt, mod_l, g1n, w_qkv, b_qkv, qg, kg, rowt, colt, bd)


def _attn_kernel(sink_ref, q_ref, kp_ref, kc_ref, kn_ref, vp_ref, vc_ref, vn_ref, kx_ref, vx_ref, bias_ref,
                 o_ref):
    qb = q_ref.shape[0]
    band = 3 * qb
    pairs_per_group = (N_HEADS // N_KV_HEADS) // 2
    rows = pairs_per_group * qb

    bias = bias_ref[0]
    kband = jnp.concatenate([kp_ref[...], kc_ref[...], kn_ref[...]], axis=0)
    vband = jnp.concatenate([vp_ref[...], vc_ref[...], vn_ref[...]], axis=0)
    kctx = kx_ref[...]
    vctx = vx_ref[...]
    dn_t = (((1,), (1,)), ((), ()))
    rsub = lax.broadcasted_iota(I32, (rows, 1), 0) // qb

    for g in range(N_KV_HEADS):
        qg = jnp.concatenate(
            [q_ref[:, (g * pairs_per_group + j) * LANES:(g * pairs_per_group + j + 1) * LANES]
             for j in range(pairs_per_group)], axis=0)
        acc = jnp.zeros((rows, LANES), F32)
        for par in range(2):
            col = (2 * g + par) * LANES
            s_b = lax.dot_general(qg, kband[:, col:col + LANES], dn_t, preferred_element_type=F32)
            s_c = lax.dot_general(qg, kctx[:, col:col + LANES], dn_t, preferred_element_type=F32)
            s_b = s_b + bias
            sink = jnp.zeros((rows, 1), F32)
            for j in range(pairs_per_group):
                hd = 2 * (g * pairs_per_group + j) + par
                sink = jnp.where(rsub == j, sink_ref[hd], sink)
            m = jnp.maximum(jnp.maximum(jnp.max(s_b, axis=-1, keepdims=True),
                                        jnp.max(s_c, axis=-1, keepdims=True)), sink)
            p_b = jnp.exp(s_b - m)
            p_c = jnp.exp(s_c - m)
            l = (jnp.sum(p_b, axis=-1, keepdims=True) + jnp.sum(p_c, axis=-1, keepdims=True)
                 + jnp.exp(sink - m))
            o = (jnp.dot(p_b.astype(BF16), vband[:, col:col + LANES], preferred_element_type=F32)
                 + jnp.dot(p_c.astype(BF16), vctx[:, col:col + LANES], preferred_element_type=F32))
            acc = acc + o / l
        for j in range(pairs_per_group):
            pcol = (g * pairs_per_group + j) * LANES
            o_ref[:, pcol:pcol + LANES] = acc[j * qb:(j + 1) * qb].astype(BF16)


def _attention(q, kk, vv, kkc, vvc, sinks):
    t, nq = q.shape
    qb = Q_BLOCK
    nb = t // qb
    c = kkc.shape[0]
    w = kk.shape[1]
    cur = lambda i, s: (i, 0)
    prv = lambda i, s: (jnp.maximum(i - 1, 0), 0)
    nxt = lambda i, s: (jnp.minimum(i + 1, nb - 1), 0)
    zero = lambda i, s: (0, 0)
    rows = (N_HEADS // N_KV_HEADS) // 2 * qb
    r = (jnp.arange(rows, dtype=I32) % qb)[:, None]
    cpos = jnp.arange(3 * qb, dtype=I32)[None, :] - qb
    inside = jnp.abs(r - cpos) <= WINDOW
    variants = [inside & ((cpos >= 0) | ((v & 1) == 0)) & ((cpos < qb) | ((v & 2) == 0)) for v in range(4)]
    bias = jnp.where(jnp.stack(variants), 0.0, NEG_BIG).astype(F32)
    which = lambda i, s: (jnp.where(i == 0, 1, 0) + jnp.where(i == nb - 1, 2, 0), 0, 0)
    gs = pltpu.PrefetchScalarGridSpec(
        num_scalar_prefetch=1,
        grid=(nb,),
        in_specs=[pl.BlockSpec((qb, nq), cur),
                  pl.BlockSpec((qb, w), prv), pl.BlockSpec((qb, w), cur), pl.BlockSpec((qb, w), nxt),
                  pl.BlockSpec((qb, w), prv), pl.BlockSpec((qb, w), cur), pl.BlockSpec((qb, w), nxt),
                  pl.BlockSpec((c, w), zero), pl.BlockSpec((c, w), zero),
                  pl.BlockSpec((1, rows, 3 * qb), which)],
        out_specs=pl.BlockSpec((qb, nq), cur),
    )
    return pl.pallas_call(
        _attn_kernel,
        grid_spec=gs,
        out_shape=jax.ShapeDtypeStruct((t, nq), BF16),
        compiler_params=_cparams(1),
        name="window_attention",
    )(sinks, q, kk, kk, kk, vv, vv, vv, kkc, vvc, bias)


def _dispatch_out(t, d, tile):
    nt = t // tile
    tok = lambda i: (i, 0)
    specs = [pl.BlockSpec((tile, d), tok), pl.BlockSpec((MOE_REGION, d), tok),
             pl.BlockSpec((tile, LANES), tok), pl.BlockSpec((tile, LANES), tok),
             pl.BlockSpec((1, SUBLANES, LANES), lambda i: (i, 0, 0))]
    shapes = [jax.ShapeDtypeStruct((t, d), F32), jax.ShapeDtypeStruct((nt * MOE_REGION, d), F32),
              jax.ShapeDtypeStruct((t, LANES), I32), jax.ShapeDtypeStruct((t, LANES), F32),
              jax.ShapeDtypeStruct((nt, SUBLANES, LANES), I32)]
    return specs, shapes


def _oproj_kernel(x_ref, o_ref, mod_ref, wo_ref, bo_ref, g2_ref, wr_ref, br_ref,
                  x1_ref, xs_ref, pos_ref, gate_ref, cnt_ref, wb_ref):
    d = x_ref.shape[1]

    @pl.when(pl.program_id(0) == 0)
    def _():
        wb_ref[...] = wo_ref[...].astype(BF16)

    y = jnp.dot(o_ref[...], wb_ref[...], preferred_element_type=F32) + bo_ref[...]
    x1 = x_ref[...] + mod_ref[0:1, 2 * d:3 * d] * y
    x1_ref[...] = x1
    _router_dispatch(x1, g2_ref[...], mod_ref[0:1, 3 * d:4 * d], mod_ref[0:1, 4 * d:5 * d],
                     wr_ref[...], br_ref[...], xs_ref, pos_ref, gate_ref, cnt_ref)


def _oproj_router(xt, o, mod_l, w_o, b_o, g2n, wr, br):
    t, d = xt.shape
    tile = MOE_TILE
    tok = lambda i: (i, 0)
    out_specs, out_shape = _dispatch_out(t, d, tile)
    return pl.pallas_call(
        _oproj_kernel,
        grid=(t // tile,),
        in_specs=[pl.BlockSpec((tile, d), tok), pl.BlockSpec((tile, o.shape[1]), tok), _full(mod_l.shape),
                  _full(w_o.shape), _full((1, d)), _full((1, d)), _full(wr.shape), _full(br.shape)],
        out_specs=out_specs,
        out_shape=out_shape,
        scratch_shapes=[pltpu.VMEM(w_o.shape, BF16)],
        compiler_params=_cparams(1),
        name="oproj_router",
    )(xt, o, mod_l, w_o, b_o, g2n, wr, br)


def _moe_kernel(bexp_ref, nvg_ref, gsrc_ref, xs_hbm, w1_ref, b1_ref, w2_ref, b2_ref, ys_hbm,
                xbuf, obuf, w1b, w2b, gsem, ssem):
    b = pl.program_id(0)
    nb = pl.num_programs(0)
    gpb = GRANULES_PER_BLOCK
    dff = w2_ref.shape[1]
    slot = b % 2

    def granule_rows(blk, gi):
        return pl.ds(pl.multiple_of(gsrc_ref[blk * gpb + gi] * GRANULE, GRANULE), GRANULE)

    def local_rows(gi):
        return pl.ds(pl.multiple_of(gi * GRANULE, GRANULE), GRANULE)

    def gather_copy(blk, s, gi):
        return pltpu.make_async_copy(xs_hbm.at[granule_rows(blk, gi)], xbuf.at[s, local_rows(gi)], gsem.at[s])

    def scatter_copy(blk, s, gi):
        return pltpu.make_async_copy(obuf.at[s, local_rows(gi)], ys_hbm.at[granule_rows(blk, gi)], ssem.at[s])

    grp = DMA_GROUP
    log_grp = grp.bit_length() - 1

    def start_granules(n, copy_fn):
        groups = n >> log_grp

        def group(it, carry):
            for u in range(grp):
                copy_fn(it * grp + u).start()
            return carry

        def single(gi, carry):
            copy_fn(gi).start()
            return carry

        lax.fori_loop(0, groups, group, 0)
        lax.fori_loop(groups * grp, n, single, 0)

    def wait_granules(n, group_copy, one_copy):
        groups = n >> log_grp

        def group(it, carry):
            group_copy.wait()
            return carry

        def single(gi, carry):
            one_copy.wait()
            return carry

        lax.fori_loop(0, groups, group, 0)
        lax.fori_loop(groups * grp, n, single, 0)

    def wait_gather(s, n):
        rows = pl.ds(0, grp * GRANULE)
        wait_granules(n, pltpu.make_async_copy(xs_hbm.at[rows], xbuf.at[s, rows], gsem.at[s]),
                      gather_copy(0, s, 0))

    def wait_scatter(s, n):
        rows = pl.ds(0, grp * GRANULE)
        wait_granules(n, pltpu.make_async_copy(obuf.at[s, rows], ys_hbm.at[rows], ssem.at[s]),
                      scatter_copy(0, s, 0))

    nv = nvg_ref[b]

    @pl.when(b == 0)
    def _():
        xbuf[...] = jnp.zeros(xbuf.shape, F32)
        start_granules(nv, lambda gi: gather_copy(0, 0, gi))

    @pl.when(b + 1 < nb)
    def _():
        nxt = jnp.minimum(b + 1, nb - 1)
        start_granules(nvg_ref[nxt], lambda gi: gather_copy(nxt, 1 - slot, gi))

    @pl.when(b >= 2)
    def _():
        wait_scatter(slot, nvg_ref[jnp.maximum(b - 2, 0)])

    @pl.when(nv > 0)
    def _():
        changed = jnp.logical_or(b == 0, bexp_ref[b] != bexp_ref[jnp.maximum(b - 1, 0)])

        @pl.when(changed)
        def _():
            w1b[...] = w1_ref[0].astype(BF16)
            w2b[...] = w2_ref[0].astype(BF16)

        wait_gather(slot, nv)

        def expert_mlp(rows):
            xb = xbuf[slot, 0:rows, :].astype(BF16)
            gu = jnp.dot(xb, w1b[...], preferred_element_type=F32) + b1_ref[0]
            gt = jnp.minimum(gu[:, :dff], SWIGLU_LIMIT)
            up = jnp.clip(gu[:, dff:], -SWIGLU_LIMIT, SWIGLU_LIMIT)
            act = gt * jax.nn.sigmoid(SWIGLU_ALPHA * gt) * (up + 1.0)
            obuf[slot, 0:rows, :] = jnp.dot(act.astype(BF16), w2b[...], preferred_element_type=F32) + b2_ref[0]

        half = EXPERT_ROWS // 2

        @pl.when(nv * GRANULE > half)
        def _():
            expert_mlp(EXPERT_ROWS)

        @pl.when(nv * GRANULE <= half)
        def _():
            expert_mlp(half)

        start_granules(nv, lambda gi: scatter_copy(b, slot, gi))

    @pl.when(b == nb - 1)
    def _():
        wait_scatter(slot, nv)

        @pl.when(nb >= 2)
        def _():
            wait_scatter(1 - slot, nvg_ref[jnp.maximum(b - 1, 0)])


def _moe_experts(xs, block_exp, n_granules, granule_src, layer, w1, b1, w2, b2):
    d = xs.shape[1]
    n_blocks = block_exp.shape[0]
    dff = w2.shape[2]
    n_layers, n_exp = w1.shape[:2]
    by_expert = lambda b, be, ng, gs: (layer, be[b], 0, 0)
    gs = pltpu.PrefetchScalarGridSpec(
        num_scalar_prefetch=3,
        grid=(n_blocks,),
        in_specs=[pl.BlockSpec(memory_space=pl.ANY),
                  pl.BlockSpec((None, 1, d, 2 * dff), by_expert), pl.BlockSpec((None, 1, 1, 2 * dff), by_expert),
                  pl.BlockSpec((None, 1, dff, d), by_expert), pl.BlockSpec((None, 1, 1, d), by_expert)],
        out_specs=pl.BlockSpec(memory_space=pl.ANY),
        scratch_shapes=[pltpu.VMEM((2, EXPERT_ROWS, d), F32), pltpu.VMEM((2, EXPERT_ROWS, d), F32),
                        pltpu.VMEM((d, 2 * dff), BF16), pltpu.VMEM((dff, d), BF16),
                        pltpu.SemaphoreType.DMA((2,)), pltpu.SemaphoreType.DMA((2,))],
    )
    return pl.pallas_call(
        _moe_kernel,
        grid_spec=gs,
        out_shape=jax.ShapeDtypeStruct(xs.shape, F32),
        input_output_aliases={3: 0},
        compiler_params=_cparams(1),
        name="moe_experts",
    )(block_exp, n_granules, granule_src, xs, w1, b1.reshape(n_layers, n_exp, 1, 2 * dff),
      w2, b2.reshape(n_layers, n_exp, 1, d))


TABLE_BLOCKS = 8


def _tables_kernel(cnt_ref, bexp_ref, ngran_ref, gsrc_ref):
    gpb = GRANULES_PER_BLOCK
    log_b = gpb.bit_length() - 1
    n = cnt_ref.shape[0]
    rows = TABLE_BLOCKS * gpb
    nt_dims = (((1,), (1,)), ((), ()))

    cnt = cnt_ref[...].astype(F32)
    gl_len = cnt * (1.0 / GRANULE)
    ri = lax.broadcasted_iota(I32, (n, n), 0)
    ci = lax.broadcasted_iota(I32, (n, n), 1)
    lower = jnp.where(ci < ri, 1.0, 0.0).astype(BF16)
    upper = jnp.where(ri < ci, 1.0, 0.0).astype(BF16)
    pre = jnp.dot(lower, gl_len.astype(BF16), preferred_element_type=F32)
    off = jnp.dot(cnt.astype(BF16), upper, preferred_element_type=F32)
    src = ri.astype(F32) * float(MOE_REGION // GRANULE) + off * (1.0 / GRANULE)
    n_g = jnp.sum(gl_len, axis=0, keepdims=True)
    blocks = ((n_g.astype(I32) + (gpb - 1)) >> log_b).astype(F32)
    bstart = jnp.dot(jnp.broadcast_to(blocks, (SUBLANES, n)).astype(BF16), upper,
                     preferred_element_type=F32)[0:1]
    bend = bstart + blocks

    def digits(v, base):
        hi = jnp.floor(v * (1.0 / base))
        return hi.astype(BF16), (v - hi * base).astype(BF16)

    def pick(onehot, table):
        return lax.dot_general(onehot, table, nt_dims, preferred_element_type=F32)

    def pick2(onehot, v, base):
        hi, lo = digits(v, base)
        return pick(onehot, hi) * base + pick(onehot, lo)

    q = pl.program_id(0) * rows + lax.broadcasted_iota(I32, (rows, n), 0)
    bq = (q >> log_b).astype(F32)
    iq = (q & (gpb - 1)).astype(F32)
    own = jnp.where((bstart <= bq) & (bq < bend), 1.0, 0.0).astype(BF16)
    pre_q = pick2(own, pre, 64.0)
    len_q = pick(own, gl_len.astype(BF16))
    src_q = pick2(own, src, 128.0)
    bstart_q = pick(own, jnp.broadcast_to(bstart, (n, n)).astype(BF16))
    gl = (bq - bstart_q) * float(gpb) + iq
    inside = (pre_q <= gl) & (gl < pre_q + len_q)
    hit = jnp.sum(jnp.where(inside, src_q - pre_q + gl, 0.0), axis=-1, keepdims=True)
    gsrc_ref[...] = jnp.broadcast_to(hit, (rows, n)).astype(I32)

    bb = (pl.program_id(0) * TABLE_BLOCKS + lax.broadcasted_iota(I32, (TABLE_BLOCKS, n), 0)).astype(F32)
    lane = lax.broadcasted_iota(I32, (TABLE_BLOCKS, n), 1).astype(F32)
    mine = (bstart <= bb) & (bb < bend)
    red = lambda v: jnp.sum(jnp.where(mine, v, 0.0), axis=-1, keepdims=True)
    active = red(jnp.ones_like(lane))
    n_left = red(jnp.broadcast_to(n_g, mine.shape)) - (bb[:, 0:1] - red(jnp.broadcast_to(bstart, mine.shape))) * gpb
    ngran_ref[...] = jnp.broadcast_to(jnp.clip(n_left, 0.0, float(gpb)) * active, (TABLE_BLOCKS, n)).astype(I32)
    expert = jnp.where(active > 0.0, red(lane), float(N_EXPERTS - 1))
    bexp_ref[...] = jnp.broadcast_to(expert, (TABLE_BLOCKS, n)).astype(I32)


def _block_tables(cnt):
    nt = cnt.shape[0]
    assert nt <= LANES
    max_blocks = (nt * (MOE_TILE * TOP_K + N_EXPERTS * (GRANULE - 1))) // EXPERT_ROWS + N_EXPERTS
    n_blocks = -(-max_blocks // TABLE_BLOCKS) * TABLE_BLOCKS
    gpb = GRANULES_PER_BLOCK
    cnt_sq = jnp.zeros((LANES, LANES), I32).at[:nt].set(cnt[:, 0, :])
    by_step = lambda i: (i, 0)
    bexp, ngran, gsrc = pl.pallas_call(
        _tables_kernel,
        grid=(n_blocks // TABLE_BLOCKS,),
        in_specs=[_full((LANES, LANES))],
        out_specs=[pl.BlockSpec((TABLE_BLOCKS, LANES), by_step), pl.BlockSpec((TABLE_BLOCKS, LANES), by_step),
                   pl.BlockSpec((TABLE_BLOCKS * gpb, LANES), by_step)],
        out_shape=[jax.ShapeDtypeStruct((n_blocks, LANES), I32), jax.ShapeDtypeStruct((n_blocks, LANES), I32),
                   jax.ShapeDtypeStruct((n_blocks * gpb, LANES), I32)],
        compiler_params=_cparams(1),
        name="moe_block_tables",
    )(cnt_sq)
    return bexp[:, 0], ngran[:, 0], gsrc[:, 0]


def _combine_kernel(x_ref, ys_ref, pos_ref, gate_ref, mod_ref, o_ref):
    d = x_ref.shape[1]
    o_ref[...] = x_ref[...] + mod_ref[0:1, 5 * d:6 * d] * _moe_combine(ys_ref, pos_ref, gate_ref)


def _combine_glu_kernel(x_ref, ys_ref, pos_ref, gate_ref, modp_ref, modn_ref, g_ref, w_ref, b_ref,
                        o_ref, u_ref, wb_ref):
    d = x_ref.shape[1]

    @pl.when(pl.program_id(0) == 0)
    def _():
        wb_ref[...] = w_ref[...].astype(BF16)

    x2 = x_ref[...] + modp_ref[0:1, 5 * d:6 * d] * _moe_combine(ys_ref, pos_ref, gate_ref)
    o_ref[...] = x2
    h = _rms_mod(x2, g_ref[...], modn_ref[0:1, 0:d], modn_ref[0:1, d:2 * d])
    u = jnp.dot(h.astype(BF16), wb_ref[...], preferred_element_type=F32) + b_ref[...]
    u_ref[...] = u[:, :d] * jax.nn.sigmoid(u[:, d:])


def _combine_in_specs(d, tile):
    tok = lambda i: (i, 0)
    return [pl.BlockSpec((tile, d), tok), pl.BlockSpec((MOE_REGION, d), tok),
            pl.BlockSpec((tile, LANES), tok), pl.BlockSpec((tile, LANES), tok)]


def _combine(x1, ys, pos, gates, mod_l):
    t, d = x1.shape
    tile = MOE_TILE
    return pl.pallas_call(
        _combine_kernel,
        grid=(t // tile,),
        in_specs=_combine_in_specs(d, tile) + [_full(mod_l.shape)],
        out_specs=pl.BlockSpec((tile, d), lambda i: (i, 0)),
        out_shape=jax.ShapeDtypeStruct((t, d), F32),
        compiler_params=_cparams(1),
        name="moe_combine",
    )(x1, ys, pos, gates, mod_l)


def _combine_glu(x1, ys, pos, gates, mod_prev, mod_next, g1n, w_pw1, b_pw1):
    t, d = x1.shape
    tile = MOE_TILE
    tok = lambda i: (i, 0)
    return pl.pallas_call(
        _combine_glu_kernel,
        grid=(t // tile,),
        in_specs=_combine_in_specs(d, tile)
                 + [_full(mod_prev.shape), _full(mod_next.shape), _full((1, d)),
                    _full(w_pw1.shape), _full((1, 2 * d))],
        out_specs=[pl.BlockSpec((tile, d), tok), pl.BlockSpec((tile, d), tok)],
        out_shape=[jax.ShapeDtypeStruct((t, d), F32), jax.ShapeDtypeStruct((t, d), F32)],
        scratch_shapes=[pltpu.VMEM(w_pw1.shape, BF16)],
        compiler_params=_cparams(1),
        name="combine_pw1_glu",
    )(x1, ys, pos, gates, mod_prev, mod_next, g1n, w_pw1, b_pw1)


def _conv_kernel(x_ref, up_ref, uc_ref, un_ref, mod_ref, wdw_ref, bdw_ref, lg_ref, lb_ref, w2_ref, b2_ref,
                 g2_ref, wr_ref, br_ref, x1_ref, xs_ref, pos_ref, gate_ref, cnt_ref, wb_ref, ubuf, shbuf, cbuf):
    i = pl.program_id(0)
    n = pl.num_programs(0)
    d = x_ref.shape[1]
    tile = x_ref.shape[0]
    halo = up_ref.shape[0]

    @pl.when(i == 0)
    def _():
        wb_ref[...] = w2_ref[...].astype(BF16)

    ubuf[0:halo, :] = jnp.where(i > 0, up_ref[...], 0.0)
    ubuf[halo:halo + tile, :] = uc_ref[...]
    ubuf[halo + tile:, :] = jnp.where(i < n - 1, un_ref[...], 0.0)

    span = tile + 2 * halo - SUBLANES
    for r in range(SUBLANES):
        shbuf[r, 0:span, :] = ubuf[r:r + span, :]
    for c0 in range(0, tile, CONV_CHUNK):
        part = jnp.zeros((CONV_CHUNK, d), F32) + bdw_ref[...]
        for j in range(CONV_WIDTH):
            a, r = divmod(halo + j - CONV_PAD, SUBLANES)
            lo = c0 + a * SUBLANES
            part = part + shbuf[r, lo:lo + CONV_CHUNK, :] * wdw_ref[j:j + 1, :]
        cbuf[c0:c0 + CONV_CHUNK, :] = part
    acc = cbuf[...]

    mu = jnp.mean(acc, axis=-1, keepdims=True)
    cen = acc - mu
    var = jnp.mean(cen * cen, axis=-1, keepdims=True)
    z = _silu(cen * lax.rsqrt(var + EPS) * lg_ref[...] + lb_ref[...])
    y = jnp.dot(z.astype(BF16), wb_ref[...], preferred_element_type=F32) + b2_ref[...]
    x1 = x_ref[...] + mod_ref[0:1, 2 * d:3 * d] * y
    x1_ref[...] = x1
    _router_dispatch(x1, g2_ref[...], mod_ref[0:1, 3 * d:4 * d], mod_ref[0:1, 4 * d:5 * d],
                     wr_ref[...], br_ref[...], xs_ref, pos_ref, gate_ref, cnt_ref)


def _conv_router(x2, u, mod_l, w_dw, b_dw, ln_g, ln_b, w_pw2, b_pw2, g2n, wr, br):
    t, d = x2.shape
    tile = MOE_TILE
    nt = t // tile
    hb = tile // CONV_HALO
    nh = t // CONV_HALO
    tok = lambda i: (i, 0)
    prv = lambda i: (jnp.maximum(i * hb - 1, 0), 0)
    nxt = lambda i: (jnp.minimum((i + 1) * hb, nh - 1), 0)
    wpad = jnp.zeros((32, d), F32).at[:CONV_WIDTH].set(w_dw)
    out_specs, out_shape = _dispatch_out(t, d, tile)
    return pl.pallas_call(
        _conv_kernel,
        grid=(nt,),
        in_specs=[pl.BlockSpec((tile, d), tok), pl.BlockSpec((CONV_HALO, d), prv),
                  pl.BlockSpec((tile, d), tok), pl.BlockSpec((CONV_HALO, d), nxt),
                  _full(mod_l.shape), _full((32, d)), _full((1, d)), _full((1, d)), _full((1, d)),
                  _full(w_pw2.shape), _full((1, d)), _full((1, d)), _full(wr.shape), _full(br.shape)],
        out_specs=out_specs,
        out_shape=out_shape,
        scratch_shapes=[pltpu.VMEM(w_pw2.shape, BF16), pltpu.VMEM((tile + 2 * CONV_HALO, d), F32),
                        pltpu.VMEM((SUBLANES, tile + 2 * CONV_HALO, d), F32), pltpu.VMEM((tile, d), F32)],
        compiler_params=_cparams(1),
        name="conv_router",
    )(x2, u, u, u, mod_l, wpad, b_dw, ln_g, ln_b, w_pw2, b_pw2, g2n, wr, br)


def _rope_tables(n_rows):
    quarter = HEAD_DIM // 4
    inv = ROPE_BASE ** (-jnp.arange(quarter, dtype=F32) / quarter)

    def tables(npos):
        ang = jnp.arange(npos, dtype=I32).astype(F32)[:, None] * inv[None, :]
        z = jnp.zeros_like(ang)
        c, s = jnp.cos(ang), jnp.sin(ang)
        half = lambda a, b: jnp.concatenate([a, b], axis=1)
        full = lambda h: jnp.concatenate([h, h, h, h], axis=1)
        return jnp.stack([full(half(c, c)), full(half(-s, z)), full(half(z, s))])

    return tables(n_rows), tables(GRID_W)


def _router_params(w_router, b_router):
    d, e = w_router.shape
    wr = jnp.zeros((d, LANES), F32).at[:, :e].set(w_router)
    br = jnp.full((1, LANES), NEG_BIG, F32).at[0, :e].set(b_router)
    hi = wr.astype(BF16)
    lo = (wr - hi.astype(F32)).astype(BF16)
    return jnp.stack([hi, lo]), br


def kernel(x, c, ctx, c_ctx, w_mod, b_mod, norm1_g, norm2_g, attn_w_qkv, attn_b_qkv, attn_w_o, attn_b_o,
           attn_q_norm, attn_k_norm, attn_sinks, conv_w_pw1, conv_b_pw1, conv_w_dw, conv_b_dw, conv_ln_g,
           conv_ln_b, conv_w_pw2, conv_b_pw2, moe_w_router, moe_b_router, moe_w1, moe_b1, moe_w2, moe_b2):
    bsz, t, d = x.shape
    assert bsz == 1 and w_mod.shape[0] == 2
    n_ctx = ctx.shape[1]
    tile = min(TOKEN_TILE, t)
    assert t % tile == 0 and t % Q_BLOCK == 0 and t % MOE_TILE == 0 and tile % GRID_W == 0
    xt = x.reshape(t, d)
    row = lambda v: v.reshape(1, -1)

    mod = _mod_vectors(c, c_ctx, w_mod, b_mod)
    mod0, mod1 = mod[0], mod[1]

    bd = (jnp.arange(2 * LANES)[:, None] // HEAD_DIM == jnp.arange(2 * LANES)[None, :] // HEAD_DIM).astype(BF16)
    qg = row(jnp.tile(attn_q_norm[0], N_HEADS))
    kg = row(jnp.tile(attn_k_norm[0], N_KV_HEADS))
    rowt, colt = _rope_tables(t // GRID_W)
    g1n = row(norm1_g[0])
    bq = row(attn_b_qkv[0])
    q, kk, vv = _qkv_project(xt, mod0, 0, True, g1n, attn_w_qkv[0], bq, qg, kg, rowt, colt, bd, tile)
    _, kkc, vvc = _qkv_project(ctx.reshape(n_ctx, d), mod0, 1, False, g1n, attn_w_qkv[0], bq, qg, kg,
                               rowt[:, :max(n_ctx // GRID_W, 1)], colt, bd, n_ctx)
    o = _attention(q, kk, vv, kkc, vvc, attn_sinks[0])
    wr0, br0 = _router_params(moe_w_router[0], moe_b_router[0])
    x1, xs, pos, gates, cnt = _oproj_router(xt, o, mod0, attn_w_o[0], row(attn_b_o[0]), row(norm2_g[0]),
                                            wr0, br0)
    ys = _moe_experts(xs, *_block_tables(cnt), 0, moe_w1, moe_b1, moe_w2, moe_b2)

    x2, u = _combine_glu(x1, ys, pos, gates, mod0, mod1, row(norm1_g[1]), conv_w_pw1[0], row(conv_b_pw1[0]))
    wr1, br1 = _router_params(moe_w_router[1], moe_b_router[1])
    x3, xs, pos, gates, cnt = _conv_router(x2, u, mod1, conv_w_dw[0], row(conv_b_dw[0]), row(conv_ln_g[0]),
                                           row(conv_ln_b[0]), conv_w_pw2[0], row(conv_b_pw2[0]),
                                           row(norm2_g[1]), wr1, br1)
    ys = _moe_experts(xs, *_block_tables(cnt), 1, moe_w1, moe_b1, moe_w2, moe_b2)
    out = _combine(x3, ys, pos, gates, mod1)
    return out.reshape(1, t, d)
```

```python
import functools

import jax
import jax.numpy as jnp
from jax import lax
from jax.experimental import pallas as pl
from jax.experimental.pallas import tpu as pltpu

F32 = jnp.float32
BF16 = jnp.bfloat16
I32 = jnp.int32

N_HEADS = 16
N_KV_HEADS = 2
HEAD_DIM = 64
GRID_W = 64
WINDOW = 128
Q_BLOCK = 128
ROPE_BASE = 10000.0
CONV_WIDTH = 31
CONV_PAD = CONV_WIDTH // 2
N_EXPERTS = 32
TOP_K = 4
SWIGLU_LIMIT = 7.0
SWIGLU_ALPHA = 1.702
EPS = 1e-6

LANES = 128
SUBLANES = 8
TOKEN_TILE = 512
MOE_TILE = 256
GRANULE = SUBLANES
EXPERT_ROWS = 512
GRANULES_PER_BLOCK = EXPERT_ROWS // GRANULE
REGION_DATA_GRANULES = (MOE_TILE * TOP_K + N_EXPERTS * (GRANULE - 1)) // GRANULE


def _spare_per_region(nt):
    return -(-GRANULES_PER_BLOCK // nt)


def _region_rows(nt):
    rows = (REGION_DATA_GRANULES + 2 * _spare_per_region(nt) + 1) * GRANULE
    return -(-rows // 256) * 256
CONV_HALO = 16
CONV_CHUNK = 32
NEG_BIG = -1e30
VMEM_LIMIT = 56 * 1024 * 1024


def _cparams(n_axes=1, vmem=VMEM_LIMIT):
    return pltpu.CompilerParams(dimension_semantics=("arbitrary",) * n_axes, vmem_limit_bytes=vmem)


def _full(shape):
    nd = len(shape)
    return pl.BlockSpec(shape, lambda *_: (0,) * nd)


def _silu(v):
    return v * jax.nn.sigmoid(v)


def _rms_mod(xv, g, shift, scale):
    ms = jnp.mean(xv * xv, axis=-1, keepdims=True)
    return (xv * lax.rsqrt(ms + EPS)) * g * (1.0 + scale) + shift


def _split_dot(a, b_bf16):
    hi = a.astype(BF16)
    lo = (a - hi.astype(F32)).astype(BF16)
    return (jnp.dot(hi, b_bf16, preferred_element_type=F32)
            + jnp.dot(lo, b_bf16, preferred_element_type=F32))


def _router_dispatch(x_new, g2n, sh2, sc2, wr, br, xs_ref, pos_ref, gate_ref, cnt_ref):
    tt = x_new.shape[0]
    rt = xs_ref.shape[0]
    h2 = _rms_mod(x_new, g2n, sh2, sc2)
    h_hi = h2.astype(BF16)
    h_lo = (h2 - h_hi.astype(F32)).astype(BF16)
    logits = (jnp.dot(h_hi, wr[0], preferred_element_type=F32)
              + (jnp.dot(h_lo, wr[0], preferred_element_type=F32)
                 + jnp.dot(h_hi, wr[1], preferred_element_type=F32))) + br
    lane = lax.broadcasted_iota(I32, logits.shape, 1)
    val_out = jnp.full(logits.shape, NEG_BIG, F32)
    onehot = jnp.zeros(logits.shape, F32)
    sels = []
    cur = logits
    for k in range(TOP_K):
        m = jnp.max(cur, axis=-1, keepdims=True)
        sel = jnp.min(jnp.where(cur == m, lane, LANES), axis=-1, keepdims=True)
        hit = lane == sel
        sels.append(sel)
        val_out = jnp.where(lane == k, m, val_out)
        onehot = jnp.where(hit, 1.0, onehot)
        cur = jnp.where(hit, NEG_BIG * 2.0, cur)
    vmax = jnp.max(val_out, axis=-1, keepdims=True)
    ev = jnp.where(lane < TOP_K, jnp.exp(val_out - vmax), 0.0)
    gate_ref[...] = ev / jnp.sum(ev, axis=-1, keepdims=True)

    ri = lax.broadcasted_iota(I32, (tt, tt), 0)
    ci = lax.broadcasted_iota(I32, (tt, tt), 1)
    earlier = jnp.where(ci < ri, 1.0, 0.0).astype(BF16)
    rank = jnp.dot(earlier, onehot.astype(BF16), preferred_element_type=F32)
    cnt = jnp.sum(onehot, axis=0, keepdims=True).astype(I32)
    c8 = jnp.bitwise_and(cnt + (GRANULE - 1), -GRANULE)
    ue = lax.broadcasted_iota(I32, (LANES, LANES), 0)
    ve = lax.broadcasted_iota(I32, (LANES, LANES), 1)
    c8_rows = jnp.broadcast_to(c8.astype(F32), (SUBLANES, LANES)).astype(BF16)
    before = jnp.where(ue < ve, 1.0, 0.0).astype(BF16)
    off = jnp.dot(c8_rows, before, preferred_element_type=F32)[0:1]
    cnt_ref[0] = jnp.broadcast_to(c8, (SUBLANES, LANES))

    slot_of = off + rank
    pos_out = jnp.zeros(logits.shape, I32)
    r_iota = lax.broadcasted_iota(I32, (tt, rt), 1)
    perm_t = jnp.zeros((tt, rt), F32)
    for k in range(TOP_K):
        pk = jnp.sum(jnp.where(lane == sels[k], slot_of, 0.0), axis=-1, keepdims=True).astype(I32)
        pos_out = jnp.where(lane == k, pk, pos_out)
        perm_t = jnp.where(r_iota == pk, 1.0, perm_t)
    pos_ref[...] = pos_out
    perm = perm_t.T.astype(BF16)
    xs_ref[...] = jnp.dot(perm, h_hi, preferred_element_type=F32)


def _moe_combine(ys_ref, pos_ref, gate_ref):
    tt = pos_ref.shape[0]
    rt = ys_ref.shape[0]
    r_iota = lax.broadcasted_iota(I32, (tt, rt), 1)
    g = jnp.zeros((tt, rt), F32)
    for k in range(TOP_K):
        g = jnp.where(r_iota == pos_ref[:, k:k + 1], gate_ref[:, k:k + 1], g)
    return jnp.dot(g.astype(BF16), ys_ref[...].astype(BF16), preferred_element_type=F32)


def _mod_kernel(cc_ref, w_ref, b_ref, o_ref):
    a = _silu(cc_ref[...])
    o_ref[0] = jnp.dot(a, w_ref[0], precision=lax.Precision.HIGHEST,
                       preferred_element_type=F32) + b_ref[0]


def _mod_vectors(c, c_ctx, w_mod, b_mod):
    depth, d, d6 = w_mod.shape
    cc = jnp.zeros((SUBLANES, d), F32).at[0].set(c[0]).at[1].set(c_ctx)
    ncol = 4
    cw = d6 // ncol
    return pl.pallas_call(
        _mod_kernel,
        grid=(depth, ncol),
        in_specs=[pl.BlockSpec((SUBLANES, d), lambda l, j: (0, 0)),
                  pl.BlockSpec((1, d, cw), lambda l, j: (l, 0, j)),
                  pl.BlockSpec((1, 1, cw), lambda l, j: (l, 0, j))],
        out_specs=pl.BlockSpec((1, SUBLANES, cw), lambda l, j: (l, 0, j)),
        out_shape=jax.ShapeDtypeStruct((depth, SUBLANES, d6), F32),
        compiler_params=_cparams(2),
        name="mod_vectors",
    )(cc, w_mod, b_mod.reshape(depth, 1, d6))


def _qkv_kernel(mod_row, use_rope, x_ref, mod_ref, g_ref, w_ref, b_ref, qg_ref, kg_ref, rowt_ref, colt_ref,
                bd_ref, q_ref, k_ref, v_ref, wb_ref):
    d = x_ref.shape[1]
    tile = x_ref.shape[0]
    nq = N_HEADS * HEAD_DIM
    nk = N_KV_HEADS * HEAD_DIM

    @pl.when(pl.program_id(0) == 0)
    def _():
        wb_ref[...] = w_ref[...].astype(BF16)

    sh = mod_ref[mod_row:mod_row + 1, 0:d]
    sc = mod_ref[mod_row:mod_row + 1, d:2 * d]
    h = _rms_mod(x_ref[...], g_ref[...], sh, sc)
    qkv = jnp.dot(h.astype(BF16), wb_ref[...], preferred_element_type=F32) + b_ref[...]
    q = qkv[:, :nq]
    k = qkv[:, nq:nq + nk]
    v = qkv[:, nq + nk:]
    bd = bd_ref[...]

    def head_norm(t, gain):
        tt = t * t
        w = bd.shape[0]
        if t.shape[1] >= w:
            ss = jnp.concatenate([_split_dot(tt[:, c:c + w], bd) for c in range(0, t.shape[1], w)], axis=1)
        else:
            ss = _split_dot(tt, bd[:t.shape[1], :t.shape[1]])
        return t * lax.rsqrt(ss * (1.0 / HEAD_DIM) + EPS) * gain

    qn = head_norm(q, qg_ref[...])
    kn = head_norm(k, kg_ref[...])

    if use_rope:
        lane = lax.broadcasted_iota(I32, (tile, LANES), 1)
        is_row = (lane % HEAD_DIM) < HEAD_DIM // 2
        nrow = tile // GRID_W

        def table(idx):
            rt = jnp.concatenate([jnp.broadcast_to(rowt_ref[idx, r:r + 1, :], (GRID_W, LANES))
                                  for r in range(nrow)], axis=0)
            ct = jnp.concatenate([colt_ref[idx]] * nrow, axis=0)
            return jnp.where(is_row, rt, ct)

        cosv, sav, sbv = table(0), table(1), table(2)

        def rope(t):
            n = t.shape[1]
            reps = n // LANES
            ct = jnp.concatenate([cosv] * reps, axis=1) if reps > 1 else cosv
            at = jnp.concatenate([sav] * reps, axis=1) if reps > 1 else sav
            bt = jnp.concatenate([sbv] * reps, axis=1) if reps > 1 else sbv
            up = pltpu.roll(t, n - HEAD_DIM // 4, 1)
            dn = pltpu.roll(t, HEAD_DIM // 4, 1)
            return t * ct + up * at + dn * bt

        qn = rope(qn)
        kn = rope(kn)

    q_ref[...] = (qn * (HEAD_DIM ** -0.5)).astype(BF16)
    klane = lax.broadcasted_iota(I32, kn.shape, 1)
    low = klane < HEAD_DIM

    def variants(t):
        sw = pltpu.roll(t, HEAD_DIM, 1)
        z = jnp.zeros_like(t)
        return jnp.concatenate([jnp.where(low, t, z), jnp.where(low, z, sw),
                                jnp.where(low, sw, z), jnp.where(low, z, t)], axis=1)

    k_ref[...] = variants(kn).astype(BF16)
    v_ref[...] = variants(v).astype(BF16)


def _qkv_project(xt, mod_l, mod_row, use_rope, g1n, w_qkv, b_qkv, qg, kg, rowt, colt, bd, tile):
    t, d = xt.shape
    nqkv = w_qkv.shape[1]
    nq = N_HEADS * HEAD_DIM
    tok = lambda i: (i, 0)
    nrow = max(tile // GRID_W, 1)
    return pl.pallas_call(
        functools.partial(_qkv_kernel, mod_row, use_rope),
        grid=(t // tile,),
        in_specs=[pl.BlockSpec((tile, d), tok), _full(mod_l.shape), _full((1, d)),
                  _full((d, nqkv)), _full((1, nqkv)), _full((1, nq)), _full((1, LANES)),
                  pl.BlockSpec((3, nrow, LANES), lambda i: (0, i, 0)), _full(colt.shape), _full(bd.shape)],
        out_specs=[pl.BlockSpec((tile, nq), tok), pl.BlockSpec((tile, 4 * LANES), tok),
                   pl.BlockSpec((tile, 4 * LANES), tok)],
        out_shape=[jax.ShapeDtypeStruct((t, nq), BF16), jax.ShapeDtypeStruct((t, 4 * LANES), BF16),
                   jax.ShapeDtypeStruct((t, 4 * LANES), BF16)],
        scratch_shapes=[pltpu.VMEM((d, nqkv), BF16)],
        compiler_params=_cparams(1),
        name="qkv_project",
    )(xt, mod_l, g1n, w_qkv, b_qkv, qg, kg, rowt, colt, bd)


def _attn_kernel(sink_ref, q_ref, kp_ref, kc_ref, kn_ref, vp_ref, vc_ref, vn_ref, kx_ref, vx_ref, bias_ref,
                 o_ref):
    qb = q_ref.shape[0]
    band = 3 * qb
    pairs_per_group = (N_HEADS // N_KV_HEADS) // 2
    rows = pairs_per_group * qb

    bias = bias_ref[0]
    kband = jnp.concatenate([kp_ref[...], kc_ref[...], kn_ref[...]], axis=0)
    vband = jnp.concatenate([vp_ref[...], vc_ref[...], vn_ref[...]], axis=0)
    kctx = kx_ref[...]
    vctx = vx_ref[...]
    dn_t = (((1,), (1,)), ((), ()))
    rsub = lax.broadcasted_iota(I32, (rows, 1), 0) // qb

    for g in range(N_KV_HEADS):
        qg = jnp.concatenate(
            [q_ref[:, (g * pairs_per_group + j) * LANES:(g * pairs_per_group + j + 1) * LANES]
             for j in range(pairs_per_group)], axis=0)
        acc = jnp.zeros((rows, LANES), F32)
        for par in range(2):
            col = (2 * g + par) * LANES
            s_b = lax.dot_general(qg, kband[:, col:col + LANES], dn_t, preferred_element_type=F32)
            s_c = lax.dot_general(qg, kctx[:, col:col + LANES], dn_t, preferred_element_type=F32)
            s_b = s_b + bias
            sink = jnp.zeros((rows, 1), F32)
            for j in range(pairs_per_group):
                hd = 2 * (g * pairs_per_group + j) + par
                sink = jnp.where(rsub == j, sink_ref[hd], sink)
            m = jnp.maximum(jnp.maximum(jnp.max(s_b, axis=-1, keepdims=True),
                                        jnp.max(s_c, axis=-1, keepdims=True)), sink)
            p_b = jnp.exp(s_b - m)
            p_c = jnp.exp(s_c - m)
            l = (jnp.sum(p_b, axis=-1, keepdims=True) + jnp.sum(p_c, axis=-1, keepdims=True)
                 + jnp.exp(sink - m))
            o = (jnp.dot(p_b.astype(BF16), vband[:, col:col + LANES], preferred_element_type=F32)
                 + jnp.dot(p_c.astype(BF16), vctx[:, col:col + LANES], preferred_element_type=F32))
            acc = acc + o / l
        for j in range(pairs_per_group):
            pcol = (g * pairs_per_group + j) * LANES
            o_ref[:, pcol:pcol + LANES] = acc[j * qb:(j + 1) * qb].astype(BF16)


def _attention(q, kk, vv, kkc, vvc, sinks):
    t, nq = q.shape
    qb = Q_BLOCK
    nb = t // qb
    c = kkc.shape[0]
    w = kk.shape[1]
    cur = lambda i, s: (i, 0)
    prv = lambda i, s: (jnp.maximum(i - 1, 0), 0)
    nxt = lambda i, s: (jnp.minimum(i + 1, nb - 1), 0)
    zero = lambda i, s: (0, 0)
    rows = (N_HEADS // N_KV_HEADS) // 2 * qb
    r = (jnp.arange(rows, dtype=I32) % qb)[:, None]
    cpos = jnp.arange(3 * qb, dtype=I32)[None, :] - qb
    inside = jnp.abs(r - cpos) <= WINDOW
    variants = [inside & ((cpos >= 0) | ((v & 1) == 0)) & ((cpos < qb) | ((v & 2) == 0)) for v in range(4)]
    bias = jnp.where(jnp.stack(variants), 0.0, NEG_BIG).astype(F32)
    which = lambda i, s: (jnp.where(i == 0, 1, 0) + jnp.where(i == nb - 1, 2, 0), 0, 0)
    gs = pltpu.PrefetchScalarGridSpec(
        num_scalar_prefetch=1,
        grid=(nb,),
        in_specs=[pl.BlockSpec((qb, nq), cur),
                  pl.BlockSpec((qb, w), prv), pl.BlockSpec((qb, w), cur), pl.BlockSpec((qb, w), nxt),
                  pl.BlockSpec((qb, w), prv), pl.BlockSpec((qb, w), cur), pl.BlockSpec((qb, w), nxt),
                  pl.BlockSpec((c, w), zero), pl.BlockSpec((c, w), zero),
                  pl.BlockSpec((1, rows, 3 * qb), which)],
        out_specs=pl.BlockSpec((qb, nq), cur),
    )
    return pl.pallas_call(
        _attn_kernel,
        grid_spec=gs,
        out_shape=jax.ShapeDtypeStruct((t, nq), BF16),
        compiler_params=_cparams(1),
        name="window_attention",
    )(sinks, q, kk, kk, kk, vv, vv, vv, kkc, vvc, bias)


def _dispatch_out(t, d, tile):
    nt = t // tile
    tok = lambda i: (i, 0)
    region = _region_rows(nt)
    specs = [pl.BlockSpec((tile, d), tok), pl.BlockSpec((region, d), tok),
             pl.BlockSpec((tile, LANES), tok), pl.BlockSpec((tile, LANES), tok),
             pl.BlockSpec((1, SUBLANES, LANES), lambda i: (i, 0, 0))]
    shapes = [jax.ShapeDtypeStruct((t, d), F32), jax.ShapeDtypeStruct((nt * region, d), F32),
              jax.ShapeDtypeStruct((t, LANES), I32), jax.ShapeDtypeStruct((t, LANES), F32),
              jax.ShapeDtypeStruct((nt, SUBLANES, LANES), I32)]
    return specs, shapes


def _oproj_kernel(x_ref, o_ref, mod_ref, wo_ref, bo_ref, g2_ref, wr_ref, br_ref,
                  x1_ref, xs_ref, pos_ref, gate_ref, cnt_ref, wb_ref):
    d = x_ref.shape[1]

    @pl.when(pl.program_id(0) == 0)
    def _():
        wb_ref[...] = wo_ref[...].astype(BF16)

    y = jnp.dot(o_ref[...], wb_ref[...], preferred_element_type=F32) + bo_ref[...]
    x1 = x_ref[...] + mod_ref[0:1, 2 * d:3 * d] * y
    x1_ref[...] = x1
    _router_dispatch(x1, g2_ref[...], mod_ref[0:1, 3 * d:4 * d], mod_ref[0:1, 4 * d:5 * d],
                     wr_ref[...], br_ref[...], xs_ref, pos_ref, gate_ref, cnt_ref)


def _oproj_router(xt, o, mod_l, w_o, b_o, g2n, wr, br):
    t, d = xt.shape
    tile = MOE_TILE
    tok = lambda i: (i, 0)
    out_specs, out_shape = _dispatch_out(t, d, tile)
    return pl.pallas_call(
        _oproj_kernel,
        grid=(t // tile,),
        in_specs=[pl.BlockSpec((tile, d), tok), pl.BlockSpec((tile, o.shape[1]), tok), _full(mod_l.shape),
                  _full(w_o.shape), _full((1, d)), _full((1, d)), _full(wr.shape), _full(br.shape)],
        out_specs=out_specs,
        out_shape=out_shape,
        scratch_shapes=[pltpu.VMEM(w_o.shape, BF16)],
        compiler_params=_cparams(1),
        name="oproj_router",
    )(xt, o, mod_l, w_o, b_o, g2n, wr, br)


def _moe_kernel(nt, region_granules, bexp_ref, nvg_ref, gsrc_ref, xs_hbm, w1_ref, b1_ref, w2_ref, b2_ref,
                ys_hbm, xbuf, obuf, w1b, w2b, gsem, ssem):
    b = pl.program_id(0)
    nb = pl.num_programs(0)
    gpb = GRANULES_PER_BLOCK
    dff = w2_ref.shape[1]
    slot = b % 2
    spare = _spare_per_region(nt)
    idle_granule = REGION_DATA_GRANULES + 2 * spare

    def rows_of(granule):
        return pl.ds(pl.multiple_of(granule * GRANULE, GRANULE), GRANULE)

    def gather_copy(blk, s, gi):
        g = gsrc_ref[blk * gpb + gi]
        src = jnp.where(g >= 0, g, idle_granule)
        return pltpu.make_async_copy(xs_hbm.at[rows_of(src)], xbuf.at[s, pl.ds(gi * GRANULE, GRANULE)],
                                     gsem.at[s])

    def scatter_copy(blk, s, gi):
        g = gsrc_ref[blk * gpb + gi]
        own_spare = (gi % nt) * region_granules + REGION_DATA_GRANULES + gi // nt + s * spare
        dst = jnp.where(g >= 0, g, own_spare)
        return pltpu.make_async_copy(obuf.at[s, pl.ds(gi * GRANULE, GRANULE)], ys_hbm.at[rows_of(dst)],
                                     ssem.at[s])

    def start_gather(blk, s):
        for gi in range(gpb):
            gather_copy(blk, s, gi).start()

    def start_scatter(blk, s):
        for gi in range(gpb):
            scatter_copy(blk, s, gi).start()

    def wait_gather(s):
        pltpu.make_async_copy(xs_hbm.at[pl.ds(0, EXPERT_ROWS)], xbuf.at[s], gsem.at[s]).wait()

    def wait_scatter(s):
        pltpu.make_async_copy(obuf.at[s], ys_hbm.at[pl.ds(0, EXPERT_ROWS)], ssem.at[s]).wait()

    nv = nvg_ref[b]
    prev_active = jnp.logical_and(b >= 1, nvg_ref[jnp.maximum(b - 1, 0)] > 0)

    @pl.when(b == 0)
    def _():
        obuf[...] = jnp.zeros(obuf.shape, F32)
        start_gather(0, 0)

    @pl.when(jnp.logical_or(b == 0, prev_active))
    def _():
        wait_gather(slot)

    @pl.when(jnp.logical_and(b >= 2, nvg_ref[jnp.maximum(b - 2, 0)] > 0))
    def _():
        wait_scatter(slot)

    @pl.when(nv > 0)
    def _():
        changed = jnp.logical_or(b == 0, bexp_ref[b] != bexp_ref[jnp.maximum(b - 1, 0)])

        @pl.when(changed)
        def _():
            w1b[...] = w1_ref[0].astype(BF16)
            w2b[...] = w2_ref[0].astype(BF16)

        def expert_block(rows):
            start_gather(b + 1, 1 - slot)
            xb = xbuf[slot, 0:rows, :].astype(BF16)
            gu = jnp.dot(xb, w1b[...], preferred_element_type=F32) + b1_ref[0]
            gt = jnp.minimum(gu[:, :dff], SWIGLU_LIMIT)
            up = jnp.clip(gu[:, dff:], -SWIGLU_LIMIT, SWIGLU_LIMIT)
            act = gt * jax.nn.sigmoid(SWIGLU_ALPHA * gt) * (up + 1.0)
            obuf[slot, 0:rows, :] = jnp.dot(act.astype(BF16), w2b[...], preferred_element_type=F32) + b2_ref[0]
            start_scatter(b, slot)

        half = EXPERT_ROWS // 2

        @pl.when(nv * GRANULE > half)
        def _():
            expert_block(EXPERT_ROWS)

        @pl.when(nv * GRANULE <= half)
        def _():
            expert_block(half)

    @pl.when(jnp.logical_and(b == nb - 1, prev_active))
    def _():
        wait_scatter(1 - slot)


def _moe_experts(xs, nt, block_exp, n_granules, granule_src, layer, w1, b1, w2, b2):
    d = xs.shape[1]
    n_blocks = block_exp.shape[0]
    dff = w2.shape[2]
    n_layers, n_exp = w1.shape[:2]
    by_expert = lambda b, be, ng, gs: (layer, be[b], 0, 0)
    gs = pltpu.PrefetchScalarGridSpec(
        num_scalar_prefetch=3,
        grid=(n_blocks,),
        in_specs=[pl.BlockSpec(memory_space=pl.ANY),
                  pl.BlockSpec((None, 1, d, 2 * dff), by_expert), pl.BlockSpec((None, 1, 1, 2 * dff), by_expert),
                  pl.BlockSpec((None, 1, dff, d), by_expert), pl.BlockSpec((None, 1, 1, d), by_expert)],
        out_specs=pl.BlockSpec(memory_space=pl.ANY),
        scratch_shapes=[pltpu.VMEM((2, EXPERT_ROWS, d), F32), pltpu.VMEM((2, EXPERT_ROWS, d), F32),
                        pltpu.VMEM((d, 2 * dff), BF16), pltpu.VMEM((dff, d), BF16),
                        pltpu.SemaphoreType.DMA((2,)), pltpu.SemaphoreType.DMA((2,))],
    )
    return pl.pallas_call(
        functools.partial(_moe_kernel, nt, _region_rows(nt) // GRANULE),
        grid_spec=gs,
        out_shape=jax.ShapeDtypeStruct(xs.shape, F32),
        input_output_aliases={3: 0},
        compiler_params=_cparams(1),
        name="moe_experts",
    )(block_exp, n_granules, granule_src, xs, w1, b1.reshape(n_layers, n_exp, 1, 2 * dff),
      w2, b2.reshape(n_layers, n_exp, 1, d))


TABLE_BLOCKS = 8


def _tables_kernel(region_granules, cnt_ref, bexp_ref, ngran_ref, gsrc_ref):
    gpb = GRANULES_PER_BLOCK
    log_b = gpb.bit_length() - 1
    n = cnt_ref.shape[0]
    rows = TABLE_BLOCKS * gpb
    nt_dims = (((1,), (1,)), ((), ()))

    cnt = cnt_ref[...].astype(F32)
    gl_len = cnt * (1.0 / GRANULE)
    ri = lax.broadcasted_iota(I32, (n, n), 0)
    ci = lax.broadcasted_iota(I32, (n, n), 1)
    lower = jnp.where(ci < ri, 1.0, 0.0).astype(BF16)
    upper = jnp.where(ri < ci, 1.0, 0.0).astype(BF16)
    pre = jnp.dot(lower, gl_len.astype(BF16), preferred_element_type=F32)
    off = jnp.dot(cnt.astype(BF16), upper, preferred_element_type=F32)
    src = ri.astype(F32) * float(region_granules) + off * (1.0 / GRANULE)
    n_g = jnp.sum(gl_len, axis=0, keepdims=True)
    blocks = ((n_g.astype(I32) + (gpb - 1)) >> log_b).astype(F32)
    bstart = jnp.dot(jnp.broadcast_to(blocks, (SUBLANES, n)).astype(BF16), upper,
                     preferred_element_type=F32)[0:1]
    bend = bstart + blocks

    def digits(v, base):
        hi = jnp.floor(v * (1.0 / base))
        return hi.astype(BF16), (v - hi * base).astype(BF16)

    def pick(onehot, table):
        return lax.dot_general(onehot, table, nt_dims, preferred_element_type=F32)

    def pick2(onehot, v, base):
        hi, lo = digits(v, base)
        return pick(onehot, hi) * base + pick(onehot, lo)

    q = pl.program_id(0) * rows + lax.broadcasted_iota(I32, (rows, n), 0)
    bq = (q >> log_b).astype(F32)
    iq = (q & (gpb - 1)).astype(F32)
    own = jnp.where((bstart <= bq) & (bq < bend), 1.0, 0.0).astype(BF16)
    pre_q = pick2(own, pre, 64.0)
    len_q = pick(own, gl_len.astype(BF16))
    src_q = pick2(own, src, 128.0)
    bstart_q = pick(own, jnp.broadcast_to(bstart, (n, n)).astype(BF16))
    gl = (bq - bstart_q) * float(gpb) + iq
    inside = (pre_q <= gl) & (gl < pre_q + len_q)
    hit = jnp.sum(jnp.where(inside, src_q - pre_q + gl + 1.0, 0.0), axis=-1, keepdims=True)
    gsrc_ref[...] = jnp.broadcast_to(hit - 1.0, (rows, n)).astype(I32)

    bb = (pl.program_id(0) * TABLE_BLOCKS + lax.broadcasted_iota(I32, (TABLE_BLOCKS, n), 0)).astype(F32)
    lane = lax.broadcasted_iota(I32, (TABLE_BLOCKS, n), 1).astype(F32)
    mine = (bstart <= bb) & (bb < bend)
    red = lambda v: jnp.sum(jnp.where(mine, v, 0.0), axis=-1, keepdims=True)
    active = red(jnp.ones_like(lane))
    n_left = red(jnp.broadcast_to(n_g, mine.shape)) - (bb[:, 0:1] - red(jnp.broadcast_to(bstart, mine.shape))) * gpb
    ngran_ref[...] = jnp.broadcast_to(jnp.clip(n_left, 0.0, float(gpb)) * active, (TABLE_BLOCKS, n)).astype(I32)
    expert = jnp.where(active > 0.0, red(lane), float(N_EXPERTS - 1))
    bexp_ref[...] = jnp.broadcast_to(expert, (TABLE_BLOCKS, n)).astype(I32)


def _block_tables(cnt):
    nt = cnt.shape[0]
    assert nt <= LANES
    max_blocks = (nt * REGION_DATA_GRANULES) // GRANULES_PER_BLOCK + N_EXPERTS
    n_blocks = -(-(max_blocks + 1) // TABLE_BLOCKS) * TABLE_BLOCKS
    gpb = GRANULES_PER_BLOCK
    cnt_sq = jnp.zeros((LANES, LANES), I32).at[:nt].set(cnt[:, 0, :])
    by_step = lambda i: (i, 0)
    bexp, ngran, gsrc = pl.pallas_call(
        functools.partial(_tables_kernel, _region_rows(nt) // GRANULE),
        grid=(n_blocks // TABLE_BLOCKS,),
        in_specs=[_full((LANES, LANES))],
        out_specs=[pl.BlockSpec((TABLE_BLOCKS, LANES), by_step), pl.BlockSpec((TABLE_BLOCKS, LANES), by_step),
                   pl.BlockSpec((TABLE_BLOCKS * gpb, LANES), by_step)],
        out_shape=[jax.ShapeDtypeStruct((n_blocks, LANES), I32), jax.ShapeDtypeStruct((n_blocks, LANES), I32),
                   jax.ShapeDtypeStruct((n_blocks * gpb, LANES), I32)],
        compiler_params=_cparams(1),
        name="moe_block_tables",
    )(cnt_sq)
    return bexp[:, 0], ngran[:, 0], gsrc[:, 0]


def _combine_kernel(x_ref, ys_ref, pos_ref, gate_ref, mod_ref, o_ref):
    d = x_ref.shape[1]
    o_ref[...] = x_ref[...] + mod_ref[0:1, 5 * d:6 * d] * _moe_combine(ys_ref, pos_ref, gate_ref)


def _combine_glu_kernel(x_ref, ys_ref, pos_ref, gate_ref, modp_ref, modn_ref, g_ref, w_ref, b_ref,
                        o_ref, u_ref, wb_ref):
    d = x_ref.shape[1]

    @pl.when(pl.program_id(0) == 0)
    def _():
        wb_ref[...] = w_ref[...].astype(BF16)

    x2 = x_ref[...] + modp_ref[0:1, 5 * d:6 * d] * _moe_combine(ys_ref, pos_ref, gate_ref)
    o_ref[...] = x2
    h = _rms_mod(x2, g_ref[...], modn_ref[0:1, 0:d], modn_ref[0:1, d:2 * d])
    u = jnp.dot(h.astype(BF16), wb_ref[...], preferred_element_type=F32) + b_ref[...]
    u_ref[...] = u[:, :d] * jax.nn.sigmoid(u[:, d:])


def _combine_in_specs(t, d, tile):
    tok = lambda i: (i, 0)
    return [pl.BlockSpec((tile, d), tok), pl.BlockSpec((_region_rows(t // tile), d), tok),
            pl.BlockSpec((tile, LANES), tok), pl.BlockSpec((tile, LANES), tok)]


def _combine(x1, ys, pos, gates, mod_l):
    t, d = x1.shape
    tile = MOE_TILE
    return pl.pallas_call(
        _combine_kernel,
        grid=(t // tile,),
        in_specs=_combine_in_specs(t, d, tile) + [_full(mod_l.shape)],
        out_specs=pl.BlockSpec((tile, d), lambda i: (i, 0)),
        out_shape=jax.ShapeDtypeStruct((t, d), F32),
        compiler_params=_cparams(1),
        name="moe_combine",
    )(x1, ys, pos, gates, mod_l)


def _combine_glu(x1, ys, pos, gates, mod_prev, mod_next, g1n, w_pw1, b_pw1):
    t, d = x1.shape
    tile = MOE_TILE
    tok = lambda i: (i, 0)
    return pl.pallas_call(
        _combine_glu_kernel,
        grid=(t // tile,),
        in_specs=_combine_in_specs(t, d, tile)
                 + [_full(mod_prev.shape), _full(mod_next.shape), _full((1, d)),
                    _full(w_pw1.shape), _full((1, 2 * d))],
        out_specs=[pl.BlockSpec((tile, d), tok), pl.BlockSpec((tile, d), tok)],
        out_shape=[jax.ShapeDtypeStruct((t, d), F32), jax.ShapeDtypeStruct((t, d), F32)],
        scratch_shapes=[pltpu.VMEM(w_pw1.shape, BF16)],
        compiler_params=_cparams(1),
        name="combine_pw1_glu",
    )(x1, ys, pos, gates, mod_prev, mod_next, g1n, w_pw1, b_pw1)


def _conv_kernel(x_ref, up_ref, uc_ref, un_ref, mod_ref, wdw_ref, bdw_ref, lg_ref, lb_ref, w2_ref, b2_ref,
                 g2_ref, wr_ref, br_ref, x1_ref, xs_ref, pos_ref, gate_ref, cnt_ref, wb_ref, ubuf, shbuf, cbuf):
    i = pl.program_id(0)
    n = pl.num_programs(0)
    d = x_ref.shape[1]
    tile = x_ref.shape[0]
    halo = up_ref.shape[0]

    @pl.when(i == 0)
    def _():
        wb_ref[...] = w2_ref[...].astype(BF16)

    ubuf[0:halo, :] = jnp.where(i > 0, up_ref[...], 0.0)
    ubuf[halo:halo + tile, :] = uc_ref[...]
    ubuf[halo + tile:, :] = jnp.where(i < n - 1, un_ref[...], 0.0)

    span = tile + 2 * halo - SUBLANES
    for r in range(SUBLANES):
        shbuf[r, 0:span, :] = ubuf[r:r + span, :]
    for c0 in range(0, tile, CONV_CHUNK):
        part = jnp.zeros((CONV_CHUNK, d), F32) + bdw_ref[...]
        for j in range(CONV_WIDTH):
            a, r = divmod(halo + j - CONV_PAD, SUBLANES)
            lo = c0 + a * SUBLANES
            part = part + shbuf[r, lo:lo + CONV_CHUNK, :] * wdw_ref[j:j + 1, :]
        cbuf[c0:c0 + CONV_CHUNK, :] = part
    acc = cbuf[...]

    mu = jnp.mean(acc, axis=-1, keepdims=True)
    cen = acc - mu
    var = jnp.mean(cen * cen, axis=-1, keepdims=True)
    z = _silu(cen * lax.rsqrt(var + EPS) * lg_ref[...] + lb_ref[...])
    y = jnp.dot(z.astype(BF16), wb_ref[...], preferred_element_type=F32) + b2_ref[...]
    x1 = x_ref[...] + mod_ref[0:1, 2 * d:3 * d] * y
    x1_ref[...] = x1
    _router_dispatch(x1, g2_ref[...], mod_ref[0:1, 3 * d:4 * d], mod_ref[0:1, 4 * d:5 * d],
                     wr_ref[...], br_ref[...], xs_ref, pos_ref, gate_ref, cnt_ref)


def _conv_router(x2, u, mod_l, w_dw, b_dw, ln_g, ln_b, w_pw2, b_pw2, g2n, wr, br):
    t, d = x2.shape
    tile = MOE_TILE
    nt = t // tile
    hb = tile // CONV_HALO
    nh = t // CONV_HALO
    tok = lambda i: (i, 0)
    prv = lambda i: (jnp.maximum(i * hb - 1, 0), 0)
    nxt = lambda i: (jnp.minimum((i + 1) * hb, nh - 1), 0)
    wpad = jnp.zeros((32, d), F32).at[:CONV_WIDTH].set(w_dw)
    out_specs, out_shape = _dispatch_out(t, d, tile)
    return pl.pallas_call(
        _conv_kernel,
        grid=(nt,),
        in_specs=[pl.BlockSpec((tile, d), tok), pl.BlockSpec((CONV_HALO, d), prv),
                  pl.BlockSpec((tile, d), tok), pl.BlockSpec((CONV_HALO, d), nxt),
                  _full(mod_l.shape), _full((32, d)), _full((1, d)), _full((1, d)), _full((1, d)),
                  _full(w_pw2.shape), _full((1, d)), _full((1, d)), _full(wr.shape), _full(br.shape)],
        out_specs=out_specs,
        out_shape=out_shape,
        scratch_shapes=[pltpu.VMEM(w_pw2.shape, BF16), pltpu.VMEM((tile + 2 * CONV_HALO, d), F32),
                        pltpu.VMEM((SUBLANES, tile + 2 * CONV_HALO, d), F32), pltpu.VMEM((tile, d), F32)],
        compiler_params=_cparams(1),
        name="conv_router",
    )(x2, u, u, u, mod_l, wpad, b_dw, ln_g, ln_b, w_pw2, b_pw2, g2n, wr, br)


def _rope_tables(n_rows):
    quarter = HEAD_DIM // 4
    inv = ROPE_BASE ** (-jnp.arange(quarter, dtype=F32) / quarter)

    def tables(npos):
        ang = jnp.arange(npos, dtype=I32).astype(F32)[:, None] * inv[None, :]
        z = jnp.zeros_like(ang)
        c, s = jnp.cos(ang), jnp.sin(ang)
        half = lambda a, b: jnp.concatenate([a, b], axis=1)
        full = lambda h: jnp.concatenate([h, h, h, h], axis=1)
        return jnp.stack([full(half(c, c)), full(half(-s, z)), full(half(z, s))])

    return tables(n_rows), tables(GRID_W)


def _router_params(w_router, b_router):
    d, e = w_router.shape
    wr = jnp.zeros((d, LANES), F32).at[:, :e].set(w_router)
    br = jnp.full((1, LANES), NEG_BIG, F32).at[0, :e].set(b_router)
    hi = wr.astype(BF16)
    lo = (wr - hi.astype(F32)).astype(BF16)
    return jnp.stack([hi, lo]), br


def kernel(x, c, ctx, c_ctx, w_mod, b_mod, norm1_g, norm2_g, attn_w_qkv, attn_b_qkv, attn_w_o, attn_b_o,
           attn_q_norm, attn_k_norm, attn_sinks, conv_w_pw1, conv_b_pw1, conv_w_dw, conv_b_dw, conv_ln_g,
           conv_ln_b, conv_w_pw2, conv_b_pw2, moe_w_router, moe_b_router, moe_w1, moe_b1, moe_w2, moe_b2):
    bsz, t, d = x.shape
    assert bsz == 1 and w_mod.shape[0] == 2
    n_ctx = ctx.shape[1]
    tile = min(TOKEN_TILE, t)
    assert t % tile == 0 and t % Q_BLOCK == 0 and t % MOE_TILE == 0 and tile % GRID_W == 0
    xt = x.reshape(t, d)
    row = lambda v: v.reshape(1, -1)

    mod = _mod_vectors(c, c_ctx, w_mod, b_mod)
    mod0, mod1 = mod[0], mod[1]

    bd = (jnp.arange(2 * LANES)[:, None] // HEAD_DIM == jnp.arange(2 * LANES)[None, :] // HEAD_DIM).astype(BF16)
    qg = row(jnp.tile(attn_q_norm[0], N_HEADS))
    kg = row(jnp.tile(attn_k_norm[0], N_KV_HEADS))
    rowt, colt = _rope_tables(t // GRID_W)
    g1n = row(norm1_g[0])
    bq = row(attn_b_qkv[0])
    q, kk, vv = _qkv_project(xt, mod0, 0, True, g1n, attn_w_qkv[0], bq, qg, kg, rowt, colt, bd, tile)
    _, kkc, vvc = _qkv_project(ctx.reshape(n_ctx, d), mod0, 1, False, g1n, attn_w_qkv[0], bq, qg, kg,
                               rowt[:, :max(n_ctx // GRID_W, 1)], colt, bd, n_ctx)
    o = _attention(q, kk, vv, kkc, vvc, attn_sinks[0])
    wr0, br0 = _router_params(moe_w_router[0], moe_b_router[0])
    x1, xs, pos, gates, cnt = _oproj_router(xt, o, mod0, attn_w_o[0], row(attn_b_o[0]), row(norm2_g[0]),
                                            wr0, br0)
    ys = _moe_experts(xs, cnt.shape[0], *_block_tables(cnt), 0, moe_w1, moe_b1, moe_w2, moe_b2)

    x2, u = _combine_glu(x1, ys, pos, gates, mod0, mod1, row(norm1_g[1]), conv_w_pw1[0], row(conv_b_pw1[0]))
    wr1, br1 = _router_params(moe_w_router[1], moe_b_router[1])
    x3, xs, pos, gates, cnt = _conv_router(x2, u, mod1, conv_w_dw[0], row(conv_b_dw[0]), row(conv_ln_g[0]),
                                           row(conv_ln_b[0]), conv_w_pw2[0], row(conv_b_pw2[0]),
                                           row(norm2_g[1]), wr1, br1)
    ys = _moe_experts(xs, cnt.shape[0], *_block_tables(cnt), 1, moe_w1, moe_b1, moe_w2, moe_b2)
    out = _combine(x3, ys, pos, gates, mod1)
    return out.reshape(1, t, d)
```

```python
import functools

import jax
import jax.numpy as jnp
from jax import lax
from jax.experimental import pallas as pl
from jax.experimental.pallas import tpu as pltpu

F32 = jnp.float32
BF16 = jnp.bfloat16
I32 = jnp.int32

N_HEADS = 16
N_KV_HEADS = 2
HEAD_DIM = 64
GRID_W = 64
WINDOW = 128
Q_BLOCK = 128
ROPE_BASE = 10000.0
CONV_WIDTH = 31
CONV_PAD = CONV_WIDTH // 2
N_EXPERTS = 32
TOP_K = 4
SWIGLU_LIMIT = 7.0
SWIGLU_ALPHA = 1.702
EPS = 1e-6

LANES = 128
SUBLANES = 8
TOKEN_TILE = 512
MOE_TILE = 256
GRANULE = SUBLANES
EXPERT_ROWS = 512
GRANULES_PER_BLOCK = EXPERT_ROWS // GRANULE
REGION_DATA_GRANULES = (MOE_TILE * TOP_K + N_EXPERTS * (GRANULE - 1)) // GRANULE


def _spare_per_region(nt):
    return -(-GRANULES_PER_BLOCK // nt)


def _region_rows(nt):
    rows = (REGION_DATA_GRANULES + 2 * _spare_per_region(nt) + 1) * GRANULE
    return -(-rows // 256) * 256
CONV_HALO = 16
CONV_CHUNK = 32
NEG_BIG = -1e30
VMEM_LIMIT = 56 * 1024 * 1024


def _cparams(n_axes=1, vmem=VMEM_LIMIT):
    return pltpu.CompilerParams(dimension_semantics=("arbitrary",) * n_axes, vmem_limit_bytes=vmem)


def _full(shape):
    nd = len(shape)
    return pl.BlockSpec(shape, lambda *_: (0,) * nd)


def _silu(v):
    return v * jax.nn.sigmoid(v)


def _rms_mod(xv, g, shift, scale):
    ms = jnp.mean(xv * xv, axis=-1, keepdims=True)
    return (xv * lax.rsqrt(ms + EPS)) * g * (1.0 + scale) + shift


def _split_dot(a, b_bf16):
    hi = a.astype(BF16)
    lo = (a - hi.astype(F32)).astype(BF16)
    return (jnp.dot(hi, b_bf16, preferred_element_type=F32)
            + jnp.dot(lo, b_bf16, preferred_element_type=F32))


def _router_dispatch(x_new, g2n, sh2, sc2, wr, br, xs_ref, pos_ref, gate_ref, cnt_ref):
    tt = x_new.shape[0]
    rt = xs_ref.shape[0]
    h2 = _rms_mod(x_new, g2n, sh2, sc2)
    h_hi = h2.astype(BF16)
    h_lo = (h2 - h_hi.astype(F32)).astype(BF16)
    logits = (jnp.dot(h_hi, wr[0], preferred_element_type=F32)
              + (jnp.dot(h_lo, wr[0], preferred_element_type=F32)
                 + jnp.dot(h_hi, wr[1], preferred_element_type=F32))) + br
    lane = lax.broadcasted_iota(I32, logits.shape, 1)
    val_out = jnp.full(logits.shape, NEG_BIG, F32)
    onehot = jnp.zeros(logits.shape, F32)
    sels = []
    cur = logits
    for k in range(TOP_K):
        m = jnp.max(cur, axis=-1, keepdims=True)
        sel = jnp.min(jnp.where(cur == m, lane, LANES), axis=-1, keepdims=True)
        hit = lane == sel
        sels.append(sel)
        val_out = jnp.where(lane == k, m, val_out)
        onehot = jnp.where(hit, 1.0, onehot)
        cur = jnp.where(hit, NEG_BIG * 2.0, cur)
    vmax = jnp.max(val_out, axis=-1, keepdims=True)
    ev = jnp.where(lane < TOP_K, jnp.exp(val_out - vmax), 0.0)
    gate_ref[...] = ev / jnp.sum(ev, axis=-1, keepdims=True)

    ri = lax.broadcasted_iota(I32, (tt, tt), 0)
    ci = lax.broadcasted_iota(I32, (tt, tt), 1)
    earlier = jnp.where(ci < ri, 1.0, 0.0).astype(BF16)
    rank = jnp.dot(earlier, onehot.astype(BF16), preferred_element_type=F32)
    cnt = jnp.sum(onehot, axis=0, keepdims=True).astype(I32)
    c8 = jnp.bitwise_and(cnt + (GRANULE - 1), -GRANULE)
    ue = lax.broadcasted_iota(I32, (LANES, LANES), 0)
    ve = lax.broadcasted_iota(I32, (LANES, LANES), 1)
    c8_rows = jnp.broadcast_to(c8.astype(F32), (SUBLANES, LANES)).astype(BF16)
    before = jnp.where(ue < ve, 1.0, 0.0).astype(BF16)
    off = jnp.dot(c8_rows, before, preferred_element_type=F32)[0:1]
    cnt_ref[0] = jnp.broadcast_to(c8, (SUBLANES, LANES))

    slot_of = off + rank
    pos_out = jnp.zeros(logits.shape, I32)
    r_iota = lax.broadcasted_iota(I32, (tt, rt), 1)
    perm_t = jnp.zeros((tt, rt), F32)
    for k in range(TOP_K):
        pk = jnp.sum(jnp.where(lane == sels[k], slot_of, 0.0), axis=-1, keepdims=True).astype(I32)
        pos_out = jnp.where(lane == k, pk, pos_out)
        perm_t = jnp.where(r_iota == pk, 1.0, perm_t)
    pos_ref[...] = pos_out
    perm = perm_t.T.astype(BF16)
    xs_ref[...] = jnp.dot(perm, h_hi, preferred_element_type=F32)


def _moe_combine(ys_ref, pos_ref, gate_ref):
    tt = pos_ref.shape[0]
    rt = ys_ref.shape[0]
    r_iota = lax.broadcasted_iota(I32, (tt, rt), 1)
    g = jnp.zeros((tt, rt), F32)
    for k in range(TOP_K):
        g = jnp.where(r_iota == pos_ref[:, k:k + 1], gate_ref[:, k:k + 1], g)
    return jnp.dot(g.astype(BF16), ys_ref[...].astype(BF16), preferred_element_type=F32)


def _mod_kernel(cc_ref, w_ref, b_ref, o_ref):
    a = _silu(cc_ref[...])
    o_ref[0] = jnp.dot(a, w_ref[0], precision=lax.Precision.HIGHEST,
                       preferred_element_type=F32) + b_ref[0]


def _mod_vectors(c, c_ctx, w_mod, b_mod):
    depth, d, d6 = w_mod.shape
    cc = jnp.zeros((SUBLANES, d), F32).at[0].set(c[0]).at[1].set(c_ctx)
    ncol = 4
    cw = d6 // ncol
    return pl.pallas_call(
        _mod_kernel,
        grid=(depth, ncol),
        in_specs=[pl.BlockSpec((SUBLANES, d), lambda l, j: (0, 0)),
                  pl.BlockSpec((1, d, cw), lambda l, j: (l, 0, j)),
                  pl.BlockSpec((1, 1, cw), lambda l, j: (l, 0, j))],
        out_specs=pl.BlockSpec((1, SUBLANES, cw), lambda l, j: (l, 0, j)),
        out_shape=jax.ShapeDtypeStruct((depth, SUBLANES, d6), F32),
        compiler_params=_cparams(2),
        name="mod_vectors",
    )(cc, w_mod, b_mod.reshape(depth, 1, d6))


def _qkv_kernel(mod_row, use_rope, x_ref, mod_ref, g_ref, w_ref, b_ref, qg_ref, kg_ref, rowt_ref, colt_ref,
                bd_ref, q_ref, k_ref, v_ref, wb_ref):
    d = x_ref.shape[1]
    tile = x_ref.shape[0]
    nq = N_HEADS * HEAD_DIM
    nk = N_KV_HEADS * HEAD_DIM

    @pl.when(pl.program_id(0) == 0)
    def _():
        wb_ref[...] = w_ref[...].astype(BF16)

    sh = mod_ref[mod_row:mod_row + 1, 0:d]
    sc = mod_ref[mod_row:mod_row + 1, d:2 * d]
    h = _rms_mod(x_ref[...], g_ref[...], sh, sc)
    qkv = jnp.dot(h.astype(BF16), wb_ref[...], preferred_element_type=F32) + b_ref[...]
    q = qkv[:, :nq]
    k = qkv[:, nq:nq + nk]
    v = qkv[:, nq + nk:]
    bd = bd_ref[...]

    def head_norm(t, gain):
        tt = t * t
        w = bd.shape[0]
        if t.shape[1] >= w:
            ss = jnp.concatenate([_split_dot(tt[:, c:c + w], bd) for c in range(0, t.shape[1], w)], axis=1)
        else:
            ss = _split_dot(tt, bd[:t.shape[1], :t.shape[1]])
        return t * lax.rsqrt(ss * (1.0 / HEAD_DIM) + EPS) * gain

    qn = head_norm(q, qg_ref[...])
    kn = head_norm(k, kg_ref[...])

    if use_rope:
        lane = lax.broadcasted_iota(I32, (tile, LANES), 1)
        is_row = (lane % HEAD_DIM) < HEAD_DIM // 2
        nrow = tile // GRID_W

        def table(idx):
            rt = jnp.concatenate([jnp.broadcast_to(rowt_ref[idx, r:r + 1, :], (GRID_W, LANES))
                                  for r in range(nrow)], axis=0)
            ct = jnp.concatenate([colt_ref[idx]] * nrow, axis=0)
            return jnp.where(is_row, rt, ct)

        cosv, sav, sbv = table(0), table(1), table(2)

        def rope(t):
            n = t.shape[1]
            reps = n // LANES
            ct = jnp.concatenate([cosv] * reps, axis=1) if reps > 1 else cosv
            at = jnp.concatenate([sav] * reps, axis=1) if reps > 1 else sav
            bt = jnp.concatenate([sbv] * reps, axis=1) if reps > 1 else sbv
            up = pltpu.roll(t, n - HEAD_DIM // 4, 1)
            dn = pltpu.roll(t, HEAD_DIM // 4, 1)
            return t * ct + up * at + dn * bt

        qn = rope(qn)
        kn = rope(kn)

    q_ref[...] = (qn * (HEAD_DIM ** -0.5)).astype(BF16)
    klane = lax.broadcasted_iota(I32, kn.shape, 1)
    low = klane < HEAD_DIM

    def variants(t):
        sw = pltpu.roll(t, HEAD_DIM, 1)
        z = jnp.zeros_like(t)
        return jnp.concatenate([jnp.where(low, t, z), jnp.where(low, z, sw),
                                jnp.where(low, sw, z), jnp.where(low, z, t)], axis=1)

    k_ref[...] = variants(kn).astype(BF16)
    v_ref[...] = variants(v).astype(BF16)


def _qkv_project(xt, mod_l, mod_row, use_rope, g1n, w_qkv, b_qkv, qg, kg, rowt, colt, bd, tile):
    t, d = xt.shape
    nqkv = w_qkv.shape[1]
    nq = N_HEADS * HEAD_DIM
    tok = lambda i: (i, 0)
    nrow = max(tile // GRID_W, 1)
    return pl.pallas_call(
        functools.partial(_qkv_kernel, mod_row, use_rope),
        grid=(t // tile,),
        in_specs=[pl.BlockSpec((tile, d), tok), _full(mod_l.shape), _full((1, d)),
                  _full((d, nqkv)), _full((1, nqkv)), _full((1, nq)), _full((1, LANES)),
                  pl.BlockSpec((3, nrow, LANES), lambda i: (0, i, 0)), _full(colt.shape), _full(bd.shape)],
        out_specs=[pl.BlockSpec((tile, nq), tok), pl.BlockSpec((tile, 4 * LANES), tok),
                   pl.BlockSpec((tile, 4 * LANES), tok)],
        out_shape=[jax.ShapeDtypeStruct((t, nq), BF16), jax.ShapeDtypeStruct((t, 4 * LANES), BF16),
                   jax.ShapeDtypeStruct((t, 4 * LANES), BF16)],
        scratch_shapes=[pltpu.VMEM((d, nqkv), BF16)],
        compiler_params=_cparams(1),
        name="qkv_project",
    )(xt, mod_l, g1n, w_qkv, b_qkv, qg, kg, rowt, colt, bd)


def _attn_kernel(sink_ref, q_ref, kp_ref, kc_ref, kn_ref, vp_ref, vc_ref, vn_ref, kx_ref, vx_ref, bias_ref,
                 o_ref):
    qb = q_ref.shape[0]
    band = 3 * qb
    pairs_per_group = (N_HEADS // N_KV_HEADS) // 2
    rows = pairs_per_group * qb

    bias = bias_ref[0]
    kband = jnp.concatenate([kp_ref[...], kc_ref[...], kn_ref[...]], axis=0)
    vband = jnp.concatenate([vp_ref[...], vc_ref[...], vn_ref[...]], axis=0)
    kctx = kx_ref[...]
    vctx = vx_ref[...]
    dn_t = (((1,), (1,)), ((), ()))
    rsub = lax.broadcasted_iota(I32, (rows, 1), 0) // qb

    for g in range(N_KV_HEADS):
        qg = jnp.concatenate(
            [q_ref[:, (g * pairs_per_group + j) * LANES:(g * pairs_per_group + j + 1) * LANES]
             for j in range(pairs_per_group)], axis=0)
        acc = jnp.zeros((rows, LANES), F32)
        for par in range(2):
            col = (2 * g + par) * LANES
            s_b = lax.dot_general(qg, kband[:, col:col + LANES], dn_t, preferred_element_type=F32)
            s_c = lax.dot_general(qg, kctx[:, col:col + LANES], dn_t, preferred_element_type=F32)
            s_b = s_b + bias
            sink = jnp.zeros((rows, 1), F32)
            for j in range(pairs_per_group):
                hd = 2 * (g * pairs_per_group + j) + par
                sink = jnp.where(rsub == j, sink_ref[hd], sink)
            m = jnp.maximum(jnp.maximum(jnp.max(s_b, axis=-1, keepdims=True),
                                        jnp.max(s_c, axis=-1, keepdims=True)), sink)
            p_b = jnp.exp(s_b - m)
            p_c = jnp.exp(s_c - m)
            l = (jnp.sum(p_b, axis=-1, keepdims=True) + jnp.sum(p_c, axis=-1, keepdims=True)
                 + jnp.exp(sink - m))
            o = (jnp.dot(p_b.astype(BF16), vband[:, col:col + LANES], preferred_element_type=F32)
                 + jnp.dot(p_c.astype(BF16), vctx[:, col:col + LANES], preferred_element_type=F32))
            acc = acc + o / l
        for j in range(pairs_per_group):
            pcol = (g * pairs_per_group + j) * LANES
            o_ref[:, pcol:pcol + LANES] = acc[j * qb:(j + 1) * qb].astype(BF16)


def _attention(q, kk, vv, kkc, vvc, sinks):
    t, nq = q.shape
    qb = Q_BLOCK
    nb = t // qb
    c = kkc.shape[0]
    w = kk.shape[1]
    cur = lambda i, s: (i, 0)
    prv = lambda i, s: (jnp.maximum(i - 1, 0), 0)
    nxt = lambda i, s: (jnp.minimum(i + 1, nb - 1), 0)
    zero = lambda i, s: (0, 0)
    rows = (N_HEADS // N_KV_HEADS) // 2 * qb
    r = (jnp.arange(rows, dtype=I32) % qb)[:, None]
    cpos = jnp.arange(3 * qb, dtype=I32)[None, :] - qb
    inside = jnp.abs(r - cpos) <= WINDOW
    variants = [inside & ((cpos >= 0) | ((v & 1) == 0)) & ((cpos < qb) | ((v & 2) == 0)) for v in range(4)]
    bias = jnp.where(jnp.stack(variants), 0.0, NEG_BIG).astype(F32)
    which = lambda i, s: (jnp.where(i == 0, 1, 0) + jnp.where(i == nb - 1, 2, 0), 0, 0)
    gs = pltpu.PrefetchScalarGridSpec(
        num_scalar_prefetch=1,
        grid=(nb,),
        in_specs=[pl.BlockSpec((qb, nq), cur),
                  pl.BlockSpec((qb, w), prv), pl.BlockSpec((qb, w), cur), pl.BlockSpec((qb, w), nxt),
                  pl.BlockSpec((qb, w), prv), pl.BlockSpec((qb, w), cur), pl.BlockSpec((qb, w), nxt),
                  pl.BlockSpec((c, w), zero), pl.BlockSpec((c, w), zero),
                  pl.BlockSpec((1, rows, 3 * qb), which)],
        out_specs=pl.BlockSpec((qb, nq), cur),
    )
    return pl.pallas_call(
        _attn_kernel,
        grid_spec=gs,
        out_shape=jax.ShapeDtypeStruct((t, nq), BF16),
        compiler_params=_cparams(1),
        name="window_attention",
    )(sinks, q, kk, kk, kk, vv, vv, vv, kkc, vvc, bias)


def _dispatch_out(t, d, tile):
    nt = t // tile
    tok = lambda i: (i, 0)
    region = _region_rows(nt)
    specs = [pl.BlockSpec((tile, d), tok), pl.BlockSpec((region, d), tok),
             pl.BlockSpec((tile, LANES), tok), pl.BlockSpec((tile, LANES), tok),
             pl.BlockSpec((1, SUBLANES, LANES), lambda i: (i, 0, 0))]
    shapes = [jax.ShapeDtypeStruct((t, d), F32), jax.ShapeDtypeStruct((nt * region, d), F32),
              jax.ShapeDtypeStruct((t, LANES), I32), jax.ShapeDtypeStruct((t, LANES), F32),
              jax.ShapeDtypeStruct((nt, SUBLANES, LANES), I32)]
    return specs, shapes


def _oproj_kernel(x_ref, o_ref, mod_ref, wo_ref, bo_ref, g2_ref, wr_ref, br_ref,
                  x1_ref, xs_ref, pos_ref, gate_ref, cnt_ref, wb_ref):
    d = x_ref.shape[1]

    @pl.when(pl.program_id(0) == 0)
    def _():
        wb_ref[...] = wo_ref[...].astype(BF16)

    y = jnp.dot(o_ref[...], wb_ref[...], preferred_element_type=F32) + bo_ref[...]
    x1 = x_ref[...] + mod_ref[0:1, 2 * d:3 * d] * y
    x1_ref[...] = x1
    _router_dispatch(x1, g2_ref[...], mod_ref[0:1, 3 * d:4 * d], mod_ref[0:1, 4 * d:5 * d],
                     wr_ref[...], br_ref[...], xs_ref, pos_ref, gate_ref, cnt_ref)


def _oproj_router(xt, o, mod_l, w_o, b_o, g2n, wr, br):
    t, d = xt.shape
    tile = MOE_TILE
    tok = lambda i: (i, 0)
    out_specs, out_shape = _dispatch_out(t, d, tile)
    return pl.pallas_call(
        _oproj_kernel,
        grid=(t // tile,),
        in_specs=[pl.BlockSpec((tile, d), tok), pl.BlockSpec((tile, o.shape[1]), tok), _full(mod_l.shape),
                  _full(w_o.shape), _full((1, d)), _full((1, d)), _full(wr.shape), _full(br.shape)],
        out_specs=out_specs,
        out_shape=out_shape,
        scratch_shapes=[pltpu.VMEM(w_o.shape, BF16)],
        compiler_params=_cparams(1),
        name="oproj_router",
    )(xt, o, mod_l, w_o, b_o, g2n, wr, br)


def _moe_kernel(nt, region_granules, bexp_ref, nvg_ref, gsrc_ref, xs_hbm, w1_ref, b1_ref, w2_ref, b2_ref,
                ys_hbm, xbuf, obuf, w1b, w2b, gsem, ssem):
    b = pl.program_id(0)
    nb = pl.num_programs(0)
    gpb = GRANULES_PER_BLOCK
    dff = w2_ref.shape[1]
    slot = b % 2
    spare = _spare_per_region(nt)
    idle_granule = REGION_DATA_GRANULES + 2 * spare

    def rows_of(granule):
        return pl.ds(pl.multiple_of(granule * GRANULE, GRANULE), GRANULE)

    def gather_copy(blk, s, gi):
        g = gsrc_ref[blk * gpb + gi]
        src = jnp.where(g >= 0, g, idle_granule)
        return pltpu.make_async_copy(xs_hbm.at[rows_of(src)], xbuf.at[s, pl.ds(gi * GRANULE, GRANULE)],
                                     gsem.at[s])

    def scatter_copy(blk, s, gi):
        g = jnp.where(blk >= 0, gsrc_ref[jnp.maximum(blk, 0) * gpb + gi], -1)
        own_spare = (gi % nt) * region_granules + REGION_DATA_GRANULES + gi // nt + s * spare
        dst = jnp.where(g >= 0, g, own_spare)
        return pltpu.make_async_copy(obuf.at[s, pl.ds(gi * GRANULE, GRANULE)], ys_hbm.at[rows_of(dst)],
                                     ssem.at[s])

    def wait_gather(s):
        pltpu.make_async_copy(xs_hbm.at[pl.ds(0, EXPERT_ROWS)], xbuf.at[s], gsem.at[s]).wait()

    def wait_scatter(s):
        pltpu.make_async_copy(obuf.at[s], ys_hbm.at[pl.ds(0, EXPERT_ROWS)], ssem.at[s]).wait()

    def start_moves(lo, hi):
        for gi in range(lo, hi):
            gather_copy(b + 1, 1 - slot, gi).start()
            scatter_copy(b - 1, 1 - slot, gi).start()

    nv = nvg_ref[b]
    prev_active = jnp.logical_and(b >= 1, nvg_ref[jnp.maximum(b - 1, 0)] > 0)
    wrote_back = jnp.where(b >= 2, nvg_ref[jnp.maximum(b - 2, 0)] > 0, jnp.logical_and(b == 1, nvg_ref[0] > 0))

    @pl.when(b == 0)
    def _():
        obuf[...] = jnp.zeros(obuf.shape, F32)
        for gi in range(gpb):
            gather_copy(0, 0, gi).start()

    @pl.when(jnp.logical_or(b == 0, prev_active))
    def _():
        wait_gather(slot)

    @pl.when(wrote_back)
    def _():
        wait_scatter(slot)

    @pl.when(jnp.logical_and(nv == 0, prev_active))
    def _():
        for gi in range(gpb):
            scatter_copy(b - 1, 1 - slot, gi).start()

    @pl.when(nv > 0)
    def _():
        changed = jnp.logical_or(b == 0, bexp_ref[b] != bexp_ref[jnp.maximum(b - 1, 0)])

        @pl.when(changed)
        def _():
            w1b[...] = w1_ref[0].astype(BF16)
            w2b[...] = w2_ref[0].astype(BF16)

        def expert_block(rows):
            hf = dff // 2
            per = gpb // 4
            xb = xbuf[slot, 0:rows, :].astype(BF16)

            def glu_half(c, point):
                start_moves(point * per, (point + 1) * per)
                gt = jnp.dot(xb, w1b[:, c * hf:(c + 1) * hf], preferred_element_type=F32)
                gt = jnp.minimum(gt + b1_ref[0, :, c * hf:(c + 1) * hf], SWIGLU_LIMIT)
                start_moves((point + 1) * per, (point + 2) * per)
                up = jnp.dot(xb, w1b[:, dff + c * hf:dff + (c + 1) * hf], preferred_element_type=F32)
                up = jnp.clip(up + b1_ref[0, :, dff + c * hf:dff + (c + 1) * hf], -SWIGLU_LIMIT, SWIGLU_LIMIT)
                act = gt * jax.nn.sigmoid(SWIGLU_ALPHA * gt) * (up + 1.0)
                return jnp.dot(act.astype(BF16), w2b[c * hf:(c + 1) * hf, :], preferred_element_type=F32)

            out = glu_half(0, 0)
            out = out + glu_half(1, 2)
            obuf[slot, 0:rows, :] = out + b2_ref[0]

        half = EXPERT_ROWS // 2

        @pl.when(nv * GRANULE > half)
        def _():
            expert_block(EXPERT_ROWS)

        @pl.when(nv * GRANULE <= half)
        def _():
            expert_block(half)


def _moe_experts(xs, nt, block_exp, n_granules, granule_src, layer, w1, b1, w2, b2):
    d = xs.shape[1]
    n_blocks = block_exp.shape[0]
    dff = w2.shape[2]
    n_layers, n_exp = w1.shape[:2]
    by_expert = lambda b, be, ng, gs: (layer, be[b], 0, 0)
    gs = pltpu.PrefetchScalarGridSpec(
        num_scalar_prefetch=3,
        grid=(n_blocks,),
        in_specs=[pl.BlockSpec(memory_space=pl.ANY),
                  pl.BlockSpec((None, 1, d, 2 * dff), by_expert), pl.BlockSpec((None, 1, 1, 2 * dff), by_expert),
                  pl.BlockSpec((None, 1, dff, d), by_expert), pl.BlockSpec((None, 1, 1, d), by_expert)],
        out_specs=pl.BlockSpec(memory_space=pl.ANY),
        scratch_shapes=[pltpu.VMEM((2, EXPERT_ROWS, d), F32), pltpu.VMEM((2, EXPERT_ROWS, d), F32),
                        pltpu.VMEM((d, 2 * dff), BF16), pltpu.VMEM((dff, d), BF16),
                        pltpu.SemaphoreType.DMA((2,)), pltpu.SemaphoreType.DMA((2,))],
    )
    return pl.pallas_call(
        functools.partial(_moe_kernel, nt, _region_rows(nt) // GRANULE),
        grid_spec=gs,
        out_shape=jax.ShapeDtypeStruct(xs.shape, F32),
        input_output_aliases={3: 0},
        compiler_params=_cparams(1),
        name="moe_experts",
    )(block_exp, n_granules, granule_src, xs, w1, b1.reshape(n_layers, n_exp, 1, 2 * dff),
      w2, b2.reshape(n_layers, n_exp, 1, d))


TABLE_BLOCKS = 8


def _tables_kernel(region_granules, cnt_ref, bexp_ref, ngran_ref, gsrc_ref):
    gpb = GRANULES_PER_BLOCK
    log_b = gpb.bit_length() - 1
    n = cnt_ref.shape[0]
    rows = TABLE_BLOCKS * gpb
    nt_dims = (((1,), (1,)), ((), ()))

    cnt = cnt_ref[...].astype(F32)
    gl_len = cnt * (1.0 / GRANULE)
    ri = lax.broadcasted_iota(I32, (n, n), 0)
    ci = lax.broadcasted_iota(I32, (n, n), 1)
    lower = jnp.where(ci < ri, 1.0, 0.0).astype(BF16)
    upper = jnp.where(ri < ci, 1.0, 0.0).astype(BF16)
    pre = jnp.dot(lower, gl_len.astype(BF16), preferred_element_type=F32)
    off = jnp.dot(cnt.astype(BF16), upper, preferred_element_type=F32)
    src = ri.astype(F32) * float(region_granules) + off * (1.0 / GRANULE)
    n_g = jnp.sum(gl_len, axis=0, keepdims=True)
    blocks = ((n_g.astype(I32) + (gpb - 1)) >> log_b).astype(F32)
    bstart = jnp.dot(jnp.broadcast_to(blocks, (SUBLANES, n)).astype(BF16), upper,
                     preferred_element_type=F32)[0:1]
    bend = bstart + blocks

    def digits(v, base):
        hi = jnp.floor(v * (1.0 / base))
        return hi.astype(BF16), (v - hi * base).astype(BF16)

    def pick(onehot, table):
        return lax.dot_general(onehot, table, nt_dims, preferred_element_type=F32)

    def pick2(onehot, v, base):
        hi, lo = digits(v, base)
        return pick(onehot, hi) * base + pick(onehot, lo)

    q = pl.program_id(0) * rows + lax.broadcasted_iota(I32, (rows, n), 0)
    bq = (q >> log_b).astype(F32)
    iq = (q & (gpb - 1)).astype(F32)
    own = jnp.where((bstart <= bq) & (bq < bend), 1.0, 0.0).astype(BF16)
    pre_q = pick2(own, pre, 64.0)
    len_q = pick(own, gl_len.astype(BF16))
    src_q = pick2(own, src, 128.0)
    bstart_q = pick(own, jnp.broadcast_to(bstart, (n, n)).astype(BF16))
    gl = (bq - bstart_q) * float(gpb) + iq
    inside = (pre_q <= gl) & (gl < pre_q + len_q)
    hit = jnp.sum(jnp.where(inside, src_q - pre_q + gl + 1.0, 0.0), axis=-1, keepdims=True)
    gsrc_ref[...] = jnp.broadcast_to(hit - 1.0, (rows, n)).astype(I32)

    bb = (pl.program_id(0) * TABLE_BLOCKS + lax.broadcasted_iota(I32, (TABLE_BLOCKS, n), 0)).astype(F32)
    lane = lax.broadcasted_iota(I32, (TABLE_BLOCKS, n), 1).astype(F32)
    mine = (bstart <= bb) & (bb < bend)
    red = lambda v: jnp.sum(jnp.where(mine, v, 0.0), axis=-1, keepdims=True)
    active = red(jnp.ones_like(lane))
    n_left = red(jnp.broadcast_to(n_g, mine.shape)) - (bb[:, 0:1] - red(jnp.broadcast_to(bstart, mine.shape))) * gpb
    ngran_ref[...] = jnp.broadcast_to(jnp.clip(n_left, 0.0, float(gpb)) * active, (TABLE_BLOCKS, n)).astype(I32)
    expert = jnp.where(active > 0.0, red(lane), float(N_EXPERTS - 1))
    bexp_ref[...] = jnp.broadcast_to(expert, (TABLE_BLOCKS, n)).astype(I32)


def _block_tables(cnt):
    nt = cnt.shape[0]
    assert nt <= LANES
    max_blocks = (nt * REGION_DATA_GRANULES) // GRANULES_PER_BLOCK + N_EXPERTS
    n_blocks = -(-(max_blocks + 2) // TABLE_BLOCKS) * TABLE_BLOCKS
    gpb = GRANULES_PER_BLOCK
    cnt_sq = jnp.zeros((LANES, LANES), I32).at[:nt].set(cnt[:, 0, :])
    by_step = lambda i: (i, 0)
    bexp, ngran, gsrc = pl.pallas_call(
        functools.partial(_tables_kernel, _region_rows(nt) // GRANULE),
        grid=(n_blocks // TABLE_BLOCKS,),
        in_specs=[_full((LANES, LANES))],
        out_specs=[pl.BlockSpec((TABLE_BLOCKS, LANES), by_step), pl.BlockSpec((TABLE_BLOCKS, LANES), by_step),
                   pl.BlockSpec((TABLE_BLOCKS * gpb, LANES), by_step)],
        out_shape=[jax.ShapeDtypeStruct((n_blocks, LANES), I32), jax.ShapeDtypeStruct((n_blocks, LANES), I32),
                   jax.ShapeDtypeStruct((n_blocks * gpb, LANES), I32)],
        compiler_params=_cparams(1),
        name="moe_block_tables",
    )(cnt_sq)
    return bexp[:, 0], ngran[:, 0], gsrc[:, 0]


def _combine_kernel(x_ref, ys_ref, pos_ref, gate_ref, mod_ref, o_ref):
    d = x_ref.shape[1]
    o_ref[...] = x_ref[...] + mod_ref[0:1, 5 * d:6 * d] * _moe_combine(ys_ref, pos_ref, gate_ref)


def _combine_glu_kernel(x_ref, ys_ref, pos_ref, gate_ref, modp_ref, modn_ref, g_ref, w_ref, b_ref,
                        o_ref, u_ref, wb_ref):
    d = x_ref.shape[1]

    @pl.when(pl.program_id(0) == 0)
    def _():
        wb_ref[...] = w_ref[...].astype(BF16)

    x2 = x_ref[...] + modp_ref[0:1, 5 * d:6 * d] * _moe_combine(ys_ref, pos_ref, gate_ref)
    o_ref[...] = x2
    h = _rms_mod(x2, g_ref[...], modn_ref[0:1, 0:d], modn_ref[0:1, d:2 * d])
    u = jnp.dot(h.astype(BF16), wb_ref[...], preferred_element_type=F32) + b_ref[...]
    u_ref[...] = u[:, :d] * jax.nn.sigmoid(u[:, d:])


def _combine_in_specs(t, d, tile):
    tok = lambda i: (i, 0)
    return [pl.BlockSpec((tile, d), tok), pl.BlockSpec((_region_rows(t // tile), d), tok),
            pl.BlockSpec((tile, LANES), tok), pl.BlockSpec((tile, LANES), tok)]


def _combine(x1, ys, pos, gates, mod_l):
    t, d = x1.shape
    tile = MOE_TILE
    return pl.pallas_call(
        _combine_kernel,
        grid=(t // tile,),
        in_specs=_combine_in_specs(t, d, tile) + [_full(mod_l.shape)],
        out_specs=pl.BlockSpec((tile, d), lambda i: (i, 0)),
        out_shape=jax.ShapeDtypeStruct((t, d), F32),
        compiler_params=_cparams(1),
        name="moe_combine",
    )(x1, ys, pos, gates, mod_l)


def _combine_glu(x1, ys, pos, gates, mod_prev, mod_next, g1n, w_pw1, b_pw1):
    t, d = x1.shape
    tile = MOE_TILE
    tok = lambda i: (i, 0)
    return pl.pallas_call(
        _combine_glu_kernel,
        grid=(t // tile,),
        in_specs=_combine_in_specs(t, d, tile)
                 + [_full(mod_prev.shape), _full(mod_next.shape), _full((1, d)),
                    _full(w_pw1.shape), _full((1, 2 * d))],
        out_specs=[pl.BlockSpec((tile, d), tok), pl.BlockSpec((tile, d), tok)],
        out_shape=[jax.ShapeDtypeStruct((t, d), F32), jax.ShapeDtypeStruct((t, d), F32)],
        scratch_shapes=[pltpu.VMEM(w_pw1.shape, BF16)],
        compiler_params=_cparams(1),
        name="combine_pw1_glu",
    )(x1, ys, pos, gates, mod_prev, mod_next, g1n, w_pw1, b_pw1)


def _conv_kernel(x_ref, up_ref, uc_ref, un_ref, mod_ref, wdw_ref, bdw_ref, lg_ref, lb_ref, w2_ref, b2_ref,
                 g2_ref, wr_ref, br_ref, x1_ref, xs_ref, pos_ref, gate_ref, cnt_ref, wb_ref, ubuf, shbuf, cbuf):
    i = pl.program_id(0)
    n = pl.num_programs(0)
    d = x_ref.shape[1]
    tile = x_ref.shape[0]
    halo = up_ref.shape[0]

    @pl.when(i == 0)
    def _():
        wb_ref[...] = w2_ref[...].astype(BF16)

    ubuf[0:halo, :] = jnp.where(i > 0, up_ref[...], 0.0)
    ubuf[halo:halo + tile, :] = uc_ref[...]
    ubuf[halo + tile:, :] = jnp.where(i < n - 1, un_ref[...], 0.0)

    span = tile + 2 * halo - SUBLANES
    for r in range(SUBLANES):
        shbuf[r, 0:span, :] = ubuf[r:r + span, :]
    for c0 in range(0, tile, CONV_CHUNK):
        part = jnp.zeros((CONV_CHUNK, d), F32) + bdw_ref[...]
        for j in range(CONV_WIDTH):
            a, r = divmod(halo + j - CONV_PAD, SUBLANES)
            lo = c0 + a * SUBLANES
            part = part + shbuf[r, lo:lo + CONV_CHUNK, :] * wdw_ref[j:j + 1, :]
        cbuf[c0:c0 + CONV_CHUNK, :] = part
    acc = cbuf[...]

    mu = jnp.mean(acc, axis=-1, keepdims=True)
    cen = acc - mu
    var = jnp.mean(cen * cen, axis=-1, keepdims=True)
    z = _silu(cen * lax.rsqrt(var + EPS) * lg_ref[...] + lb_ref[...])
    y = jnp.dot(z.astype(BF16), wb_ref[...], preferred_element_type=F32) + b2_ref[...]
    x1 = x_ref[...] + mod_ref[0:1, 2 * d:3 * d] * y
    x1_ref[...] = x1
    _router_dispatch(x1, g2_ref[...], mod_ref[0:1, 3 * d:4 * d], mod_ref[0:1, 4 * d:5 * d],
                     wr_ref[...], br_ref[...], xs_ref, pos_ref, gate_ref, cnt_ref)


def _conv_router(x2, u, mod_l, w_dw, b_dw, ln_g, ln_b, w_pw2, b_pw2, g2n, wr, br):
    t, d = x2.shape
    tile = MOE_TILE
    nt = t // tile
    hb = tile // CONV_HALO
    nh = t // CONV_HALO
    tok = lambda i: (i, 0)
    prv = lambda i: (jnp.maximum(i * hb - 1, 0), 0)
    nxt = lambda i: (jnp.minimum((i + 1) * hb, nh - 1), 0)
    wpad = jnp.zeros((32, d), F32).at[:CONV_WIDTH].set(w_dw)
    out_specs, out_shape = _dispatch_out(t, d, tile)
    return pl.pallas_call(
        _conv_kernel,
        grid=(nt,),
        in_specs=[pl.BlockSpec((tile, d), tok), pl.BlockSpec((CONV_HALO, d), prv),
                  pl.BlockSpec((tile, d), tok), pl.BlockSpec((CONV_HALO, d), nxt),
                  _full(mod_l.shape), _full((32, d)), _full((1, d)), _full((1, d)), _full((1, d)),
                  _full(w_pw2.shape), _full((1, d)), _full((1, d)), _full(wr.shape), _full(br.shape)],
        out_specs=out_specs,
        out_shape=out_shape,
        scratch_shapes=[pltpu.VMEM(w_pw2.shape, BF16), pltpu.VMEM((tile + 2 * CONV_HALO, d), F32),
                        pltpu.VMEM((SUBLANES, tile + 2 * CONV_HALO, d), F32), pltpu.VMEM((tile, d), F32)],
        compiler_params=_cparams(1),
        name="conv_router",
    )(x2, u, u, u, mod_l, wpad, b_dw, ln_g, ln_b, w_pw2, b_pw2, g2n, wr, br)


def _rope_tables(n_rows):
    quarter = HEAD_DIM // 4
    inv = ROPE_BASE ** (-jnp.arange(quarter, dtype=F32) / quarter)

    def tables(npos):
        ang = jnp.arange(npos, dtype=I32).astype(F32)[:, None] * inv[None, :]
        z = jnp.zeros_like(ang)
        c, s = jnp.cos(ang), jnp.sin(ang)
        half = lambda a, b: jnp.concatenate([a, b], axis=1)
        full = lambda h: jnp.concatenate([h, h, h, h], axis=1)
        return jnp.stack([full(half(c, c)), full(half(-s, z)), full(half(z, s))])

    return tables(n_rows), tables(GRID_W)


def _router_params(w_router, b_router):
    d, e = w_router.shape
    wr = jnp.zeros((d, LANES), F32).at[:, :e].set(w_router)
    br = jnp.full((1, LANES), NEG_BIG, F32).at[0, :e].set(b_router)
    hi = wr.astype(BF16)
    lo = (wr - hi.astype(F32)).astype(BF16)
    return jnp.stack([hi, lo]), br


def kernel(x, c, ctx, c_ctx, w_mod, b_mod, norm1_g, norm2_g, attn_w_qkv, attn_b_qkv, attn_w_o, attn_b_o,
           attn_q_norm, attn_k_norm, attn_sinks, conv_w_pw1, conv_b_pw1, conv_w_dw, conv_b_dw, conv_ln_g,
           conv_ln_b, conv_w_pw2, conv_b_pw2, moe_w_router, moe_b_router, moe_w1, moe_b1, moe_w2, moe_b2):
    bsz, t, d = x.shape
    assert bsz == 1 and w_mod.shape[0] == 2
    n_ctx = ctx.shape[1]
    tile = min(TOKEN_TILE, t)
    assert t % tile == 0 and t % Q_BLOCK == 0 and t % MOE_TILE == 0 and tile % GRID_W == 0
    xt = x.reshape(t, d)
    row = lambda v: v.reshape(1, -1)

    mod = _mod_vectors(c, c_ctx, w_mod, b_mod)
    mod0, mod1 = mod[0], mod[1]

    bd = (jnp.arange(2 * LANES)[:, None] // HEAD_DIM == jnp.arange(2 * LANES)[None, :] // HEAD_DIM).astype(BF16)
    qg = row(jnp.tile(attn_q_norm[0], N_HEADS))
    kg = row(jnp.tile(attn_k_norm[0], N_KV_HEADS))
    rowt, colt = _rope_tables(t // GRID_W)
    g1n = row(norm1_g[0])
    bq = row(attn_b_qkv[0])
    q, kk, vv = _qkv_project(xt, mod0, 0, True, g1n, attn_w_qkv[0], bq, qg, kg, rowt, colt, bd, tile)
    _, kkc, vvc = _qkv_project(ctx.reshape(n_ctx, d), mod0, 1, False, g1n, attn_w_qkv[0], bq, qg, kg,
                               rowt[:, :max(n_ctx // GRID_W, 1)], colt, bd, n_ctx)
    o = _attention(q, kk, vv, kkc, vvc, attn_sinks[0])
    wr0, br0 = _router_params(moe_w_router[0], moe_b_router[0])
    x1, xs, pos, gates, cnt = _oproj_router(xt, o, mod0, attn_w_o[0], row(attn_b_o[0]), row(norm2_g[0]),
                                            wr0, br0)
    ys = _moe_experts(xs, cnt.shape[0], *_block_tables(cnt), 0, moe_w1, moe_b1, moe_w2, moe_b2)

    x2, u = _combine_glu(x1, ys, pos, gates, mod0, mod1, row(norm1_g[1]), conv_w_pw1[0], row(conv_b_pw1[0]))
    wr1, br1 = _router_params(moe_w_router[1], moe_b_router[1])
    x3, xs, pos, gates, cnt = _conv_router(x2, u, mod1, conv_w_dw[0], row(conv_b_dw[0]), row(conv_ln_g[0]),
                                           row(conv_ln_b[0]), conv_w_pw2[0], row(conv_b_pw2[0]),
                                           row(norm2_g[1]), wr1, br1)
    ys = _moe_experts(xs, cnt.shape[0], *_block_tables(cnt), 1, moe_w1, moe_b1, moe_w2, moe_b2)
    out = _combine(x3, ys, pos, gates, mod1)
    return out.reshape(1, t, d)
```

```python
import functools

import jax
import jax.numpy as jnp
from jax import lax
from jax.experimental import pallas as pl
from jax.experimental.pallas import tpu as pltpu

F32 = jnp.float32
BF16 = jnp.bfloat16
I32 = jnp.int32

N_HEADS = 16
N_KV_HEADS = 2
HEAD_DIM = 64
GRID_W = 64
WINDOW = 128
Q_BLOCK = 128
ROPE_BASE = 10000.0
CONV_WIDTH = 31
CONV_PAD = CONV_WIDTH // 2
N_EXPERTS = 32
TOP_K = 4
SWIGLU_LIMIT = 7.0
SWIGLU_ALPHA = 1.702
EPS = 1e-6

LANES = 128
SUBLANES = 8
TOKEN_TILE = 512
MOE_TILE = 256
GRANULE = SUBLANES
EXPERT_ROWS = 512
GRANULES_PER_BLOCK = EXPERT_ROWS // GRANULE
REGION_DATA_GRANULES = (MOE_TILE * TOP_K + N_EXPERTS * (GRANULE - 1)) // GRANULE


def _spare_per_region(nt):
    return -(-GRANULES_PER_BLOCK // nt)


def _region_rows(nt):
    rows = (REGION_DATA_GRANULES + 2 * _spare_per_region(nt) + 1) * GRANULE
    return -(-rows // 256) * 256
CONV_HALO = 16
CONV_CHUNK = 32
NEG_BIG = -1e30
VMEM_LIMIT = 56 * 1024 * 1024


def _cparams(n_axes=1, vmem=VMEM_LIMIT):
    return pltpu.CompilerParams(dimension_semantics=("arbitrary",) * n_axes, vmem_limit_bytes=vmem)


def _full(shape):
    nd = len(shape)
    return pl.BlockSpec(shape, lambda *_: (0,) * nd)


def _silu(v):
    return v * jax.nn.sigmoid(v)


def _rms_mod(xv, g, shift, scale):
    ms = jnp.mean(xv * xv, axis=-1, keepdims=True)
    return (xv * lax.rsqrt(ms + EPS)) * g * (1.0 + scale) + shift


def _split_dot(a, b_bf16):
    hi = a.astype(BF16)
    lo = (a - hi.astype(F32)).astype(BF16)
    return (jnp.dot(hi, b_bf16, preferred_element_type=F32)
            + jnp.dot(lo, b_bf16, preferred_element_type=F32))


def _router_dispatch(x_new, g2n, sh2, sc2, wr, br, xs_ref, pos_ref, gate_ref, cnt_ref):
    tt = x_new.shape[0]
    rt = xs_ref.shape[0]
    h2 = _rms_mod(x_new, g2n, sh2, sc2)
    h_hi = h2.astype(BF16)
    h_lo = (h2 - h_hi.astype(F32)).astype(BF16)
    logits = (jnp.dot(h_hi, wr[0], preferred_element_type=F32)
              + (jnp.dot(h_lo, wr[0], preferred_element_type=F32)
                 + jnp.dot(h_hi, wr[1], preferred_element_type=F32))) + br
    lane = lax.broadcasted_iota(I32, logits.shape, 1)
    val_out = jnp.full(logits.shape, NEG_BIG, F32)
    onehot = jnp.zeros(logits.shape, F32)
    sels = []
    cur = logits
    for k in range(TOP_K):
        m = jnp.max(cur, axis=-1, keepdims=True)
        sel = jnp.min(jnp.where(cur == m, lane, LANES), axis=-1, keepdims=True)
        hit = lane == sel
        sels.append(sel)
        val_out = jnp.where(lane == k, m, val_out)
        onehot = jnp.where(hit, 1.0, onehot)
        cur = jnp.where(hit, NEG_BIG * 2.0, cur)
    vmax = jnp.max(val_out, axis=-1, keepdims=True)
    ev = jnp.where(lane < TOP_K, jnp.exp(val_out - vmax), 0.0)
    gate_ref[...] = ev / jnp.sum(ev, axis=-1, keepdims=True)

    ri = lax.broadcasted_iota(I32, (tt, tt), 0)
    ci = lax.broadcasted_iota(I32, (tt, tt), 1)
    earlier = jnp.where(ci < ri, 1.0, 0.0).astype(BF16)
    rank = jnp.dot(earlier, onehot.astype(BF16), preferred_element_type=F32)
    cnt = jnp.sum(onehot, axis=0, keepdims=True).astype(I32)
    c8 = jnp.bitwise_and(cnt + (GRANULE - 1), -GRANULE)
    ue = lax.broadcasted_iota(I32, (LANES, LANES), 0)
    ve = lax.broadcasted_iota(I32, (LANES, LANES), 1)
    c8_rows = jnp.broadcast_to(c8.astype(F32), (SUBLANES, LANES)).astype(BF16)
    before = jnp.where(ue < ve, 1.0, 0.0).astype(BF16)
    off = jnp.dot(c8_rows, before, preferred_element_type=F32)[0:1]
    cnt_ref[0] = jnp.broadcast_to(c8, (SUBLANES, LANES))

    slot_of = off + rank
    pos_out = jnp.zeros(logits.shape, I32)
    r_iota = lax.broadcasted_iota(I32, (tt, rt), 1)
    perm_t = jnp.zeros((tt, rt), F32)
    for k in range(TOP_K):
        pk = jnp.sum(jnp.where(lane == sels[k], slot_of, 0.0), axis=-1, keepdims=True).astype(I32)
        pos_out = jnp.where(lane == k, pk, pos_out)
        perm_t = jnp.where(r_iota == pk, 1.0, perm_t)
    pos_ref[...] = pos_out
    perm = perm_t.T.astype(BF16)
    xs_ref[...] = jnp.dot(perm, h_hi, preferred_element_type=F32)


def _moe_combine(ys_ref, pos_ref, gate_ref):
    tt = pos_ref.shape[0]
    rt = ys_ref.shape[0]
    r_iota = lax.broadcasted_iota(I32, (tt, rt), 1)
    g = jnp.zeros((tt, rt), F32)
    for k in range(TOP_K):
        g = jnp.where(r_iota == pos_ref[:, k:k + 1], gate_ref[:, k:k + 1], g)
    return jnp.dot(g.astype(BF16), ys_ref[...].astype(BF16), preferred_element_type=F32)


def _mod_kernel(cc_ref, w_ref, b_ref, o_ref):
    a = _silu(cc_ref[...])
    o_ref[0] = jnp.dot(a, w_ref[0], precision=lax.Precision.HIGHEST,
                       preferred_element_type=F32) + b_ref[0]


def _mod_vectors(c, c_ctx, w_mod, b_mod):
    depth, d, d6 = w_mod.shape
    cc = jnp.zeros((SUBLANES, d), F32).at[0].set(c[0]).at[1].set(c_ctx)
    ncol = 4
    cw = d6 // ncol
    return pl.pallas_call(
        _mod_kernel,
        grid=(depth, ncol),
        in_specs=[pl.BlockSpec((SUBLANES, d), lambda l, j: (0, 0)),
                  pl.BlockSpec((1, d, cw), lambda l, j: (l, 0, j)),
                  pl.BlockSpec((1, 1, cw), lambda l, j: (l, 0, j))],
        out_specs=pl.BlockSpec((1, SUBLANES, cw), lambda l, j: (l, 0, j)),
        out_shape=jax.ShapeDtypeStruct((depth, SUBLANES, d6), F32),
        compiler_params=_cparams(2),
        name="mod_vectors",
    )(cc, w_mod, b_mod.reshape(depth, 1, d6))


def _qkv_kernel(mod_row, use_rope, x_ref, mod_ref, g_ref, w_ref, b_ref, qg_ref, kg_ref, rowt_ref, colt_ref,
                bd_ref, q_ref, k_ref, v_ref, wb_ref):
    d = x_ref.shape[1]
    tile = x_ref.shape[0]
    nq = N_HEADS * HEAD_DIM
    nk = N_KV_HEADS * HEAD_DIM

    @pl.when(pl.program_id(0) == 0)
    def _():
        wb_ref[...] = w_ref[...].astype(BF16)

    sh = mod_ref[mod_row:mod_row + 1, 0:d]
    sc = mod_ref[mod_row:mod_row + 1, d:2 * d]
    h = _rms_mod(x_ref[...], g_ref[...], sh, sc)
    qkv = jnp.dot(h.astype(BF16), wb_ref[...], preferred_element_type=F32) + b_ref[...]
    q = qkv[:, :nq]
    k = qkv[:, nq:nq + nk]
    v = qkv[:, nq + nk:]
    bd = bd_ref[...]

    def head_norm(t, gain):
        tt = t * t
        w = bd.shape[0]
        if t.shape[1] >= w:
            ss = jnp.concatenate([_split_dot(tt[:, c:c + w], bd) for c in range(0, t.shape[1], w)], axis=1)
        else:
            ss = _split_dot(tt, bd[:t.shape[1], :t.shape[1]])
        return t * lax.rsqrt(ss * (1.0 / HEAD_DIM) + EPS) * gain

    qn = head_norm(q, qg_ref[...])
    kn = head_norm(k, kg_ref[...])

    if use_rope:
        lane = lax.broadcasted_iota(I32, (tile, LANES), 1)
        is_row = (lane % HEAD_DIM) < HEAD_DIM // 2
        nrow = tile // GRID_W

        def table(idx):
            rt = jnp.concatenate([jnp.broadcast_to(rowt_ref[idx, r:r + 1, :], (GRID_W, LANES))
                                  for r in range(nrow)], axis=0)
            ct = jnp.concatenate([colt_ref[idx]] * nrow, axis=0)
            return jnp.where(is_row, rt, ct)

        cosv, sav, sbv = table(0), table(1), table(2)

        def rope(t):
            n = t.shape[1]
            reps = n // LANES
            ct = jnp.concatenate([cosv] * reps, axis=1) if reps > 1 else cosv
            at = jnp.concatenate([sav] * reps, axis=1) if reps > 1 else sav
            bt = jnp.concatenate([sbv] * reps, axis=1) if reps > 1 else sbv
            up = pltpu.roll(t, n - HEAD_DIM // 4, 1)
            dn = pltpu.roll(t, HEAD_DIM // 4, 1)
            return t * ct + up * at + dn * bt

        qn = rope(qn)
        kn = rope(kn)

    q_ref[...] = (qn * (HEAD_DIM ** -0.5)).astype(BF16)
    klane = lax.broadcasted_iota(I32, kn.shape, 1)
    low = klane < HEAD_DIM

    def variants(t):
        sw = pltpu.roll(t, HEAD_DIM, 1)
        z = jnp.zeros_like(t)
        return jnp.concatenate([jnp.where(low, t, z), jnp.where(low, z, sw),
                                jnp.where(low, sw, z), jnp.where(low, z, t)], axis=1)

    k_ref[...] = variants(kn).astype(BF16)
    v_ref[...] = variants(v).astype(BF16)


def _qkv_project(xt, mod_l, mod_row, use_rope, g1n, w_qkv, b_qkv, qg, kg, rowt, colt, bd, tile):
    t, d = xt.shape
    nqkv = w_qkv.shape[1]
    nq = N_HEADS * HEAD_DIM
    tok = lambda i: (i, 0)
    nrow = max(tile // GRID_W, 1)
    return pl.pallas_call(
        functools.partial(_qkv_kernel, mod_row, use_rope),
        grid=(t // tile,),
        in_specs=[pl.BlockSpec((tile, d), tok), _full(mod_l.shape), _full((1, d)),
                  _full((d, nqkv)), _full((1, nqkv)), _full((1, nq)), _full((1, LANES)),
                  pl.BlockSpec((3, nrow, LANES), lambda i: (0, i, 0)), _full(colt.shape), _full(bd.shape)],
        out_specs=[pl.BlockSpec((tile, nq), tok), pl.BlockSpec((tile, 4 * LANES), tok),
                   pl.BlockSpec((tile, 4 * LANES), tok)],
        out_shape=[jax.ShapeDtypeStruct((t, nq), BF16), jax.ShapeDtypeStruct((t, 4 * LANES), BF16),
                   jax.ShapeDtypeStruct((t, 4 * LANES), BF16)],
        scratch_shapes=[pltpu.VMEM((d, nqkv), BF16)],
        compiler_params=_cparams(1),
        name="qkv_project",
    )(xt, mod_l, g1n, w_qkv, b_qkv, qg, kg, rowt, colt, bd)


def _attn_kernel(sink_ref, q_ref, kp_ref, kc_ref, kn_ref, vp_ref, vc_ref, vn_ref, kx_ref, vx_ref, bias_ref,
                 o_ref):
    qb = q_ref.shape[0]
    band = 3 * qb
    pairs_per_group = (N_HEADS // N_KV_HEADS) // 2
    rows = pairs_per_group * qb

    bias = bias_ref[0]
    kband = jnp.concatenate([kp_ref[...], kc_ref[...], kn_ref[...]], axis=0)
    vband = jnp.concatenate([vp_ref[...], vc_ref[...], vn_ref[...]], axis=0)
    kctx = kx_ref[...]
    vctx = vx_ref[...]
    dn_t = (((1,), (1,)), ((), ()))
    rsub = lax.broadcasted_iota(I32, (rows, 1), 0) // qb

    for g in range(N_KV_HEADS):
        qg = jnp.concatenate(
            [q_ref[:, (g * pairs_per_group + j) * LANES:(g * pairs_per_group + j + 1) * LANES]
             for j in range(pairs_per_group)], axis=0)
        acc = jnp.zeros((rows, LANES), F32)
        for par in range(2):
            col = (2 * g + par) * LANES
            s_b = lax.dot_general(qg, kband[:, col:col + LANES], dn_t, preferred_element_type=F32)
            s_c = lax.dot_general(qg, kctx[:, col:col + LANES], dn_t, preferred_element_type=F32)
            s_b = s_b + bias
            sink = jnp.zeros((rows, 1), F32)
            for j in range(pairs_per_group):
                hd = 2 * (g * pairs_per_group + j) + par
                sink = jnp.where(rsub == j, sink_ref[hd], sink)
            m = jnp.maximum(jnp.maximum(jnp.max(s_b, axis=-1, keepdims=True),
                                        jnp.max(s_c, axis=-1, keepdims=True)), sink)
            p_b = jnp.exp(s_b - m)
            p_c = jnp.exp(s_c - m)
            l = (jnp.sum(p_b, axis=-1, keepdims=True) + jnp.sum(p_c, axis=-1, keepdims=True)
                 + jnp.exp(sink - m))
            o = (jnp.dot(p_b.astype(BF16), vband[:, col:col + LANES], preferred_element_type=F32)
                 + jnp.dot(p_c.astype(BF16), vctx[:, col:col + LANES], preferred_element_type=F32))
            acc = acc + o / l
        for j in range(pairs_per_group):
            pcol = (g * pairs_per_group + j) * LANES
            o_ref[:, pcol:pcol + LANES] = acc[j * qb:(j + 1) * qb].astype(BF16)


def _attention(q, kk, vv, kkc, vvc, sinks):
    t, nq = q.shape
    qb = Q_BLOCK
    nb = t // qb
    c = kkc.shape[0]
    w = kk.shape[1]
    cur = lambda i, s: (i, 0)
    prv = lambda i, s: (jnp.maximum(i - 1, 0), 0)
    nxt = lambda i, s: (jnp.minimum(i + 1, nb - 1), 0)
    zero = lambda i, s: (0, 0)
    rows = (N_HEADS // N_KV_HEADS) // 2 * qb
    r = (jnp.arange(rows, dtype=I32) % qb)[:, None]
    cpos = jnp.arange(3 * qb, dtype=I32)[None, :] - qb
    inside = jnp.abs(r - cpos) <= WINDOW
    variants = [inside & ((cpos >= 0) | ((v & 1) == 0)) & ((cpos < qb) | ((v & 2) == 0)) for v in range(4)]
    bias = jnp.where(jnp.stack(variants), 0.0, NEG_BIG).astype(F32)
    which = lambda i, s: (jnp.where(i == 0, 1, 0) + jnp.where(i == nb - 1, 2, 0), 0, 0)
    gs = pltpu.PrefetchScalarGridSpec(
        num_scalar_prefetch=1,
        grid=(nb,),
        in_specs=[pl.BlockSpec((qb, nq), cur),
                  pl.BlockSpec((qb, w), prv), pl.BlockSpec((qb, w), cur), pl.BlockSpec((qb, w), nxt),
                  pl.BlockSpec((qb, w), prv), pl.BlockSpec((qb, w), cur), pl.BlockSpec((qb, w), nxt),
                  pl.BlockSpec((c, w), zero), pl.BlockSpec((c, w), zero),
                  pl.BlockSpec((1, rows, 3 * qb), which)],
        out_specs=pl.BlockSpec((qb, nq), cur),
    )
    return pl.pallas_call(
        _attn_kernel,
        grid_spec=gs,
        out_shape=jax.ShapeDtypeStruct((t, nq), BF16),
        compiler_params=_cparams(1),
        name="window_attention",
    )(sinks, q, kk, kk, kk, vv, vv, vv, kkc, vvc, bias)


def _dispatch_out(t, d, tile):
    nt = t // tile
    tok = lambda i: (i, 0)
    region = _region_rows(nt)
    specs = [pl.BlockSpec((tile, d), tok), pl.BlockSpec((region, d), tok),
             pl.BlockSpec((tile, LANES), tok), pl.BlockSpec((tile, LANES), tok),
             pl.BlockSpec((1, SUBLANES, LANES), lambda i: (i, 0, 0))]
    shapes = [jax.ShapeDtypeStruct((t, d), F32), jax.ShapeDtypeStruct((nt * region, d), F32),
              jax.ShapeDtypeStruct((t, LANES), I32), jax.ShapeDtypeStruct((t, LANES), F32),
              jax.ShapeDtypeStruct((nt, SUBLANES, LANES), I32)]
    return specs, shapes


def _oproj_kernel(x_ref, o_ref, mod_ref, wo_ref, bo_ref, g2_ref, wr_ref, br_ref,
                  x1_ref, xs_ref, pos_ref, gate_ref, cnt_ref, wb_ref):
    d = x_ref.shape[1]

    @pl.when(pl.program_id(0) == 0)
    def _():
        wb_ref[...] = wo_ref[...].astype(BF16)

    y = jnp.dot(o_ref[...], wb_ref[...], preferred_element_type=F32) + bo_ref[...]
    x1 = x_ref[...] + mod_ref[0:1, 2 * d:3 * d] * y
    x1_ref[...] = x1
    _router_dispatch(x1, g2_ref[...], mod_ref[0:1, 3 * d:4 * d], mod_ref[0:1, 4 * d:5 * d],
                     wr_ref[...], br_ref[...], xs_ref, pos_ref, gate_ref, cnt_ref)


def _oproj_router(xt, o, mod_l, w_o, b_o, g2n, wr, br):
    t, d = xt.shape
    tile = MOE_TILE
    tok = lambda i: (i, 0)
    out_specs, out_shape = _dispatch_out(t, d, tile)
    return pl.pallas_call(
        _oproj_kernel,
        grid=(t // tile,),
        in_specs=[pl.BlockSpec((tile, d), tok), pl.BlockSpec((tile, o.shape[1]), tok), _full(mod_l.shape),
                  _full(w_o.shape), _full((1, d)), _full((1, d)), _full(wr.shape), _full(br.shape)],
        out_specs=out_specs,
        out_shape=out_shape,
        scratch_shapes=[pltpu.VMEM(w_o.shape, BF16)],
        compiler_params=_cparams(1),
        name="oproj_router",
    )(xt, o, mod_l, w_o, b_o, g2n, wr, br)


def _moe_kernel(nt, region_granules, bexp_ref, nvg_ref, gsrc_ref, xs_hbm, w1_ref, b1_ref, w2_ref, b2_ref,
                ys_hbm, xbuf, obuf, w1b, w2b, gsem, ssem):
    b = pl.program_id(0)
    nb = pl.num_programs(0)
    gpb = GRANULES_PER_BLOCK
    dff = w2_ref.shape[1]
    slot = b % 2
    spare = _spare_per_region(nt)
    idle_granule = REGION_DATA_GRANULES + 2 * spare

    def rows_of(granule):
        return pl.ds(pl.multiple_of(granule * GRANULE, GRANULE), GRANULE)

    def gather_copy(blk, gi):
        g = gsrc_ref[blk * gpb + gi]
        src = jnp.where(g >= 0, g, idle_granule)
        s = blk % 3
        return pltpu.make_async_copy(xs_hbm.at[rows_of(src)], xbuf.at[s, pl.ds(gi * GRANULE, GRANULE)],
                                     gsem.at[s])

    def scatter_copy(blk, s, gi):
        g = gsrc_ref[blk * gpb + gi]
        own_spare = (gi % nt) * region_granules + REGION_DATA_GRANULES + gi // nt + s * spare
        dst = jnp.where(g >= 0, g, own_spare)
        return pltpu.make_async_copy(obuf.at[s, pl.ds(gi * GRANULE, GRANULE)], ys_hbm.at[rows_of(dst)],
                                     ssem.at[s])

    def start_gather(blk):
        for gi in range(gpb):
            gather_copy(blk, gi).start()

    def wait_gather(s):
        pltpu.make_async_copy(xs_hbm.at[pl.ds(0, EXPERT_ROWS)], xbuf.at[s], gsem.at[s]).wait()

    def wait_scatter(s):
        pltpu.make_async_copy(obuf.at[s], ys_hbm.at[pl.ds(0, EXPERT_ROWS)], ssem.at[s]).wait()

    nv = nvg_ref[b]
    active2 = jnp.logical_and(b >= 2, nvg_ref[jnp.maximum(b - 2, 0)] > 0)

    @pl.when(b == 0)
    def _():
        obuf[...] = jnp.zeros(obuf.shape, F32)
        start_gather(0)
        start_gather(1)

    @pl.when(jnp.logical_or(b <= 1, active2))
    def _():
        wait_gather(b % 3)

    @pl.when(active2)
    def _():
        wait_scatter(slot)

    @pl.when(nv > 0)
    def _():
        changed = jnp.logical_or(b == 0, bexp_ref[b] != bexp_ref[jnp.maximum(b - 1, 0)])

        @pl.when(changed)
        def _():
            w1b[...] = w1_ref[0].astype(BF16)
            w2b[...] = w2_ref[0].astype(BF16)

        def expert_block(rows):
            start_gather(b + 2)
            xb = xbuf[b % 3, 0:rows, :].astype(BF16)
            gu = jnp.dot(xb, w1b[...], preferred_element_type=F32) + b1_ref[0]
            gt = jnp.minimum(gu[:, :dff], SWIGLU_LIMIT)
            up = jnp.clip(gu[:, dff:], -SWIGLU_LIMIT, SWIGLU_LIMIT)
            act = gt * jax.nn.sigmoid(SWIGLU_ALPHA * gt) * (up + 1.0)
            obuf[slot, 0:rows, :] = jnp.dot(act.astype(BF16), w2b[...], preferred_element_type=F32) + b2_ref[0]
            for gi in range(gpb):
                scatter_copy(b, slot, gi).start()

        half = EXPERT_ROWS // 2

        @pl.when(nv * GRANULE > half)
        def _():
            expert_block(EXPERT_ROWS)

        @pl.when(nv * GRANULE <= half)
        def _():
            expert_block(half)


def _moe_experts(xs, nt, block_exp, n_granules, granule_src, layer, w1, b1, w2, b2):
    d = xs.shape[1]
    n_blocks = block_exp.shape[0]
    dff = w2.shape[2]
    n_layers, n_exp = w1.shape[:2]
    by_expert = lambda b, be, ng, gs: (layer, be[b], 0, 0)
    gs = pltpu.PrefetchScalarGridSpec(
        num_scalar_prefetch=3,
        grid=(n_blocks,),
        in_specs=[pl.BlockSpec(memory_space=pl.ANY),
                  pl.BlockSpec((None, 1, d, 2 * dff), by_expert), pl.BlockSpec((None, 1, 1, 2 * dff), by_expert),
                  pl.BlockSpec((None, 1, dff, d), by_expert), pl.BlockSpec((None, 1, 1, d), by_expert)],
        out_specs=pl.BlockSpec(memory_space=pl.ANY),
        scratch_shapes=[pltpu.VMEM((3, EXPERT_ROWS, d), F32), pltpu.VMEM((2, EXPERT_ROWS, d), F32),
                        pltpu.VMEM((d, 2 * dff), BF16), pltpu.VMEM((dff, d), BF16),
                        pltpu.SemaphoreType.DMA((3,)), pltpu.SemaphoreType.DMA((2,))],
    )
    return pl.pallas_call(
        functools.partial(_moe_kernel, nt, _region_rows(nt) // GRANULE),
        grid_spec=gs,
        out_shape=jax.ShapeDtypeStruct(xs.shape, F32),
        input_output_aliases={3: 0},
        compiler_params=_cparams(1),
        name="moe_experts",
    )(block_exp, n_granules, granule_src, xs, w1, b1.reshape(n_layers, n_exp, 1, 2 * dff),
      w2, b2.reshape(n_layers, n_exp, 1, d))


TABLE_BLOCKS = 8


def _tables_kernel(region_granules, cnt_ref, bexp_ref, ngran_ref, gsrc_ref):
    gpb = GRANULES_PER_BLOCK
    log_b = gpb.bit_length() - 1
    n = cnt_ref.shape[0]
    rows = TABLE_BLOCKS * gpb
    nt_dims = (((1,), (1,)), ((), ()))

    cnt = cnt_ref[...].astype(F32)
    gl_len = cnt * (1.0 / GRANULE)
    ri = lax.broadcasted_iota(I32, (n, n), 0)
    ci = lax.broadcasted_iota(I32, (n, n), 1)
    lower = jnp.where(ci < ri, 1.0, 0.0).astype(BF16)
    upper = jnp.where(ri < ci, 1.0, 0.0).astype(BF16)
    pre = jnp.dot(lower, gl_len.astype(BF16), preferred_element_type=F32)
    off = jnp.dot(cnt.astype(BF16), upper, preferred_element_type=F32)
    src = ri.astype(F32) * float(region_granules) + off * (1.0 / GRANULE)
    n_g = jnp.sum(gl_len, axis=0, keepdims=True)
    blocks = ((n_g.astype(I32) + (gpb - 1)) >> log_b).astype(F32)
    bstart = jnp.dot(jnp.broadcast_to(blocks, (SUBLANES, n)).astype(BF16), upper,
                     preferred_element_type=F32)[0:1]
    bend = bstart + blocks

    def digits(v, base):
        hi = jnp.floor(v * (1.0 / base))
        return hi.astype(BF16), (v - hi * base).astype(BF16)

    def pick(onehot, table):
        return lax.dot_general(onehot, table, nt_dims, preferred_element_type=F32)

    def pick2(onehot, v, base):
        hi, lo = digits(v, base)
        return pick(onehot, hi) * base + pick(onehot, lo)

    q = pl.program_id(0) * rows + lax.broadcasted_iota(I32, (rows, n), 0)
    bq = (q >> log_b).astype(F32)
    iq = (q & (gpb - 1)).astype(F32)
    own = jnp.where((bstart <= bq) & (bq < bend), 1.0, 0.0).astype(BF16)
    pre_q = pick2(own, pre, 64.0)
    len_q = pick(own, gl_len.astype(BF16))
    src_q = pick2(own, src, 128.0)
    bstart_q = pick(own, jnp.broadcast_to(bstart, (n, n)).astype(BF16))
    gl = (bq - bstart_q) * float(gpb) + iq
    inside = (pre_q <= gl) & (gl < pre_q + len_q)
    hit = jnp.sum(jnp.where(inside, src_q - pre_q + gl + 1.0, 0.0), axis=-1, keepdims=True)
    gsrc_ref[...] = jnp.broadcast_to(hit - 1.0, (rows, n)).astype(I32)

    bb = (pl.program_id(0) * TABLE_BLOCKS + lax.broadcasted_iota(I32, (TABLE_BLOCKS, n), 0)).astype(F32)
    lane = lax.broadcasted_iota(I32, (TABLE_BLOCKS, n), 1).astype(F32)
    mine = (bstart <= bb) & (bb < bend)
    red = lambda v: jnp.sum(jnp.where(mine, v, 0.0), axis=-1, keepdims=True)
    active = red(jnp.ones_like(lane))
    n_left = red(jnp.broadcast_to(n_g, mine.shape)) - (bb[:, 0:1] - red(jnp.broadcast_to(bstart, mine.shape))) * gpb
    ngran_ref[...] = jnp.broadcast_to(jnp.clip(n_left, 0.0, float(gpb)) * active, (TABLE_BLOCKS, n)).astype(I32)
    expert = jnp.where(active > 0.0, red(lane), float(N_EXPERTS - 1))
    bexp_ref[...] = jnp.broadcast_to(expert, (TABLE_BLOCKS, n)).astype(I32)


def _block_tables(cnt):
    nt = cnt.shape[0]
    assert nt <= LANES
    max_blocks = (nt * REGION_DATA_GRANULES) // GRANULES_PER_BLOCK + N_EXPERTS
    n_blocks = -(-(max_blocks + 2) // TABLE_BLOCKS) * TABLE_BLOCKS
    gpb = GRANULES_PER_BLOCK
    cnt_sq = jnp.zeros((LANES, LANES), I32).at[:nt].set(cnt[:, 0, :])
    by_step = lambda i: (i, 0)
    bexp, ngran, gsrc = pl.pallas_call(
        functools.partial(_tables_kernel, _region_rows(nt) // GRANULE),
        grid=(n_blocks // TABLE_BLOCKS,),
        in_specs=[_full((LANES, LANES))],
        out_specs=[pl.BlockSpec((TABLE_BLOCKS, LANES), by_step), pl.BlockSpec((TABLE_BLOCKS, LANES), by_step),
                   pl.BlockSpec((TABLE_BLOCKS * gpb, LANES), by_step)],
        out_shape=[jax.ShapeDtypeStruct((n_blocks, LANES), I32), jax.ShapeDtypeStruct((n_blocks, LANES), I32),
                   jax.ShapeDtypeStruct((n_blocks * gpb, LANES), I32)],
        compiler_params=_cparams(1),
        name="moe_block_tables",
    )(cnt_sq)
    return bexp[:, 0], ngran[:, 0], gsrc[:, 0]


def _combine_kernel(x_ref, ys_ref, pos_ref, gate_ref, mod_ref, o_ref):
    d = x_ref.shape[1]
    o_ref[...] = x_ref[...] + mod_ref[0:1, 5 * d:6 * d] * _moe_combine(ys_ref, pos_ref, gate_ref)


def _combine_glu_kernel(x_ref, ys_ref, pos_ref, gate_ref, modp_ref, modn_ref, g_ref, w_ref, b_ref,
                        o_ref, u_ref, wb_ref):
    d = x_ref.shape[1]

    @pl.when(pl.program_id(0) == 0)
    def _():
        wb_ref[...] = w_ref[...].astype(BF16)

    x2 = x_ref[...] + modp_ref[0:1, 5 * d:6 * d] * _moe_combine(ys_ref, pos_ref, gate_ref)
    o_ref[...] = x2
    h = _rms_mod(x2, g_ref[...], modn_ref[0:1, 0:d], modn_ref[0:1, d:2 * d])
    u = jnp.dot(h.astype(BF16), wb_ref[...], preferred_element_type=F32) + b_ref[...]
    u_ref[...] = u[:, :d] * jax.nn.sigmoid(u[:, d:])


def _combine_in_specs(t, d, tile):
    tok = lambda i: (i, 0)
    return [pl.BlockSpec((tile, d), tok), pl.BlockSpec((_region_rows(t // tile), d), tok),
            pl.BlockSpec((tile, LANES), tok), pl.BlockSpec((tile, LANES), tok)]


def _combine(x1, ys, pos, gates, mod_l):
    t, d = x1.shape
    tile = MOE_TILE
    return pl.pallas_call(
        _combine_kernel,
        grid=(t // tile,),
        in_specs=_combine_in_specs(t, d, tile) + [_full(mod_l.shape)],
        out_specs=pl.BlockSpec((tile, d), lambda i: (i, 0)),
        out_shape=jax.ShapeDtypeStruct((t, d), F32),
        compiler_params=_cparams(1),
        name="moe_combine",
    )(x1, ys, pos, gates, mod_l)


def _combine_glu(x1, ys, pos, gates, mod_prev, mod_next, g1n, w_pw1, b_pw1):
    t, d = x1.shape
    tile = MOE_TILE
    tok = lambda i: (i, 0)
    return pl.pallas_call(
        _combine_glu_kernel,
        grid=(t // tile,),
        in_specs=_combine_in_specs(t, d, tile)
                 + [_full(mod_prev.shape), _full(mod_next.shape), _full((1, d)),
                    _full(w_pw1.shape), _full((1, 2 * d))],
        out_specs=[pl.BlockSpec((tile, d), tok), pl.BlockSpec((tile, d), tok)],
        out_shape=[jax.ShapeDtypeStruct((t, d), F32), jax.ShapeDtypeStruct((t, d), F32)],
        scratch_shapes=[pltpu.VMEM(w_pw1.shape, BF16)],
        compiler_params=_cparams(1),
        name="combine_pw1_glu",
    )(x1, ys, pos, gates, mod_prev, mod_next, g1n, w_pw1, b_pw1)


def _conv_kernel(x_ref, up_ref, uc_ref, un_ref, mod_ref, wdw_ref, bdw_ref, lg_ref, lb_ref, w2_ref, b2_ref,
                 g2_ref, wr_ref, br_ref, x1_ref, xs_ref, pos_ref, gate_ref, cnt_ref, wb_ref, ubuf, shbuf, cbuf):
    i = pl.program_id(0)
    n = pl.num_programs(0)
    d = x_ref.shape[1]
    tile = x_ref.shape[0]
    halo = up_ref.shape[0]

    @pl.when(i == 0)
    def _():
        wb_ref[...] = w2_ref[...].astype(BF16)

    ubuf[0:halo, :] = jnp.where(i > 0, up_ref[...], 0.0)
    ubuf[halo:halo + tile, :] = uc_ref[...]
    ubuf[halo + tile:, :] = jnp.where(i < n - 1, un_ref[...], 0.0)

    span = tile + 2 * halo - SUBLANES
    for r in range(SUBLANES):
        shbuf[r, 0:span, :] = ubuf[r:r + span, :]
    for c0 in range(0, tile, CONV_CHUNK):
        part = jnp.zeros((CONV_CHUNK, d), F32) + bdw_ref[...]
        for j in range(CONV_WIDTH):
            a, r = divmod(halo + j - CONV_PAD, SUBLANES)
            lo = c0 + a * SUBLANES
            part = part + shbuf[r, lo:lo + CONV_CHUNK, :] * wdw_ref[j:j + 1, :]
        cbuf[c0:c0 + CONV_CHUNK, :] = part
    acc = cbuf[...]

    mu = jnp.mean(acc, axis=-1, keepdims=True)
    cen = acc - mu
    var = jnp.mean(cen * cen, axis=-1, keepdims=True)
    z = _silu(cen * lax.rsqrt(var + EPS) * lg_ref[...] + lb_ref[...])
    y = jnp.dot(z.astype(BF16), wb_ref[...], preferred_element_type=F32) + b2_ref[...]
    x1 = x_ref[...] + mod_ref[0:1, 2 * d:3 * d] * y
    x1_ref[...] = x1
    _router_dispatch(x1, g2_ref[...], mod_ref[0:1, 3 * d:4 * d], mod_ref[0:1, 4 * d:5 * d],
                     wr_ref[...], br_ref[...], xs_ref, pos_ref, gate_ref, cnt_ref)


def _conv_router(x2, u, mod_l, w_dw, b_dw, ln_g, ln_b, w_pw2, b_pw2, g2n, wr, br):
    t, d = x2.shape
    tile = MOE_TILE
    nt = t // tile
    hb = tile // CONV_HALO
    nh = t // CONV_HALO
    tok = lambda i: (i, 0)
    prv = lambda i: (jnp.maximum(i * hb - 1, 0), 0)
    nxt = lambda i: (jnp.minimum((i + 1) * hb, nh - 1), 0)
    wpad = jnp.zeros((32, d), F32).at[:CONV_WIDTH].set(w_dw)
    out_specs, out_shape = _dispatch_out(t, d, tile)
    return pl.pallas_call(
        _conv_kernel,
        grid=(nt,),
        in_specs=[pl.BlockSpec((tile, d), tok), pl.BlockSpec((CONV_HALO, d), prv),
                  pl.BlockSpec((tile, d), tok), pl.BlockSpec((CONV_HALO, d), nxt),
                  _full(mod_l.shape), _full((32, d)), _full((1, d)), _full((1, d)), _full((1, d)),
                  _full(w_pw2.shape), _full((1, d)), _full((1, d)), _full(wr.shape), _full(br.shape)],
        out_specs=out_specs,
        out_shape=out_shape,
        scratch_shapes=[pltpu.VMEM(w_pw2.shape, BF16), pltpu.VMEM((tile + 2 * CONV_HALO, d), F32),
                        pltpu.VMEM((SUBLANES, tile + 2 * CONV_HALO, d), F32), pltpu.VMEM((tile, d), F32)],
        compiler_params=_cparams(1),
        name="conv_router",
    )(x2, u, u, u, mod_l, wpad, b_dw, ln_g, ln_b, w_pw2, b_pw2, g2n, wr, br)


def _rope_tables(n_rows):
    quarter = HEAD_DIM // 4
    inv = ROPE_BASE ** (-jnp.arange(quarter, dtype=F32) / quarter)

    def tables(npos):
        ang = jnp.arange(npos, dtype=I32).astype(F32)[:, None] * inv[None, :]
        z = jnp.zeros_like(ang)
        c, s = jnp.cos(ang), jnp.sin(ang)
        half = lambda a, b: jnp.concatenate([a, b], axis=1)
        full = lambda h: jnp.concatenate([h, h, h, h], axis=1)
        return jnp.stack([full(half(c, c)), full(half(-s, z)), full(half(z, s))])

    return tables(n_rows), tables(GRID_W)


def _router_params(w_router, b_router):
    d, e = w_router.shape
    wr = jnp.zeros((d, LANES), F32).at[:, :e].set(w_router)
    br = jnp.full((1, LANES), NEG_BIG, F32).at[0, :e].set(b_router)
    hi = wr.astype(BF16)
    lo = (wr - hi.astype(F32)).astype(BF16)
    return jnp.stack([hi, lo]), br


def kernel(x, c, ctx, c_ctx, w_mod, b_mod, norm1_g, norm2_g, attn_w_qkv, attn_b_qkv, attn_w_o, attn_b_o,
           attn_q_norm, attn_k_norm, attn_sinks, conv_w_pw1, conv_b_pw1, conv_w_dw, conv_b_dw, conv_ln_g,
           conv_ln_b, conv_w_pw2, conv_b_pw2, moe_w_router, moe_b_router, moe_w1, moe_b1, moe_w2, moe_b2):
    bsz, t, d = x.shape
    assert bsz == 1 and w_mod.shape[0] == 2
    n_ctx = ctx.shape[1]
    tile = min(TOKEN_TILE, t)
    assert t % tile == 0 and t % Q_BLOCK == 0 and t % MOE_TILE == 0 and tile % GRID_W == 0
    xt = x.reshape(t, d)
    row = lambda v: v.reshape(1, -1)

    mod = _mod_vectors(c, c_ctx, w_mod, b_mod)
    mod0, mod1 = mod[0], mod[1]

    bd = (jnp.arange(2 * LANES)[:, None] // HEAD_DIM == jnp.arange(2 * LANES)[None, :] // HEAD_DIM).astype(BF16)
    qg = row(jnp.tile(attn_q_norm[0], N_HEADS))
    kg = row(jnp.tile(attn_k_norm[0], N_KV_HEADS))
    rowt, colt = _rope_tables(t // GRID_W)
    g1n = row(norm1_g[0])
    bq = row(attn_b_qkv[0])
    q, kk, vv = _qkv_project(xt, mod0, 0, True, g1n, attn_w_qkv[0], bq, qg, kg, rowt, colt, bd, tile)
    _, kkc, vvc = _qkv_project(ctx.reshape(n_ctx, d), mod0, 1, False, g1n, attn_w_qkv[0], bq, qg, kg,
                               rowt[:, :max(n_ctx // GRID_W, 1)], colt, bd, n_ctx)
    o = _attention(q, kk, vv, kkc, vvc, attn_sinks[0])
    wr0, br0 = _router_params(moe_w_router[0], moe_b_router[0])
    x1, xs, pos, gates, cnt = _oproj_router(xt, o, mod0, attn_w_o[0], row(attn_b_o[0]), row(norm2_g[0]),
                                            wr0, br0)
    ys = _moe_experts(xs, cnt.shape[0], *_block_tables(cnt), 0, moe_w1, moe_b1, moe_w2, moe_b2)

    x2, u = _combine_glu(x1, ys, pos, gates, mod0, mod1, row(norm1_g[1]), conv_w_pw1[0], row(conv_b_pw1[0]))
    wr1, br1 = _router_params(moe_w_router[1], moe_b_router[1])
    x3, xs, pos, gates, cnt = _conv_router(x2, u, mod1, conv_w_dw[0], row(conv_b_dw[0]), row(conv_ln_g[0]),
                                           row(conv_ln_b[0]), conv_w_pw2[0], row(conv_b_pw2[0]),
                                           row(norm2_g[1]), wr1, br1)
    ys = _moe_experts(xs, cnt.shape[0], *_block_tables(cnt), 1, moe_w1, moe_b1, moe_w2, moe_b2)
    out = _combine(x3, ys, pos, gates, mod1)
    return out.reshape(1, t, d)
```

```python
import functools

import jax
import jax.numpy as jnp
from jax import lax
from jax.experimental import pallas as pl
from jax.experimental.pallas import tpu as pltpu

F32 = jnp.float32
BF16 = jnp.bfloat16
I32 = jnp.int32

N_HEADS = 16
N_KV_HEADS = 2
HEAD_DIM = 64
GRID_W = 64
WINDOW = 128
Q_BLOCK = 128
ROPE_BASE = 10000.0
CONV_WIDTH = 31
CONV_PAD = CONV_WIDTH // 2
N_EXPERTS = 32
TOP_K = 4
SWIGLU_LIMIT = 7.0
SWIGLU_ALPHA = 1.702
EPS = 1e-6

LANES = 128
SUBLANES = 8
TOKEN_TILE = 512
MOE_TILE = 256
GRANULE = SUBLANES
EXPERT_ROWS = 512
GRANULES_PER_BLOCK = EXPERT_ROWS // GRANULE
REGION_DATA_GRANULES = (MOE_TILE * TOP_K + N_EXPERTS * (GRANULE - 1)) // GRANULE


def _spare_per_region(nt):
    return -(-GRANULES_PER_BLOCK // nt)


def _region_rows(nt):
    rows = (REGION_DATA_GRANULES + 2 * _spare_per_region(nt) + 1) * GRANULE
    return -(-rows // 256) * 256
CONV_HALO = 16
CONV_CHUNK = 32
NEG_BIG = -1e30
VMEM_LIMIT = 56 * 1024 * 1024


def _cparams(n_axes=1, vmem=VMEM_LIMIT):
    return pltpu.CompilerParams(dimension_semantics=("arbitrary",) * n_axes, vmem_limit_bytes=vmem)


def _full(shape):
    nd = len(shape)
    return pl.BlockSpec(shape, lambda *_: (0,) * nd)


def _silu(v):
    return v * jax.nn.sigmoid(v)


def _rms_mod(xv, g, shift, scale):
    ms = jnp.mean(xv * xv, axis=-1, keepdims=True)
    return (xv * lax.rsqrt(ms + EPS)) * g * (1.0 + scale) + shift


def _split_dot(a, b_bf16):
    hi = a.astype(BF16)
    lo = (a - hi.astype(F32)).astype(BF16)
    return (jnp.dot(hi, b_bf16, preferred_element_type=F32)
            + jnp.dot(lo, b_bf16, preferred_element_type=F32))


def _router_dispatch(x_new, g2n, sh2, sc2, wr, br, xs_ref, pos_ref, gate_ref, cnt_ref):
    tt = x_new.shape[0]
    rt = xs_ref.shape[0]
    h2 = _rms_mod(x_new, g2n, sh2, sc2)
    h_hi = h2.astype(BF16)
    h_lo = (h2 - h_hi.astype(F32)).astype(BF16)
    logits = (jnp.dot(h_hi, wr[0], preferred_element_type=F32)
              + (jnp.dot(h_lo, wr[0], preferred_element_type=F32)
                 + jnp.dot(h_hi, wr[1], preferred_element_type=F32))) + br
    lane = lax.broadcasted_iota(I32, logits.shape, 1)
    val_out = jnp.full(logits.shape, NEG_BIG, F32)
    onehot = jnp.zeros(logits.shape, F32)
    sels = []
    cur = logits
    for k in range(TOP_K):
        m = jnp.max(cur, axis=-1, keepdims=True)
        sel = jnp.min(jnp.where(cur == m, lane, LANES), axis=-1, keepdims=True)
        hit = lane == sel
        sels.append(sel)
        val_out = jnp.where(lane == k, m, val_out)
        onehot = jnp.where(hit, 1.0, onehot)
        cur = jnp.where(hit, NEG_BIG * 2.0, cur)
    vmax = jnp.max(val_out, axis=-1, keepdims=True)
    ev = jnp.where(lane < TOP_K, jnp.exp(val_out - vmax), 0.0)
    gate_ref[...] = ev / jnp.sum(ev, axis=-1, keepdims=True)

    ri = lax.broadcasted_iota(I32, (tt, tt), 0)
    ci = lax.broadcasted_iota(I32, (tt, tt), 1)
    earlier = jnp.where(ci < ri, 1.0, 0.0).astype(BF16)
    rank = jnp.dot(earlier, onehot.astype(BF16), preferred_element_type=F32)
    cnt = jnp.sum(onehot, axis=0, keepdims=True).astype(I32)
    c8 = jnp.bitwise_and(cnt + (GRANULE - 1), -GRANULE)
    ue = lax.broadcasted_iota(I32, (LANES, LANES), 0)
    ve = lax.broadcasted_iota(I32, (LANES, LANES), 1)
    c8_rows = jnp.broadcast_to(c8.astype(F32), (SUBLANES, LANES)).astype(BF16)
    before = jnp.where(ue < ve, 1.0, 0.0).astype(BF16)
    off = jnp.dot(c8_rows, before, preferred_element_type=F32)[0:1]
    cnt_ref[0] = jnp.broadcast_to(c8, (SUBLANES, LANES))

    slot_of = off + rank
    pos_out = jnp.zeros(logits.shape, I32)
    r_iota = lax.broadcasted_iota(I32, (tt, rt), 1)
    perm_t = jnp.zeros((tt, rt), F32)
    for k in range(TOP_K):
        pk = jnp.sum(jnp.where(lane == sels[k], slot_of, 0.0), axis=-1, keepdims=True).astype(I32)
        pos_out = jnp.where(lane == k, pk, pos_out)
        perm_t = jnp.where(r_iota == pk, 1.0, perm_t)
    pos_ref[...] = pos_out
    perm = perm_t.T.astype(BF16)
    xs_ref[...] = jnp.dot(perm, h_hi, preferred_element_type=F32)


def _moe_combine(ys_ref, pos_ref, gate_ref):
    tt = pos_ref.shape[0]
    rt = ys_ref.shape[0]
    r_iota = lax.broadcasted_iota(I32, (tt, rt), 1)
    g = jnp.zeros((tt, rt), F32)
    for k in range(TOP_K):
        g = jnp.where(r_iota == pos_ref[:, k:k + 1], gate_ref[:, k:k + 1], g)
    return jnp.dot(g.astype(BF16), ys_ref[...].astype(BF16), preferred_element_type=F32)


def _mod_kernel(cc_ref, w_ref, b_ref, o_ref):
    a = _silu(cc_ref[...])
    o_ref[0] = jnp.dot(a, w_ref[0], precision=lax.Precision.HIGHEST,
                       preferred_element_type=F32) + b_ref[0]


def _mod_vectors(c, c_ctx, w_mod, b_mod):
    depth, d, d6 = w_mod.shape
    cc = jnp.zeros((SUBLANES, d), F32).at[0].set(c[0]).at[1].set(c_ctx)
    ncol = 4
    cw = d6 // ncol
    return pl.pallas_call(
        _mod_kernel,
        grid=(depth, ncol),
        in_specs=[pl.BlockSpec((SUBLANES, d), lambda l, j: (0, 0)),
                  pl.BlockSpec((1, d, cw), lambda l, j: (l, 0, j)),
                  pl.BlockSpec((1, 1, cw), lambda l, j: (l, 0, j))],
        out_specs=pl.BlockSpec((1, SUBLANES, cw), lambda l, j: (l, 0, j)),
        out_shape=jax.ShapeDtypeStruct((depth, SUBLANES, d6), F32),
        compiler_params=_cparams(2),
        name="mod_vectors",
    )(cc, w_mod, b_mod.reshape(depth, 1, d6))


def _qkv_kernel(mod_row, use_rope, x_ref, mod_ref, g_ref, w_ref, b_ref, qg_ref, kg_ref, rowt_ref, colt_ref,
                bd_ref, q_ref, k_ref, v_ref, wb_ref):
    d = x_ref.shape[1]
    tile = x_ref.shape[0]
    nq = N_HEADS * HEAD_DIM
    nk = N_KV_HEADS * HEAD_DIM

    @pl.when(pl.program_id(0) == 0)
    def _():
        wb_ref[...] = w_ref[...].astype(BF16)

    sh = mod_ref[mod_row:mod_row + 1, 0:d]
    sc = mod_ref[mod_row:mod_row + 1, d:2 * d]
    h = _rms_mod(x_ref[...], g_ref[...], sh, sc)
    qkv = jnp.dot(h.astype(BF16), wb_ref[...], preferred_element_type=F32) + b_ref[...]
    q = qkv[:, :nq]
    k = qkv[:, nq:nq + nk]
    v = qkv[:, nq + nk:]
    bd = bd_ref[...]

    def head_norm(t, gain):
        tt = t * t
        w = bd.shape[0]
        if t.shape[1] >= w:
            ss = jnp.concatenate([_split_dot(tt[:, c:c + w], bd) for c in range(0, t.shape[1], w)], axis=1)
        else:
            ss = _split_dot(tt, bd[:t.shape[1], :t.shape[1]])
        return t * lax.rsqrt(ss * (1.0 / HEAD_DIM) + EPS) * gain

    qn = head_norm(q, qg_ref[...])
    kn = head_norm(k, kg_ref[...])

    if use_rope:
        lane = lax.broadcasted_iota(I32, (tile, LANES), 1)
        is_row = (lane % HEAD_DIM) < HEAD_DIM // 2
        nrow = tile // GRID_W

        def table(idx):
            rt = jnp.concatenate([jnp.broadcast_to(rowt_ref[idx, r:r + 1, :], (GRID_W, LANES))
                                  for r in range(nrow)], axis=0)
            ct = jnp.concatenate([colt_ref[idx]] * nrow, axis=0)
            return jnp.where(is_row, rt, ct)

        cosv, sav, sbv = table(0), table(1), table(2)

        def rope(t):
            n = t.shape[1]
            reps = n // LANES
            ct = jnp.concatenate([cosv] * reps, axis=1) if reps > 1 else cosv
            at = jnp.concatenate([sav] * reps, axis=1) if reps > 1 else sav
            bt = jnp.concatenate([sbv] * reps, axis=1) if reps > 1 else sbv
            up = pltpu.roll(t, n - HEAD_DIM // 4, 1)
            dn = pltpu.roll(t, HEAD_DIM // 4, 1)
            return t * ct + up * at + dn * bt

        qn = rope(qn)
        kn = rope(kn)

    q_ref[...] = (qn * (HEAD_DIM ** -0.5)).astype(BF16)
    klane = lax.broadcasted_iota(I32, kn.shape, 1)
    low = klane < HEAD_DIM

    def variants(t):
        sw = pltpu.roll(t, HEAD_DIM, 1)
        z = jnp.zeros_like(t)
        return jnp.concatenate([jnp.where(low, t, z), jnp.where(low, z, sw),
                                jnp.where(low, sw, z), jnp.where(low, z, t)], axis=1)

    k_ref[...] = variants(kn).astype(BF16)
    v_ref[...] = variants(v).astype(BF16)


def _qkv_project(xt, mod_l, mod_row, use_rope, g1n, w_qkv, b_qkv, qg, kg, rowt, colt, bd, tile):
    t, d = xt.shape
    nqkv = w_qkv.shape[1]
    nq = N_HEADS * HEAD_DIM
    tok = lambda i: (i, 0)
    nrow = max(tile // GRID_W, 1)
    return pl.pallas_call(
        functools.partial(_qkv_kernel, mod_row, use_rope),
        grid=(t // tile,),
        in_specs=[pl.BlockSpec((tile, d), tok), _full(mod_l.shape), _full((1, d)),
                  _full((d, nqkv)), _full((1, nqkv)), _full((1, nq)), _full((1, LANES)),
                  pl.BlockSpec((3, nrow, LANES), lambda i: (0, i, 0)), _full(colt.shape), _full(bd.shape)],
        out_specs=[pl.BlockSpec((tile, nq), tok), pl.BlockSpec((tile, 4 * LANES), tok),
                   pl.BlockSpec((tile, 4 * LANES), tok)],
        out_shape=[jax.ShapeDtypeStruct((t, nq), BF16), jax.ShapeDtypeStruct((t, 4 * LANES), BF16),
                   jax.ShapeDtypeStruct((t, 4 * LANES), BF16)],
        scratch_shapes=[pltpu.VMEM((d, nqkv), BF16)],
        compiler_params=_cparams(1),
        name="qkv_project",
    )(xt, mod_l, g1n, w_qkv, b_qkv, qg, kg, rowt, colt, bd)


def _attn_kernel(sink_ref, q_ref, kp_ref, kc_ref, kn_ref, vp_ref, vc_ref, vn_ref, kx_ref, vx_ref, bias_ref,
                 o_ref):
    qb = q_ref.shape[0]
    band = 3 * qb
    pairs_per_group = (N_HEADS // N_KV_HEADS) // 2
    rows = pairs_per_group * qb

    bias = bias_ref[0]
    kband = jnp.concatenate([kp_ref[...], kc_ref[...], kn_ref[...]], axis=0)
    vband = jnp.concatenate([vp_ref[...], vc_ref[...], vn_ref[...]], axis=0)
    kctx = kx_ref[...]
    vctx = vx_ref[...]
    dn_t = (((1,), (1,)), ((), ()))
    rsub = lax.broadcasted_iota(I32, (rows, 1), 0) // qb

    for g in range(N_KV_HEADS):
        qg = jnp.concatenate(
            [q_ref[:, (g * pairs_per_group + j) * LANES:(g * pairs_per_group + j + 1) * LANES]
             for j in range(pairs_per_group)], axis=0)
        acc = jnp.zeros((rows, LANES), F32)
        for par in range(2):
            col = (2 * g + par) * LANES
            s_b = lax.dot_general(qg, kband[:, col:col + LANES], dn_t, preferred_element_type=F32)
            s_c = lax.dot_general(qg, kctx[:, col:col + LANES], dn_t, preferred_element_type=F32)
            s_b = s_b + bias
            sink = jnp.zeros((rows, 1), F32)
            for j in range(pairs_per_group):
                hd = 2 * (g * pairs_per_group + j) + par
                sink = jnp.where(rsub == j, sink_ref[hd], sink)
            m = jnp.maximum(jnp.maximum(jnp.max(s_b, axis=-1, keepdims=True),
                                        jnp.max(s_c, axis=-1, keepdims=True)), sink)
            p_b = jnp.exp(s_b - m)
            p_c = jnp.exp(s_c - m)
            l = (jnp.sum(p_b, axis=-1, keepdims=True) + jnp.sum(p_c, axis=-1, keepdims=True)
                 + jnp.exp(sink - m))
            o = (jnp.dot(p_b.astype(BF16), vband[:, col:col + LANES], preferred_element_type=F32)
                 + jnp.dot(p_c.astype(BF16), vctx[:, col:col + LANES], preferred_element_type=F32))
            acc = acc + o / l
        for j in range(pairs_per_group):
            pcol = (g * pairs_per_group + j) * LANES
            o_ref[:, pcol:pcol + LANES] = acc[j * qb:(j + 1) * qb].astype(BF16)


def _attention(q, kk, vv, kkc, vvc, sinks):
    t, nq = q.shape
    qb = Q_BLOCK
    nb = t // qb
    c = kkc.shape[0]
    w = kk.shape[1]
    cur = lambda i, s: (i, 0)
    prv = lambda i, s: (jnp.maximum(i - 1, 0), 0)
    nxt = lambda i, s: (jnp.minimum(i + 1, nb - 1), 0)
    zero = lambda i, s: (0, 0)
    rows = (N_HEADS // N_KV_HEADS) // 2 * qb
    r = (jnp.arange(rows, dtype=I32) % qb)[:, None]
    cpos = jnp.arange(3 * qb, dtype=I32)[None, :] - qb
    inside = jnp.abs(r - cpos) <= WINDOW
    variants = [inside & ((cpos >= 0) | ((v & 1) == 0)) & ((cpos < qb) | ((v & 2) == 0)) for v in range(4)]
    bias = jnp.where(jnp.stack(variants), 0.0, NEG_BIG).astype(F32)
    which = lambda i, s: (jnp.where(i == 0, 1, 0) + jnp.where(i == nb - 1, 2, 0), 0, 0)
    gs = pltpu.PrefetchScalarGridSpec(
        num_scalar_prefetch=1,
        grid=(nb,),
        in_specs=[pl.BlockSpec((qb, nq), cur),
                  pl.BlockSpec((qb, w), prv), pl.BlockSpec((qb, w), cur), pl.BlockSpec((qb, w), nxt),
                  pl.BlockSpec((qb, w), prv), pl.BlockSpec((qb, w), cur), pl.BlockSpec((qb, w), nxt),
                  pl.BlockSpec((c, w), zero), pl.BlockSpec((c, w), zero),
                  pl.BlockSpec((1, rows, 3 * qb), which)],
        out_specs=pl.BlockSpec((qb, nq), cur),
    )
    return pl.pallas_call(
        _attn_kernel,
        grid_spec=gs,
        out_shape=jax.ShapeDtypeStruct((t, nq), BF16),
        compiler_params=_cparams(1),
        name="window_attention",
    )(sinks, q, kk, kk, kk, vv, vv, vv, kkc, vvc, bias)


def _dispatch_out(t, d, tile):
    nt = t // tile
    tok = lambda i: (i, 0)
    region = _region_rows(nt)
    specs = [pl.BlockSpec((tile, d), tok), pl.BlockSpec((region, d), tok),
             pl.BlockSpec((tile, LANES), tok), pl.BlockSpec((tile, LANES), tok),
             pl.BlockSpec((1, SUBLANES, LANES), lambda i: (i, 0, 0))]
    shapes = [jax.ShapeDtypeStruct((t, d), F32), jax.ShapeDtypeStruct((nt * region, d), F32),
              jax.ShapeDtypeStruct((t, LANES), I32), jax.ShapeDtypeStruct((t, LANES), F32),
              jax.ShapeDtypeStruct((nt, SUBLANES, LANES), I32)]
    return specs, shapes


def _oproj_kernel(x_ref, o_ref, mod_ref, wo_ref, bo_ref, g2_ref, wr_ref, br_ref,
                  x1_ref, xs_ref, pos_ref, gate_ref, cnt_ref, wb_ref):
    d = x_ref.shape[1]

    @pl.when(pl.program_id(0) == 0)
    def _():
        wb_ref[...] = wo_ref[...].astype(BF16)

    y = jnp.dot(o_ref[...], wb_ref[...], preferred_element_type=F32) + bo_ref[...]
    x1 = x_ref[...] + mod_ref[0:1, 2 * d:3 * d] * y
    x1_ref[...] = x1
    _router_dispatch(x1, g2_ref[...], mod_ref[0:1, 3 * d:4 * d], mod_ref[0:1, 4 * d:5 * d],
                     wr_ref[...], br_ref[...], xs_ref, pos_ref, gate_ref, cnt_ref)


def _oproj_router(xt, o, mod_l, w_o, b_o, g2n, wr, br):
    t, d = xt.shape
    tile = MOE_TILE
    tok = lambda i: (i, 0)
    out_specs, out_shape = _dispatch_out(t, d, tile)
    return pl.pallas_call(
        _oproj_kernel,
        grid=(t // tile,),
        in_specs=[pl.BlockSpec((tile, d), tok), pl.BlockSpec((tile, o.shape[1]), tok), _full(mod_l.shape),
                  _full(w_o.shape), _full((1, d)), _full((1, d)), _full(wr.shape), _full(br.shape)],
        out_specs=out_specs,
        out_shape=out_shape,
        scratch_shapes=[pltpu.VMEM(w_o.shape, BF16)],
        compiler_params=_cparams(1),
        name="oproj_router",
    )(xt, o, mod_l, w_o, b_o, g2n, wr, br)


def _moe_kernel(nt, region_granules, bexp_ref, nvg_ref, gsrc_ref, xs_hbm, w1_ref, b1_ref, w2_ref, b2_ref,
                ys_hbm, xbuf, obuf, w1b, w2b, gsem, ssem):
    b = pl.program_id(0)
    nb = pl.num_programs(0)
    gpb = GRANULES_PER_BLOCK
    dff = w2_ref.shape[1]
    slot = b % 2
    spare = _spare_per_region(nt)
    idle_granule = REGION_DATA_GRANULES + 2 * spare

    def rows_of(granule):
        return pl.ds(pl.multiple_of(granule * GRANULE, GRANULE), GRANULE)

    def gather_copy(blk, gi):
        g = gsrc_ref[blk * gpb + gi]
        src = jnp.where(g >= 0, g, idle_granule)
        s = blk % 3
        return pltpu.make_async_copy(xs_hbm.at[rows_of(src)], xbuf.at[s, pl.ds(gi * GRANULE, GRANULE)],
                                     gsem.at[s])

    def scatter_copy(blk, s, gi):
        g = gsrc_ref[blk * gpb + gi]
        own_spare = (gi % nt) * region_granules + REGION_DATA_GRANULES + gi // nt + s * spare
        dst = jnp.where(g >= 0, g, own_spare)
        return pltpu.make_async_copy(obuf.at[s, pl.ds(gi * GRANULE, GRANULE)], ys_hbm.at[rows_of(dst)],
                                     ssem.at[s])

    def start_gather(blk):
        for gi in range(gpb):
            gather_copy(blk, gi).start()

    def wait_gather(s):
        pltpu.make_async_copy(xs_hbm.at[pl.ds(0, EXPERT_ROWS)], xbuf.at[s], gsem.at[s]).wait()

    def wait_scatter(s):
        pltpu.make_async_copy(obuf.at[s], ys_hbm.at[pl.ds(0, EXPERT_ROWS)], ssem.at[s]).wait()

    nv = nvg_ref[b]
    active2 = jnp.logical_and(b >= 2, nvg_ref[jnp.maximum(b - 2, 0)] > 0)

    @pl.when(b == 0)
    def _():
        obuf[...] = jnp.zeros(obuf.shape, F32)
        start_gather(0)
        start_gather(1)

    @pl.when(jnp.logical_or(b <= 1, active2))
    def _():
        wait_gather(b % 3)

    @pl.when(active2)
    def _():
        wait_scatter(slot)

    @pl.when(nv > 0)
    def _():
        changed = jnp.logical_or(b == 0, bexp_ref[b] != bexp_ref[jnp.maximum(b - 1, 0)])

        @pl.when(changed)
        def _():
            w1b[...] = w1_ref[0].astype(BF16)
            w2b[...] = w2_ref[0].astype(BF16)

        def expert_block(rows):
            start_gather(b + 2)
            xb = xbuf[b % 3, 0:rows, :].astype(BF16)
            gu = jnp.dot(xb, w1b[...], preferred_element_type=F32) + b1_ref[0]
            gt = jnp.minimum(gu[:, :dff], SWIGLU_LIMIT)
            up = jnp.clip(gu[:, dff:], -SWIGLU_LIMIT, SWIGLU_LIMIT)
            act = gt * jax.nn.sigmoid(SWIGLU_ALPHA * gt) * (up + 1.0)
            obuf[slot, 0:rows, :] = jnp.dot(act.astype(BF16), w2b[...], preferred_element_type=F32) + b2_ref[0]
            for gi in range(gpb):
                scatter_copy(b, slot, gi).start()

        half = EXPERT_ROWS // 2

        @pl.when(nv * GRANULE > half)
        def _():
            expert_block(EXPERT_ROWS)

        @pl.when(nv * GRANULE <= half)
        def _():
            expert_block(half)


def _moe_experts(xs, nt, block_exp, n_granules, granule_src, layer, w1, b1, w2, b2):
    d = xs.shape[1]
    n_blocks = block_exp.shape[0]
    dff = w2.shape[2]
    n_layers, n_exp = w1.shape[:2]
    by_expert = lambda b, be, ng, gs: (layer, be[b], 0, 0)
    gs = pltpu.PrefetchScalarGridSpec(
        num_scalar_prefetch=3,
        grid=(n_blocks,),
        in_specs=[pl.BlockSpec(memory_space=pl.ANY),
                  pl.BlockSpec((None, 1, d, 2 * dff), by_expert), pl.BlockSpec((None, 1, 1, 2 * dff), by_expert),
                  pl.BlockSpec((None, 1, dff, d), by_expert), pl.BlockSpec((None, 1, 1, d), by_expert)],
        out_specs=pl.BlockSpec(memory_space=pl.ANY),
        scratch_shapes=[pltpu.VMEM((3, EXPERT_ROWS, d), F32), pltpu.VMEM((2, EXPERT_ROWS, d), F32),
                        pltpu.VMEM((d, 2 * dff), BF16), pltpu.VMEM((dff, d), BF16),
                        pltpu.SemaphoreType.DMA((3,)), pltpu.SemaphoreType.DMA((2,))],
    )
    return pl.pallas_call(
        functools.partial(_moe_kernel, nt, _region_rows(nt) // GRANULE),
        grid_spec=gs,
        out_shape=jax.ShapeDtypeStruct(xs.shape, F32),
        input_output_aliases={3: 0},
        compiler_params=_cparams(1),
        name="moe_experts",
    )(block_exp, n_granules, granule_src, xs, w1, b1.reshape(n_layers, n_exp, 1, 2 * dff),
      w2, b2.reshape(n_layers, n_exp, 1, d))


TABLE_BLOCKS = 8


def _tables_kernel(region_granules, cnt_ref, bexp_ref, ngran_ref, gsrc_ref):
    gpb = GRANULES_PER_BLOCK
    log_b = gpb.bit_length() - 1
    n = cnt_ref.shape[0]
    rows = TABLE_BLOCKS * gpb
    nt_dims = (((1,), (1,)), ((), ()))

    cnt = cnt_ref[...].astype(F32)
    gl_len = cnt * (1.0 / GRANULE)
    ri = lax.broadcasted_iota(I32, (n, n), 0)
    ci = lax.broadcasted_iota(I32, (n, n), 1)
    lower = jnp.where(ci < ri, 1.0, 0.0).astype(BF16)
    upper = jnp.where(ri < ci, 1.0, 0.0).astype(BF16)
    pre = jnp.dot(lower, gl_len.astype(BF16), preferred_element_type=F32)
    off = jnp.dot(cnt.astype(BF16), upper, preferred_element_type=F32)
    src = ri.astype(F32) * float(region_granules) + off * (1.0 / GRANULE)
    n_g = jnp.sum(gl_len, axis=0, keepdims=True)
    blocks = ((n_g.astype(I32) + (gpb - 1)) >> log_b).astype(F32)
    bstart = jnp.dot(jnp.broadcast_to(blocks, (SUBLANES, n)).astype(BF16), upper,
                     preferred_element_type=F32)[0:1]
    bend = bstart + blocks

    def digits(v, base):
        hi = jnp.floor(v * (1.0 / base))
        return hi.astype(BF16), (v - hi * base).astype(BF16)

    def pick(onehot, table):
        return lax.dot_general(onehot, table, nt_dims, preferred_element_type=F32)

    def pick2(onehot, v, base):
        hi, lo = digits(v, base)
        return pick(onehot, hi) * base + pick(onehot, lo)

    q = pl.program_id(0) * rows + lax.broadcasted_iota(I32, (rows, n), 0)
    bq = (q >> log_b).astype(F32)
    iq = (q & (gpb - 1)).astype(F32)
    own = jnp.where((bstart <= bq) & (bq < bend), 1.0, 0.0).astype(BF16)
    pre_q = pick2(own, pre, 64.0)
    len_q = pick(own, gl_len.astype(BF16))
    src_q = pick2(own, src, 128.0)
    bstart_q = pick(own, jnp.broadcast_to(bstart, (n, n)).astype(BF16))
    gl = (bq - bstart_q) * float(gpb) + iq
    inside = (pre_q <= gl) & (gl < pre_q + len_q)
    hit = jnp.sum(jnp.where(inside, src_q - pre_q + gl + 1.0, 0.0), axis=-1, keepdims=True)
    gsrc_ref[...] = jnp.broadcast_to(hit - 1.0, (rows, n)).astype(I32)

    bb = (pl.program_id(0) * TABLE_BLOCKS + lax.broadcasted_iota(I32, (TABLE_BLOCKS, n), 0)).astype(F32)
    lane = lax.broadcasted_iota(I32, (TABLE_BLOCKS, n), 1).astype(F32)
    mine = (bstart <= bb) & (bb < bend)
    red = lambda v: jnp.sum(jnp.where(mine, v, 0.0), axis=-1, keepdims=True)
    active = red(jnp.ones_like(lane))
    n_left = red(jnp.broadcast_to(n_g, mine.shape)) - (bb[:, 0:1] - red(jnp.broadcast_to(bstart, mine.shape))) * gpb
    ngran_ref[...] = jnp.broadcast_to(jnp.clip(n_left, 0.0, float(gpb)) * active, (TABLE_BLOCKS, n)).astype(I32)
    expert = jnp.where(active > 0.0, red(lane), float(N_EXPERTS - 1))
    bexp_ref[...] = jnp.broadcast_to(expert, (TABLE_BLOCKS, n)).astype(I32)


def _block_tables(cnt):
    nt = cnt.shape[0]
    assert nt <= LANES
    max_blocks = (nt * REGION_DATA_GRANULES) // GRANULES_PER_BLOCK + N_EXPERTS
    n_blocks = -(-(max_blocks + 2) // TABLE_BLOCKS) * TABLE_BLOCKS
    gpb = GRANULES_PER_BLOCK
    cnt_sq = jnp.zeros((LANES, LANES), I32).at[:nt].set(cnt[:, 0, :])
    by_step = lambda i: (i, 0)
    bexp, ngran, gsrc = pl.pallas_call(
        functools.partial(_tables_kernel, _region_rows(nt) // GRANULE),
        grid=(n_blocks // TABLE_BLOCKS,),
        in_specs=[_full((LANES, LANES))],
        out_specs=[pl.BlockSpec((TABLE_BLOCKS, LANES), by_step), pl.BlockSpec((TABLE_BLOCKS, LANES), by_step),
                   pl.BlockSpec((TABLE_BLOCKS * gpb, LANES), by_step)],
        out_shape=[jax.ShapeDtypeStruct((n_blocks, LANES), I32), jax.ShapeDtypeStruct((n_blocks, LANES), I32),
                   jax.ShapeDtypeStruct((n_blocks * gpb, LANES), I32)],
        compiler_params=_cparams(1),
        name="moe_block_tables",
    )(cnt_sq)
    return bexp[:, 0], ngran[:, 0], gsrc[:, 0]


def _combine_kernel(x_ref, ys_ref, pos_ref, gate_ref, mod_ref, o_ref):
    d = x_ref.shape[1]
    o_ref[...] = x_ref[...] + mod_ref[0:1, 5 * d:6 * d] * _moe_combine(ys_ref, pos_ref, gate_ref)


def _combine_glu_kernel(x_ref, ys_ref, pos_ref, gate_ref, modp_ref, modn_ref, g_ref, w_ref, b_ref,
                        o_ref, u_ref, wb_ref):
    d = x_ref.shape[1]

    @pl.when(pl.program_id(0) == 0)
    def _():
        wb_ref[...] = w_ref[...].astype(BF16)

    x2 = x_ref[...] + modp_ref[0:1, 5 * d:6 * d] * _moe_combine(ys_ref, pos_ref, gate_ref)
    o_ref[...] = x2
    h = _rms_mod(x2, g_ref[...], modn_ref[0:1, 0:d], modn_ref[0:1, d:2 * d])
    u = jnp.dot(h.astype(BF16), wb_ref[...], preferred_element_type=F32) + b_ref[...]
    u_ref[...] = u[:, :d] * jax.nn.sigmoid(u[:, d:])


def _combine_in_specs(t, d, tile):
    tok = lambda i: (i, 0)
    return [pl.BlockSpec((tile, d), tok), pl.BlockSpec((_region_rows(t // tile), d), tok),
            pl.BlockSpec((tile, LANES), tok), pl.BlockSpec((tile, LANES), tok)]


def _combine(x1, ys, pos, gates, mod_l):
    t, d = x1.shape
    tile = MOE_TILE
    return pl.pallas_call(
        _combine_kernel,
        grid=(t // tile,),
        in_specs=_combine_in_specs(t, d, tile) + [_full(mod_l.shape)],
        out_specs=pl.BlockSpec((tile, d), lambda i: (i, 0)),
        out_shape=jax.ShapeDtypeStruct((t, d), F32),
        compiler_params=_cparams(1),
        name="moe_combine",
    )(x1, ys, pos, gates, mod_l)


def _combine_glu(x1, ys, pos, gates, mod_prev, mod_next, g1n, w_pw1, b_pw1):
    t, d = x1.shape
    tile = MOE_TILE
    tok = lambda i: (i, 0)
    return pl.pallas_call(
        _combine_glu_kernel,
        grid=(t // tile,),
        in_specs=_combine_in_specs(t, d, tile)
                 + [_full(mod_prev.shape), _full(mod_next.shape), _full((1, d)),
                    _full(w_pw1.shape), _full((1, 2 * d))],
        out_specs=[pl.BlockSpec((tile, d), tok), pl.BlockSpec((tile, d), tok)],
        out_shape=[jax.ShapeDtypeStruct((t, d), F32), jax.ShapeDtypeStruct((t, d), F32)],
        scratch_shapes=[pltpu.VMEM(w_pw1.shape, BF16)],
        compiler_params=_cparams(1),
        name="combine_pw1_glu",
    )(x1, ys, pos, gates, mod_prev, mod_next, g1n, w_pw1, b_pw1)


def _conv_kernel(x_ref, up_ref, uc_ref, un_ref, mod_ref, wdw_ref, bdw_ref, lg_ref, lb_ref, w2_ref, b2_ref,
                 g2_ref, wr_ref, br_ref, x1_ref, xs_ref, pos_ref, gate_ref, cnt_ref, wb_ref, ubuf, shbuf, cbuf):
    i = pl.program_id(0)
    n = pl.num_programs(0)
    d = x_ref.shape[1]
    tile = x_ref.shape[0]
    halo = up_ref.shape[0]

    @pl.when(i == 0)
    def _():
        wb_ref[...] = w2_ref[...].astype(BF16)

    ubuf[0:halo, :] = jnp.where(i > 0, up_ref[...], 0.0)
    ubuf[halo:halo + tile, :] = uc_ref[...]
    ubuf[halo + tile:, :] = jnp.where(i < n - 1, un_ref[...], 0.0)

    span = tile + 2 * halo - SUBLANES
    for r in range(SUBLANES):
        shbuf[r, 0:span, :] = ubuf[r:r + span, :]
    sub = CONV_CHUNK // SUBLANES
    for c0 in range(0, tile, CONV_CHUNK):
        parts = [jnp.zeros((SUBLANES, d), F32) + bdw_ref[...] for _ in range(sub)]
        for j in range(CONV_WIDTH):
            a, r = divmod(halo + j - CONV_PAD, SUBLANES)
            wj = wdw_ref[j * SUBLANES:(j + 1) * SUBLANES, :]
            for k in range(sub):
                lo = c0 + (a + k) * SUBLANES
                parts[k] = parts[k] + shbuf[r, lo:lo + SUBLANES, :] * wj
        for k in range(sub):
            cbuf[c0 + k * SUBLANES:c0 + (k + 1) * SUBLANES, :] = parts[k]
    acc = cbuf[...]

    mu = jnp.mean(acc, axis=-1, keepdims=True)
    cen = acc - mu
    var = jnp.mean(cen * cen, axis=-1, keepdims=True)
    z = _silu(cen * lax.rsqrt(var + EPS) * lg_ref[...] + lb_ref[...])
    y = jnp.dot(z.astype(BF16), wb_ref[...], preferred_element_type=F32) + b2_ref[...]
    x1 = x_ref[...] + mod_ref[0:1, 2 * d:3 * d] * y
    x1_ref[...] = x1
    _router_dispatch(x1, g2_ref[...], mod_ref[0:1, 3 * d:4 * d], mod_ref[0:1, 4 * d:5 * d],
                     wr_ref[...], br_ref[...], xs_ref, pos_ref, gate_ref, cnt_ref)


def _conv_router(x2, u, mod_l, w_dw, b_dw, ln_g, ln_b, w_pw2, b_pw2, g2n, wr, br):
    t, d = x2.shape
    tile = MOE_TILE
    nt = t // tile
    hb = tile // CONV_HALO
    nh = t // CONV_HALO
    tok = lambda i: (i, 0)
    prv = lambda i: (jnp.maximum(i * hb - 1, 0), 0)
    nxt = lambda i: (jnp.minimum((i + 1) * hb, nh - 1), 0)
    wpad = jnp.repeat(w_dw, SUBLANES, axis=0)
    out_specs, out_shape = _dispatch_out(t, d, tile)
    return pl.pallas_call(
        _conv_kernel,
        grid=(nt,),
        in_specs=[pl.BlockSpec((tile, d), tok), pl.BlockSpec((CONV_HALO, d), prv),
                  pl.BlockSpec((tile, d), tok), pl.BlockSpec((CONV_HALO, d), nxt),
                  _full(mod_l.shape), _full(wpad.shape), _full((1, d)), _full((1, d)), _full((1, d)),
                  _full(w_pw2.shape), _full((1, d)), _full((1, d)), _full(wr.shape), _full(br.shape)],
        out_specs=out_specs,
        out_shape=out_shape,
        scratch_shapes=[pltpu.VMEM(w_pw2.shape, BF16), pltpu.VMEM((tile + 2 * CONV_HALO, d), F32),
                        pltpu.VMEM((SUBLANES, tile + 2 * CONV_HALO, d), F32), pltpu.VMEM((tile, d), F32)],
        compiler_params=_cparams(1),
        name="conv_router",
    )(x2, u, u, u, mod_l, wpad, b_dw, ln_g, ln_b, w_pw2, b_pw2, g2n, wr, br)


def _rope_tables(n_rows):
    quarter = HEAD_DIM // 4
    inv = ROPE_BASE ** (-jnp.arange(quarter, dtype=F32) / quarter)

    def tables(npos):
        ang = jnp.arange(npos, dtype=I32).astype(F32)[:, None] * inv[None, :]
        z = jnp.zeros_like(ang)
        c, s = jnp.cos(ang), jnp.sin(ang)
        half = lambda a, b: jnp.concatenate([a, b], axis=1)
        full = lambda h: jnp.concatenate([h, h, h, h], axis=1)
        return jnp.stack([full(half(c, c)), full(half(-s, z)), full(half(z, s))])

    return tables(n_rows), tables(GRID_W)


def _router_params(w_router, b_router):
    d, e = w_router.shape
    wr = jnp.zeros((d, LANES), F32).at[:, :e].set(w_router)
    br = jnp.full((1, LANES), NEG_BIG, F32).at[0, :e].set(b_router)
    hi = wr.astype(BF16)
    lo = (wr - hi.astype(F32)).astype(BF16)
    return jnp.stack([hi, lo]), br


def kernel(x, c, ctx, c_ctx, w_mod, b_mod, norm1_g, norm2_g, attn_w_qkv, attn_b_qkv, attn_w_o, attn_b_o,
           attn_q_norm, attn_k_norm, attn_sinks, conv_w_pw1, conv_b_pw1, conv_w_dw, conv_b_dw, conv_ln_g,
           conv_ln_b, conv_w_pw2, conv_b_pw2, moe_w_router, moe_b_router, moe_w1, moe_b1, moe_w2, moe_b2):
    bsz, t, d = x.shape
    assert bsz == 1 and w_mod.shape[0] == 2
    n_ctx = ctx.shape[1]
    tile = min(TOKEN_TILE, t)
    assert t % tile == 0 and t % Q_BLOCK == 0 and t % MOE_TILE == 0 and tile % GRID_W == 0
    xt = x.reshape(t, d)
    row = lambda v: v.reshape(1, -1)

    mod = _mod_vectors(c, c_ctx, w_mod, b_mod)
    mod0, mod1 = mod[0], mod[1]

    bd = (jnp.arange(2 * LANES)[:, None] // HEAD_DIM == jnp.arange(2 * LANES)[None, :] // HEAD_DIM).astype(BF16)
    qg = row(jnp.tile(attn_q_norm[0], N_HEADS))
    kg = row(jnp.tile(attn_k_norm[0], N_KV_HEADS))
    rowt, colt = _rope_tables(t // GRID_W)
    g1n = row(norm1_g[0])
    bq = row(attn_b_qkv[0])
    q, kk, vv = _qkv_project(xt, mod0, 0, True, g1n, attn_w_qkv[0], bq, qg, kg, rowt, colt, bd, tile)
    _, kkc, vvc = _qkv_project(ctx.reshape(n_ctx, d), mod0, 1, False, g1n, attn_w_qkv[0], bq, qg, kg,
                               rowt[:, :max(n_ctx // GRID_W, 1)], colt, bd, n_ctx)
    o = _attention(q, kk, vv, kkc, vvc, attn_sinks[0])
    wr0, br0 = _router_params(moe_w_router[0], moe_b_router[0])
    x1, xs, pos, gates, cnt = _oproj_router(xt, o, mod0, attn_w_o[0], row(attn_b_o[0]), row(norm2_g[0]),
                                            wr0, br0)
    ys = _moe_experts(xs, cnt.shape[0], *_block_tables(cnt), 0, moe_w1, moe_b1, moe_w2, moe_b2)

    x2, u = _combine_glu(x1, ys, pos, gates, mod0, mod1, row(norm1_g[1]), conv_w_pw1[0], row(conv_b_pw1[0]))
    wr1, br1 = _router_params(moe_w_router[1], moe_b_router[1])
    x3, xs, pos, gates, cnt = _conv_router(x2, u, mod1, conv_w_dw[0], row(conv_b_dw[0]), row(conv_ln_g[0]),
                                           row(conv_ln_b[0]), conv_w_pw2[0], row(conv_b_pw2[0]),
                                           row(norm2_g[1]), wr1, br1)
    ys = _moe_experts(xs, cnt.shape[0], *_block_tables(cnt), 1, moe_w1, moe_b1, moe_w2, moe_b2)
    out = _combine(x3, ys, pos, gates, mod1)
    return out.reshape(1, t, d)
```

```python
import functools

import jax
import jax.numpy as jnp
from jax import lax
from jax.experimental import pallas as pl
from jax.experimental.pallas import tpu as pltpu

F32 = jnp.float32
BF16 = jnp.bfloat16
I32 = jnp.int32

N_HEADS = 16
N_KV_HEADS = 2
HEAD_DIM = 64
GRID_W = 64
WINDOW = 128
Q_BLOCK = 128
ROPE_BASE = 10000.0
CONV_WIDTH = 31
CONV_PAD = CONV_WIDTH // 2
N_EXPERTS = 32
TOP_K = 4
SWIGLU_LIMIT = 7.0
SWIGLU_ALPHA = 1.702
EPS = 1e-6

LANES = 128
SUBLANES = 8
TOKEN_TILE = 512
MOE_TILE = 256
GRANULE = SUBLANES
EXPERT_ROWS = 512
GRANULES_PER_BLOCK = EXPERT_ROWS // GRANULE
REGION_DATA_GRANULES = (MOE_TILE * TOP_K + N_EXPERTS * (GRANULE - 1)) // GRANULE


def _spare_per_region(nt):
    return -(-GRANULES_PER_BLOCK // nt)


def _region_rows(nt):
    rows = (REGION_DATA_GRANULES + 2 * _spare_per_region(nt) + 1) * GRANULE
    return -(-rows // 256) * 256
CONV_HALO = 16
CONV_CHUNK = 32
NEG_BIG = -1e30
VMEM_LIMIT = 56 * 1024 * 1024


def _cparams(n_axes=1, vmem=VMEM_LIMIT):
    return pltpu.CompilerParams(dimension_semantics=("arbitrary",) * n_axes, vmem_limit_bytes=vmem)


def _full(shape):
    nd = len(shape)
    return pl.BlockSpec(shape, lambda *_: (0,) * nd)


def _silu(v):
    return v * jax.nn.sigmoid(v)


def _rms_mod(xv, g, shift, scale):
    ms = jnp.mean(xv * xv, axis=-1, keepdims=True)
    return (xv * lax.rsqrt(ms + EPS)) * g * (1.0 + scale) + shift


def _split_dot(a, b_bf16):
    hi = a.astype(BF16)
    lo = (a - hi.astype(F32)).astype(BF16)
    return (jnp.dot(hi, b_bf16, preferred_element_type=F32)
            + jnp.dot(lo, b_bf16, preferred_element_type=F32))


def _router_dispatch(x_new, g2n, sh2, sc2, wr, br, xs_ref, pos_ref, gate_ref, cnt_ref):
    tt = x_new.shape[0]
    rt = xs_ref.shape[0]
    h2 = _rms_mod(x_new, g2n, sh2, sc2)
    h_hi = h2.astype(BF16)
    h_lo = (h2 - h_hi.astype(F32)).astype(BF16)
    logits = (jnp.dot(h_hi, wr[0], preferred_element_type=F32)
              + (jnp.dot(h_lo, wr[0], preferred_element_type=F32)
                 + jnp.dot(h_hi, wr[1], preferred_element_type=F32))) + br
    lane = lax.broadcasted_iota(I32, logits.shape, 1)
    val_out = jnp.full(logits.shape, NEG_BIG, F32)
    onehot = jnp.zeros(logits.shape, F32)
    sels = []
    cur = logits
    for k in range(TOP_K):
        m = jnp.max(cur, axis=-1, keepdims=True)
        sel = jnp.min(jnp.where(cur == m, lane, LANES), axis=-1, keepdims=True)
        hit = lane == sel
        sels.append(sel)
        val_out = jnp.where(lane == k, m, val_out)
        onehot = jnp.where(hit, 1.0, onehot)
        cur = jnp.where(hit, NEG_BIG * 2.0, cur)
    vmax = jnp.max(val_out, axis=-1, keepdims=True)
    ev = jnp.where(lane < TOP_K, jnp.exp(val_out - vmax), 0.0)
    gate_ref[...] = ev / jnp.sum(ev, axis=-1, keepdims=True)

    ri = lax.broadcasted_iota(I32, (tt, tt), 0)
    ci = lax.broadcasted_iota(I32, (tt, tt), 1)
    earlier = jnp.where(ci < ri, 1.0, 0.0).astype(BF16)
    rank = jnp.dot(earlier, onehot.astype(BF16), preferred_element_type=F32)
    cnt = jnp.sum(onehot, axis=0, keepdims=True).astype(I32)
    c8 = jnp.bitwise_and(cnt + (GRANULE - 1), -GRANULE)
    ue = lax.broadcasted_iota(I32, (LANES, LANES), 0)
    ve = lax.broadcasted_iota(I32, (LANES, LANES), 1)
    c8_rows = jnp.broadcast_to(c8.astype(F32), (SUBLANES, LANES)).astype(BF16)
    before = jnp.where(ue < ve, 1.0, 0.0).astype(BF16)
    off = jnp.dot(c8_rows, before, preferred_element_type=F32)[0:1]
    cnt_ref[0] = jnp.broadcast_to(c8, (SUBLANES, LANES))

    slot_of = off + rank
    pos_out = jnp.zeros(logits.shape, I32)
    r_iota = lax.broadcasted_iota(I32, (tt, rt), 1)
    perm_t = jnp.zeros((tt, rt), F32)
    for k in range(TOP_K):
        pk = jnp.sum(jnp.where(lane == sels[k], slot_of, 0.0), axis=-1, keepdims=True).astype(I32)
        pos_out = jnp.where(lane == k, pk, pos_out)
        perm_t = jnp.where(r_iota == pk, 1.0, perm_t)
    pos_ref[...] = pos_out
    perm = perm_t.T.astype(BF16)
    xs_ref[...] = jnp.dot(perm, h_hi, preferred_element_type=F32)


def _moe_combine(ys_ref, pos_ref, gate_ref):
    tt = pos_ref.shape[0]
    rt = ys_ref.shape[0]
    r_iota = lax.broadcasted_iota(I32, (tt, rt), 1)
    g = jnp.zeros((tt, rt), F32)
    for k in range(TOP_K):
        g = jnp.where(r_iota == pos_ref[:, k:k + 1], gate_ref[:, k:k + 1], g)
    return jnp.dot(g.astype(BF16), ys_ref[...].astype(BF16), preferred_element_type=F32)


def _mod_kernel(cc_ref, w_ref, b_ref, o_ref):
    a = _silu(cc_ref[...])
    o_ref[0] = jnp.dot(a, w_ref[0], precision=lax.Precision.HIGHEST,
                       preferred_element_type=F32) + b_ref[0]


def _mod_vectors(c, c_ctx, w_mod, b_mod):
    depth, d, d6 = w_mod.shape
    cc = jnp.zeros((SUBLANES, d), F32).at[0].set(c[0]).at[1].set(c_ctx)
    ncol = 4
    cw = d6 // ncol
    return pl.pallas_call(
        _mod_kernel,
        grid=(depth, ncol),
        in_specs=[pl.BlockSpec((SUBLANES, d), lambda l, j: (0, 0)),
                  pl.BlockSpec((1, d, cw), lambda l, j: (l, 0, j)),
                  pl.BlockSpec((1, 1, cw), lambda l, j: (l, 0, j))],
        out_specs=pl.BlockSpec((1, SUBLANES, cw), lambda l, j: (l, 0, j)),
        out_shape=jax.ShapeDtypeStruct((depth, SUBLANES, d6), F32),
        compiler_params=_cparams(2),
        name="mod_vectors",
    )(cc, w_mod, b_mod.reshape(depth, 1, d6))


def _qkv_kernel(mod_row, use_rope, x_ref, mod_ref, g_ref, w_ref, b_ref, qg_ref, kg_ref, rowt_ref, colt_ref,
                bd_ref, q_ref, k_ref, v_ref, wb_ref):
    d = x_ref.shape[1]
    tile = x_ref.shape[0]
    nq = N_HEADS * HEAD_DIM
    nk = N_KV_HEADS * HEAD_DIM

    @pl.when(pl.program_id(0) == 0)
    def _():
        wb_ref[...] = w_ref[...].astype(BF16)

    sh = mod_ref[mod_row:mod_row + 1, 0:d]
    sc = mod_ref[mod_row:mod_row + 1, d:2 * d]
    h = _rms_mod(x_ref[...], g_ref[...], sh, sc)
    qkv = jnp.dot(h.astype(BF16), wb_ref[...], preferred_element_type=F32) + b_ref[...]
    q = qkv[:, :nq]
    k = qkv[:, nq:nq + nk]
    v = qkv[:, nq + nk:]
    bd = bd_ref[...]

    def head_norm(t, gain):
        tt = t * t
        w = bd.shape[0]
        if t.shape[1] >= w:
            ss = jnp.concatenate([_split_dot(tt[:, c:c + w], bd) for c in range(0, t.shape[1], w)], axis=1)
        else:
            ss = _split_dot(tt, bd[:t.shape[1], :t.shape[1]])
        return t * lax.rsqrt(ss * (1.0 / HEAD_DIM) + EPS) * gain

    qn = head_norm(q, qg_ref[...])
    kn = head_norm(k, kg_ref[...])

    if use_rope:
        lane = lax.broadcasted_iota(I32, (tile, LANES), 1)
        is_row = (lane % HEAD_DIM) < HEAD_DIM // 2
        nrow = tile // GRID_W

        def table(idx):
            rt = jnp.concatenate([jnp.broadcast_to(rowt_ref[idx, r:r + 1, :], (GRID_W, LANES))
                                  for r in range(nrow)], axis=0)
            ct = jnp.concatenate([colt_ref[idx]] * nrow, axis=0)
            return jnp.where(is_row, rt, ct)

        cosv, sav, sbv = table(0), table(1), table(2)

        def rope(t):
            n = t.shape[1]
            reps = n // LANES
            ct = jnp.concatenate([cosv] * reps, axis=1) if reps > 1 else cosv
            at = jnp.concatenate([sav] * reps, axis=1) if reps > 1 else sav
            bt = jnp.concatenate([sbv] * reps, axis=1) if reps > 1 else sbv
            up = pltpu.roll(t, n - HEAD_DIM // 4, 1)
            dn = pltpu.roll(t, HEAD_DIM // 4, 1)
            return t * ct + up * at + dn * bt

        qn = rope(qn)
        kn = rope(kn)

    q_ref[...] = (qn * (HEAD_DIM ** -0.5)).astype(BF16)
    klane = lax.broadcasted_iota(I32, kn.shape, 1)
    low = klane < HEAD_DIM

    def variants(t):
        sw = pltpu.roll(t, HEAD_DIM, 1)
        z = jnp.zeros_like(t)
        return jnp.concatenate([jnp.where(low, t, z), jnp.where(low, z, sw),
                                jnp.where(low, sw, z), jnp.where(low, z, t)], axis=1)

    k_ref[...] = variants(kn).astype(BF16)
    v_ref[...] = variants(v).astype(BF16)


def _qkv_project(xt, mod_l, mod_row, use_rope, g1n, w_qkv, b_qkv, qg, kg, rowt, colt, bd, tile):
    t, d = xt.shape
    nqkv = w_qkv.shape[1]
    nq = N_HEADS * HEAD_DIM
    tok = lambda i: (i, 0)
    nrow = max(tile // GRID_W, 1)
    return pl.pallas_call(
        functools.partial(_qkv_kernel, mod_row, use_rope),
        grid=(t // tile,),
        in_specs=[pl.BlockSpec((tile, d), tok), _full(mod_l.shape), _full((1, d)),
                  _full((d, nqkv)), _full((1, nqkv)), _full((1, nq)), _full((1, LANES)),
                  pl.BlockSpec((3, nrow, LANES), lambda i: (0, i, 0)), _full(colt.shape), _full(bd.shape)],
        out_specs=[pl.BlockSpec((tile, nq), tok), pl.BlockSpec((tile, 4 * LANES), tok),
                   pl.BlockSpec((tile, 4 * LANES), tok)],
        out_shape=[jax.ShapeDtypeStruct((t, nq), BF16), jax.ShapeDtypeStruct((t, 4 * LANES), BF16),
                   jax.ShapeDtypeStruct((t, 4 * LANES), BF16)],
        scratch_shapes=[pltpu.VMEM((d, nqkv), BF16)],
        compiler_params=_cparams(1),
        name="qkv_project",
    )(xt, mod_l, g1n, w_qkv, b_qkv, qg, kg, rowt, colt, bd)


def _attn_kernel(sink_ref, q_ref, kp_ref, kc_ref, kn_ref, vp_ref, vc_ref, vn_ref, kx_ref, vx_ref, bias_ref,
                 o_ref):
    qb = q_ref.shape[0]
    band = 3 * qb
    pairs_per_group = (N_HEADS // N_KV_HEADS) // 2
    rows = pairs_per_group * qb

    bias = bias_ref[0]
    kband = jnp.concatenate([kp_ref[...], kc_ref[...], kn_ref[...]], axis=0)
    vband = jnp.concatenate([vp_ref[...], vc_ref[...], vn_ref[...]], axis=0)
    kctx = kx_ref[...]
    vctx = vx_ref[...]
    dn_t = (((1,), (1,)), ((), ()))
    rsub = lax.broadcasted_iota(I32, (rows, 1), 0) // qb

    for g in range(N_KV_HEADS):
        qg = jnp.concatenate(
            [q_ref[:, (g * pairs_per_group + j) * LANES:(g * pairs_per_group + j + 1) * LANES]
             for j in range(pairs_per_group)], axis=0)
        acc = jnp.zeros((rows, LANES), F32)
        for par in range(2):
            col = (2 * g + par) * LANES
            s_b = lax.dot_general(qg, kband[:, col:col + LANES], dn_t, preferred_element_type=F32)
            s_c = lax.dot_general(qg, kctx[:, col:col + LANES], dn_t, preferred_element_type=F32)
            s_b = s_b + bias
            sink = jnp.zeros((rows, 1), F32)
            for j in range(pairs_per_group):
                hd = 2 * (g * pairs_per_group + j) + par
                sink = jnp.where(rsub == j, sink_ref[hd], sink)
            m = jnp.maximum(jnp.maximum(jnp.max(s_b, axis=-1, keepdims=True),
                                        jnp.max(s_c, axis=-1, keepdims=True)), sink)
            p_b = jnp.exp(s_b - m)
            p_c = jnp.exp(s_c - m)
            l = (jnp.sum(p_b, axis=-1, keepdims=True) + jnp.sum(p_c, axis=-1, keepdims=True)
                 + jnp.exp(sink - m))
            o = (jnp.dot(p_b.astype(BF16), vband[:, col:col + LANES], preferred_element_type=F32)
                 + jnp.dot(p_c.astype(BF16), vctx[:, col:col + LANES], preferred_element_type=F32))
            acc = acc + o / l
        for j in range(pairs_per_group):
            pcol = (g * pairs_per_group + j) * LANES
            o_ref[:, pcol:pcol + LANES] = acc[j * qb:(j + 1) * qb].astype(BF16)


def _attention(q, kk, vv, kkc, vvc, sinks):
    t, nq = q.shape
    qb = Q_BLOCK
    nb = t // qb
    c = kkc.shape[0]
    w = kk.shape[1]
    cur = lambda i, s: (i, 0)
    prv = lambda i, s: (jnp.maximum(i - 1, 0), 0)
    nxt = lambda i, s: (jnp.minimum(i + 1, nb - 1), 0)
    zero = lambda i, s: (0, 0)
    rows = (N_HEADS // N_KV_HEADS) // 2 * qb
    r = (jnp.arange(rows, dtype=I32) % qb)[:, None]
    cpos = jnp.arange(3 * qb, dtype=I32)[None, :] - qb
    inside = jnp.abs(r - cpos) <= WINDOW
    variants = [inside & ((cpos >= 0) | ((v & 1) == 0)) & ((cpos < qb) | ((v & 2) == 0)) for v in range(4)]
    bias = jnp.where(jnp.stack(variants), 0.0, NEG_BIG).astype(F32)
    which = lambda i, s: (jnp.where(i == 0, 1, 0) + jnp.where(i == nb - 1, 2, 0), 0, 0)
    gs = pltpu.PrefetchScalarGridSpec(
        num_scalar_prefetch=1,
        grid=(nb,),
        in_specs=[pl.BlockSpec((qb, nq), cur),
                  pl.BlockSpec((qb, w), prv), pl.BlockSpec((qb, w), cur), pl.BlockSpec((qb, w), nxt),
                  pl.BlockSpec((qb, w), prv), pl.BlockSpec((qb, w), cur), pl.BlockSpec((qb, w), nxt),
                  pl.BlockSpec((c, w), zero), pl.BlockSpec((c, w), zero),
                  pl.BlockSpec((1, rows, 3 * qb), which)],
        out_specs=pl.BlockSpec((qb, nq), cur),
    )
    return pl.pallas_call(
        _attn_kernel,
        grid_spec=gs,
        out_shape=jax.ShapeDtypeStruct((t, nq), BF16),
        compiler_params=_cparams(1),
        name="window_attention",
    )(sinks, q, kk, kk, kk, vv, vv, vv, kkc, vvc, bias)


def _dispatch_out(t, d, tile):
    nt = t // tile
    tok = lambda i: (i, 0)
    region = _region_rows(nt)
    specs = [pl.BlockSpec((tile, d), tok), pl.BlockSpec((region, d), tok),
             pl.BlockSpec((tile, LANES), tok), pl.BlockSpec((tile, LANES), tok),
             pl.BlockSpec((1, SUBLANES, LANES), lambda i: (i, 0, 0))]
    shapes = [jax.ShapeDtypeStruct((t, d), F32), jax.ShapeDtypeStruct((nt * region, d), F32),
              jax.ShapeDtypeStruct((t, LANES), I32), jax.ShapeDtypeStruct((t, LANES), F32),
              jax.ShapeDtypeStruct((nt, SUBLANES, LANES), I32)]
    return specs, shapes


def _oproj_kernel(x_ref, o_ref, mod_ref, wo_ref, bo_ref, g2_ref, wr_ref, br_ref,
                  x1_ref, xs_ref, pos_ref, gate_ref, cnt_ref, wb_ref):
    d = x_ref.shape[1]

    @pl.when(pl.program_id(0) == 0)
    def _():
        wb_ref[...] = wo_ref[...].astype(BF16)

    y = jnp.dot(o_ref[...], wb_ref[...], preferred_element_type=F32) + bo_ref[...]
    x1 = x_ref[...] + mod_ref[0:1, 2 * d:3 * d] * y
    x1_ref[...] = x1
    _router_dispatch(x1, g2_ref[...], mod_ref[0:1, 3 * d:4 * d], mod_ref[0:1, 4 * d:5 * d],
                     wr_ref[...], br_ref[...], xs_ref, pos_ref, gate_ref, cnt_ref)


def _oproj_router(xt, o, mod_l, w_o, b_o, g2n, wr, br):
    t, d = xt.shape
    tile = MOE_TILE
    tok = lambda i: (i, 0)
    out_specs, out_shape = _dispatch_out(t, d, tile)
    return pl.pallas_call(
        _oproj_kernel,
        grid=(t // tile,),
        in_specs=[pl.BlockSpec((tile, d), tok), pl.BlockSpec((tile, o.shape[1]), tok), _full(mod_l.shape),
                  _full(w_o.shape), _full((1, d)), _full((1, d)), _full(wr.shape), _full(br.shape)],
        out_specs=out_specs,
        out_shape=out_shape,
        scratch_shapes=[pltpu.VMEM(w_o.shape, BF16)],
        compiler_params=_cparams(1),
        name="oproj_router",
    )(xt, o, mod_l, w_o, b_o, g2n, wr, br)


def _moe_kernel(nt, region_granules, brange_ref, nvg_ref, gsrc_ref, xs_hbm, w1_ref, b1_ref, w2_ref, b2_ref,
                ys_hbm, xbuf, obuf, w1b, w2b, gsem, ssem):
    e = pl.program_id(0)
    ne = pl.num_programs(0)
    b_lo = brange_ref[e]
    b_hi = brange_ref[ne + e]
    total = brange_ref[2 * ne - 1]
    gpb = GRANULES_PER_BLOCK
    dff = w2_ref.shape[1]
    spare = _spare_per_region(nt)
    idle_granule = REGION_DATA_GRANULES + 2 * spare

    def rows_of(granule):
        return pl.ds(pl.multiple_of(granule * GRANULE, GRANULE), GRANULE)

    def gather_copy(blk, gi):
        g = gsrc_ref[blk * gpb + gi]
        src = jnp.where(g >= 0, g, idle_granule)
        s = blk % 3
        return pltpu.make_async_copy(xs_hbm.at[rows_of(src)], xbuf.at[s, pl.ds(gi * GRANULE, GRANULE)],
                                     gsem.at[s])

    def scatter_copy(blk, s, gi):
        g = gsrc_ref[blk * gpb + gi]
        own_spare = (gi % nt) * region_granules + REGION_DATA_GRANULES + gi // nt + s * spare
        dst = jnp.where(g >= 0, g, own_spare)
        return pltpu.make_async_copy(obuf.at[s, pl.ds(gi * GRANULE, GRANULE)], ys_hbm.at[rows_of(dst)],
                                     ssem.at[s])

    def start_gather(blk):
        for gi in range(gpb):
            gather_copy(blk, gi).start()

    def wait_gather(s):
        pltpu.make_async_copy(xs_hbm.at[pl.ds(0, EXPERT_ROWS)], xbuf.at[s], gsem.at[s]).wait()

    def wait_scatter(s):
        pltpu.make_async_copy(obuf.at[s], ys_hbm.at[pl.ds(0, EXPERT_ROWS)], ssem.at[s]).wait()

    @pl.when(e == 0)
    def _():
        obuf[...] = jnp.zeros(obuf.shape, F32)
        start_gather(0)
        start_gather(1)

    def expert_block(b, rows):
        slot = b % 2
        start_gather(b + 2)
        xb = xbuf[b % 3, 0:rows, :].astype(BF16)
        gu = jnp.dot(xb, w1b[...], preferred_element_type=F32) + b1_ref[0]
        gt = jnp.minimum(gu[:, :dff], SWIGLU_LIMIT)
        up = jnp.clip(gu[:, dff:], -SWIGLU_LIMIT, SWIGLU_LIMIT)
        act = gt * jax.nn.sigmoid(SWIGLU_ALPHA * gt) * (up + 1.0)
        obuf[slot, 0:rows, :] = jnp.dot(act.astype(BF16), w2b[...], preferred_element_type=F32) + b2_ref[0]
        for gi in range(gpb):
            scatter_copy(b, slot, gi).start()

    def block_body(b, carry):
        wait_gather(b % 3)

        @pl.when(b >= 2)
        def _():
            wait_scatter(b % 2)

        half = EXPERT_ROWS // 2
        rows_needed = nvg_ref[b] * GRANULE

        @pl.when(rows_needed > half)
        def _():
            expert_block(b, EXPERT_ROWS)

        @pl.when(rows_needed <= half)
        def _():
            expert_block(b, half)

        return carry

    @pl.when(b_hi > b_lo)
    def _():
        w1b[...] = w1_ref[0].astype(BF16)
        w2b[...] = w2_ref[0].astype(BF16)
        lax.fori_loop(b_lo, b_hi, block_body, 0)

    @pl.when(e == ne - 1)
    def _():
        wait_gather(total % 3)
        wait_gather((total + 1) % 3)

        @pl.when(total >= 2)
        def _():
            wait_scatter(total % 2)

        @pl.when(total >= 1)
        def _():
            wait_scatter((total + 1) % 2)


def _moe_experts(xs, nt, block_range, n_granules, granule_src, layer, w1, b1, w2, b2):
    d = xs.shape[1]
    dff = w2.shape[2]
    n_layers, n_exp = w1.shape[:2]
    by_expert = lambda e, br, ng, gs: (layer, e, 0, 0)
    gs = pltpu.PrefetchScalarGridSpec(
        num_scalar_prefetch=3,
        grid=(n_exp,),
        in_specs=[pl.BlockSpec(memory_space=pl.ANY),
                  pl.BlockSpec((None, 1, d, 2 * dff), by_expert), pl.BlockSpec((None, 1, 1, 2 * dff), by_expert),
                  pl.BlockSpec((None, 1, dff, d), by_expert), pl.BlockSpec((None, 1, 1, d), by_expert)],
        out_specs=pl.BlockSpec(memory_space=pl.ANY),
        scratch_shapes=[pltpu.VMEM((3, EXPERT_ROWS, d), F32), pltpu.VMEM((2, EXPERT_ROWS, d), F32),
                        pltpu.VMEM((d, 2 * dff), BF16), pltpu.VMEM((dff, d), BF16),
                        pltpu.SemaphoreType.DMA((3,)), pltpu.SemaphoreType.DMA((2,))],
    )
    return pl.pallas_call(
        functools.partial(_moe_kernel, nt, _region_rows(nt) // GRANULE),
        grid_spec=gs,
        out_shape=jax.ShapeDtypeStruct(xs.shape, F32),
        input_output_aliases={3: 0},
        compiler_params=_cparams(1),
        name="moe_experts",
    )(block_range, n_granules, granule_src, xs, w1, b1.reshape(n_layers, n_exp, 1, 2 * dff),
      w2, b2.reshape(n_layers, n_exp, 1, d))


TABLE_BLOCKS = 8


def _tables_kernel(region_granules, cnt_ref, brange_ref, ngran_ref, gsrc_ref):
    gpb = GRANULES_PER_BLOCK
    log_b = gpb.bit_length() - 1
    n = cnt_ref.shape[0]
    rows = TABLE_BLOCKS * gpb
    nt_dims = (((1,), (1,)), ((), ()))

    cnt = cnt_ref[...].astype(F32)
    gl_len = cnt * (1.0 / GRANULE)
    ri = lax.broadcasted_iota(I32, (n, n), 0)
    ci = lax.broadcasted_iota(I32, (n, n), 1)
    lower = jnp.where(ci < ri, 1.0, 0.0).astype(BF16)
    upper = jnp.where(ri < ci, 1.0, 0.0).astype(BF16)
    pre = jnp.dot(lower, gl_len.astype(BF16), preferred_element_type=F32)
    off = jnp.dot(cnt.astype(BF16), upper, preferred_element_type=F32)
    src = ri.astype(F32) * float(region_granules) + off * (1.0 / GRANULE)
    n_g = jnp.sum(gl_len, axis=0, keepdims=True)
    blocks = ((n_g.astype(I32) + (gpb - 1)) >> log_b).astype(F32)
    bstart = jnp.dot(jnp.broadcast_to(blocks, (SUBLANES, n)).astype(BF16), upper,
                     preferred_element_type=F32)[0:1]
    bend = bstart + blocks

    def digits(v, base):
        hi = jnp.floor(v * (1.0 / base))
        return hi.astype(BF16), (v - hi * base).astype(BF16)

    def pick(onehot, table):
        return lax.dot_general(onehot, table, nt_dims, preferred_element_type=F32)

    def pick2(onehot, v, base):
        hi, lo = digits(v, base)
        return pick(onehot, hi) * base + pick(onehot, lo)

    q = pl.program_id(0) * rows + lax.broadcasted_iota(I32, (rows, n), 0)
    bq = (q >> log_b).astype(F32)
    iq = (q & (gpb - 1)).astype(F32)
    own = jnp.where((bstart <= bq) & (bq < bend), 1.0, 0.0).astype(BF16)
    pre_q = pick2(own, pre, 64.0)
    len_q = pick(own, gl_len.astype(BF16))
    src_q = pick2(own, src, 128.0)
    bstart_q = pick(own, jnp.broadcast_to(bstart, (n, n)).astype(BF16))
    gl = (bq - bstart_q) * float(gpb) + iq
    inside = (pre_q <= gl) & (gl < pre_q + len_q)
    hit = jnp.sum(jnp.where(inside, src_q - pre_q + gl + 1.0, 0.0), axis=-1, keepdims=True)
    gsrc_ref[...] = jnp.broadcast_to(hit - 1.0, (rows, n)).astype(I32)

    bb = (pl.program_id(0) * TABLE_BLOCKS + lax.broadcasted_iota(I32, (TABLE_BLOCKS, n), 0)).astype(F32)
    lane = lax.broadcasted_iota(I32, (TABLE_BLOCKS, n), 1).astype(F32)
    mine = (bstart <= bb) & (bb < bend)
    red = lambda v: jnp.sum(jnp.where(mine, v, 0.0), axis=-1, keepdims=True)
    active = red(jnp.ones_like(lane))
    n_left = red(jnp.broadcast_to(n_g, mine.shape)) - (bb[:, 0:1] - red(jnp.broadcast_to(bstart, mine.shape))) * gpb
    ngran_ref[...] = jnp.broadcast_to(jnp.clip(n_left, 0.0, float(gpb)) * active, (TABLE_BLOCKS, n)).astype(I32)

    sub = lax.broadcasted_iota(I32, (SUBLANES, n), 0)
    brange_ref[...] = jnp.where(sub == 0, bstart, jnp.where(sub == 1, bend, 0.0)).astype(I32)


def _block_tables(cnt):
    nt = cnt.shape[0]
    assert nt <= LANES
    max_blocks = (nt * REGION_DATA_GRANULES) // GRANULES_PER_BLOCK + N_EXPERTS
    n_blocks = -(-(max_blocks + 2) // TABLE_BLOCKS) * TABLE_BLOCKS
    gpb = GRANULES_PER_BLOCK
    cnt_sq = jnp.zeros((LANES, LANES), I32).at[:nt].set(cnt[:, 0, :])
    by_step = lambda i: (i, 0)
    brange, ngran, gsrc = pl.pallas_call(
        functools.partial(_tables_kernel, _region_rows(nt) // GRANULE),
        grid=(n_blocks // TABLE_BLOCKS,),
        in_specs=[_full((LANES, LANES))],
        out_specs=[_full((SUBLANES, LANES)), pl.BlockSpec((TABLE_BLOCKS, LANES), by_step),
                   pl.BlockSpec((TABLE_BLOCKS * gpb, LANES), by_step)],
        out_shape=[jax.ShapeDtypeStruct((SUBLANES, LANES), I32), jax.ShapeDtypeStruct((n_blocks, LANES), I32),
                   jax.ShapeDtypeStruct((n_blocks * gpb, LANES), I32)],
        compiler_params=_cparams(1),
        name="moe_block_tables",
    )(cnt_sq)
    return brange[0:2, :N_EXPERTS].reshape(-1), ngran[:, 0], gsrc[:, 0]


def _combine_kernel(x_ref, ys_ref, pos_ref, gate_ref, mod_ref, o_ref):
    d = x_ref.shape[1]
    o_ref[...] = x_ref[...] + mod_ref[0:1, 5 * d:6 * d] * _moe_combine(ys_ref, pos_ref, gate_ref)


def _combine_glu_kernel(x_ref, ys_ref, pos_ref, gate_ref, modp_ref, modn_ref, g_ref, w_ref, b_ref,
                        o_ref, u_ref, wb_ref):
    d = x_ref.shape[1]

    @pl.when(pl.program_id(0) == 0)
    def _():
        wb_ref[...] = w_ref[...].astype(BF16)

    x2 = x_ref[...] + modp_ref[0:1, 5 * d:6 * d] * _moe_combine(ys_ref, pos_ref, gate_ref)
    o_ref[...] = x2
    h = _rms_mod(x2, g_ref[...], modn_ref[0:1, 0:d], modn_ref[0:1, d:2 * d])
    u = jnp.dot(h.astype(BF16), wb_ref[...], preferred_element_type=F32) + b_ref[...]
    u_ref[...] = u[:, :d] * jax.nn.sigmoid(u[:, d:])


def _combine_in_specs(t, d, tile):
    tok = lambda i: (i, 0)
    return [pl.BlockSpec((tile, d), tok), pl.BlockSpec((_region_rows(t // tile), d), tok),
            pl.BlockSpec((tile, LANES), tok), pl.BlockSpec((tile, LANES), tok)]


def _combine(x1, ys, pos, gates, mod_l):
    t, d = x1.shape
    tile = MOE_TILE
    return pl.pallas_call(
        _combine_kernel,
        grid=(t // tile,),
        in_specs=_combine_in_specs(t, d, tile) + [_full(mod_l.shape)],
        out_specs=pl.BlockSpec((tile, d), lambda i: (i, 0)),
        out_shape=jax.ShapeDtypeStruct((t, d), F32),
        compiler_params=_cparams(1),
        name="moe_combine",
    )(x1, ys, pos, gates, mod_l)


def _combine_glu(x1, ys, pos, gates, mod_prev, mod_next, g1n, w_pw1, b_pw1):
    t, d = x1.shape
    tile = MOE_TILE
    tok = lambda i: (i, 0)
    return pl.pallas_call(
        _combine_glu_kernel,
        grid=(t // tile,),
        in_specs=_combine_in_specs(t, d, tile)
                 + [_full(mod_prev.shape), _full(mod_next.shape), _full((1, d)),
                    _full(w_pw1.shape), _full((1, 2 * d))],
        out_specs=[pl.BlockSpec((tile, d), tok), pl.BlockSpec((tile, d), tok)],
        out_shape=[jax.ShapeDtypeStruct((t, d), F32), jax.ShapeDtypeStruct((t, d), F32)],
        scratch_shapes=[pltpu.VMEM(w_pw1.shape, BF16)],
        compiler_params=_cparams(1),
        name="combine_pw1_glu",
    )(x1, ys, pos, gates, mod_prev, mod_next, g1n, w_pw1, b_pw1)


def _conv_kernel(x_ref, up_ref, uc_ref, un_ref, mod_ref, wdw_ref, bdw_ref, lg_ref, lb_ref, w2_ref, b2_ref,
                 g2_ref, wr_ref, br_ref, x1_ref, xs_ref, pos_ref, gate_ref, cnt_ref, wb_ref, ubuf, shbuf, cbuf):
    i = pl.program_id(0)
    n = pl.num_programs(0)
    d = x_ref.shape[1]
    tile = x_ref.shape[0]
    halo = up_ref.shape[0]

    @pl.when(i == 0)
    def _():
        wb_ref[...] = w2_ref[...].astype(BF16)

    ubuf[0:halo, :] = jnp.where(i > 0, up_ref[...], 0.0)
    ubuf[halo:halo + tile, :] = uc_ref[...]
    ubuf[halo + tile:, :] = jnp.where(i < n - 1, un_ref[...], 0.0)

    span = tile + 2 * halo - SUBLANES
    for r in range(SUBLANES):
        shbuf[r, 0:span, :] = ubuf[r:r + span, :]
    sub = CONV_CHUNK // SUBLANES
    for c0 in range(0, tile, CONV_CHUNK):
        parts = [jnp.zeros((SUBLANES, d), F32) + bdw_ref[...] for _ in range(sub)]
        for j in range(CONV_WIDTH):
            a, r = divmod(halo + j - CONV_PAD, SUBLANES)
            wj = wdw_ref[j * SUBLANES:(j + 1) * SUBLANES, :]
            for k in range(sub):
                lo = c0 + (a + k) * SUBLANES
                parts[k] = parts[k] + shbuf[r, lo:lo + SUBLANES, :] * wj
        for k in range(sub):
            cbuf[c0 + k * SUBLANES:c0 + (k + 1) * SUBLANES, :] = parts[k]
    acc = cbuf[...]

    mu = jnp.mean(acc, axis=-1, keepdims=True)
    cen = acc - mu
    var = jnp.mean(cen * cen, axis=-1, keepdims=True)
    z = _silu(cen * lax.rsqrt(var + EPS) * lg_ref[...] + lb_ref[...])
    y = jnp.dot(z.astype(BF16), wb_ref[...], preferred_element_type=F32) + b2_ref[...]
    x1 = x_ref[...] + mod_ref[0:1, 2 * d:3 * d] * y
    x1_ref[...] = x1
    _router_dispatch(x1, g2_ref[...], mod_ref[0:1, 3 * d:4 * d], mod_ref[0:1, 4 * d:5 * d],
                     wr_ref[...], br_ref[...], xs_ref, pos_ref, gate_ref, cnt_ref)


def _conv_router(x2, u, mod_l, w_dw, b_dw, ln_g, ln_b, w_pw2, b_pw2, g2n, wr, br):
    t, d = x2.shape
    tile = MOE_TILE
    nt = t // tile
    hb = tile // CONV_HALO
    nh = t // CONV_HALO
    tok = lambda i: (i, 0)
    prv = lambda i: (jnp.maximum(i * hb - 1, 0), 0)
    nxt = lambda i: (jnp.minimum((i + 1) * hb, nh - 1), 0)
    wpad = jnp.repeat(w_dw, SUBLANES, axis=0)
    out_specs, out_shape = _dispatch_out(t, d, tile)
    return pl.pallas_call(
        _conv_kernel,
        grid=(nt,),
        in_specs=[pl.BlockSpec((tile, d), tok), pl.BlockSpec((CONV_HALO, d), prv),
                  pl.BlockSpec((tile, d), tok), pl.BlockSpec((CONV_HALO, d), nxt),
                  _full(mod_l.shape), _full(wpad.shape), _full((1, d)), _full((1, d)), _full((1, d)),
                  _full(w_pw2.shape), _full((1, d)), _full((1, d)), _full(wr.shape), _full(br.shape)],
        out_specs=out_specs,
        out_shape=out_shape,
        scratch_shapes=[pltpu.VMEM(w_pw2.shape, BF16), pltpu.VMEM((tile + 2 * CONV_HALO, d), F32),
                        pltpu.VMEM((SUBLANES, tile + 2 * CONV_HALO, d), F32), pltpu.VMEM((tile, d), F32)],
        compiler_params=_cparams(1),
        name="conv_router",
    )(x2, u, u, u, mod_l, wpad, b_dw, ln_g, ln_b, w_pw2, b_pw2, g2n, wr, br)


def _rope_tables(n_rows):
    quarter = HEAD_DIM // 4
    inv = ROPE_BASE ** (-jnp.arange(quarter, dtype=F32) / quarter)

    def tables(npos):
        ang = jnp.arange(npos, dtype=I32).astype(F32)[:, None] * inv[None, :]
        z = jnp.zeros_like(ang)
        c, s = jnp.cos(ang), jnp.sin(ang)
        half = lambda a, b: jnp.concatenate([a, b], axis=1)
        full = lambda h: jnp.concatenate([h, h, h, h], axis=1)
        return jnp.stack([full(half(c, c)), full(half(-s, z)), full(half(z, s))])

    return tables(n_rows), tables(GRID_W)


def _router_params(w_router, b_router):
    d, e = w_router.shape
    wr = jnp.zeros((d, LANES), F32).at[:, :e].set(w_router)
    br = jnp.full((1, LANES), NEG_BIG, F32).at[0, :e].set(b_router)
    hi = wr.astype(BF16)
    lo = (wr - hi.astype(F32)).astype(BF16)
    return jnp.stack([hi, lo]), br


def kernel(x, c, ctx, c_ctx, w_mod, b_mod, norm1_g, norm2_g, attn_w_qkv, attn_b_qkv, attn_w_o, attn_b_o,
           attn_q_norm, attn_k_norm, attn_sinks, conv_w_pw1, conv_b_pw1, conv_w_dw, conv_b_dw, conv_ln_g,
           conv_ln_b, conv_w_pw2, conv_b_pw2, moe_w_router, moe_b_router, moe_w1, moe_b1, moe_w2, moe_b2):
    bsz, t, d = x.shape
    assert bsz == 1 and w_mod.shape[0] == 2
    n_ctx = ctx.shape[1]
    tile = min(TOKEN_TILE, t)
    assert t % tile == 0 and t % Q_BLOCK == 0 and t % MOE_TILE == 0 and tile % GRID_W == 0
    xt = x.reshape(t, d)
    row = lambda v: v.reshape(1, -1)

    mod = _mod_vectors(c, c_ctx, w_mod, b_mod)
    mod0, mod1 = mod[0], mod[1]

    bd = (jnp.arange(2 * LANES)[:, None] // HEAD_DIM == jnp.arange(2 * LANES)[None, :] // HEAD_DIM).astype(BF16)
    qg = row(jnp.tile(attn_q_norm[0], N_HEADS))
    kg = row(jnp.tile(attn_k_norm[0], N_KV_HEADS))
    rowt, colt = _rope_tables(t // GRID_W)
    g1n = row(norm1_g[0])
    bq = row(attn_b_qkv[0])
    q, kk, vv = _qkv_project(xt, mod0, 0, True, g1n, attn_w_qkv[0], bq, qg, kg, rowt, colt, bd, tile)
    _, kkc, vvc = _qkv_project(ctx.reshape(n_ctx, d), mod0, 1, False, g1n, attn_w_qkv[0], bq, qg, kg,
                               rowt[:, :max(n_ctx // GRID_W, 1)], colt, bd, n_ctx)
    o = _attention(q, kk, vv, kkc, vvc, attn_sinks[0])
    wr0, br0 = _router_params(moe_w_router[0], moe_b_router[0])
    x1, xs, pos, gates, cnt = _oproj_router(xt, o, mod0, attn_w_o[0], row(attn_b_o[0]), row(norm2_g[0]),
                                            wr0, br0)
    ys = _moe_experts(xs, cnt.shape[0], *_block_tables(cnt), 0, moe_w1, moe_b1, moe_w2, moe_b2)

    x2, u = _combine_glu(x1, ys, pos, gates, mod0, mod1, row(norm1_g[1]), conv_w_pw1[0], row(conv_b_pw1[0]))
    wr1, br1 = _router_params(moe_w_router[1], moe_b_router[1])
    x3, xs, pos, gates, cnt = _conv_router(x2, u, mod1, conv_w_dw[0], row(conv_b_dw[0]), row(conv_ln_g[0]),
                                           row(conv_ln_b[0]), conv_w_pw2[0], row(conv_b_pw2[0]),
                                           row(norm2_g[1]), wr1, br1)
    ys = _moe_experts(xs, cnt.shape[0], *_block_tables(cnt), 1, moe_w1, moe_b1, moe_w2, moe_b2)
    out = _combine(x3, ys, pos, gates, mod1)
    return out.reshape(1, t, d)
```

```python
import functools

import jax
import jax.numpy as jnp
from jax import lax
from jax.experimental import pallas as pl
from jax.experimental.pallas import tpu as pltpu

F32 = jnp.float32
BF16 = jnp.bfloat16
I32 = jnp.int32

N_HEADS = 16
N_KV_HEADS = 2
HEAD_DIM = 64
GRID_W = 64
WINDOW = 128
Q_BLOCK = 128
ROPE_BASE = 10000.0
CONV_WIDTH = 31
CONV_PAD = CONV_WIDTH // 2
N_EXPERTS = 32
TOP_K = 4
SWIGLU_LIMIT = 7.0
SWIGLU_ALPHA = 1.702
EPS = 1e-6

LANES = 128
SUBLANES = 8
TOKEN_TILE = 512
MOE_TILE = 256
GRANULE = SUBLANES
EXPERT_ROWS = 512
GRANULES_PER_BLOCK = EXPERT_ROWS // GRANULE
REGION_DATA_GRANULES = (MOE_TILE * TOP_K + N_EXPERTS * (GRANULE - 1)) // GRANULE


def _spare_per_region(nt):
    return -(-GRANULES_PER_BLOCK // nt)


def _region_rows(nt):
    rows = (REGION_DATA_GRANULES + 2 * _spare_per_region(nt) + 1) * GRANULE
    return -(-rows // 256) * 256
CONV_HALO = 16
CONV_CHUNK = 32
NEG_BIG = -1e30
LOG2_E = 1.4426950408889634
VMEM_LIMIT = 56 * 1024 * 1024


def _cparams(n_axes=1, vmem=VMEM_LIMIT):
    return pltpu.CompilerParams(dimension_semantics=("arbitrary",) * n_axes, vmem_limit_bytes=vmem)


def _full(shape):
    nd = len(shape)
    return pl.BlockSpec(shape, lambda *_: (0,) * nd)


def _silu(v):
    return v * jax.nn.sigmoid(v)


def _rms_mod(xv, g, shift, scale):
    ms = jnp.mean(xv * xv, axis=-1, keepdims=True)
    return (xv * lax.rsqrt(ms + EPS)) * g * (1.0 + scale) + shift


def _split_dot(a, b_bf16):
    hi = a.astype(BF16)
    lo = (a - hi.astype(F32)).astype(BF16)
    return (jnp.dot(hi, b_bf16, preferred_element_type=F32)
            + jnp.dot(lo, b_bf16, preferred_element_type=F32))


def _router_dispatch(x_new, g2n, sh2, sc2, wr, br, xs_ref, pos_ref, gate_ref, cnt_ref):
    tt = x_new.shape[0]
    rt = xs_ref.shape[0]
    h2 = _rms_mod(x_new, g2n, sh2, sc2)
    h_hi = h2.astype(BF16)
    h_lo = (h2 - h_hi.astype(F32)).astype(BF16)
    nt_dims = (((1,), (1,)), ((), ()))
    wdot = lambda w, h: lax.dot_general(w, h, nt_dims, preferred_element_type=F32)
    logits = (wdot(wr[0], h_hi) + (wdot(wr[0], h_lo) + wdot(wr[1], h_hi)) + br)[0:N_EXPERTS]
    erow = lax.broadcasted_iota(I32, (N_EXPERTS, tt), 0)
    hits, vals = [], []
    cur = logits
    for k in range(TOP_K):
        m = jnp.max(cur, axis=0, keepdims=True)
        sel = jnp.min(jnp.where(cur == m, erow, N_EXPERTS), axis=0, keepdims=True)
        hit = erow == sel
        hits.append(hit)
        vals.append(m)
        cur = jnp.where(hit, NEG_BIG * 2.0, cur)
    evs = [jnp.exp(v - vals[0]) for v in vals]
    den = (evs[0] + evs[1]) + (evs[2] + evs[3])
    gates = [ev / den for ev in evs]

    onehot = jnp.where((hits[0] | hits[1]) | (hits[2] | hits[3]), 1.0, 0.0).astype(BF16)
    ri = lax.broadcasted_iota(I32, (tt, tt), 0)
    ci = lax.broadcasted_iota(I32, (tt, tt), 1)
    earlier = jnp.where(ri < ci, 1.0, 0.0).astype(BF16)
    rank = jnp.dot(onehot, earlier, preferred_element_type=F32)
    cnt = jnp.dot(onehot, jnp.ones((tt, LANES), BF16), preferred_element_type=F32)
    c8 = jnp.bitwise_and(cnt.astype(I32) + (GRANULE - 1), -GRANULE).astype(F32)
    ue = lax.broadcasted_iota(I32, (N_EXPERTS, N_EXPERTS), 0)
    ve = lax.broadcasted_iota(I32, (N_EXPERTS, N_EXPERTS), 1)
    below = jnp.where(ve < ue, 1.0, 0.0).astype(BF16)
    off = jnp.dot(below, c8.astype(BF16), preferred_element_type=F32)
    slot_of = jnp.concatenate([off] * (tt // LANES), axis=1) + rank

    onehot_pad = jnp.concatenate([onehot, jnp.zeros((LANES - N_EXPERTS, tt), BF16)], axis=0)
    cnt_row = lax.dot_general(jnp.ones((2 * SUBLANES, tt), BF16), onehot_pad, nt_dims,
                              preferred_element_type=F32)[0:SUBLANES]
    cnt_ref[0] = jnp.bitwise_and(cnt_row.astype(I32) + (GRANULE - 1), -GRANULE)

    r_iota = lax.broadcasted_iota(I32, (rt, tt), 0)
    krow = lax.broadcasted_iota(I32, (LANES, tt), 0)
    perm = jnp.zeros((rt, tt), F32)
    pos_t = jnp.zeros((LANES, tt), F32)
    gate_t = jnp.zeros((LANES, tt), F32)
    for k in range(TOP_K):
        pk = jnp.sum(jnp.where(hits[k], slot_of, 0.0), axis=0, keepdims=True)
        perm = jnp.where(r_iota == pk.astype(I32), 1.0, perm)
        pos_t = jnp.where(krow == k, pk, pos_t)
        gate_t = jnp.where(krow == k, gates[k], gate_t)
    pos_ref[...] = pos_t.T.astype(I32)
    gate_ref[...] = gate_t.T
    xs_ref[...] = jnp.dot(perm.astype(BF16), h_hi, preferred_element_type=F32)


def _moe_combine(ys_ref, pos_ref, gate_ref):
    tt = pos_ref.shape[0]
    rt = ys_ref.shape[0]
    r_iota = lax.broadcasted_iota(I32, (tt, rt), 1)
    g = jnp.zeros((tt, rt), F32)
    for k in range(TOP_K):
        g = jnp.where(r_iota == pos_ref[:, k:k + 1], gate_ref[:, k:k + 1], g)
    return jnp.dot(g.astype(BF16), ys_ref[...].astype(BF16), preferred_element_type=F32)


def _mod_kernel(cc_ref, w_ref, b_ref, o_ref):
    a = _silu(cc_ref[...])
    o_ref[0] = jnp.dot(a, w_ref[0], precision=lax.Precision.HIGHEST,
                       preferred_element_type=F32) + b_ref[0]


def _mod_vectors(c, c_ctx, w_mod, b_mod):
    depth, d, d6 = w_mod.shape
    cc = jnp.zeros((SUBLANES, d), F32).at[0].set(c[0]).at[1].set(c_ctx)
    ncol = 4
    cw = d6 // ncol
    return pl.pallas_call(
        _mod_kernel,
        grid=(depth, ncol),
        in_specs=[pl.BlockSpec((SUBLANES, d), lambda l, j: (0, 0)),
                  pl.BlockSpec((1, d, cw), lambda l, j: (l, 0, j)),
                  pl.BlockSpec((1, 1, cw), lambda l, j: (l, 0, j))],
        out_specs=pl.BlockSpec((1, SUBLANES, cw), lambda l, j: (l, 0, j)),
        out_shape=jax.ShapeDtypeStruct((depth, SUBLANES, d6), F32),
        compiler_params=_cparams(2),
        name="mod_vectors",
    )(cc, w_mod, b_mod.reshape(depth, 1, d6))


def _qkv_kernel(mod_row, use_rope, x_ref, mod_ref, g_ref, w_ref, b_ref, qg_ref, kg_ref, rowt_ref, colt_ref,
                bd_ref, q_ref, k_ref, v_ref, wb_ref):
    d = x_ref.shape[1]
    tile = x_ref.shape[0]
    nq = N_HEADS * HEAD_DIM
    nk = N_KV_HEADS * HEAD_DIM

    @pl.when(pl.program_id(0) == 0)
    def _():
        wb_ref[...] = w_ref[...].astype(BF16)

    sh = mod_ref[mod_row:mod_row + 1, 0:d]
    sc = mod_ref[mod_row:mod_row + 1, d:2 * d]
    h = _rms_mod(x_ref[...], g_ref[...], sh, sc)
    qkv = jnp.dot(h.astype(BF16), wb_ref[...], preferred_element_type=F32) + b_ref[...]
    q = qkv[:, :nq]
    k = qkv[:, nq:nq + nk]
    v = qkv[:, nq + nk:]
    bd = bd_ref[...]

    def head_norm(t, gain):
        tt = t * t
        w = bd.shape[0]
        if t.shape[1] >= w:
            ss = jnp.concatenate([_split_dot(tt[:, c:c + w], bd) for c in range(0, t.shape[1], w)], axis=1)
        else:
            ss = _split_dot(tt, bd[:t.shape[1], :t.shape[1]])
        return t * lax.rsqrt(ss * (1.0 / HEAD_DIM) + EPS) * gain

    qn = head_norm(q, qg_ref[...])
    kn = head_norm(k, kg_ref[...])

    if use_rope:
        lane = lax.broadcasted_iota(I32, (tile, LANES), 1)
        is_row = (lane % HEAD_DIM) < HEAD_DIM // 2
        nrow = tile // GRID_W

        def table(idx):
            rt = jnp.concatenate([jnp.broadcast_to(rowt_ref[idx, r:r + 1, :], (GRID_W, LANES))
                                  for r in range(nrow)], axis=0)
            ct = jnp.concatenate([colt_ref[idx]] * nrow, axis=0)
            return jnp.where(is_row, rt, ct)

        cosv, sav, sbv = table(0), table(1), table(2)

        def rope(t):
            n = t.shape[1]
            reps = n // LANES
            ct = jnp.concatenate([cosv] * reps, axis=1) if reps > 1 else cosv
            at = jnp.concatenate([sav] * reps, axis=1) if reps > 1 else sav
            bt = jnp.concatenate([sbv] * reps, axis=1) if reps > 1 else sbv
            up = pltpu.roll(t, n - HEAD_DIM // 4, 1)
            dn = pltpu.roll(t, HEAD_DIM // 4, 1)
            return t * ct + up * at + dn * bt

        qn = rope(qn)
        kn = rope(kn)

    q_ref[...] = (qn * (HEAD_DIM ** -0.5 * LOG2_E)).astype(BF16)
    klane = lax.broadcasted_iota(I32, kn.shape, 1)
    low = klane < HEAD_DIM

    def variants(t):
        sw = pltpu.roll(t, HEAD_DIM, 1)
        z = jnp.zeros_like(t)
        return jnp.concatenate([jnp.where(low, t, z), jnp.where(low, z, sw),
                                jnp.where(low, sw, z), jnp.where(low, z, t)], axis=1)

    k_ref[...] = variants(kn).astype(BF16)
    v_ref[...] = variants(v).astype(BF16)


def _qkv_project(xt, mod_l, mod_row, use_rope, g1n, w_qkv, b_qkv, qg, kg, rowt, colt, bd, tile):
    t, d = xt.shape
    nqkv = w_qkv.shape[1]
    nq = N_HEADS * HEAD_DIM
    tok = lambda i: (i, 0)
    nrow = max(tile // GRID_W, 1)
    return pl.pallas_call(
        functools.partial(_qkv_kernel, mod_row, use_rope),
        grid=(t // tile,),
        in_specs=[pl.BlockSpec((tile, d), tok), _full(mod_l.shape), _full((1, d)),
                  _full((d, nqkv)), _full((1, nqkv)), _full((1, nq)), _full((1, LANES)),
                  pl.BlockSpec((3, nrow, LANES), lambda i: (0, i, 0)), _full(colt.shape), _full(bd.shape)],
        out_specs=[pl.BlockSpec((tile, nq), tok), pl.BlockSpec((tile, 4 * LANES), tok),
                   pl.BlockSpec((tile, 4 * LANES), tok)],
        out_shape=[jax.ShapeDtypeStruct((t, nq), BF16), jax.ShapeDtypeStruct((t, 4 * LANES), BF16),
                   jax.ShapeDtypeStruct((t, 4 * LANES), BF16)],
        scratch_shapes=[pltpu.VMEM((d, nqkv), BF16)],
        compiler_params=_cparams(1),
        name="qkv_project",
    )(xt, mod_l, g1n, w_qkv, b_qkv, qg, kg, rowt, colt, bd)


def _attn_kernel(sink_ref, q_ref, kp_ref, kc_ref, kn_ref, vp_ref, vc_ref, vn_ref, kx_ref, vx_ref, bias_ref,
                 o_ref):
    qb = q_ref.shape[0]
    band = 3 * qb
    pairs_per_group = (N_HEADS // N_KV_HEADS) // 2
    rows = pairs_per_group * qb

    bias = bias_ref[0]
    kband = jnp.concatenate([kp_ref[...], kc_ref[...], kn_ref[...]], axis=0)
    vband = jnp.concatenate([vp_ref[...], vc_ref[...], vn_ref[...]], axis=0)
    kctx = kx_ref[...]
    vctx = vx_ref[...]
    dn_t = (((1,), (1,)), ((), ()))
    rsub = lax.broadcasted_iota(I32, (rows, 1), 0) // qb

    for g in range(N_KV_HEADS):
        qg = jnp.concatenate(
            [q_ref[:, (g * pairs_per_group + j) * LANES:(g * pairs_per_group + j + 1) * LANES]
             for j in range(pairs_per_group)], axis=0)
        acc = jnp.zeros((rows, LANES), F32)
        for par in range(2):
            col = (2 * g + par) * LANES
            s_b = lax.dot_general(qg, kband[:, col:col + LANES], dn_t, preferred_element_type=F32)
            s_c = lax.dot_general(qg, kctx[:, col:col + LANES], dn_t, preferred_element_type=F32)
            s_b = s_b + bias
            sink = jnp.zeros((rows, 1), F32)
            for j in range(pairs_per_group):
                hd = 2 * (g * pairs_per_group + j) + par
                sink = jnp.where(rsub == j, sink_ref[hd] * LOG2_E, sink)
            m = jnp.maximum(jnp.maximum(jnp.max(s_b, axis=-1, keepdims=True),
                                        jnp.max(s_c, axis=-1, keepdims=True)), sink)
            p_b = jnp.exp2(s_b - m)
            p_c = jnp.exp2(s_c - m)
            l = (jnp.sum(p_b, axis=-1, keepdims=True) + jnp.sum(p_c, axis=-1, keepdims=True)
                 + jnp.exp2(sink - m))
            o = (jnp.dot(p_b.astype(BF16), vband[:, col:col + LANES], preferred_element_type=F32)
                 + jnp.dot(p_c.astype(BF16), vctx[:, col:col + LANES], preferred_element_type=F32))
            acc = acc + o / l
        for j in range(pairs_per_group):
            pcol = (g * pairs_per_group + j) * LANES
            o_ref[:, pcol:pcol + LANES] = acc[j * qb:(j + 1) * qb].astype(BF16)


def _attention(q, kk, vv, kkc, vvc, sinks):
    t, nq = q.shape
    qb = Q_BLOCK
    nb = t // qb
    c = kkc.shape[0]
    w = kk.shape[1]
    cur = lambda i, s: (i, 0)
    prv = lambda i, s: (jnp.maximum(i - 1, 0), 0)
    nxt = lambda i, s: (jnp.minimum(i + 1, nb - 1), 0)
    zero = lambda i, s: (0, 0)
    rows = (N_HEADS // N_KV_HEADS) // 2 * qb
    r = (jnp.arange(rows, dtype=I32) % qb)[:, None]
    cpos = jnp.arange(3 * qb, dtype=I32)[None, :] - qb
    inside = jnp.abs(r - cpos) <= WINDOW
    variants = [inside & ((cpos >= 0) | ((v & 1) == 0)) & ((cpos < qb) | ((v & 2) == 0)) for v in range(4)]
    bias = jnp.where(jnp.stack(variants), 0.0, NEG_BIG).astype(F32)
    which = lambda i, s: (jnp.where(i == 0, 1, 0) + jnp.where(i == nb - 1, 2, 0), 0, 0)
    gs = pltpu.PrefetchScalarGridSpec(
        num_scalar_prefetch=1,
        grid=(nb,),
        in_specs=[pl.BlockSpec((qb, nq), cur),
                  pl.BlockSpec((qb, w), prv), pl.BlockSpec((qb, w), cur), pl.BlockSpec((qb, w), nxt),
                  pl.BlockSpec((qb, w), prv), pl.BlockSpec((qb, w), cur), pl.BlockSpec((qb, w), nxt),
                  pl.BlockSpec((c, w), zero), pl.BlockSpec((c, w), zero),
                  pl.BlockSpec((1, rows, 3 * qb), which)],
        out_specs=pl.BlockSpec((qb, nq), cur),
    )
    return pl.pallas_call(
        _attn_kernel,
        grid_spec=gs,
        out_shape=jax.ShapeDtypeStruct((t, nq), BF16),
        compiler_params=_cparams(1),
        name="window_attention",
    )(sinks, q, kk, kk, kk, vv, vv, vv, kkc, vvc, bias)


def _dispatch_out(t, d, tile):
    nt = t // tile
    tok = lambda i: (i, 0)
    region = _region_rows(nt)
    specs = [pl.BlockSpec((tile, d), tok), pl.BlockSpec((region, d), tok),
             pl.BlockSpec((tile, LANES), tok), pl.BlockSpec((tile, LANES), tok),
             pl.BlockSpec((1, SUBLANES, LANES), lambda i: (i, 0, 0))]
    shapes = [jax.ShapeDtypeStruct((t, d), F32), jax.ShapeDtypeStruct((nt * region, d), F32),
              jax.ShapeDtypeStruct((t, LANES), I32), jax.ShapeDtypeStruct((t, LANES), F32),
              jax.ShapeDtypeStruct((nt, SUBLANES, LANES), I32)]
    return specs, shapes


def _oproj_kernel(x_ref, o_ref, mod_ref, wo_ref, bo_ref, g2_ref, wr_ref, br_ref,
                  x1_ref, xs_ref, pos_ref, gate_ref, cnt_ref, wb_ref):
    d = x_ref.shape[1]

    @pl.when(pl.program_id(0) == 0)
    def _():
        wb_ref[...] = wo_ref[...].astype(BF16)

    y = jnp.dot(o_ref[...], wb_ref[...], preferred_element_type=F32) + bo_ref[...]
    x1 = x_ref[...] + mod_ref[0:1, 2 * d:3 * d] * y
    x1_ref[...] = x1
    _router_dispatch(x1, g2_ref[...], mod_ref[0:1, 3 * d:4 * d], mod_ref[0:1, 4 * d:5 * d],
                     wr_ref[...], br_ref[...], xs_ref, pos_ref, gate_ref, cnt_ref)


def _oproj_router(xt, o, mod_l, w_o, b_o, g2n, wr, br):
    t, d = xt.shape
    tile = MOE_TILE
    tok = lambda i: (i, 0)
    out_specs, out_shape = _dispatch_out(t, d, tile)
    return pl.pallas_call(
        _oproj_kernel,
        grid=(t // tile,),
        in_specs=[pl.BlockSpec((tile, d), tok), pl.BlockSpec((tile, o.shape[1]), tok), _full(mod_l.shape),
                  _full(w_o.shape), _full((1, d)), _full((1, d)), _full(wr.shape), _full(br.shape)],
        out_specs=out_specs,
        out_shape=out_shape,
        scratch_shapes=[pltpu.VMEM(w_o.shape, BF16)],
        compiler_params=_cparams(1),
        name="oproj_router",
    )(xt, o, mod_l, w_o, b_o, g2n, wr, br)


def _moe_kernel(nt, region_granules, brange_ref, nvg_ref, gsrc_ref, xs_hbm, w1_ref, b1_ref, w2_ref, b2_ref,
                ys_hbm, xbuf, obuf, w1b, w2b, gsem, ssem):
    e = pl.program_id(0)
    ne = pl.num_programs(0)
    b_lo = brange_ref[e]
    b_hi = brange_ref[ne + e]
    total = brange_ref[2 * ne - 1]
    gpb = GRANULES_PER_BLOCK
    dff = w2_ref.shape[1]
    spare = _spare_per_region(nt)
    idle_granule = REGION_DATA_GRANULES + 2 * spare

    def rows_of(granule):
        return pl.ds(pl.multiple_of(granule * GRANULE, GRANULE), GRANULE)

    def gather_copy(blk, gi):
        g = gsrc_ref[blk * gpb + gi]
        src = jnp.where(g >= 0, g, idle_granule)
        s = blk % 3
        return pltpu.make_async_copy(xs_hbm.at[rows_of(src)], xbuf.at[s, pl.ds(gi * GRANULE, GRANULE)],
                                     gsem.at[s])

    def scatter_copy(blk, s, gi):
        g = gsrc_ref[blk * gpb + gi]
        own_spare = (gi % nt) * region_granules + REGION_DATA_GRANULES + gi // nt + s * spare
        dst = jnp.where(g >= 0, g, own_spare)
        return pltpu.make_async_copy(obuf.at[s, pl.ds(gi * GRANULE, GRANULE)], ys_hbm.at[rows_of(dst)],
                                     ssem.at[s])

    def start_gather(blk):
        for gi in range(gpb):
            gather_copy(blk, gi).start()

    def wait_gather(s):
        pltpu.make_async_copy(xs_hbm.at[pl.ds(0, EXPERT_ROWS)], xbuf.at[s], gsem.at[s]).wait()

    def wait_scatter(s):
        pltpu.make_async_copy(obuf.at[s], ys_hbm.at[pl.ds(0, EXPERT_ROWS)], ssem.at[s]).wait()

    @pl.when(e == 0)
    def _():
        obuf[...] = jnp.zeros(obuf.shape, F32)
        start_gather(0)
        start_gather(1)

    def expert_block(b, rows):
        slot = b % 2
        start_gather(b + 2)
        xb = xbuf[b % 3, 0:rows, :].astype(BF16)
        gu = jnp.dot(xb, w1b[...], preferred_element_type=F32) + b1_ref[0]
        gt = jnp.minimum(gu[:, :dff], SWIGLU_LIMIT)
        up = jnp.clip(gu[:, dff:], -SWIGLU_LIMIT, SWIGLU_LIMIT)
        act = gt * jax.nn.sigmoid(SWIGLU_ALPHA * gt) * (up + 1.0)
        obuf[slot, 0:rows, :] = jnp.dot(act.astype(BF16), w2b[...], preferred_element_type=F32) + b2_ref[0]
        for gi in range(gpb):
            scatter_copy(b, slot, gi).start()

    def block_body(b, carry):
        wait_gather(b % 3)

        @pl.when(b >= 2)
        def _():
            wait_scatter(b % 2)

        half = EXPERT_ROWS // 2
        rows_needed = nvg_ref[b] * GRANULE

        @pl.when(rows_needed > half)
        def _():
            expert_block(b, EXPERT_ROWS)

        @pl.when(rows_needed <= half)
        def _():
            expert_block(b, half)

        return carry

    @pl.when(b_hi > b_lo)
    def _():
        w1b[...] = w1_ref[0].astype(BF16)
        w2b[...] = w2_ref[0].astype(BF16)
        lax.fori_loop(b_lo, b_hi, block_body, 0)

    @pl.when(e == ne - 1)
    def _():
        wait_gather(total % 3)
        wait_gather((total + 1) % 3)

        @pl.when(total >= 2)
        def _():
            wait_scatter(total % 2)

        @pl.when(total >= 1)
        def _():
            wait_scatter((total + 1) % 2)


def _moe_experts(xs, nt, block_range, n_granules, granule_src, layer, w1, b1, w2, b2):
    d = xs.shape[1]
    dff = w2.shape[2]
    n_layers, n_exp = w1.shape[:2]
    by_expert = lambda e, br, ng, gs: (layer, e, 0, 0)
    gs = pltpu.PrefetchScalarGridSpec(
        num_scalar_prefetch=3,
        grid=(n_exp,),
        in_specs=[pl.BlockSpec(memory_space=pl.ANY),
                  pl.BlockSpec((None, 1, d, 2 * dff), by_expert), pl.BlockSpec((None, 1, 1, 2 * dff), by_expert),
                  pl.BlockSpec((None, 1, dff, d), by_expert), pl.BlockSpec((None, 1, 1, d), by_expert)],
        out_specs=pl.BlockSpec(memory_space=pl.ANY),
        scratch_shapes=[pltpu.VMEM((3, EXPERT_ROWS, d), F32), pltpu.VMEM((2, EXPERT_ROWS, d), F32),
                        pltpu.VMEM((d, 2 * dff), BF16), pltpu.VMEM((dff, d), BF16),
                        pltpu.SemaphoreType.DMA((3,)), pltpu.SemaphoreType.DMA((2,))],
    )
    return pl.pallas_call(
        functools.partial(_moe_kernel, nt, _region_rows(nt) // GRANULE),
        grid_spec=gs,
        out_shape=jax.ShapeDtypeStruct(xs.shape, F32),
        input_output_aliases={3: 0},
        compiler_params=_cparams(1),
        name="moe_experts",
    )(block_range, n_granules, granule_src, xs, w1, b1.reshape(n_layers, n_exp, 1, 2 * dff),
      w2, b2.reshape(n_layers, n_exp, 1, d))


TABLE_BLOCKS = 8


def _tables_kernel(region_granules, cnt_ref, brange_ref, ngran_ref, gsrc_ref):
    gpb = GRANULES_PER_BLOCK
    log_b = gpb.bit_length() - 1
    n = cnt_ref.shape[0]
    rows = TABLE_BLOCKS * gpb
    nt_dims = (((1,), (1,)), ((), ()))

    cnt = cnt_ref[...].astype(F32)
    gl_len = cnt * (1.0 / GRANULE)
    ri = lax.broadcasted_iota(I32, (n, n), 0)
    ci = lax.broadcasted_iota(I32, (n, n), 1)
    lower = jnp.where(ci < ri, 1.0, 0.0).astype(BF16)
    upper = jnp.where(ri < ci, 1.0, 0.0).astype(BF16)
    pre = jnp.dot(lower, gl_len.astype(BF16), preferred_element_type=F32)
    off = jnp.dot(cnt.astype(BF16), upper, preferred_element_type=F32)
    src = ri.astype(F32) * float(region_granules) + off * (1.0 / GRANULE)
    n_g = jnp.sum(gl_len, axis=0, keepdims=True)
    blocks = ((n_g.astype(I32) + (gpb - 1)) >> log_b).astype(F32)
    bstart = jnp.dot(jnp.broadcast_to(blocks, (SUBLANES, n)).astype(BF16), upper,
                     preferred_element_type=F32)[0:1]
    bend = bstart + blocks

    def digits(v, base):
        hi = jnp.floor(v * (1.0 / base))
        return hi.astype(BF16), (v - hi * base).astype(BF16)

    def pick(onehot, table):
        return lax.dot_general(onehot, table, nt_dims, preferred_element_type=F32)

    def pick2(onehot, v, base):
        hi, lo = digits(v, base)
        return pick(onehot, hi) * base + pick(onehot, lo)

    q = pl.program_id(0) * rows + lax.broadcasted_iota(I32, (rows, n), 0)
    bq = (q >> log_b).astype(F32)
    iq = (q & (gpb - 1)).astype(F32)
    own = jnp.where((bstart <= bq) & (bq < bend), 1.0, 0.0).astype(BF16)
    pre_q = pick2(own, pre, 64.0)
    len_q = pick(own, gl_len.astype(BF16))
    src_q = pick2(own, src, 128.0)
    bstart_q = pick(own, jnp.broadcast_to(bstart, (n, n)).astype(BF16))
    gl = (bq - bstart_q) * float(gpb) + iq
    inside = (pre_q <= gl) & (gl < pre_q + len_q)
    hit = jnp.sum(jnp.where(inside, src_q - pre_q + gl + 1.0, 0.0), axis=-1, keepdims=True)
    gsrc_ref[...] = jnp.broadcast_to(hit - 1.0, (rows, n)).astype(I32)

    bb = (pl.program_id(0) * TABLE_BLOCKS + lax.broadcasted_iota(I32, (TABLE_BLOCKS, n), 0)).astype(F32)
    lane = lax.broadcasted_iota(I32, (TABLE_BLOCKS, n), 1).astype(F32)
    mine = (bstart <= bb) & (bb < bend)
    red = lambda v: jnp.sum(jnp.where(mine, v, 0.0), axis=-1, keepdims=True)
    active = red(jnp.ones_like(lane))
    n_left = red(jnp.broadcast_to(n_g, mine.shape)) - (bb[:, 0:1] - red(jnp.broadcast_to(bstart, mine.shape))) * gpb
    ngran_ref[...] = jnp.broadcast_to(jnp.clip(n_left, 0.0, float(gpb)) * active, (TABLE_BLOCKS, n)).astype(I32)

    sub = lax.broadcasted_iota(I32, (SUBLANES, n), 0)
    brange_ref[...] = jnp.where(sub == 0, bstart, jnp.where(sub == 1, bend, 0.0)).astype(I32)


def _block_tables(cnt):
    nt = cnt.shape[0]
    assert nt <= LANES
    max_blocks = (nt * REGION_DATA_GRANULES) // GRANULES_PER_BLOCK + N_EXPERTS
    n_blocks = -(-(max_blocks + 2) // TABLE_BLOCKS) * TABLE_BLOCKS
    gpb = GRANULES_PER_BLOCK
    cnt_sq = jnp.zeros((LANES, LANES), I32).at[:nt].set(cnt[:, 0, :])
    by_step = lambda i: (i, 0)
    brange, ngran, gsrc = pl.pallas_call(
        functools.partial(_tables_kernel, _region_rows(nt) // GRANULE),
        grid=(n_blocks // TABLE_BLOCKS,),
        in_specs=[_full((LANES, LANES))],
        out_specs=[_full((SUBLANES, LANES)), pl.BlockSpec((TABLE_BLOCKS, LANES), by_step),
                   pl.BlockSpec((TABLE_BLOCKS * gpb, LANES), by_step)],
        out_shape=[jax.ShapeDtypeStruct((SUBLANES, LANES), I32), jax.ShapeDtypeStruct((n_blocks, LANES), I32),
                   jax.ShapeDtypeStruct((n_blocks * gpb, LANES), I32)],
        compiler_params=_cparams(1),
        name="moe_block_tables",
    )(cnt_sq)
    return brange[0:2, :N_EXPERTS].reshape(-1), ngran[:, 0], gsrc[:, 0]


def _combine_kernel(x_ref, ys_ref, pos_ref, gate_ref, mod_ref, o_ref):
    d = x_ref.shape[1]
    o_ref[...] = x_ref[...] + mod_ref[0:1, 5 * d:6 * d] * _moe_combine(ys_ref, pos_ref, gate_ref)


def _combine_glu_kernel(x_ref, ys_ref, pos_ref, gate_ref, modp_ref, modn_ref, g_ref, w_ref, b_ref,
                        o_ref, u_ref, wb_ref):
    d = x_ref.shape[1]

    @pl.when(pl.program_id(0) == 0)
    def _():
        wb_ref[...] = w_ref[...].astype(BF16)

    x2 = x_ref[...] + modp_ref[0:1, 5 * d:6 * d] * _moe_combine(ys_ref, pos_ref, gate_ref)
    o_ref[...] = x2
    h = _rms_mod(x2, g_ref[...], modn_ref[0:1, 0:d], modn_ref[0:1, d:2 * d])
    u = jnp.dot(h.astype(BF16), wb_ref[...], preferred_element_type=F32) + b_ref[...]
    u_ref[...] = u[:, :d] * jax.nn.sigmoid(u[:, d:])


def _combine_in_specs(t, d, tile):
    tok = lambda i: (i, 0)
    return [pl.BlockSpec((tile, d), tok), pl.BlockSpec((_region_rows(t // tile), d), tok),
            pl.BlockSpec((tile, LANES), tok), pl.BlockSpec((tile, LANES), tok)]


def _combine(x1, ys, pos, gates, mod_l):
    t, d = x1.shape
    tile = MOE_TILE
    return pl.pallas_call(
        _combine_kernel,
        grid=(t // tile,),
        in_specs=_combine_in_specs(t, d, tile) + [_full(mod_l.shape)],
        out_specs=pl.BlockSpec((tile, d), lambda i: (i, 0)),
        out_shape=jax.ShapeDtypeStruct((t, d), F32),
        compiler_params=_cparams(1),
        name="moe_combine",
    )(x1, ys, pos, gates, mod_l)


def _combine_glu(x1, ys, pos, gates, mod_prev, mod_next, g1n, w_pw1, b_pw1):
    t, d = x1.shape
    tile = MOE_TILE
    tok = lambda i: (i, 0)
    return pl.pallas_call(
        _combine_glu_kernel,
        grid=(t // tile,),
        in_specs=_combine_in_specs(t, d, tile)
                 + [_full(mod_prev.shape), _full(mod_next.shape), _full((1, d)),
                    _full(w_pw1.shape), _full((1, 2 * d))],
        out_specs=[pl.BlockSpec((tile, d), tok), pl.BlockSpec((tile, d), tok)],
        out_shape=[jax.ShapeDtypeStruct((t, d), F32), jax.ShapeDtypeStruct((t, d), F32)],
        scratch_shapes=[pltpu.VMEM(w_pw1.shape, BF16)],
        compiler_params=_cparams(1),
        name="combine_pw1_glu",
    )(x1, ys, pos, gates, mod_prev, mod_next, g1n, w_pw1, b_pw1)


def _conv_kernel(x_ref, up_ref, uc_ref, un_ref, mod_ref, wdw_ref, bdw_ref, lg_ref, lb_ref, w2_ref, b2_ref,
                 g2_ref, wr_ref, br_ref, x1_ref, xs_ref, pos_ref, gate_ref, cnt_ref, wb_ref, ubuf, shbuf, cbuf):
    i = pl.program_id(0)
    n = pl.num_programs(0)
    d = x_ref.shape[1]
    tile = x_ref.shape[0]
    halo = up_ref.shape[0]

    @pl.when(i == 0)
    def _():
        wb_ref[...] = w2_ref[...].astype(BF16)

    ubuf[0:halo, :] = jnp.where(i > 0, up_ref[...], 0.0)
    ubuf[halo:halo + tile, :] = uc_ref[...]
    ubuf[halo + tile:, :] = jnp.where(i < n - 1, un_ref[...], 0.0)

    span = tile + 2 * halo - SUBLANES
    for r in range(SUBLANES):
        shbuf[r, 0:span, :] = ubuf[r:r + span, :]
    sub = CONV_CHUNK // SUBLANES
    for c0 in range(0, tile, CONV_CHUNK):
        parts = [jnp.zeros((SUBLANES, d), F32) + bdw_ref[...] for _ in range(sub)]
        for j in range(CONV_WIDTH):
            a, r = divmod(halo + j - CONV_PAD, SUBLANES)
            wj = wdw_ref[j * SUBLANES:(j + 1) * SUBLANES, :]
            for k in range(sub):
                lo = c0 + (a + k) * SUBLANES
                parts[k] = parts[k] + shbuf[r, lo:lo + SUBLANES, :] * wj
        for k in range(sub):
            cbuf[c0 + k * SUBLANES:c0 + (k + 1) * SUBLANES, :] = parts[k]
    acc = cbuf[...]

    mu = jnp.mean(acc, axis=-1, keepdims=True)
    cen = acc - mu
    var = jnp.mean(cen * cen, axis=-1, keepdims=True)
    z = _silu(cen * lax.rsqrt(var + EPS) * lg_ref[...] + lb_ref[...])
    y = jnp.dot(z.astype(BF16), wb_ref[...], preferred_element_type=F32) + b2_ref[...]
    x1 = x_ref[...] + mod_ref[0:1, 2 * d:3 * d] * y
    x1_ref[...] = x1
    _router_dispatch(x1, g2_ref[...], mod_ref[0:1, 3 * d:4 * d], mod_ref[0:1, 4 * d:5 * d],
                     wr_ref[...], br_ref[...], xs_ref, pos_ref, gate_ref, cnt_ref)


def _conv_router(x2, u, mod_l, w_dw, b_dw, ln_g, ln_b, w_pw2, b_pw2, g2n, wr, br):
    t, d = x2.shape
    tile = MOE_TILE
    nt = t // tile
    hb = tile // CONV_HALO
    nh = t // CONV_HALO
    tok = lambda i: (i, 0)
    prv = lambda i: (jnp.maximum(i * hb - 1, 0), 0)
    nxt = lambda i: (jnp.minimum((i + 1) * hb, nh - 1), 0)
    wpad = jnp.repeat(w_dw, SUBLANES, axis=0)
    out_specs, out_shape = _dispatch_out(t, d, tile)
    return pl.pallas_call(
        _conv_kernel,
        grid=(nt,),
        in_specs=[pl.BlockSpec((tile, d), tok), pl.BlockSpec((CONV_HALO, d), prv),
                  pl.BlockSpec((tile, d), tok), pl.BlockSpec((CONV_HALO, d), nxt),
                  _full(mod_l.shape), _full(wpad.shape), _full((1, d)), _full((1, d)), _full((1, d)),
                  _full(w_pw2.shape), _full((1, d)), _full((1, d)), _full(wr.shape), _full(br.shape)],
        out_specs=out_specs,
        out_shape=out_shape,
        scratch_shapes=[pltpu.VMEM(w_pw2.shape, BF16), pltpu.VMEM((tile + 2 * CONV_HALO, d), F32),
                        pltpu.VMEM((SUBLANES, tile + 2 * CONV_HALO, d), F32), pltpu.VMEM((tile, d), F32)],
        compiler_params=_cparams(1),
        name="conv_router",
    )(x2, u, u, u, mod_l, wpad, b_dw, ln_g, ln_b, w_pw2, b_pw2, g2n, wr, br)


def _rope_tables(n_rows):
    quarter = HEAD_DIM // 4
    inv = ROPE_BASE ** (-jnp.arange(quarter, dtype=F32) / quarter)

    def tables(npos):
        ang = jnp.arange(npos, dtype=I32).astype(F32)[:, None] * inv[None, :]
        z = jnp.zeros_like(ang)
        c, s = jnp.cos(ang), jnp.sin(ang)
        half = lambda a, b: jnp.concatenate([a, b], axis=1)
        full = lambda h: jnp.concatenate([h, h, h, h], axis=1)
        return jnp.stack([full(half(c, c)), full(half(-s, z)), full(half(z, s))])

    return tables(n_rows), tables(GRID_W)


def _router_params(w_router, b_router):
    d, e = w_router.shape
    wr = jnp.zeros((LANES, d), F32).at[:e].set(w_router.T)
    br = jnp.zeros((LANES, MOE_TILE), F32).at[:e].set(jnp.broadcast_to(b_router[:, None], (e, MOE_TILE)))
    hi = wr.astype(BF16)
    lo = (wr - hi.astype(F32)).astype(BF16)
    return jnp.stack([hi, lo]), br


def kernel(x, c, ctx, c_ctx, w_mod, b_mod, norm1_g, norm2_g, attn_w_qkv, attn_b_qkv, attn_w_o, attn_b_o,
           attn_q_norm, attn_k_norm, attn_sinks, conv_w_pw1, conv_b_pw1, conv_w_dw, conv_b_dw, conv_ln_g,
           conv_ln_b, conv_w_pw2, conv_b_pw2, moe_w_router, moe_b_router, moe_w1, moe_b1, moe_w2, moe_b2):
    bsz, t, d = x.shape
    assert bsz == 1 and w_mod.shape[0] == 2
    n_ctx = ctx.shape[1]
    tile = min(TOKEN_TILE, t)
    assert t % tile == 0 and t % Q_BLOCK == 0 and t % MOE_TILE == 0 and tile % GRID_W == 0
    xt = x.reshape(t, d)
    row = lambda v: v.reshape(1, -1)

    mod = _mod_vectors(c, c_ctx, w_mod, b_mod)
    mod0, mod1 = mod[0], mod[1]

    bd = (jnp.arange(2 * LANES)[:, None] // HEAD_DIM == jnp.arange(2 * LANES)[None, :] // HEAD_DIM).astype(BF16)
    qg = row(jnp.tile(attn_q_norm[0], N_HEADS))
    kg = row(jnp.tile(attn_k_norm[0], N_KV_HEADS))
    rowt, colt = _rope_tables(t // GRID_W)
    g1n = row(norm1_g[0])
    bq = row(attn_b_qkv[0])
    q, kk, vv = _qkv_project(xt, mod0, 0, True, g1n, attn_w_qkv[0], bq, qg, kg, rowt, colt, bd, tile)
    _, kkc, vvc = _qkv_project(ctx.reshape(n_ctx, d), mod0, 1, False, g1n, attn_w_qkv[0], bq, qg, kg,
                               rowt[:, :max(n_ctx // GRID_W, 1)], colt, bd, n_ctx)
    o = _attention(q, kk, vv, kkc, vvc, attn_sinks[0])
    wr0, br0 = _router_params(moe_w_router[0], moe_b_router[0])
    x1, xs, pos, gates, cnt = _oproj_router(xt, o, mod0, attn_w_o[0], row(attn_b_o[0]), row(norm2_g[0]),
                                            wr0, br0)
    ys = _moe_experts(xs, cnt.shape[0], *_block_tables(cnt), 0, moe_w1, moe_b1, moe_w2, moe_b2)

    x2, u = _combine_glu(x1, ys, pos, gates, mod0, mod1, row(norm1_g[1]), conv_w_pw1[0], row(conv_b_pw1[0]))
    wr1, br1 = _router_params(moe_w_router[1], moe_b_router[1])
    x3, xs, pos, gates, cnt = _conv_router(x2, u, mod1, conv_w_dw[0], row(conv_b_dw[0]), row(conv_ln_g[0]),
                                           row(conv_ln_b[0]), conv_w_pw2[0], row(conv_b_pw2[0]),
                                           row(norm2_g[1]), wr1, br1)
    ys = _moe_experts(xs, cnt.shape[0], *_block_tables(cnt), 1, moe_w1, moe_b1, moe_w2, moe_b2)
    out = _combine(x3, ys, pos, gates, mod1)
    return out.reshape(1, t, d)
```

```python
import functools

import jax
import jax.numpy as jnp
from jax import lax
from jax.experimental import pallas as pl
from jax.experimental.pallas import tpu as pltpu

F32 = jnp.float32
BF16 = jnp.bfloat16
I32 = jnp.int32

N_HEADS = 16
N_KV_HEADS = 2
HEAD_DIM = 64
GRID_W = 64
WINDOW = 128
Q_BLOCK = 128
ROPE_BASE = 10000.0
CONV_WIDTH = 31
CONV_PAD = CONV_WIDTH // 2
N_EXPERTS = 32
TOP_K = 4
SWIGLU_LIMIT = 7.0
SWIGLU_ALPHA = 1.702
EPS = 1e-6

LANES = 128
SUBLANES = 8
TOKEN_TILE = 512
MOE_TILE = 256
GRANULE = SUBLANES
EXPERT_ROWS = 512
GRANULES_PER_BLOCK = EXPERT_ROWS // GRANULE
REGION_DATA_GRANULES = (MOE_TILE * TOP_K + N_EXPERTS * (GRANULE - 1)) // GRANULE


def _spare_per_region(nt):
    return -(-GRANULES_PER_BLOCK // nt)


def _region_rows(nt):
    rows = (REGION_DATA_GRANULES + 2 * _spare_per_region(nt) + 1) * GRANULE
    return -(-rows // 256) * 256
CONV_HALO = 16
CONV_CHUNK = 32
NEG_BIG = -1e30
LOG2_E = 1.4426950408889634
VMEM_LIMIT = 56 * 1024 * 1024


def _cparams(n_axes=1, vmem=VMEM_LIMIT):
    return pltpu.CompilerParams(dimension_semantics=("arbitrary",) * n_axes, vmem_limit_bytes=vmem)


def _full(shape):
    nd = len(shape)
    return pl.BlockSpec(shape, lambda *_: (0,) * nd)


def _silu(v):
    return v * jax.nn.sigmoid(v)


def _rms_mod(xv, g, shift, scale):
    ms = jnp.mean(xv * xv, axis=-1, keepdims=True)
    return (xv * lax.rsqrt(ms + EPS)) * g * (1.0 + scale) + shift


def _split_dot(a, b_bf16):
    hi = a.astype(BF16)
    lo = (a - hi.astype(F32)).astype(BF16)
    return (jnp.dot(hi, b_bf16, preferred_element_type=F32)
            + jnp.dot(lo, b_bf16, preferred_element_type=F32))


def _router_dispatch(x_new, g2n, sh2, sc2, wr, br, xs_ref, pos_ref, gate_ref, cnt_ref):
    tt = x_new.shape[0]
    rt = xs_ref.shape[0]
    h2 = _rms_mod(x_new, g2n, sh2, sc2)
    h_hi = h2.astype(BF16)
    h_lo = (h2 - h_hi.astype(F32)).astype(BF16)
    nt_dims = (((1,), (1,)), ((), ()))
    wdot = lambda w, h: lax.dot_general(w, h, nt_dims, preferred_element_type=F32)
    logits = (wdot(wr[0], h_hi) + (wdot(wr[0], h_lo) + wdot(wr[1], h_hi)) + br)[0:N_EXPERTS]
    erow = lax.broadcasted_iota(I32, (N_EXPERTS, tt), 0)
    hits, vals = [], []
    cur = logits
    for k in range(TOP_K):
        m = jnp.max(cur, axis=0, keepdims=True)
        sel = jnp.min(jnp.where(cur == m, erow, N_EXPERTS), axis=0, keepdims=True)
        hit = erow == sel
        hits.append(hit)
        vals.append(m)
        cur = jnp.where(hit, NEG_BIG * 2.0, cur)
    evs = [jnp.exp(v - vals[0]) for v in vals]
    den = (evs[0] + evs[1]) + (evs[2] + evs[3])
    gates = [ev / den for ev in evs]

    onehot = jnp.where((hits[0] | hits[1]) | (hits[2] | hits[3]), 1.0, 0.0).astype(BF16)
    ri = lax.broadcasted_iota(I32, (tt, tt), 0)
    ci = lax.broadcasted_iota(I32, (tt, tt), 1)
    earlier = jnp.where(ri < ci, 1.0, 0.0).astype(BF16)
    rank = jnp.dot(onehot, earlier, preferred_element_type=F32)
    cnt = jnp.dot(onehot, jnp.ones((tt, LANES), BF16), preferred_element_type=F32)
    c8 = jnp.bitwise_and(cnt.astype(I32) + (GRANULE - 1), -GRANULE).astype(F32)
    ue = lax.broadcasted_iota(I32, (N_EXPERTS, N_EXPERTS), 0)
    ve = lax.broadcasted_iota(I32, (N_EXPERTS, N_EXPERTS), 1)
    below = jnp.where(ve < ue, 1.0, 0.0).astype(BF16)
    off = jnp.dot(below, c8.astype(BF16), preferred_element_type=F32)
    slot_of = jnp.concatenate([off] * (tt // LANES), axis=1) + rank

    onehot_pad = jnp.concatenate([onehot, jnp.zeros((LANES - N_EXPERTS, tt), BF16)], axis=0)
    cnt_row = lax.dot_general(jnp.ones((2 * SUBLANES, tt), BF16), onehot_pad, nt_dims,
                              preferred_element_type=F32)[0:SUBLANES]
    cnt_ref[0] = jnp.bitwise_and(cnt_row.astype(I32) + (GRANULE - 1), -GRANULE)

    r_iota = lax.broadcasted_iota(I32, (rt, tt), 0)
    krow = lax.broadcasted_iota(I32, (LANES, tt), 0)
    perm = jnp.zeros((rt, tt), F32)
    pos_t = jnp.zeros((LANES, tt), F32)
    gate_t = jnp.zeros((LANES, tt), F32)
    for k in range(TOP_K):
        pk = jnp.sum(jnp.where(hits[k], slot_of, 0.0), axis=0, keepdims=True)
        perm = jnp.where(r_iota == pk.astype(I32), 1.0, perm)
        pos_t = jnp.where(krow == k, pk, pos_t)
        gate_t = jnp.where(krow == k, gates[k], gate_t)
    pos_ref[...] = pos_t.T.astype(I32)
    gate_ref[...] = gate_t.T
    xs_ref[...] = _pack_halves(jnp.dot(perm.astype(BF16), h_hi, preferred_element_type=F32))


def _pack_halves(v):
    half = v.shape[1] // 2
    lo = lax.bitcast_convert_type(v[:, :half], jnp.uint32) >> 16
    hi = lax.bitcast_convert_type(v[:, half:], jnp.uint32) & jnp.uint32(0xFFFF0000)
    return lo | hi


def _unpack_halves(w):
    return (lax.bitcast_convert_type(w << 16, F32),
            lax.bitcast_convert_type(w & jnp.uint32(0xFFFF0000), F32))


def _moe_combine(ys_ref, pos_ref, gate_ref):
    tt = pos_ref.shape[0]
    rt = ys_ref.shape[0]
    r_iota = lax.broadcasted_iota(I32, (tt, rt), 1)
    g = jnp.zeros((tt, rt), F32)
    for k in range(TOP_K):
        g = jnp.where(r_iota == pos_ref[:, k:k + 1], gate_ref[:, k:k + 1], g)
    g = g.astype(BF16)
    ya, yb = _unpack_halves(ys_ref[...])
    return jnp.concatenate([jnp.dot(g, ya.astype(BF16), preferred_element_type=F32),
                            jnp.dot(g, yb.astype(BF16), preferred_element_type=F32)], axis=1)


def _mod_kernel(cc_ref, w_ref, b_ref, o_ref):
    a = _silu(cc_ref[...])
    o_ref[0] = jnp.dot(a, w_ref[0], precision=lax.Precision.HIGHEST,
                       preferred_element_type=F32) + b_ref[0]


def _mod_vectors(c, c_ctx, w_mod, b_mod):
    depth, d, d6 = w_mod.shape
    cc = jnp.zeros((SUBLANES, d), F32).at[0].set(c[0]).at[1].set(c_ctx)
    ncol = 4
    cw = d6 // ncol
    return pl.pallas_call(
        _mod_kernel,
        grid=(depth, ncol),
        in_specs=[pl.BlockSpec((SUBLANES, d), lambda l, j: (0, 0)),
                  pl.BlockSpec((1, d, cw), lambda l, j: (l, 0, j)),
                  pl.BlockSpec((1, 1, cw), lambda l, j: (l, 0, j))],
        out_specs=pl.BlockSpec((1, SUBLANES, cw), lambda l, j: (l, 0, j)),
        out_shape=jax.ShapeDtypeStruct((depth, SUBLANES, d6), F32),
        compiler_params=_cparams(2),
        name="mod_vectors",
    )(cc, w_mod, b_mod.reshape(depth, 1, d6))


def _qkv_kernel(mod_row, use_rope, x_ref, mod_ref, g_ref, w_ref, b_ref, qg_ref, kg_ref, rowt_ref, colt_ref,
                bd_ref, q_ref, k_ref, v_ref, wb_ref):
    d = x_ref.shape[1]
    tile = x_ref.shape[0]
    nq = N_HEADS * HEAD_DIM
    nk = N_KV_HEADS * HEAD_DIM

    @pl.when(pl.program_id(0) == 0)
    def _():
        wb_ref[...] = w_ref[...].astype(BF16)

    sh = mod_ref[mod_row:mod_row + 1, 0:d]
    sc = mod_ref[mod_row:mod_row + 1, d:2 * d]
    h = _rms_mod(x_ref[...], g_ref[...], sh, sc)
    qkv = jnp.dot(h.astype(BF16), wb_ref[...], preferred_element_type=F32) + b_ref[...]
    q = qkv[:, :nq]
    k = qkv[:, nq:nq + nk]
    v = qkv[:, nq + nk:]
    bd = bd_ref[...]

    def head_norm(t, gain):
        tt = t * t
        w = bd.shape[0]
        if t.shape[1] >= w:
            ss = jnp.concatenate([_split_dot(tt[:, c:c + w], bd) for c in range(0, t.shape[1], w)], axis=1)
        else:
            ss = _split_dot(tt, bd[:t.shape[1], :t.shape[1]])
        return t * lax.rsqrt(ss * (1.0 / HEAD_DIM) + EPS) * gain

    qn = head_norm(q, qg_ref[...])
    kn = head_norm(k, kg_ref[...])

    if use_rope:
        lane = lax.broadcasted_iota(I32, (tile, LANES), 1)
        is_row = (lane % HEAD_DIM) < HEAD_DIM // 2
        nrow = tile // GRID_W

        def table(idx):
            rt = jnp.concatenate([jnp.broadcast_to(rowt_ref[idx, r:r + 1, :], (GRID_W, LANES))
                                  for r in range(nrow)], axis=0)
            ct = jnp.concatenate([colt_ref[idx]] * nrow, axis=0)
            return jnp.where(is_row, rt, ct)

        cosv, sav, sbv = table(0), table(1), table(2)

        def rope(t):
            n = t.shape[1]
            reps = n // LANES
            ct = jnp.concatenate([cosv] * reps, axis=1) if reps > 1 else cosv
            at = jnp.concatenate([sav] * reps, axis=1) if reps > 1 else sav
            bt = jnp.concatenate([sbv] * reps, axis=1) if reps > 1 else sbv
            up = pltpu.roll(t, n - HEAD_DIM // 4, 1)
            dn = pltpu.roll(t, HEAD_DIM // 4, 1)
            return t * ct + up * at + dn * bt

        qn = rope(qn)
        kn = rope(kn)

    q_ref[...] = (qn * (HEAD_DIM ** -0.5 * LOG2_E)).astype(BF16)
    klane = lax.broadcasted_iota(I32, kn.shape, 1)
    low = klane < HEAD_DIM

    def variants(t):
        sw = pltpu.roll(t, HEAD_DIM, 1)
        z = jnp.zeros_like(t)
        return jnp.concatenate([jnp.where(low, t, z), jnp.where(low, z, sw),
                                jnp.where(low, sw, z), jnp.where(low, z, t)], axis=1)

    k_ref[...] = variants(kn).astype(BF16)
    v_ref[...] = variants(v).astype(BF16)


def _qkv_project(xt, mod_l, mod_row, use_rope, g1n, w_qkv, b_qkv, qg, kg, rowt, colt, bd, tile):
    t, d = xt.shape
    nqkv = w_qkv.shape[1]
    nq = N_HEADS * HEAD_DIM
    tok = lambda i: (i, 0)
    nrow = max(tile // GRID_W, 1)
    return pl.pallas_call(
        functools.partial(_qkv_kernel, mod_row, use_rope),
        grid=(t // tile,),
        in_specs=[pl.BlockSpec((tile, d), tok), _full(mod_l.shape), _full((1, d)),
                  _full((d, nqkv)), _full((1, nqkv)), _full((1, nq)), _full((1, LANES)),
                  pl.BlockSpec((3, nrow, LANES), lambda i: (0, i, 0)), _full(colt.shape), _full(bd.shape)],
        out_specs=[pl.BlockSpec((tile, nq), tok), pl.BlockSpec((tile, 4 * LANES), tok),
                   pl.BlockSpec((tile, 4 * LANES), tok)],
        out_shape=[jax.ShapeDtypeStruct((t, nq), BF16), jax.ShapeDtypeStruct((t, 4 * LANES), BF16),
                   jax.ShapeDtypeStruct((t, 4 * LANES), BF16)],
        scratch_shapes=[pltpu.VMEM((d, nqkv), BF16)],
        compiler_params=_cparams(1),
        name="qkv_project",
    )(xt, mod_l, g1n, w_qkv, b_qkv, qg, kg, rowt, colt, bd)


def _attn_kernel(sink_ref, q_ref, kp_ref, kc_ref, kn_ref, vp_ref, vc_ref, vn_ref, kx_ref, vx_ref, bias_ref,
                 o_ref):
    qb = q_ref.shape[0]
    band = 3 * qb
    pairs_per_group = (N_HEADS // N_KV_HEADS) // 2
    rows = pairs_per_group * qb

    bias = bias_ref[0]
    kband = jnp.concatenate([kp_ref[...], kc_ref[...], kn_ref[...]], axis=0)
    vband = jnp.concatenate([vp_ref[...], vc_ref[...], vn_ref[...]], axis=0)
    kctx = kx_ref[...]
    vctx = vx_ref[...]
    dn_t = (((1,), (1,)), ((), ()))
    rsub = lax.broadcasted_iota(I32, (rows, 1), 0) // qb

    for g in range(N_KV_HEADS):
        qg = jnp.concatenate(
            [q_ref[:, (g * pairs_per_group + j) * LANES:(g * pairs_per_group + j + 1) * LANES]
             for j in range(pairs_per_group)], axis=0)
        acc = jnp.zeros((rows, LANES), F32)
        for par in range(2):
            col = (2 * g + par) * LANES
            s_b = lax.dot_general(qg, kband[:, col:col + LANES], dn_t, preferred_element_type=F32)
            s_c = lax.dot_general(qg, kctx[:, col:col + LANES], dn_t, preferred_element_type=F32)
            s_b = s_b + bias
            sink = jnp.zeros((rows, 1), F32)
            for j in range(pairs_per_group):
                hd = 2 * (g * pairs_per_group + j) + par
                sink = jnp.where(rsub == j, sink_ref[hd] * LOG2_E, sink)
            m = jnp.maximum(jnp.maximum(jnp.max(s_b, axis=-1, keepdims=True),
                                        jnp.max(s_c, axis=-1, keepdims=True)), sink)
            p_b = jnp.exp2(s_b - m)
            p_c = jnp.exp2(s_c - m)
            l = (jnp.sum(p_b, axis=-1, keepdims=True) + jnp.sum(p_c, axis=-1, keepdims=True)
                 + jnp.exp2(sink - m))
            o = (jnp.dot(p_b.astype(BF16), vband[:, col:col + LANES], preferred_element_type=F32)
                 + jnp.dot(p_c.astype(BF16), vctx[:, col:col + LANES], preferred_element_type=F32))
            acc = acc + o / l
        for j in range(pairs_per_group):
            pcol = (g * pairs_per_group + j) * LANES
            o_ref[:, pcol:pcol + LANES] = acc[j * qb:(j + 1) * qb].astype(BF16)


def _attention(q, kk, vv, kkc, vvc, sinks):
    t, nq = q.shape
    qb = Q_BLOCK
    nb = t // qb
    c = kkc.shape[0]
    w = kk.shape[1]
    cur = lambda i, s: (i, 0)
    prv = lambda i, s: (jnp.maximum(i - 1, 0), 0)
    nxt = lambda i, s: (jnp.minimum(i + 1, nb - 1), 0)
    zero = lambda i, s: (0, 0)
    rows = (N_HEADS // N_KV_HEADS) // 2 * qb
    r = (jnp.arange(rows, dtype=I32) % qb)[:, None]
    cpos = jnp.arange(3 * qb, dtype=I32)[None, :] - qb
    inside = jnp.abs(r - cpos) <= WINDOW
    variants = [inside & ((cpos >= 0) | ((v & 1) == 0)) & ((cpos < qb) | ((v & 2) == 0)) for v in range(4)]
    bias = jnp.where(jnp.stack(variants), 0.0, NEG_BIG).astype(F32)
    which = lambda i, s: (jnp.where(i == 0, 1, 0) + jnp.where(i == nb - 1, 2, 0), 0, 0)
    gs = pltpu.PrefetchScalarGridSpec(
        num_scalar_prefetch=1,
        grid=(nb,),
        in_specs=[pl.BlockSpec((qb, nq), cur),
                  pl.BlockSpec((qb, w), prv), pl.BlockSpec((qb, w), cur), pl.BlockSpec((qb, w), nxt),
                  pl.BlockSpec((qb, w), prv), pl.BlockSpec((qb, w), cur), pl.BlockSpec((qb, w), nxt),
                  pl.BlockSpec((c, w), zero), pl.BlockSpec((c, w), zero),
                  pl.BlockSpec((1, rows, 3 * qb), which)],
        out_specs=pl.BlockSpec((qb, nq), cur),
    )
    return pl.pallas_call(
        _attn_kernel,
        grid_spec=gs,
        out_shape=jax.ShapeDtypeStruct((t, nq), BF16),
        compiler_params=_cparams(1),
        name="window_attention",
    )(sinks, q, kk, kk, kk, vv, vv, vv, kkc, vvc, bias)


def _dispatch_out(t, d, tile):
    nt = t // tile
    tok = lambda i: (i, 0)
    region = _region_rows(nt)
    specs = [pl.BlockSpec((tile, d), tok), pl.BlockSpec((region, d // 2), tok),
             pl.BlockSpec((tile, LANES), tok), pl.BlockSpec((tile, LANES), tok),
             pl.BlockSpec((1, SUBLANES, LANES), lambda i: (i, 0, 0))]
    shapes = [jax.ShapeDtypeStruct((t, d), F32), jax.ShapeDtypeStruct((nt * region, d // 2), jnp.uint32),
              jax.ShapeDtypeStruct((t, LANES), I32), jax.ShapeDtypeStruct((t, LANES), F32),
              jax.ShapeDtypeStruct((nt, SUBLANES, LANES), I32)]
    return specs, shapes


def _oproj_kernel(x_ref, o_ref, mod_ref, wo_ref, bo_ref, g2_ref, wr_ref, br_ref,
                  x1_ref, xs_ref, pos_ref, gate_ref, cnt_ref, wb_ref):
    d = x_ref.shape[1]

    @pl.when(pl.program_id(0) == 0)
    def _():
        wb_ref[...] = wo_ref[...].astype(BF16)

    y = jnp.dot(o_ref[...], wb_ref[...], preferred_element_type=F32) + bo_ref[...]
    x1 = x_ref[...] + mod_ref[0:1, 2 * d:3 * d] * y
    x1_ref[...] = x1
    _router_dispatch(x1, g2_ref[...], mod_ref[0:1, 3 * d:4 * d], mod_ref[0:1, 4 * d:5 * d],
                     wr_ref[...], br_ref[...], xs_ref, pos_ref, gate_ref, cnt_ref)


def _oproj_router(xt, o, mod_l, w_o, b_o, g2n, wr, br):
    t, d = xt.shape
    tile = MOE_TILE
    tok = lambda i: (i, 0)
    out_specs, out_shape = _dispatch_out(t, d, tile)
    return pl.pallas_call(
        _oproj_kernel,
        grid=(t // tile,),
        in_specs=[pl.BlockSpec((tile, d), tok), pl.BlockSpec((tile, o.shape[1]), tok), _full(mod_l.shape),
                  _full(w_o.shape), _full((1, d)), _full((1, d)), _full(wr.shape), _full(br.shape)],
        out_specs=out_specs,
        out_shape=out_shape,
        scratch_shapes=[pltpu.VMEM(w_o.shape, BF16)],
        compiler_params=_cparams(1),
        name="oproj_router",
    )(xt, o, mod_l, w_o, b_o, g2n, wr, br)


def _moe_kernel(nt, region_granules, brange_ref, nvg_ref, gsrc_ref, xs_hbm, w1_ref, b1_ref, w2_ref, b2_ref,
                ys_hbm, xbuf, obuf, w1b, w2b, gsem, ssem):
    e = pl.program_id(0)
    ne = pl.num_programs(0)
    b_lo = brange_ref[e]
    b_hi = brange_ref[ne + e]
    total = brange_ref[2 * ne - 1]
    gpb = GRANULES_PER_BLOCK
    dff = w2_ref.shape[1]
    spare = _spare_per_region(nt)
    idle_granule = REGION_DATA_GRANULES + 2 * spare

    def rows_of(granule):
        return pl.ds(pl.multiple_of(granule * GRANULE, GRANULE), GRANULE)

    def gather_copy(blk, gi):
        g = gsrc_ref[blk * gpb + gi]
        src = jnp.where(g >= 0, g, idle_granule)
        s = blk % 3
        return pltpu.make_async_copy(xs_hbm.at[rows_of(src)], xbuf.at[s, pl.ds(gi * GRANULE, GRANULE)],
                                     gsem.at[s])

    def scatter_copy(blk, s, gi):
        g = gsrc_ref[blk * gpb + gi]
        own_spare = (gi % nt) * region_granules + REGION_DATA_GRANULES + gi // nt + s * spare
        dst = jnp.where(g >= 0, g, own_spare)
        return pltpu.make_async_copy(obuf.at[s, pl.ds(gi * GRANULE, GRANULE)], ys_hbm.at[rows_of(dst)],
                                     ssem.at[s])

    def start_gather(blk):
        for gi in range(gpb):
            gather_copy(blk, gi).start()

    def wait_gather(s):
        pltpu.make_async_copy(xs_hbm.at[pl.ds(0, EXPERT_ROWS)], xbuf.at[s], gsem.at[s]).wait()

    def wait_scatter(s):
        pltpu.make_async_copy(obuf.at[s], ys_hbm.at[pl.ds(0, EXPERT_ROWS)], ssem.at[s]).wait()

    @pl.when(e == 0)
    def _():
        obuf[...] = jnp.zeros(obuf.shape, jnp.uint32)
        start_gather(0)
        start_gather(1)

    def expert_block(b, rows):
        slot = b % 2
        start_gather(b + 2)
        xb = jnp.concatenate(_unpack_halves(xbuf[b % 3, 0:rows, :]), axis=1).astype(BF16)
        gu = jnp.dot(xb, w1b[...], preferred_element_type=F32) + b1_ref[0]
        gt = jnp.minimum(gu[:, :dff], SWIGLU_LIMIT)
        up = jnp.clip(gu[:, dff:], -SWIGLU_LIMIT, SWIGLU_LIMIT)
        act = gt * jax.nn.sigmoid(SWIGLU_ALPHA * gt) * (up + 1.0)
        out = jnp.dot(act.astype(BF16), w2b[...], preferred_element_type=F32) + b2_ref[0]
        obuf[slot, 0:rows, :] = _pack_halves(out.astype(BF16).astype(F32))
        for gi in range(gpb):
            scatter_copy(b, slot, gi).start()

    def block_body(b, carry):
        wait_gather(b % 3)

        @pl.when(b >= 2)
        def _():
            wait_scatter(b % 2)

        half = EXPERT_ROWS // 2
        rows_needed = nvg_ref[b] * GRANULE

        @pl.when(rows_needed > half)
        def _():
            expert_block(b, EXPERT_ROWS)

        @pl.when(rows_needed <= half)
        def _():
            expert_block(b, half)

        return carry

    @pl.when(b_hi > b_lo)
    def _():
        w1b[...] = w1_ref[0].astype(BF16)
        w2b[...] = w2_ref[0].astype(BF16)
        lax.fori_loop(b_lo, b_hi, block_body, 0)

    @pl.when(e == ne - 1)
    def _():
        wait_gather(total % 3)
        wait_gather((total + 1) % 3)

        @pl.when(total >= 2)
        def _():
            wait_scatter(total % 2)

        @pl.when(total >= 1)
        def _():
            wait_scatter((total + 1) % 2)


def _moe_experts(xs, nt, block_range, n_granules, granule_src, layer, w1, b1, w2, b2):
    dw = xs.shape[1]
    d = 2 * dw
    dff = w2.shape[2]
    n_layers, n_exp = w1.shape[:2]
    by_expert = lambda e, br, ng, gs: (layer, e, 0, 0)
    gs = pltpu.PrefetchScalarGridSpec(
        num_scalar_prefetch=3,
        grid=(n_exp,),
        in_specs=[pl.BlockSpec(memory_space=pl.ANY),
                  pl.BlockSpec((None, 1, d, 2 * dff), by_expert), pl.BlockSpec((None, 1, 1, 2 * dff), by_expert),
                  pl.BlockSpec((None, 1, dff, d), by_expert), pl.BlockSpec((None, 1, 1, d), by_expert)],
        out_specs=pl.BlockSpec(memory_space=pl.ANY),
        scratch_shapes=[pltpu.VMEM((3, EXPERT_ROWS, dw), jnp.uint32), pltpu.VMEM((2, EXPERT_ROWS, dw), jnp.uint32),
                        pltpu.VMEM((d, 2 * dff), BF16), pltpu.VMEM((dff, d), BF16),
                        pltpu.SemaphoreType.DMA((3,)), pltpu.SemaphoreType.DMA((2,))],
    )
    return pl.pallas_call(
        functools.partial(_moe_kernel, nt, _region_rows(nt) // GRANULE),
        grid_spec=gs,
        out_shape=jax.ShapeDtypeStruct(xs.shape, jnp.uint32),
        input_output_aliases={3: 0},
        compiler_params=_cparams(1),
        name="moe_experts",
    )(block_range, n_granules, granule_src, xs, w1, b1.reshape(n_layers, n_exp, 1, 2 * dff),
      w2, b2.reshape(n_layers, n_exp, 1, d))


TABLE_BLOCKS = 8


def _tables_kernel(region_granules, cnt_ref, brange_ref, ngran_ref, gsrc_ref):
    gpb = GRANULES_PER_BLOCK
    log_b = gpb.bit_length() - 1
    n = cnt_ref.shape[0]
    rows = TABLE_BLOCKS * gpb
    nt_dims = (((1,), (1,)), ((), ()))

    cnt = cnt_ref[...].astype(F32)
    gl_len = cnt * (1.0 / GRANULE)
    ri = lax.broadcasted_iota(I32, (n, n), 0)
    ci = lax.broadcasted_iota(I32, (n, n), 1)
    lower = jnp.where(ci < ri, 1.0, 0.0).astype(BF16)
    upper = jnp.where(ri < ci, 1.0, 0.0).astype(BF16)
    pre = jnp.dot(lower, gl_len.astype(BF16), preferred_element_type=F32)
    off = jnp.dot(cnt.astype(BF16), upper, preferred_element_type=F32)
    src = ri.astype(F32) * float(region_granules) + off * (1.0 / GRANULE)
    n_g = jnp.sum(gl_len, axis=0, keepdims=True)
    blocks = ((n_g.astype(I32) + (gpb - 1)) >> log_b).astype(F32)
    bstart = jnp.dot(jnp.broadcast_to(blocks, (SUBLANES, n)).astype(BF16), upper,
                     preferred_element_type=F32)[0:1]
    bend = bstart + blocks

    def digits(v, base):
        hi = jnp.floor(v * (1.0 / base))
        return hi.astype(BF16), (v - hi * base).astype(BF16)

    def pick(onehot, table):
        return lax.dot_general(onehot, table, nt_dims, preferred_element_type=F32)

    def pick2(onehot, v, base):
        hi, lo = digits(v, base)
        return pick(onehot, hi) * base + pick(onehot, lo)

    q = pl.program_id(0) * rows + lax.broadcasted_iota(I32, (rows, n), 0)
    bq = (q >> log_b).astype(F32)
    iq = (q & (gpb - 1)).astype(F32)
    own = jnp.where((bstart <= bq) & (bq < bend), 1.0, 0.0).astype(BF16)
    pre_q = pick2(own, pre, 64.0)
    len_q = pick(own, gl_len.astype(BF16))
    src_q = pick2(own, src, 128.0)
    bstart_q = pick(own, jnp.broadcast_to(bstart, (n, n)).astype(BF16))
    gl = (bq - bstart_q) * float(gpb) + iq
    inside = (pre_q <= gl) & (gl < pre_q + len_q)
    hit = jnp.sum(jnp.where(inside, src_q - pre_q + gl + 1.0, 0.0), axis=-1, keepdims=True)
    gsrc_ref[...] = jnp.broadcast_to(hit - 1.0, (rows, n)).astype(I32)

    bb = (pl.program_id(0) * TABLE_BLOCKS + lax.broadcasted_iota(I32, (TABLE_BLOCKS, n), 0)).astype(F32)
    lane = lax.broadcasted_iota(I32, (TABLE_BLOCKS, n), 1).astype(F32)
    mine = (bstart <= bb) & (bb < bend)
    red = lambda v: jnp.sum(jnp.where(mine, v, 0.0), axis=-1, keepdims=True)
    active = red(jnp.ones_like(lane))
    n_left = red(jnp.broadcast_to(n_g, mine.shape)) - (bb[:, 0:1] - red(jnp.broadcast_to(bstart, mine.shape))) * gpb
    ngran_ref[...] = jnp.broadcast_to(jnp.clip(n_left, 0.0, float(gpb)) * active, (TABLE_BLOCKS, n)).astype(I32)

    sub = lax.broadcasted_iota(I32, (SUBLANES, n), 0)
    brange_ref[...] = jnp.where(sub == 0, bstart, jnp.where(sub == 1, bend, 0.0)).astype(I32)


def _block_tables(cnt):
    nt = cnt.shape[0]
    assert nt <= LANES
    max_blocks = (nt * REGION_DATA_GRANULES) // GRANULES_PER_BLOCK + N_EXPERTS
    n_blocks = -(-(max_blocks + 2) // TABLE_BLOCKS) * TABLE_BLOCKS
    gpb = GRANULES_PER_BLOCK
    cnt_sq = jnp.zeros((LANES, LANES), I32).at[:nt].set(cnt[:, 0, :])
    by_step = lambda i: (i, 0)
    brange, ngran, gsrc = pl.pallas_call(
        functools.partial(_tables_kernel, _region_rows(nt) // GRANULE),
        grid=(n_blocks // TABLE_BLOCKS,),
        in_specs=[_full((LANES, LANES))],
        out_specs=[_full((SUBLANES, LANES)), pl.BlockSpec((TABLE_BLOCKS, LANES), by_step),
                   pl.BlockSpec((TABLE_BLOCKS * gpb, LANES), by_step)],
        out_shape=[jax.ShapeDtypeStruct((SUBLANES, LANES), I32), jax.ShapeDtypeStruct((n_blocks, LANES), I32),
                   jax.ShapeDtypeStruct((n_blocks * gpb, LANES), I32)],
        compiler_params=_cparams(1),
        name="moe_block_tables",
    )(cnt_sq)
    return brange[0:2, :N_EXPERTS].reshape(-1), ngran[:, 0], gsrc[:, 0]


def _combine_kernel(x_ref, ys_ref, pos_ref, gate_ref, mod_ref, o_ref):
    d = x_ref.shape[1]
    o_ref[...] = x_ref[...] + mod_ref[0:1, 5 * d:6 * d] * _moe_combine(ys_ref, pos_ref, gate_ref)


def _combine_glu_kernel(x_ref, ys_ref, pos_ref, gate_ref, modp_ref, modn_ref, g_ref, w_ref, b_ref,
                        o_ref, u_ref, wb_ref):
    d = x_ref.shape[1]

    @pl.when(pl.program_id(0) == 0)
    def _():
        wb_ref[...] = w_ref[...].astype(BF16)

    x2 = x_ref[...] + modp_ref[0:1, 5 * d:6 * d] * _moe_combine(ys_ref, pos_ref, gate_ref)
    o_ref[...] = x2
    h = _rms_mod(x2, g_ref[...], modn_ref[0:1, 0:d], modn_ref[0:1, d:2 * d])
    u = jnp.dot(h.astype(BF16), wb_ref[...], preferred_element_type=F32) + b_ref[...]
    u_ref[...] = u[:, :d] * jax.nn.sigmoid(u[:, d:])


def _combine_in_specs(t, d, tile):
    tok = lambda i: (i, 0)
    return [pl.BlockSpec((tile, d), tok), pl.BlockSpec((_region_rows(t // tile), d // 2), tok),
            pl.BlockSpec((tile, LANES), tok), pl.BlockSpec((tile, LANES), tok)]


def _combine(x1, ys, pos, gates, mod_l):
    t, d = x1.shape
    tile = MOE_TILE
    return pl.pallas_call(
        _combine_kernel,
        grid=(t // tile,),
        in_specs=_combine_in_specs(t, d, tile) + [_full(mod_l.shape)],
        out_specs=pl.BlockSpec((tile, d), lambda i: (i, 0)),
        out_shape=jax.ShapeDtypeStruct((t, d), F32),
        compiler_params=_cparams(1),
        name="moe_combine",
    )(x1, ys, pos, gates, mod_l)


def _combine_glu(x1, ys, pos, gates, mod_prev, mod_next, g1n, w_pw1, b_pw1):
    t, d = x1.shape
    tile = MOE_TILE
    tok = lambda i: (i, 0)
    return pl.pallas_call(
        _combine_glu_kernel,
        grid=(t // tile,),
        in_specs=_combine_in_specs(t, d, tile)
                 + [_full(mod_prev.shape), _full(mod_next.shape), _full((1, d)),
                    _full(w_pw1.shape), _full((1, 2 * d))],
        out_specs=[pl.BlockSpec((tile, d), tok), pl.BlockSpec((tile, d), tok)],
        out_shape=[jax.ShapeDtypeStruct((t, d), F32), jax.ShapeDtypeStruct((t, d), F32)],
        scratch_shapes=[pltpu.VMEM(w_pw1.shape, BF16)],
        compiler_params=_cparams(1),
        name="combine_pw1_glu",
    )(x1, ys, pos, gates, mod_prev, mod_next, g1n, w_pw1, b_pw1)


def _conv_kernel(x_ref, up_ref, uc_ref, un_ref, mod_ref, wdw_ref, bdw_ref, lg_ref, lb_ref, w2_ref, b2_ref,
                 g2_ref, wr_ref, br_ref, x1_ref, xs_ref, pos_ref, gate_ref, cnt_ref, wb_ref, ubuf, shbuf, cbuf):
    i = pl.program_id(0)
    n = pl.num_programs(0)
    d = x_ref.shape[1]
    tile = x_ref.shape[0]
    halo = up_ref.shape[0]

    @pl.when(i == 0)
    def _():
        wb_ref[...] = w2_ref[...].astype(BF16)

    ubuf[0:halo, :] = jnp.where(i > 0, up_ref[...], 0.0)
    ubuf[halo:halo + tile, :] = uc_ref[...]
    ubuf[halo + tile:, :] = jnp.where(i < n - 1, un_ref[...], 0.0)

    span = tile + 2 * halo - SUBLANES
    for r in range(SUBLANES):
        shbuf[r, 0:span, :] = ubuf[r:r + span, :]
    sub = CONV_CHUNK // SUBLANES
    for c0 in range(0, tile, CONV_CHUNK):
        parts = [jnp.zeros((SUBLANES, d), F32) + bdw_ref[...] for _ in range(sub)]
        for j in range(CONV_WIDTH):
            a, r = divmod(halo + j - CONV_PAD, SUBLANES)
            wj = wdw_ref[j * SUBLANES:(j + 1) * SUBLANES, :]
            for k in range(sub):
                lo = c0 + (a + k) * SUBLANES
                parts[k] = parts[k] + shbuf[r, lo:lo + SUBLANES, :] * wj
        for k in range(sub):
            cbuf[c0 + k * SUBLANES:c0 + (k + 1) * SUBLANES, :] = parts[k]
    acc = cbuf[...]

    mu = jnp.mean(acc, axis=-1, keepdims=True)
    cen = acc - mu
    var = jnp.mean(cen * cen, axis=-1, keepdims=True)
    z = _silu(cen * lax.rsqrt(var + EPS) * lg_ref[...] + lb_ref[...])
    y = jnp.dot(z.astype(BF16), wb_ref[...], preferred_element_type=F32) + b2_ref[...]
    x1 = x_ref[...] + mod_ref[0:1, 2 * d:3 * d] * y
    x1_ref[...] = x1
    _router_dispatch(x1, g2_ref[...], mod_ref[0:1, 3 * d:4 * d], mod_ref[0:1, 4 * d:5 * d],
                     wr_ref[...], br_ref[...], xs_ref, pos_ref, gate_ref, cnt_ref)


def _conv_router(x2, u, mod_l, w_dw, b_dw, ln_g, ln_b, w_pw2, b_pw2, g2n, wr, br):
    t, d = x2.shape
    tile = MOE_TILE
    nt = t // tile
    hb = tile // CONV_HALO
    nh = t // CONV_HALO
    tok = lambda i: (i, 0)
    prv = lambda i: (jnp.maximum(i * hb - 1, 0), 0)
    nxt = lambda i: (jnp.minimum((i + 1) * hb, nh - 1), 0)
    wpad = jnp.repeat(w_dw, SUBLANES, axis=0)
    out_specs, out_shape = _dispatch_out(t, d, tile)
    return pl.pallas_call(
        _conv_kernel,
        grid=(nt,),
        in_specs=[pl.BlockSpec((tile, d), tok), pl.BlockSpec((CONV_HALO, d), prv),
                  pl.BlockSpec((tile, d), tok), pl.BlockSpec((CONV_HALO, d), nxt),
                  _full(mod_l.shape), _full(wpad.shape), _full((1, d)), _full((1, d)), _full((1, d)),
                  _full(w_pw2.shape), _full((1, d)), _full((1, d)), _full(wr.shape), _full(br.shape)],
        out_specs=out_specs,
        out_shape=out_shape,
        scratch_shapes=[pltpu.VMEM(w_pw2.shape, BF16), pltpu.VMEM((tile + 2 * CONV_HALO, d), F32),
                        pltpu.VMEM((SUBLANES, tile + 2 * CONV_HALO, d), F32), pltpu.VMEM((tile, d), F32)],
        compiler_params=_cparams(1),
        name="conv_router",
    )(x2, u, u, u, mod_l, wpad, b_dw, ln_g, ln_b, w_pw2, b_pw2, g2n, wr, br)


def _rope_tables(n_rows):
    quarter = HEAD_DIM // 4
    inv = ROPE_BASE ** (-jnp.arange(quarter, dtype=F32) / quarter)

    def tables(npos):
        ang = jnp.arange(npos, dtype=I32).astype(F32)[:, None] * inv[None, :]
        z = jnp.zeros_like(ang)
        c, s = jnp.cos(ang), jnp.sin(ang)
        half = lambda a, b: jnp.concatenate([a, b], axis=1)
        full = lambda h: jnp.concatenate([h, h, h, h], axis=1)
        return jnp.stack([full(half(c, c)), full(half(-s, z)), full(half(z, s))])

    return tables(n_rows), tables(GRID_W)


def _router_params(w_router, b_router):
    d, e = w_router.shape
    wr = jnp.zeros((LANES, d), F32).at[:e].set(w_router.T)
    br = jnp.zeros((LANES, MOE_TILE), F32).at[:e].set(jnp.broadcast_to(b_router[:, None], (e, MOE_TILE)))
    hi = wr.astype(BF16)
    lo = (wr - hi.astype(F32)).astype(BF16)
    return jnp.stack([hi, lo]), br


def kernel(x, c, ctx, c_ctx, w_mod, b_mod, norm1_g, norm2_g, attn_w_qkv, attn_b_qkv, attn_w_o, attn_b_o,
           attn_q_norm, attn_k_norm, attn_sinks, conv_w_pw1, conv_b_pw1, conv_w_dw, conv_b_dw, conv_ln_g,
           conv_ln_b, conv_w_pw2, conv_b_pw2, moe_w_router, moe_b_router, moe_w1, moe_b1, moe_w2, moe_b2):
    bsz, t, d = x.shape
    assert bsz == 1 and w_mod.shape[0] == 2
    n_ctx = ctx.shape[1]
    tile = min(TOKEN_TILE, t)
    assert t % tile == 0 and t % Q_BLOCK == 0 and t % MOE_TILE == 0 and tile % GRID_W == 0
    xt = x.reshape(t, d)
    row = lambda v: v.reshape(1, -1)

    mod = _mod_vectors(c, c_ctx, w_mod, b_mod)
    mod0, mod1 = mod[0], mod[1]

    bd = (jnp.arange(2 * LANES)[:, None] // HEAD_DIM == jnp.arange(2 * LANES)[None, :] // HEAD_DIM).astype(BF16)
    qg = row(jnp.tile(attn_q_norm[0], N_HEADS))
    kg = row(jnp.tile(attn_k_norm[0], N_KV_HEADS))
    rowt, colt = _rope_tables(t // GRID_W)
    g1n = row(norm1_g[0])
    bq = row(attn_b_qkv[0])
    q, kk, vv = _qkv_project(xt, mod0, 0, True, g1n, attn_w_qkv[0], bq, qg, kg, rowt, colt, bd, tile)
    _, kkc, vvc = _qkv_project(ctx.reshape(n_ctx, d), mod0, 1, False, g1n, attn_w_qkv[0], bq, qg, kg,
                               rowt[:, :max(n_ctx // GRID_W, 1)], colt, bd, n_ctx)
    o = _attention(q, kk, vv, kkc, vvc, attn_sinks[0])
    wr0, br0 = _router_params(moe_w_router[0], moe_b_router[0])
    x1, xs, pos, gates, cnt = _oproj_router(xt, o, mod0, attn_w_o[0], row(attn_b_o[0]), row(norm2_g[0]),
                                            wr0, br0)
    ys = _moe_experts(xs, cnt.shape[0], *_block_tables(cnt), 0, moe_w1, moe_b1, moe_w2, moe_b2)

    x2, u = _combine_glu(x1, ys, pos, gates, mod0, mod1, row(norm1_g[1]), conv_w_pw1[0], row(conv_b_pw1[0]))
    wr1, br1 = _router_params(moe_w_router[1], moe_b_router[1])
    x3, xs, pos, gates, cnt = _conv_router(x2, u, mod1, conv_w_dw[0], row(conv_b_dw[0]), row(conv_ln_g[0]),
                                           row(conv_ln_b[0]), conv_w_pw2[0], row(conv_b_pw2[0]),
                                           row(norm2_g[1]), wr1, br1)
    ys = _moe_experts(xs, cnt.shape[0], *_block_tables(cnt), 1, moe_w1, moe_b1, moe_w2, moe_b2)
    out = _combine(x3, ys, pos, gates, mod1)
    return out.reshape(1, t, d)
```

```python
import functools

import jax
import jax.numpy as jnp
from jax import lax
from jax.experimental import pallas as pl
from jax.experimental.pallas import tpu as pltpu

F32 = jnp.float32
BF16 = jnp.bfloat16
I32 = jnp.int32

N_HEADS = 16
N_KV_HEADS = 2
HEAD_DIM = 64
GRID_W = 64
WINDOW = 128
Q_BLOCK = 128
ROPE_BASE = 10000.0
CONV_WIDTH = 31
CONV_PAD = CONV_WIDTH // 2
N_EXPERTS = 32
TOP_K = 4
SWIGLU_LIMIT = 7.0
SWIGLU_ALPHA = 1.702
EPS = 1e-6

LANES = 128
SUBLANES = 8
TOKEN_TILE = 512
MOE_TILE = 256
GRANULE = SUBLANES
EXPERT_ROWS = 512
GRANULES_PER_BLOCK = EXPERT_ROWS // GRANULE
REGION_DATA_GRANULES = (MOE_TILE * TOP_K + N_EXPERTS * (GRANULE - 1)) // GRANULE


def _spare_per_region(nt):
    return -(-GRANULES_PER_BLOCK // nt)


def _region_rows(nt):
    rows = (REGION_DATA_GRANULES + 2 * _spare_per_region(nt) + 1) * GRANULE
    return -(-rows // 256) * 256
CONV_HALO = 16
CONV_CHUNK = 32
NEG_BIG = -1e30
LOG2_E = 1.4426950408889634
VMEM_LIMIT = 56 * 1024 * 1024


def _cparams(n_axes=1, vmem=VMEM_LIMIT):
    return pltpu.CompilerParams(dimension_semantics=("arbitrary",) * n_axes, vmem_limit_bytes=vmem)


def _full(shape):
    nd = len(shape)
    return pl.BlockSpec(shape, lambda *_: (0,) * nd)


def _silu(v):
    return v * jax.nn.sigmoid(v)


def _rms_mod(xv, g, shift, scale):
    ms = jnp.mean(xv * xv, axis=-1, keepdims=True)
    return (xv * lax.rsqrt(ms + EPS)) * g * (1.0 + scale) + shift


def _split_dot(a, b_bf16):
    hi = a.astype(BF16)
    lo = (a - hi.astype(F32)).astype(BF16)
    return (jnp.dot(hi, b_bf16, preferred_element_type=F32)
            + jnp.dot(lo, b_bf16, preferred_element_type=F32))


def _router_dispatch(x_new, g2n, sh2, sc2, wr, br, xs_ref, pos_ref, gate_ref, cnt_ref):
    tt = x_new.shape[0]
    rt = xs_ref.shape[0]
    h2 = _rms_mod(x_new, g2n, sh2, sc2)
    h_hi = h2.astype(BF16)
    h_lo = (h2 - h_hi.astype(F32)).astype(BF16)
    nt_dims = (((1,), (1,)), ((), ()))
    wdot = lambda w, h: lax.dot_general(w, h, nt_dims, preferred_element_type=F32)
    logits = (wdot(wr[0], h_hi) + (wdot(wr[0], h_lo) + wdot(wr[1], h_hi)) + br)[0:N_EXPERTS]
    erow = lax.broadcasted_iota(I32, (N_EXPERTS, tt), 0)
    hits, vals = [], []
    cur = logits
    for k in range(TOP_K):
        m = jnp.max(cur, axis=0, keepdims=True)
        sel = jnp.min(jnp.where(cur == m, erow, N_EXPERTS), axis=0, keepdims=True)
        hit = erow == sel
        hits.append(hit)
        vals.append(m)
        cur = jnp.where(hit, NEG_BIG * 2.0, cur)
    evs = [jnp.exp(v - vals[0]) for v in vals]
    den = (evs[0] + evs[1]) + (evs[2] + evs[3])
    gates = [ev / den for ev in evs]

    onehot = jnp.where((hits[0] | hits[1]) | (hits[2] | hits[3]), 1.0, 0.0).astype(BF16)
    ri = lax.broadcasted_iota(I32, (tt, tt), 0)
    ci = lax.broadcasted_iota(I32, (tt, tt), 1)
    earlier = jnp.where(ri < ci, 1.0, 0.0).astype(BF16)
    rank = jnp.dot(onehot, earlier, preferred_element_type=F32)
    cnt = jnp.dot(onehot, jnp.ones((tt, LANES), BF16), preferred_element_type=F32)
    c8 = jnp.bitwise_and(cnt.astype(I32) + (GRANULE - 1), -GRANULE).astype(F32)
    ue = lax.broadcasted_iota(I32, (N_EXPERTS, N_EXPERTS), 0)
    ve = lax.broadcasted_iota(I32, (N_EXPERTS, N_EXPERTS), 1)
    below = jnp.where(ve < ue, 1.0, 0.0).astype(BF16)
    off = jnp.dot(below, c8.astype(BF16), preferred_element_type=F32)
    slot_of = jnp.concatenate([off] * (tt // LANES), axis=1) + rank

    onehot_pad = jnp.concatenate([onehot, jnp.zeros((LANES - N_EXPERTS, tt), BF16)], axis=0)
    cnt_row = lax.dot_general(jnp.ones((2 * SUBLANES, tt), BF16), onehot_pad, nt_dims,
                              preferred_element_type=F32)[0:SUBLANES]
    cnt_ref[0] = jnp.bitwise_and(cnt_row.astype(I32) + (GRANULE - 1), -GRANULE)

    r_iota = lax.broadcasted_iota(I32, (rt, tt), 0)
    krow = lax.broadcasted_iota(I32, (LANES, tt), 0)
    perm = jnp.zeros((rt, tt), F32)
    pos_t = jnp.zeros((LANES, tt), F32)
    gate_t = jnp.zeros((LANES, tt), F32)
    for k in range(TOP_K):
        pk = jnp.sum(jnp.where(hits[k], slot_of, 0.0), axis=0, keepdims=True)
        perm = jnp.where(r_iota == pk.astype(I32), 1.0, perm)
        pos_t = jnp.where(krow == k, pk, pos_t)
        gate_t = jnp.where(krow == k, gates[k], gate_t)
    pos_ref[...] = pos_t.T.astype(I32)
    gate_ref[...] = gate_t.T
    xs_ref[...] = _pack_halves(jnp.dot(perm.astype(BF16), h_hi, preferred_element_type=F32))


def _pack_halves(v):
    half = v.shape[1] // 2
    lo = lax.bitcast_convert_type(v[:, :half], jnp.uint32) >> 16
    hi = lax.bitcast_convert_type(v[:, half:], jnp.uint32) & jnp.uint32(0xFFFF0000)
    return lo | hi


def _unpack_halves(w):
    return (lax.bitcast_convert_type(w << 16, F32),
            lax.bitcast_convert_type(w & jnp.uint32(0xFFFF0000), F32))


def _moe_combine(ys_ref, pos_ref, gate_ref):
    tt = MOE_TILE
    n_tiles = pos_ref.shape[0] // tt
    rt = ys_ref.shape[0] // n_tiles
    r_iota = lax.broadcasted_iota(I32, (tt, rt), 1)
    outs = []
    for j in range(n_tiles):
        pos = pos_ref[j * tt:(j + 1) * tt, :]
        gate = gate_ref[j * tt:(j + 1) * tt, :]
        g = jnp.zeros((tt, rt), F32)
        for k in range(TOP_K):
            g = jnp.where(r_iota == pos[:, k:k + 1], gate[:, k:k + 1], g)
        g = g.astype(BF16)
        ya, yb = _unpack_halves(ys_ref[j * rt:(j + 1) * rt, :])
        outs.append(jnp.concatenate([jnp.dot(g, ya.astype(BF16), preferred_element_type=F32),
                                     jnp.dot(g, yb.astype(BF16), preferred_element_type=F32)], axis=1))
    return outs[0] if n_tiles == 1 else jnp.concatenate(outs, axis=0)


def _mod_kernel(cc_ref, w_ref, b_ref, o_ref):
    a = _silu(cc_ref[...])
    o_ref[0] = jnp.dot(a, w_ref[0], precision=lax.Precision.HIGHEST,
                       preferred_element_type=F32) + b_ref[0]


def _mod_vectors(c, c_ctx, w_mod, b_mod):
    depth, d, d6 = w_mod.shape
    cc = jnp.zeros((SUBLANES, d), F32).at[0].set(c[0]).at[1].set(c_ctx)
    ncol = 4
    cw = d6 // ncol
    return pl.pallas_call(
        _mod_kernel,
        grid=(depth, ncol),
        in_specs=[pl.BlockSpec((SUBLANES, d), lambda l, j: (0, 0)),
                  pl.BlockSpec((1, d, cw), lambda l, j: (l, 0, j)),
                  pl.BlockSpec((1, 1, cw), lambda l, j: (l, 0, j))],
        out_specs=pl.BlockSpec((1, SUBLANES, cw), lambda l, j: (l, 0, j)),
        out_shape=jax.ShapeDtypeStruct((depth, SUBLANES, d6), F32),
        compiler_params=_cparams(2),
        name="mod_vectors",
    )(cc, w_mod, b_mod.reshape(depth, 1, d6))


def _qkv_kernel(mod_row, use_rope, x_ref, mod_ref, g_ref, w_ref, b_ref, qg_ref, kg_ref, rowt_ref, colt_ref,
                bd_ref, q_ref, k_ref, v_ref, wb_ref):
    d = x_ref.shape[1]
    tile = x_ref.shape[0]
    nq = N_HEADS * HEAD_DIM
    nk = N_KV_HEADS * HEAD_DIM

    @pl.when(pl.program_id(0) == 0)
    def _():
        wb_ref[...] = w_ref[...].astype(BF16)

    sh = mod_ref[mod_row:mod_row + 1, 0:d]
    sc = mod_ref[mod_row:mod_row + 1, d:2 * d]
    h = _rms_mod(x_ref[...], g_ref[...], sh, sc)
    qkv = jnp.dot(h.astype(BF16), wb_ref[...], preferred_element_type=F32) + b_ref[...]
    q = qkv[:, :nq]
    k = qkv[:, nq:nq + nk]
    v = qkv[:, nq + nk:]
    bd = bd_ref[...]

    def head_norm(t, gain):
        tt = t * t
        w = bd.shape[0]
        if t.shape[1] >= w:
            ss = jnp.concatenate([_split_dot(tt[:, c:c + w], bd) for c in range(0, t.shape[1], w)], axis=1)
        else:
            ss = _split_dot(tt, bd[:t.shape[1], :t.shape[1]])
        return t * lax.rsqrt(ss * (1.0 / HEAD_DIM) + EPS) * gain

    qn = head_norm(q, qg_ref[...])
    kn = head_norm(k, kg_ref[...])

    if use_rope:
        lane = lax.broadcasted_iota(I32, (tile, LANES), 1)
        is_row = (lane % HEAD_DIM) < HEAD_DIM // 2
        nrow = tile // GRID_W

        def table(idx):
            rt = jnp.concatenate([jnp.broadcast_to(rowt_ref[idx, r:r + 1, :], (GRID_W, LANES))
                                  for r in range(nrow)], axis=0)
            ct = jnp.concatenate([colt_ref[idx]] * nrow, axis=0)
            return jnp.where(is_row, rt, ct)

        cosv, sav, sbv = table(0), table(1), table(2)

        def rope(t):
            n = t.shape[1]
            reps = n // LANES
            ct = jnp.concatenate([cosv] * reps, axis=1) if reps > 1 else cosv
            at = jnp.concatenate([sav] * reps, axis=1) if reps > 1 else sav
            bt = jnp.concatenate([sbv] * reps, axis=1) if reps > 1 else sbv
            up = pltpu.roll(t, n - HEAD_DIM // 4, 1)
            dn = pltpu.roll(t, HEAD_DIM // 4, 1)
            return t * ct + up * at + dn * bt

        qn = rope(qn)
        kn = rope(kn)

    q_ref[...] = (qn * (HEAD_DIM ** -0.5 * LOG2_E)).astype(BF16)
    klane = lax.broadcasted_iota(I32, kn.shape, 1)
    low = klane < HEAD_DIM

    def variants(t):
        sw = pltpu.roll(t, HEAD_DIM, 1)
        z = jnp.zeros_like(t)
        return jnp.concatenate([jnp.where(low, t, z), jnp.where(low, z, sw),
                                jnp.where(low, sw, z), jnp.where(low, z, t)], axis=1)

    k_ref[...] = variants(kn).astype(BF16)
    v_ref[...] = variants(v).astype(BF16)


def _qkv_project(xt, mod_l, mod_row, use_rope, g1n, w_qkv, b_qkv, qg, kg, rowt, colt, bd, tile):
    t, d = xt.shape
    nqkv = w_qkv.shape[1]
    nq = N_HEADS * HEAD_DIM
    tok = lambda i: (i, 0)
    nrow = max(tile // GRID_W, 1)
    return pl.pallas_call(
        functools.partial(_qkv_kernel, mod_row, use_rope),
        grid=(t // tile,),
        in_specs=[pl.BlockSpec((tile, d), tok), _full(mod_l.shape), _full((1, d)),
                  _full((d, nqkv)), _full((1, nqkv)), _full((1, nq)), _full((1, LANES)),
                  pl.BlockSpec((3, nrow, LANES), lambda i: (0, i, 0)), _full(colt.shape), _full(bd.shape)],
        out_specs=[pl.BlockSpec((tile, nq), tok), pl.BlockSpec((tile, 4 * LANES), tok),
                   pl.BlockSpec((tile, 4 * LANES), tok)],
        out_shape=[jax.ShapeDtypeStruct((t, nq), BF16), jax.ShapeDtypeStruct((t, 4 * LANES), BF16),
                   jax.ShapeDtypeStruct((t, 4 * LANES), BF16)],
        scratch_shapes=[pltpu.VMEM((d, nqkv), BF16)],
        compiler_params=_cparams(1),
        name="qkv_project",
    )(xt, mod_l, g1n, w_qkv, b_qkv, qg, kg, rowt, colt, bd)


def _attn_kernel(sink_ref, q_ref, kp_ref, kc_ref, kn_ref, vp_ref, vc_ref, vn_ref, kx_ref, vx_ref, bias_ref,
                 o_ref):
    qb = q_ref.shape[0]
    band = 3 * qb
    pairs_per_group = (N_HEADS // N_KV_HEADS) // 2
    rows = pairs_per_group * qb

    bias = bias_ref[0]
    kband = jnp.concatenate([kp_ref[...], kc_ref[...], kn_ref[...]], axis=0)
    vband = jnp.concatenate([vp_ref[...], vc_ref[...], vn_ref[...]], axis=0)
    kctx = kx_ref[...]
    vctx = vx_ref[...]
    dn_t = (((1,), (1,)), ((), ()))
    rsub = lax.broadcasted_iota(I32, (rows, 1), 0) // qb

    for g in range(N_KV_HEADS):
        qg = jnp.concatenate(
            [q_ref[:, (g * pairs_per_group + j) * LANES:(g * pairs_per_group + j + 1) * LANES]
             for j in range(pairs_per_group)], axis=0)
        acc = jnp.zeros((rows, LANES), F32)
        for par in range(2):
            col = (2 * g + par) * LANES
            s_b = lax.dot_general(qg, kband[:, col:col + LANES], dn_t, preferred_element_type=F32)
            s_c = lax.dot_general(qg, kctx[:, col:col + LANES], dn_t, preferred_element_type=F32)
            s_b = s_b + bias
            sink = jnp.zeros((rows, 1), F32)
            for j in range(pairs_per_group):
                hd = 2 * (g * pairs_per_group + j) + par
                sink = jnp.where(rsub == j, sink_ref[hd] * LOG2_E, sink)
            m = jnp.maximum(jnp.maximum(jnp.max(s_b, axis=-1, keepdims=True),
                                        jnp.max(s_c, axis=-1, keepdims=True)), sink)
            p_b = jnp.exp2(s_b - m)
            p_c = jnp.exp2(s_c - m)
            l = (jnp.sum(p_b, axis=-1, keepdims=True) + jnp.sum(p_c, axis=-1, keepdims=True)
                 + jnp.exp2(sink - m))
            o = (jnp.dot(p_b.astype(BF16), vband[:, col:col + LANES], preferred_element_type=F32)
                 + jnp.dot(p_c.astype(BF16), vctx[:, col:col + LANES], preferred_element_type=F32))
            acc = acc + o / l
        for j in range(pairs_per_group):
            pcol = (g * pairs_per_group + j) * LANES
            o_ref[:, pcol:pcol + LANES] = acc[j * qb:(j + 1) * qb].astype(BF16)


def _attention(q, kk, vv, kkc, vvc, sinks):
    t, nq = q.shape
    qb = Q_BLOCK
    nb = t // qb
    c = kkc.shape[0]
    w = kk.shape[1]
    cur = lambda i, s: (i, 0)
    prv = lambda i, s: (jnp.maximum(i - 1, 0), 0)
    nxt = lambda i, s: (jnp.minimum(i + 1, nb - 1), 0)
    zero = lambda i, s: (0, 0)
    rows = (N_HEADS // N_KV_HEADS) // 2 * qb
    r = (jnp.arange(rows, dtype=I32) % qb)[:, None]
    cpos = jnp.arange(3 * qb, dtype=I32)[None, :] - qb
    inside = jnp.abs(r - cpos) <= WINDOW
    variants = [inside & ((cpos >= 0) | ((v & 1) == 0)) & ((cpos < qb) | ((v & 2) == 0)) for v in range(4)]
    bias = jnp.where(jnp.stack(variants), 0.0, NEG_BIG).astype(F32)
    which = lambda i, s: (jnp.where(i == 0, 1, 0) + jnp.where(i == nb - 1, 2, 0), 0, 0)
    gs = pltpu.PrefetchScalarGridSpec(
        num_scalar_prefetch=1,
        grid=(nb,),
        in_specs=[pl.BlockSpec((qb, nq), cur),
                  pl.BlockSpec((qb, w), prv), pl.BlockSpec((qb, w), cur), pl.BlockSpec((qb, w), nxt),
                  pl.BlockSpec((qb, w), prv), pl.BlockSpec((qb, w), cur), pl.BlockSpec((qb, w), nxt),
                  pl.BlockSpec((c, w), zero), pl.BlockSpec((c, w), zero),
                  pl.BlockSpec((1, rows, 3 * qb), which)],
        out_specs=pl.BlockSpec((qb, nq), cur),
    )
    return pl.pallas_call(
        _attn_kernel,
        grid_spec=gs,
        out_shape=jax.ShapeDtypeStruct((t, nq), BF16),
        compiler_params=_cparams(1),
        name="window_attention",
    )(sinks, q, kk, kk, kk, vv, vv, vv, kkc, vvc, bias)


def _dispatch_out(t, d, tile):
    nt = t // tile
    tok = lambda i: (i, 0)
    region = _region_rows(nt)
    specs = [pl.BlockSpec((tile, d), tok), pl.BlockSpec((region, d // 2), tok),
             pl.BlockSpec((tile, LANES), tok), pl.BlockSpec((tile, LANES), tok),
             pl.BlockSpec((1, SUBLANES, LANES), lambda i: (i, 0, 0))]
    shapes = [jax.ShapeDtypeStruct((t, d), F32), jax.ShapeDtypeStruct((nt * region, d // 2), jnp.uint32),
              jax.ShapeDtypeStruct((t, LANES), I32), jax.ShapeDtypeStruct((t, LANES), F32),
              jax.ShapeDtypeStruct((nt, SUBLANES, LANES), I32)]
    return specs, shapes


def _oproj_kernel(x_ref, o_ref, mod_ref, wo_ref, bo_ref, g2_ref, wr_ref, br_ref,
                  x1_ref, xs_ref, pos_ref, gate_ref, cnt_ref, wb_ref):
    d = x_ref.shape[1]

    @pl.when(pl.program_id(0) == 0)
    def _():
        wb_ref[...] = wo_ref[...].astype(BF16)

    y = jnp.dot(o_ref[...], wb_ref[...], preferred_element_type=F32) + bo_ref[...]
    x1 = x_ref[...] + mod_ref[0:1, 2 * d:3 * d] * y
    x1_ref[...] = x1
    _router_dispatch(x1, g2_ref[...], mod_ref[0:1, 3 * d:4 * d], mod_ref[0:1, 4 * d:5 * d],
                     wr_ref[...], br_ref[...], xs_ref, pos_ref, gate_ref, cnt_ref)


def _oproj_router(xt, o, mod_l, w_o, b_o, g2n, wr, br):
    t, d = xt.shape
    tile = MOE_TILE
    tok = lambda i: (i, 0)
    out_specs, out_shape = _dispatch_out(t, d, tile)
    return pl.pallas_call(
        _oproj_kernel,
        grid=(t // tile,),
        in_specs=[pl.BlockSpec((tile, d), tok), pl.BlockSpec((tile, o.shape[1]), tok), _full(mod_l.shape),
                  _full(w_o.shape), _full((1, d)), _full((1, d)), _full(wr.shape), _full(br.shape)],
        out_specs=out_specs,
        out_shape=out_shape,
        scratch_shapes=[pltpu.VMEM(w_o.shape, BF16)],
        compiler_params=_cparams(1),
        name="oproj_router",
    )(xt, o, mod_l, w_o, b_o, g2n, wr, br)


def _moe_kernel(nt, region_granules, brange_ref, nvg_ref, gsrc_ref, xs_hbm, w1_ref, b1_ref, w2_ref, b2_ref,
                ys_hbm, xbuf, obuf, w1b, w2b, gsem, ssem):
    e = pl.program_id(0)
    ne = pl.num_programs(0)
    b_lo = brange_ref[e]
    b_hi = brange_ref[ne + e]
    total = brange_ref[2 * ne - 1]
    gpb = GRANULES_PER_BLOCK
    dff = w2_ref.shape[1]
    spare = _spare_per_region(nt)
    idle_granule = REGION_DATA_GRANULES + 2 * spare

    def rows_of(granule):
        return pl.ds(pl.multiple_of(granule * GRANULE, GRANULE), GRANULE)

    def gather_copy(blk, gi):
        g = gsrc_ref[blk * gpb + gi]
        src = jnp.where(g >= 0, g, idle_granule)
        s = blk % 3
        return pltpu.make_async_copy(xs_hbm.at[rows_of(src)], xbuf.at[s, pl.ds(gi * GRANULE, GRANULE)],
                                     gsem.at[s])

    def scatter_copy(blk, s, gi):
        g = gsrc_ref[blk * gpb + gi]
        own_spare = (gi % nt) * region_granules + REGION_DATA_GRANULES + gi // nt + s * spare
        dst = jnp.where(g >= 0, g, own_spare)
        return pltpu.make_async_copy(obuf.at[s, pl.ds(gi * GRANULE, GRANULE)], ys_hbm.at[rows_of(dst)],
                                     ssem.at[s])

    def start_gather(blk):
        for gi in range(gpb):
            gather_copy(blk, gi).start()

    def wait_gather(s):
        pltpu.make_async_copy(xs_hbm.at[pl.ds(0, EXPERT_ROWS)], xbuf.at[s], gsem.at[s]).wait()

    def wait_scatter(s):
        pltpu.make_async_copy(obuf.at[s], ys_hbm.at[pl.ds(0, EXPERT_ROWS)], ssem.at[s]).wait()

    @pl.when(e == 0)
    def _():
        obuf[...] = jnp.zeros(obuf.shape, jnp.uint32)
        start_gather(0)
        start_gather(1)

    def expert_block(b, rows):
        slot = b % 2
        start_gather(b + 2)
        xb = jnp.concatenate(_unpack_halves(xbuf[b % 3, 0:rows, :]), axis=1).astype(BF16)
        gu = jnp.dot(xb, w1b[...], preferred_element_type=F32) + b1_ref[0]
        gt = jnp.minimum(gu[:, :dff], SWIGLU_LIMIT)
        up = jnp.clip(gu[:, dff:], -SWIGLU_LIMIT, SWIGLU_LIMIT)
        act = gt * jax.nn.sigmoid(SWIGLU_ALPHA * gt) * (up + 1.0)
        out = jnp.dot(act.astype(BF16), w2b[...], preferred_element_type=F32) + b2_ref[0]
        obuf[slot, 0:rows, :] = _pack_halves(out.astype(BF16).astype(F32))
        for gi in range(gpb):
            scatter_copy(b, slot, gi).start()

    def block_body(b, carry):
        wait_gather(b % 3)

        @pl.when(b >= 2)
        def _():
            wait_scatter(b % 2)

        half = EXPERT_ROWS // 2
        rows_needed = nvg_ref[b] * GRANULE

        @pl.when(rows_needed > half)
        def _():
            expert_block(b, EXPERT_ROWS)

        @pl.when(rows_needed <= half)
        def _():
            expert_block(b, half)

        return carry

    @pl.when(b_hi > b_lo)
    def _():
        w1b[...] = w1_ref[0].astype(BF16)
        w2b[...] = w2_ref[0].astype(BF16)
        lax.fori_loop(b_lo, b_hi, block_body, 0)

    @pl.when(e == ne - 1)
    def _():
        wait_gather(total % 3)
        wait_gather((total + 1) % 3)

        @pl.when(total >= 2)
        def _():
            wait_scatter(total % 2)

        @pl.when(total >= 1)
        def _():
            wait_scatter((total + 1) % 2)


def _moe_experts(xs, nt, block_range, n_granules, granule_src, layer, w1, b1, w2, b2):
    dw = xs.shape[1]
    d = 2 * dw
    dff = w2.shape[2]
    n_layers, n_exp = w1.shape[:2]
    by_expert = lambda e, br, ng, gs: (layer, e, 0, 0)
    gs = pltpu.PrefetchScalarGridSpec(
        num_scalar_prefetch=3,
        grid=(n_exp,),
        in_specs=[pl.BlockSpec(memory_space=pl.ANY),
                  pl.BlockSpec((None, 1, d, 2 * dff), by_expert), pl.BlockSpec((None, 1, 1, 2 * dff), by_expert),
                  pl.BlockSpec((None, 1, dff, d), by_expert), pl.BlockSpec((None, 1, 1, d), by_expert)],
        out_specs=pl.BlockSpec(memory_space=pl.ANY),
        scratch_shapes=[pltpu.VMEM((3, EXPERT_ROWS, dw), jnp.uint32), pltpu.VMEM((2, EXPERT_ROWS, dw), jnp.uint32),
                        pltpu.VMEM((d, 2 * dff), BF16), pltpu.VMEM((dff, d), BF16),
                        pltpu.SemaphoreType.DMA((3,)), pltpu.SemaphoreType.DMA((2,))],
    )
    return pl.pallas_call(
        functools.partial(_moe_kernel, nt, _region_rows(nt) // GRANULE),
        grid_spec=gs,
        out_shape=jax.ShapeDtypeStruct(xs.shape, jnp.uint32),
        input_output_aliases={3: 0},
        compiler_params=_cparams(1),
        name="moe_experts",
    )(block_range, n_granules, granule_src, xs, w1, b1.reshape(n_layers, n_exp, 1, 2 * dff),
      w2, b2.reshape(n_layers, n_exp, 1, d))


TABLE_BLOCKS = 8


def _tables_kernel(region_granules, cnt_ref, brange_ref, ngran_ref, gsrc_ref):
    gpb = GRANULES_PER_BLOCK
    log_b = gpb.bit_length() - 1
    n = cnt_ref.shape[0]
    rows = TABLE_BLOCKS * gpb
    nt_dims = (((1,), (1,)), ((), ()))

    cnt = cnt_ref[...].astype(F32)
    gl_len = cnt * (1.0 / GRANULE)
    ri = lax.broadcasted_iota(I32, (n, n), 0)
    ci = lax.broadcasted_iota(I32, (n, n), 1)
    lower = jnp.where(ci < ri, 1.0, 0.0).astype(BF16)
    upper = jnp.where(ri < ci, 1.0, 0.0).astype(BF16)
    pre = jnp.dot(lower, gl_len.astype(BF16), preferred_element_type=F32)
    off = jnp.dot(cnt.astype(BF16), upper, preferred_element_type=F32)
    src = ri.astype(F32) * float(region_granules) + off * (1.0 / GRANULE)
    n_g = jnp.sum(gl_len, axis=0, keepdims=True)
    blocks = ((n_g.astype(I32) + (gpb - 1)) >> log_b).astype(F32)
    bstart = jnp.dot(jnp.broadcast_to(blocks, (SUBLANES, n)).astype(BF16), upper,
                     preferred_element_type=F32)[0:1]
    bend = bstart + blocks

    def digits(v, base):
        hi = jnp.floor(v * (1.0 / base))
        return hi.astype(BF16), (v - hi * base).astype(BF16)

    def pick(onehot, table):
        return lax.dot_general(onehot, table, nt_dims, preferred_element_type=F32)

    def pick2(onehot, v, base):
        hi, lo = digits(v, base)
        return pick(onehot, hi) * base + pick(onehot, lo)

    q = pl.program_id(0) * rows + lax.broadcasted_iota(I32, (rows, n), 0)
    bq = (q >> log_b).astype(F32)
    iq = (q & (gpb - 1)).astype(F32)
    own = jnp.where((bstart <= bq) & (bq < bend), 1.0, 0.0).astype(BF16)
    pre_q = pick2(own, pre, 64.0)
    len_q = pick(own, gl_len.astype(BF16))
    src_q = pick2(own, src, 128.0)
    bstart_q = pick(own, jnp.broadcast_to(bstart, (n, n)).astype(BF16))
    gl = (bq - bstart_q) * float(gpb) + iq
    inside = (pre_q <= gl) & (gl < pre_q + len_q)
    hit = jnp.sum(jnp.where(inside, src_q - pre_q + gl + 1.0, 0.0), axis=-1, keepdims=True)
    gsrc_ref[...] = jnp.broadcast_to(hit - 1.0, (rows, n)).astype(I32)

    bb = (pl.program_id(0) * TABLE_BLOCKS + lax.broadcasted_iota(I32, (TABLE_BLOCKS, n), 0)).astype(F32)
    lane = lax.broadcasted_iota(I32, (TABLE_BLOCKS, n), 1).astype(F32)
    mine = (bstart <= bb) & (bb < bend)
    red = lambda v: jnp.sum(jnp.where(mine, v, 0.0), axis=-1, keepdims=True)
    active = red(jnp.ones_like(lane))
    n_left = red(jnp.broadcast_to(n_g, mine.shape)) - (bb[:, 0:1] - red(jnp.broadcast_to(bstart, mine.shape))) * gpb
    ngran_ref[...] = jnp.broadcast_to(jnp.clip(n_left, 0.0, float(gpb)) * active, (TABLE_BLOCKS, n)).astype(I32)

    sub = lax.broadcasted_iota(I32, (SUBLANES, n), 0)
    brange_ref[...] = jnp.where(sub == 0, bstart, jnp.where(sub == 1, bend, 0.0)).astype(I32)


def _block_tables(cnt):
    nt = cnt.shape[0]
    assert nt <= LANES
    max_blocks = (nt * REGION_DATA_GRANULES) // GRANULES_PER_BLOCK + N_EXPERTS
    n_blocks = -(-(max_blocks + 2) // TABLE_BLOCKS) * TABLE_BLOCKS
    gpb = GRANULES_PER_BLOCK
    cnt_sq = jnp.zeros((LANES, LANES), I32).at[:nt].set(cnt[:, 0, :])
    by_step = lambda i: (i, 0)
    brange, ngran, gsrc = pl.pallas_call(
        functools.partial(_tables_kernel, _region_rows(nt) // GRANULE),
        grid=(n_blocks // TABLE_BLOCKS,),
        in_specs=[_full((LANES, LANES))],
        out_specs=[_full((SUBLANES, LANES)), pl.BlockSpec((TABLE_BLOCKS, LANES), by_step),
                   pl.BlockSpec((TABLE_BLOCKS * gpb, LANES), by_step)],
        out_shape=[jax.ShapeDtypeStruct((SUBLANES, LANES), I32), jax.ShapeDtypeStruct((n_blocks, LANES), I32),
                   jax.ShapeDtypeStruct((n_blocks * gpb, LANES), I32)],
        compiler_params=_cparams(1),
        name="moe_block_tables",
    )(cnt_sq)
    return brange[0:2, :N_EXPERTS].reshape(-1), ngran[:, 0], gsrc[:, 0]


def _combine_kernel(x_ref, ys_ref, pos_ref, gate_ref, mod_ref, o_ref):
    d = x_ref.shape[1]
    o_ref[...] = x_ref[...] + mod_ref[0:1, 5 * d:6 * d] * _moe_combine(ys_ref, pos_ref, gate_ref)


def _combine_glu_kernel(x_ref, ys_ref, pos_ref, gate_ref, modp_ref, modn_ref, g_ref, w_ref, b_ref,
                        o_ref, u_ref, wb_ref):
    d = x_ref.shape[1]

    @pl.when(pl.program_id(0) == 0)
    def _():
        wb_ref[...] = w_ref[...].astype(BF16)

    x2 = x_ref[...] + modp_ref[0:1, 5 * d:6 * d] * _moe_combine(ys_ref, pos_ref, gate_ref)
    o_ref[...] = x2
    h = _rms_mod(x2, g_ref[...], modn_ref[0:1, 0:d], modn_ref[0:1, d:2 * d])
    u = jnp.dot(h.astype(BF16), wb_ref[...], preferred_element_type=F32) + b_ref[...]
    u_ref[...] = u[:, :d] * jax.nn.sigmoid(u[:, d:])


def _combine_tile(t):
    return MOE_TILE * (2 if (t // MOE_TILE) % 2 == 0 else 1)


def _combine_in_specs(t, d, tile):
    tok = lambda i: (i, 0)
    region = _region_rows(t // MOE_TILE) * (tile // MOE_TILE)
    return [pl.BlockSpec((tile, d), tok), pl.BlockSpec((region, d // 2), tok),
            pl.BlockSpec((tile, LANES), tok), pl.BlockSpec((tile, LANES), tok)]


def _combine(x1, ys, pos, gates, mod_l):
    t, d = x1.shape
    tile = _combine_tile(t)
    return pl.pallas_call(
        _combine_kernel,
        grid=(t // tile,),
        in_specs=_combine_in_specs(t, d, tile) + [_full(mod_l.shape)],
        out_specs=pl.BlockSpec((tile, d), lambda i: (i, 0)),
        out_shape=jax.ShapeDtypeStruct((t, d), F32),
        compiler_params=_cparams(1),
        name="moe_combine",
    )(x1, ys, pos, gates, mod_l)


def _combine_glu(x1, ys, pos, gates, mod_prev, mod_next, g1n, w_pw1, b_pw1):
    t, d = x1.shape
    tile = _combine_tile(t)
    tok = lambda i: (i, 0)
    return pl.pallas_call(
        _combine_glu_kernel,
        grid=(t // tile,),
        in_specs=_combine_in_specs(t, d, tile)
                 + [_full(mod_prev.shape), _full(mod_next.shape), _full((1, d)),
                    _full(w_pw1.shape), _full((1, 2 * d))],
        out_specs=[pl.BlockSpec((tile, d), tok), pl.BlockSpec((tile, d), tok)],
        out_shape=[jax.ShapeDtypeStruct((t, d), F32), jax.ShapeDtypeStruct((t, d), F32)],
        scratch_shapes=[pltpu.VMEM(w_pw1.shape, BF16)],
        compiler_params=_cparams(1),
        name="combine_pw1_glu",
    )(x1, ys, pos, gates, mod_prev, mod_next, g1n, w_pw1, b_pw1)


def _conv_kernel(x_ref, up_ref, uc_ref, un_ref, mod_ref, wdw_ref, bdw_ref, lg_ref, lb_ref, w2_ref, b2_ref,
                 g2_ref, wr_ref, br_ref, x1_ref, xs_ref, pos_ref, gate_ref, cnt_ref, wb_ref, ubuf, shbuf, cbuf):
    i = pl.program_id(0)
    n = pl.num_programs(0)
    d = x_ref.shape[1]
    tile = x_ref.shape[0]
    halo = up_ref.shape[0]

    @pl.when(i == 0)
    def _():
        wb_ref[...] = w2_ref[...].astype(BF16)

    ubuf[0:halo, :] = jnp.where(i > 0, up_ref[...], 0.0)
    ubuf[halo:halo + tile, :] = uc_ref[...]
    ubuf[halo + tile:, :] = jnp.where(i < n - 1, un_ref[...], 0.0)

    span = tile + 2 * halo - SUBLANES
    for r in range(1, SUBLANES):
        shbuf[r - 1, 0:span, :] = ubuf[r:r + span, :]
    sub = CONV_CHUNK // SUBLANES
    for c0 in range(0, tile, CONV_CHUNK):
        parts = [jnp.zeros((SUBLANES, d), F32) + bdw_ref[...] for _ in range(sub)]
        for j in range(CONV_WIDTH):
            a, r = divmod(halo + j - CONV_PAD, SUBLANES)
            wj = wdw_ref[j * SUBLANES:(j + 1) * SUBLANES, :]
            for k in range(sub):
                lo = c0 + (a + k) * SUBLANES
                rows = ubuf[lo:lo + SUBLANES, :] if r == 0 else shbuf[r - 1, lo:lo + SUBLANES, :]
                parts[k] = parts[k] + rows * wj
        for k in range(sub):
            cbuf[c0 + k * SUBLANES:c0 + (k + 1) * SUBLANES, :] = parts[k]
    acc = cbuf[...]

    mu = jnp.mean(acc, axis=-1, keepdims=True)
    cen = acc - mu
    var = jnp.mean(cen * cen, axis=-1, keepdims=True)
    z = _silu(cen * lax.rsqrt(var + EPS) * lg_ref[...] + lb_ref[...])
    y = jnp.dot(z.astype(BF16), wb_ref[...], preferred_element_type=F32) + b2_ref[...]
    x1 = x_ref[...] + mod_ref[0:1, 2 * d:3 * d] * y
    x1_ref[...] = x1
    _router_dispatch(x1, g2_ref[...], mod_ref[0:1, 3 * d:4 * d], mod_ref[0:1, 4 * d:5 * d],
                     wr_ref[...], br_ref[...], xs_ref, pos_ref, gate_ref, cnt_ref)


def _conv_router(x2, u, mod_l, w_dw, b_dw, ln_g, ln_b, w_pw2, b_pw2, g2n, wr, br):
    t, d = x2.shape
    tile = MOE_TILE
    nt = t // tile
    hb = tile // CONV_HALO
    nh = t // CONV_HALO
    tok = lambda i: (i, 0)
    prv = lambda i: (jnp.maximum(i * hb - 1, 0), 0)
    nxt = lambda i: (jnp.minimum((i + 1) * hb, nh - 1), 0)
    wpad = jnp.repeat(w_dw, SUBLANES, axis=0)
    out_specs, out_shape = _dispatch_out(t, d, tile)
    return pl.pallas_call(
        _conv_kernel,
        grid=(nt,),
        in_specs=[pl.BlockSpec((tile, d), tok), pl.BlockSpec((CONV_HALO, d), prv),
                  pl.BlockSpec((tile, d), tok), pl.BlockSpec((CONV_HALO, d), nxt),
                  _full(mod_l.shape), _full(wpad.shape), _full((1, d)), _full((1, d)), _full((1, d)),
                  _full(w_pw2.shape), _full((1, d)), _full((1, d)), _full(wr.shape), _full(br.shape)],
        out_specs=out_specs,
        out_shape=out_shape,
        scratch_shapes=[pltpu.VMEM(w_pw2.shape, BF16), pltpu.VMEM((tile + 2 * CONV_HALO, d), F32),
                        pltpu.VMEM((SUBLANES - 1, tile + 2 * CONV_HALO, d), F32), pltpu.VMEM((tile, d), F32)],
        compiler_params=_cparams(1),
        name="conv_router",
    )(x2, u, u, u, mod_l, wpad, b_dw, ln_g, ln_b, w_pw2, b_pw2, g2n, wr, br)


def _rope_tables(n_rows):
    quarter = HEAD_DIM // 4
    inv = ROPE_BASE ** (-jnp.arange(quarter, dtype=F32) / quarter)

    def tables(npos):
        ang = jnp.arange(npos, dtype=I32).astype(F32)[:, None] * inv[None, :]
        z = jnp.zeros_like(ang)
        c, s = jnp.cos(ang), jnp.sin(ang)
        half = lambda a, b: jnp.concatenate([a, b], axis=1)
        full = lambda h: jnp.concatenate([h, h, h, h], axis=1)
        return jnp.stack([full(half(c, c)), full(half(-s, z)), full(half(z, s))])

    return tables(n_rows), tables(GRID_W)


def _router_params(w_router, b_router):
    d, e = w_router.shape
    wr = jnp.zeros((LANES, d), F32).at[:e].set(w_router.T)
    br = jnp.zeros((LANES, MOE_TILE), F32).at[:e].set(jnp.broadcast_to(b_router[:, None], (e, MOE_TILE)))
    hi = wr.astype(BF16)
    lo = (wr - hi.astype(F32)).astype(BF16)
    return jnp.stack([hi, lo]), br


def kernel(x, c, ctx, c_ctx, w_mod, b_mod, norm1_g, norm2_g, attn_w_qkv, attn_b_qkv, attn_w_o, attn_b_o,
           attn_q_norm, attn_k_norm, attn_sinks, conv_w_pw1, conv_b_pw1, conv_w_dw, conv_b_dw, conv_ln_g,
           conv_ln_b, conv_w_pw2, conv_b_pw2, moe_w_router, moe_b_router, moe_w1, moe_b1, moe_w2, moe_b2):
    bsz, t, d = x.shape
    assert bsz == 1 and w_mod.shape[0] == 2
    n_ctx = ctx.shape[1]
    tile = min(TOKEN_TILE, t)
    assert t % tile == 0 and t % Q_BLOCK == 0 and t % MOE_TILE == 0 and tile % GRID_W == 0
    xt = x.reshape(t, d)
    row = lambda v: v.reshape(1, -1)

    mod = _mod_vectors(c, c_ctx, w_mod, b_mod)
    mod0, mod1 = mod[0], mod[1]

    bd = (jnp.arange(2 * LANES)[:, None] // HEAD_DIM == jnp.arange(2 * LANES)[None, :] // HEAD_DIM).astype(BF16)
    qg = row(jnp.tile(attn_q_norm[0], N_HEADS))
    kg = row(jnp.tile(attn_k_norm[0], N_KV_HEADS))
    rowt, colt = _rope_tables(t // GRID_W)
    g1n = row(norm1_g[0])
    bq = row(attn_b_qkv[0])
    q, kk, vv = _qkv_project(xt, mod0, 0, True, g1n, attn_w_qkv[0], bq, qg, kg, rowt, colt, bd, tile)
    _, kkc, vvc = _qkv_project(ctx.reshape(n_ctx, d), mod0, 1, False, g1n, attn_w_qkv[0], bq, qg, kg,
                               rowt[:, :max(n_ctx // GRID_W, 1)], colt, bd, n_ctx)
    o = _attention(q, kk, vv, kkc, vvc, attn_sinks[0])
    wr0, br0 = _router_params(moe_w_router[0], moe_b_router[0])
    x1, xs, pos, gates, cnt = _oproj_router(xt, o, mod0, attn_w_o[0], row(attn_b_o[0]), row(norm2_g[0]),
                                            wr0, br0)
    ys = _moe_experts(xs, cnt.shape[0], *_block_tables(cnt), 0, moe_w1, moe_b1, moe_w2, moe_b2)

    x2, u = _combine_glu(x1, ys, pos, gates, mod0, mod1, row(norm1_g[1]), conv_w_pw1[0], row(conv_b_pw1[0]))
    wr1, br1 = _router_params(moe_w_router[1], moe_b_router[1])
    x3, xs, pos, gates, cnt = _conv_router(x2, u, mod1, conv_w_dw[0], row(conv_b_dw[0]), row(conv_ln_g[0]),
                                           row(conv_ln_b[0]), conv_w_pw2[0], row(conv_b_pw2[0]),
                                           row(norm2_g[1]), wr1, br1)
    ys = _moe_experts(xs, cnt.shape[0], *_block_tables(cnt), 1, moe_w1, moe_b1, moe_w2, moe_b2)
    out = _combine(x3, ys, pos, gates, mod1)
    return out.reshape(1, t, d)
```

```python
import functools

import jax
import jax.numpy as jnp
from jax import lax
from jax.experimental import pallas as pl
from jax.experimental.pallas import tpu as pltpu

F32 = jnp.float32
BF16 = jnp.bfloat16
I32 = jnp.int32

N_HEADS = 16
N_KV_HEADS = 2
HEAD_DIM = 64
GRID_W = 64
WINDOW = 128
Q_BLOCK = 128
ROPE_BASE = 10000.0
CONV_WIDTH = 31
CONV_PAD = CONV_WIDTH // 2
N_EXPERTS = 32
TOP_K = 4
SWIGLU_LIMIT = 7.0
SWIGLU_ALPHA = 1.702
EPS = 1e-6

LANES = 128
SUBLANES = 8
TOKEN_TILE = 512
MOE_TILE = 256
GRANULE = SUBLANES
EXPERT_ROWS = 512
GRANULES_PER_BLOCK = EXPERT_ROWS // GRANULE
REGION_DATA_GRANULES = (MOE_TILE * TOP_K + N_EXPERTS * (GRANULE - 1)) // GRANULE


def _spare_per_region(nt):
    return -(-GRANULES_PER_BLOCK // nt)


def _region_rows(nt):
    rows = (REGION_DATA_GRANULES + 2 * _spare_per_region(nt) + 1) * GRANULE
    return -(-rows // 256) * 256
CONV_HALO = 16
CONV_CHUNK = 32
NEG_BIG = -1e30
LOG2_E = 1.4426950408889634
VMEM_LIMIT = 56 * 1024 * 1024


def _cparams(n_axes=1, vmem=VMEM_LIMIT):
    return pltpu.CompilerParams(dimension_semantics=("arbitrary",) * n_axes, vmem_limit_bytes=vmem)


def _full(shape):
    nd = len(shape)
    return pl.BlockSpec(shape, lambda *_: (0,) * nd)


def _silu(v):
    return v * jax.nn.sigmoid(v)


def _rms_mod(xv, g, shift, scale):
    ms = jnp.mean(xv * xv, axis=-1, keepdims=True)
    return (xv * lax.rsqrt(ms + EPS)) * g * (1.0 + scale) + shift


def _split_dot(a, b_bf16):
    hi = a.astype(BF16)
    lo = (a - hi.astype(F32)).astype(BF16)
    return (jnp.dot(hi, b_bf16, preferred_element_type=F32)
            + jnp.dot(lo, b_bf16, preferred_element_type=F32))


def _router_dispatch(x_new, g2n, sh2, sc2, wr, br, xs_ref, pos_ref, gate_ref, cnt_ref):
    tt = x_new.shape[0]
    rt = xs_ref.shape[0]
    h2 = _rms_mod(x_new, g2n, sh2, sc2)
    h_hi = h2.astype(BF16)
    h_lo = (h2 - h_hi.astype(F32)).astype(BF16)
    nt_dims = (((1,), (1,)), ((), ()))
    wdot = lambda w, h: lax.dot_general(w, h, nt_dims, preferred_element_type=F32)
    logits = (wdot(wr[0], h_hi) + (wdot(wr[0], h_lo) + wdot(wr[1], h_hi)) + br)[0:N_EXPERTS]
    erow = lax.broadcasted_iota(I32, (N_EXPERTS, tt), 0)
    hits, vals = [], []
    cur = logits
    for k in range(TOP_K):
        m = jnp.max(cur, axis=0, keepdims=True)
        sel = jnp.min(jnp.where(cur == m, erow, N_EXPERTS), axis=0, keepdims=True)
        hit = erow == sel
        hits.append(hit)
        vals.append(m)
        cur = jnp.where(hit, NEG_BIG * 2.0, cur)
    evs = [jnp.exp(v - vals[0]) for v in vals]
    den = (evs[0] + evs[1]) + (evs[2] + evs[3])
    gates = [ev / den for ev in evs]

    onehot = jnp.where((hits[0] | hits[1]) | (hits[2] | hits[3]), 1.0, 0.0).astype(BF16)
    ri = lax.broadcasted_iota(I32, (tt, tt), 0)
    ci = lax.broadcasted_iota(I32, (tt, tt), 1)
    earlier = jnp.where(ri < ci, 1.0, 0.0).astype(BF16)
    rank = jnp.dot(onehot, earlier, preferred_element_type=F32)
    cnt = jnp.dot(onehot, jnp.ones((tt, LANES), BF16), preferred_element_type=F32)
    c8 = jnp.bitwise_and(cnt.astype(I32) + (GRANULE - 1), -GRANULE).astype(F32)
    ue = lax.broadcasted_iota(I32, (N_EXPERTS, N_EXPERTS), 0)
    ve = lax.broadcasted_iota(I32, (N_EXPERTS, N_EXPERTS), 1)
    below = jnp.where(ve < ue, 1.0, 0.0).astype(BF16)
    off = jnp.dot(below, c8.astype(BF16), preferred_element_type=F32)
    slot_of = jnp.concatenate([off] * (tt // LANES), axis=1) + rank

    onehot_pad = jnp.concatenate([onehot, jnp.zeros((LANES - N_EXPERTS, tt), BF16)], axis=0)
    cnt_row = lax.dot_general(jnp.ones((2 * SUBLANES, tt), BF16), onehot_pad, nt_dims,
                              preferred_element_type=F32)[0:SUBLANES]
    cnt_ref[0] = jnp.bitwise_and(cnt_row.astype(I32) + (GRANULE - 1), -GRANULE)

    r_iota = lax.broadcasted_iota(I32, (rt, tt), 0)
    krow = lax.broadcasted_iota(I32, (LANES, tt), 0)
    perm = jnp.zeros((rt, tt), F32)
    pos_t = jnp.zeros((LANES, tt), F32)
    gate_t = jnp.zeros((LANES, tt), F32)
    for k in range(TOP_K):
        pk = jnp.sum(jnp.where(hits[k], slot_of, 0.0), axis=0, keepdims=True)
        perm = jnp.where(r_iota == pk.astype(I32), 1.0, perm)
        pos_t = jnp.where(krow == k, pk, pos_t)
        gate_t = jnp.where(krow == k, gates[k], gate_t)
    pos_ref[...] = pos_t.T.astype(I32)
    gate_ref[...] = gate_t.T
    xs_ref[...] = _pack_halves(jnp.dot(perm.astype(BF16), h_hi, preferred_element_type=F32))


def _pack_halves(v):
    half = v.shape[1] // 2
    lo = lax.bitcast_convert_type(v[:, :half], jnp.uint32) >> 16
    hi = lax.bitcast_convert_type(v[:, half:], jnp.uint32) & jnp.uint32(0xFFFF0000)
    return lo | hi


def _unpack_halves(w):
    return (lax.bitcast_convert_type(w << 16, F32),
            lax.bitcast_convert_type(w & jnp.uint32(0xFFFF0000), F32))


def _moe_combine(ys_ref, pos_ref, gate_ref):
    tt = MOE_TILE
    n_tiles = pos_ref.shape[0] // tt
    rt = ys_ref.shape[0] // n_tiles
    r_iota = lax.broadcasted_iota(I32, (tt, rt), 1)
    outs = []
    for j in range(n_tiles):
        pos = pos_ref[j * tt:(j + 1) * tt, :]
        gate = gate_ref[j * tt:(j + 1) * tt, :]
        g = jnp.zeros((tt, rt), F32)
        for k in range(TOP_K):
            g = jnp.where(r_iota == pos[:, k:k + 1], gate[:, k:k + 1], g)
        g = g.astype(BF16)
        ya, yb = _unpack_halves(ys_ref[j * rt:(j + 1) * rt, :])
        outs.append(jnp.concatenate([jnp.dot(g, ya.astype(BF16), preferred_element_type=F32),
                                     jnp.dot(g, yb.astype(BF16), preferred_element_type=F32)], axis=1))
    return outs[0] if n_tiles == 1 else jnp.concatenate(outs, axis=0)


def _mod_kernel(cc_ref, w_ref, b_ref, o_ref):
    a = _silu(cc_ref[...])
    o_ref[0] = jnp.dot(a, w_ref[0], precision=lax.Precision.HIGHEST,
                       preferred_element_type=F32) + b_ref[0]


def _mod_vectors(c, c_ctx, w_mod, b_mod):
    depth, d, d6 = w_mod.shape
    cc = jnp.zeros((SUBLANES, d), F32).at[0].set(c[0]).at[1].set(c_ctx)
    ncol = 4
    cw = d6 // ncol
    return pl.pallas_call(
        _mod_kernel,
        grid=(depth, ncol),
        in_specs=[pl.BlockSpec((SUBLANES, d), lambda l, j: (0, 0)),
                  pl.BlockSpec((1, d, cw), lambda l, j: (l, 0, j)),
                  pl.BlockSpec((1, 1, cw), lambda l, j: (l, 0, j))],
        out_specs=pl.BlockSpec((1, SUBLANES, cw), lambda l, j: (l, 0, j)),
        out_shape=jax.ShapeDtypeStruct((depth, SUBLANES, d6), F32),
        compiler_params=_cparams(2),
        name="mod_vectors",
    )(cc, w_mod, b_mod.reshape(depth, 1, d6))


def _qkv_kernel(mod_row, use_rope, x_ref, mod_ref, g_ref, w_ref, b_ref, qg_ref, kg_ref, rowt_ref, colt_ref,
                bd_ref, q_ref, k_ref, v_ref, wb_ref):
    d = x_ref.shape[1]
    tile = x_ref.shape[0]
    nq = N_HEADS * HEAD_DIM
    nk = N_KV_HEADS * HEAD_DIM

    @pl.when(pl.program_id(0) == 0)
    def _():
        wb_ref[...] = w_ref[...].astype(BF16)

    sh = mod_ref[mod_row:mod_row + 1, 0:d]
    sc = mod_ref[mod_row:mod_row + 1, d:2 * d]
    h = _rms_mod(x_ref[...], g_ref[...], sh, sc)
    qkv = jnp.dot(h.astype(BF16), wb_ref[...], preferred_element_type=F32) + b_ref[...]
    q = qkv[:, :nq]
    k = qkv[:, nq:nq + nk]
    v = qkv[:, nq + nk:]
    bd = bd_ref[...]

    def head_norm(t, gain):
        tt = t * t
        w = bd.shape[0]
        if t.shape[1] >= w:
            ss = jnp.concatenate([_split_dot(tt[:, c:c + w], bd) for c in range(0, t.shape[1], w)], axis=1)
        else:
            ss = _split_dot(tt, bd[:t.shape[1], :t.shape[1]])
        return t * lax.rsqrt(ss * (1.0 / HEAD_DIM) + EPS) * gain

    qn = head_norm(q, qg_ref[...])
    kn = head_norm(k, kg_ref[...])

    if use_rope:
        lane = lax.broadcasted_iota(I32, (tile, LANES), 1)
        is_row = (lane % HEAD_DIM) < HEAD_DIM // 2
        nrow = tile // GRID_W

        def table(idx):
            rt = jnp.concatenate([jnp.broadcast_to(rowt_ref[idx, r:r + 1, :], (GRID_W, LANES))
                                  for r in range(nrow)], axis=0)
            ct = jnp.concatenate([colt_ref[idx]] * nrow, axis=0)
            return jnp.where(is_row, rt, ct)

        cosv, sav, sbv = table(0), table(1), table(2)

        def rope(t):
            n = t.shape[1]
            reps = n // LANES
            ct = jnp.concatenate([cosv] * reps, axis=1) if reps > 1 else cosv
            at = jnp.concatenate([sav] * reps, axis=1) if reps > 1 else sav
            bt = jnp.concatenate([sbv] * reps, axis=1) if reps > 1 else sbv
            up = pltpu.roll(t, n - HEAD_DIM // 4, 1)
            dn = pltpu.roll(t, HEAD_DIM // 4, 1)
            return t * ct + up * at + dn * bt

        qn = rope(qn)
        kn = rope(kn)

    q_ref[...] = (qn * (HEAD_DIM ** -0.5 * LOG2_E)).astype(BF16)
    klane = lax.broadcasted_iota(I32, kn.shape, 1)
    low = klane < HEAD_DIM

    def variants(t):
        sw = pltpu.roll(t, HEAD_DIM, 1)
        z = jnp.zeros_like(t)
        return jnp.concatenate([jnp.where(low, t, z), jnp.where(low, z, sw),
                                jnp.where(low, sw, z), jnp.where(low, z, t)], axis=1)

    k_ref[...] = variants(kn).astype(BF16)
    v_ref[...] = variants(v).astype(BF16)


def _qkv_project(xt, mod_l, mod_row, use_rope, g1n, w_qkv, b_qkv, qg, kg, rowt, colt, bd, tile):
    t, d = xt.shape
    nqkv = w_qkv.shape[1]
    nq = N_HEADS * HEAD_DIM
    tok = lambda i: (i, 0)
    nrow = max(tile // GRID_W, 1)
    return pl.pallas_call(
        functools.partial(_qkv_kernel, mod_row, use_rope),
        grid=(t // tile,),
        in_specs=[pl.BlockSpec((tile, d), tok), _full(mod_l.shape), _full((1, d)),
                  _full((d, nqkv)), _full((1, nqkv)), _full((1, nq)), _full((1, LANES)),
                  pl.BlockSpec((3, nrow, LANES), lambda i: (0, i, 0)), _full(colt.shape), _full(bd.shape)],
        out_specs=[pl.BlockSpec((tile, nq), tok), pl.BlockSpec((tile, 4 * LANES), tok),
                   pl.BlockSpec((tile, 4 * LANES), tok)],
        out_shape=[jax.ShapeDtypeStruct((t, nq), BF16), jax.ShapeDtypeStruct((t, 4 * LANES), BF16),
                   jax.ShapeDtypeStruct((t, 4 * LANES), BF16)],
        scratch_shapes=[pltpu.VMEM((d, nqkv), BF16)],
        compiler_params=_cparams(1),
        name="qkv_project",
    )(xt, mod_l, g1n, w_qkv, b_qkv, qg, kg, rowt, colt, bd)


def _attn_kernel(sink_ref, q_ref, kp_ref, kc_ref, kn_ref, vp_ref, vc_ref, vn_ref, kx_ref, vx_ref, bias_ref,
                 o_ref):
    qb = q_ref.shape[0]
    band = 3 * qb
    pairs_per_group = (N_HEADS // N_KV_HEADS) // 2
    rows = pairs_per_group * qb

    bias = bias_ref[0]
    kband = jnp.concatenate([kp_ref[...], kc_ref[...], kn_ref[...]], axis=0)
    vband = jnp.concatenate([vp_ref[...], vc_ref[...], vn_ref[...]], axis=0)
    kctx = kx_ref[...]
    vctx = vx_ref[...]
    dn_t = (((1,), (1,)), ((), ()))
    rsub = lax.broadcasted_iota(I32, (rows, 1), 0) // qb

    for g in range(N_KV_HEADS):
        qg = jnp.concatenate(
            [q_ref[:, (g * pairs_per_group + j) * LANES:(g * pairs_per_group + j + 1) * LANES]
             for j in range(pairs_per_group)], axis=0)
        acc = jnp.zeros((rows, LANES), F32)
        for par in range(2):
            col = (2 * g + par) * LANES
            s_b = lax.dot_general(qg, kband[:, col:col + LANES], dn_t, preferred_element_type=F32)
            s_c = lax.dot_general(qg, kctx[:, col:col + LANES], dn_t, preferred_element_type=F32)
            s_b = s_b + bias
            sink = jnp.zeros((rows, 1), F32)
            for j in range(pairs_per_group):
                hd = 2 * (g * pairs_per_group + j) + par
                sink = jnp.where(rsub == j, sink_ref[hd] * LOG2_E, sink)
            m = jnp.maximum(jnp.maximum(jnp.max(s_b, axis=-1, keepdims=True),
                                        jnp.max(s_c, axis=-1, keepdims=True)), sink)
            p_b = jnp.exp2(s_b - m)
            p_c = jnp.exp2(s_c - m)
            l = (jnp.sum(p_b, axis=-1, keepdims=True) + jnp.sum(p_c, axis=-1, keepdims=True)
                 + jnp.exp2(sink - m))
            o = (jnp.dot(p_b.astype(BF16), vband[:, col:col + LANES], preferred_element_type=F32)
                 + jnp.dot(p_c.astype(BF16), vctx[:, col:col + LANES], preferred_element_type=F32))
            acc = acc + o / l
        for j in range(pairs_per_group):
            pcol = (g * pairs_per_group + j) * LANES
            o_ref[:, pcol:pcol + LANES] = acc[j * qb:(j + 1) * qb].astype(BF16)


def _attention(q, kk, vv, kkc, vvc, sinks):
    t, nq = q.shape
    qb = Q_BLOCK
    nb = t // qb
    c = kkc.shape[0]
    w = kk.shape[1]
    cur = lambda i, s: (i, 0)
    prv = lambda i, s: (jnp.maximum(i - 1, 0), 0)
    nxt = lambda i, s: (jnp.minimum(i + 1, nb - 1), 0)
    zero = lambda i, s: (0, 0)
    rows = (N_HEADS // N_KV_HEADS) // 2 * qb
    r = (jnp.arange(rows, dtype=I32) % qb)[:, None]
    cpos = jnp.arange(3 * qb, dtype=I32)[None, :] - qb
    inside = jnp.abs(r - cpos) <= WINDOW
    variants = [inside & ((cpos >= 0) | ((v & 1) == 0)) & ((cpos < qb) | ((v & 2) == 0)) for v in range(4)]
    bias = jnp.where(jnp.stack(variants), 0.0, NEG_BIG).astype(F32)
    which = lambda i, s: (jnp.where(i == 0, 1, 0) + jnp.where(i == nb - 1, 2, 0), 0, 0)
    gs = pltpu.PrefetchScalarGridSpec(
        num_scalar_prefetch=1,
        grid=(nb,),
        in_specs=[pl.BlockSpec((qb, nq), cur),
                  pl.BlockSpec((qb, w), prv), pl.BlockSpec((qb, w), cur), pl.BlockSpec((qb, w), nxt),
                  pl.BlockSpec((qb, w), prv), pl.BlockSpec((qb, w), cur), pl.BlockSpec((qb, w), nxt),
                  pl.BlockSpec((c, w), zero), pl.BlockSpec((c, w), zero),
                  pl.BlockSpec((1, rows, 3 * qb), which)],
        out_specs=pl.BlockSpec((qb, nq), cur),
    )
    return pl.pallas_call(
        _attn_kernel,
        grid_spec=gs,
        out_shape=jax.ShapeDtypeStruct((t, nq), BF16),
        compiler_params=_cparams(1),
        name="window_attention",
    )(sinks, q, kk, kk, kk, vv, vv, vv, kkc, vvc, bias)


def _dispatch_tiles(x_new, g2n, sh2, sc2, wr, br, xs_ref, pos_ref, gate_ref, cnt_ref):
    n = x_new.shape[0] // MOE_TILE
    rt = xs_ref.shape[0] // n
    for j in range(n):
        rows = pl.ds(j * MOE_TILE, MOE_TILE)
        _router_dispatch(x_new[j * MOE_TILE:(j + 1) * MOE_TILE], g2n, sh2, sc2, wr, br,
                         xs_ref.at[pl.ds(j * rt, rt)], pos_ref.at[rows], gate_ref.at[rows],
                         cnt_ref.at[pl.ds(j, 1)])


def _dispatch_out(t, d, tile):
    nt = t // MOE_TILE
    per_step = tile // MOE_TILE
    tok = lambda i: (i, 0)
    region = _region_rows(nt)
    specs = [pl.BlockSpec((tile, d), tok), pl.BlockSpec((per_step * region, d // 2), tok),
             pl.BlockSpec((tile, LANES), tok), pl.BlockSpec((tile, LANES), tok),
             pl.BlockSpec((per_step, SUBLANES, LANES), lambda i: (i, 0, 0))]
    shapes = [jax.ShapeDtypeStruct((t, d), F32), jax.ShapeDtypeStruct((nt * region, d // 2), jnp.uint32),
              jax.ShapeDtypeStruct((t, LANES), I32), jax.ShapeDtypeStruct((t, LANES), F32),
              jax.ShapeDtypeStruct((nt, SUBLANES, LANES), I32)]
    return specs, shapes


def _oproj_kernel(x_ref, o_ref, mod_ref, wo_ref, bo_ref, g2_ref, wr_ref, br_ref,
                  x1_ref, xs_ref, pos_ref, gate_ref, cnt_ref, wb_ref):
    d = x_ref.shape[1]

    @pl.when(pl.program_id(0) == 0)
    def _():
        wb_ref[...] = wo_ref[...].astype(BF16)

    y = jnp.dot(o_ref[...], wb_ref[...], preferred_element_type=F32) + bo_ref[...]
    x1 = x_ref[...] + mod_ref[0:1, 2 * d:3 * d] * y
    x1_ref[...] = x1
    _dispatch_tiles(x1, g2_ref[...], mod_ref[0:1, 3 * d:4 * d], mod_ref[0:1, 4 * d:5 * d],
                     wr_ref[...], br_ref[...], xs_ref, pos_ref, gate_ref, cnt_ref)


def _oproj_router(xt, o, mod_l, w_o, b_o, g2n, wr, br):
    t, d = xt.shape
    tile = _combine_tile(t)
    tok = lambda i: (i, 0)
    out_specs, out_shape = _dispatch_out(t, d, tile)
    return pl.pallas_call(
        _oproj_kernel,
        grid=(t // tile,),
        in_specs=[pl.BlockSpec((tile, d), tok), pl.BlockSpec((tile, o.shape[1]), tok), _full(mod_l.shape),
                  _full(w_o.shape), _full((1, d)), _full((1, d)), _full(wr.shape), _full(br.shape)],
        out_specs=out_specs,
        out_shape=out_shape,
        scratch_shapes=[pltpu.VMEM(w_o.shape, BF16)],
        compiler_params=_cparams(1),
        name="oproj_router",
    )(xt, o, mod_l, w_o, b_o, g2n, wr, br)


def _moe_kernel(nt, region_granules, brange_ref, nvg_ref, gsrc_ref, xs_hbm, w1_ref, b1_ref, w2_ref, b2_ref,
                ys_hbm, xbuf, obuf, w1b, w2b, gsem, ssem):
    e = pl.program_id(0)
    ne = pl.num_programs(0)
    b_lo = brange_ref[e]
    b_hi = brange_ref[ne + e]
    total = brange_ref[2 * ne - 1]
    gpb = GRANULES_PER_BLOCK
    dff = w2_ref.shape[1]
    spare = _spare_per_region(nt)
    idle_granule = REGION_DATA_GRANULES + 2 * spare

    def rows_of(granule):
        return pl.ds(pl.multiple_of(granule * GRANULE, GRANULE), GRANULE)

    def gather_copy(blk, gi):
        g = gsrc_ref[blk * gpb + gi]
        src = jnp.where(g >= 0, g, idle_granule)
        s = blk % 3
        return pltpu.make_async_copy(xs_hbm.at[rows_of(src)], xbuf.at[s, pl.ds(gi * GRANULE, GRANULE)],
                                     gsem.at[s])

    def scatter_copy(blk, s, gi):
        g = gsrc_ref[blk * gpb + gi]
        own_spare = (gi % nt) * region_granules + REGION_DATA_GRANULES + gi // nt + s * spare
        dst = jnp.where(g >= 0, g, own_spare)
        return pltpu.make_async_copy(obuf.at[s, pl.ds(gi * GRANULE, GRANULE)], ys_hbm.at[rows_of(dst)],
                                     ssem.at[s])

    def start_gather(blk):
        for gi in range(gpb):
            gather_copy(blk, gi).start()

    def wait_gather(s):
        pltpu.make_async_copy(xs_hbm.at[pl.ds(0, EXPERT_ROWS)], xbuf.at[s], gsem.at[s]).wait()

    def wait_scatter(s):
        pltpu.make_async_copy(obuf.at[s], ys_hbm.at[pl.ds(0, EXPERT_ROWS)], ssem.at[s]).wait()

    @pl.when(e == 0)
    def _():
        obuf[...] = jnp.zeros(obuf.shape, jnp.uint32)
        start_gather(0)
        start_gather(1)

    def expert_block(b, rows):
        slot = b % 2
        start_gather(b + 2)
        xb = jnp.concatenate(_unpack_halves(xbuf[b % 3, 0:rows, :]), axis=1).astype(BF16)
        gu = jnp.dot(xb, w1b[...], preferred_element_type=F32) + b1_ref[0]
        gt = jnp.minimum(gu[:, :dff], SWIGLU_LIMIT)
        up = jnp.clip(gu[:, dff:], -SWIGLU_LIMIT, SWIGLU_LIMIT)
        act = gt * jax.nn.sigmoid(SWIGLU_ALPHA * gt) * (up + 1.0)
        out = jnp.dot(act.astype(BF16), w2b[...], preferred_element_type=F32) + b2_ref[0]
        obuf[slot, 0:rows, :] = _pack_halves(out.astype(BF16).astype(F32))
        for gi in range(gpb):
            scatter_copy(b, slot, gi).start()

    def block_body(b, carry):
        wait_gather(b % 3)

        @pl.when(b >= 2)
        def _():
            wait_scatter(b % 2)

        half = EXPERT_ROWS // 2
        rows_needed = nvg_ref[b] * GRANULE

        @pl.when(rows_needed > half)
        def _():
            expert_block(b, EXPERT_ROWS)

        @pl.when(rows_needed <= half)
        def _():
            expert_block(b, half)

        return carry

    @pl.when(b_hi > b_lo)
    def _():
        w1b[...] = w1_ref[0].astype(BF16)
        w2b[...] = w2_ref[0].astype(BF16)
        lax.fori_loop(b_lo, b_hi, block_body, 0)

    @pl.when(e == ne - 1)
    def _():
        wait_gather(total % 3)
        wait_gather((total + 1) % 3)

        @pl.when(total >= 2)
        def _():
            wait_scatter(total % 2)

        @pl.when(total >= 1)
        def _():
            wait_scatter((total + 1) % 2)


def _moe_experts(xs, nt, block_range, n_granules, granule_src, layer, w1, b1, w2, b2):
    dw = xs.shape[1]
    d = 2 * dw
    dff = w2.shape[2]
    n_layers, n_exp = w1.shape[:2]
    by_expert = lambda e, br, ng, gs: (layer, e, 0, 0)
    gs = pltpu.PrefetchScalarGridSpec(
        num_scalar_prefetch=3,
        grid=(n_exp,),
        in_specs=[pl.BlockSpec(memory_space=pl.ANY),
                  pl.BlockSpec((None, 1, d, 2 * dff), by_expert), pl.BlockSpec((None, 1, 1, 2 * dff), by_expert),
                  pl.BlockSpec((None, 1, dff, d), by_expert), pl.BlockSpec((None, 1, 1, d), by_expert)],
        out_specs=pl.BlockSpec(memory_space=pl.ANY),
        scratch_shapes=[pltpu.VMEM((3, EXPERT_ROWS, dw), jnp.uint32), pltpu.VMEM((2, EXPERT_ROWS, dw), jnp.uint32),
                        pltpu.VMEM((d, 2 * dff), BF16), pltpu.VMEM((dff, d), BF16),
                        pltpu.SemaphoreType.DMA((3,)), pltpu.SemaphoreType.DMA((2,))],
    )
    return pl.pallas_call(
        functools.partial(_moe_kernel, nt, _region_rows(nt) // GRANULE),
        grid_spec=gs,
        out_shape=jax.ShapeDtypeStruct(xs.shape, jnp.uint32),
        input_output_aliases={3: 0},
        compiler_params=_cparams(1),
        name="moe_experts",
    )(block_range, n_granules, granule_src, xs, w1, b1.reshape(n_layers, n_exp, 1, 2 * dff),
      w2, b2.reshape(n_layers, n_exp, 1, d))


TABLE_BLOCKS = 8


def _tables_kernel(region_granules, cnt_ref, brange_ref, ngran_ref, gsrc_ref):
    gpb = GRANULES_PER_BLOCK
    log_b = gpb.bit_length() - 1
    n = cnt_ref.shape[0]
    rows = TABLE_BLOCKS * gpb
    nt_dims = (((1,), (1,)), ((), ()))

    cnt = cnt_ref[...].astype(F32)
    gl_len = cnt * (1.0 / GRANULE)
    ri = lax.broadcasted_iota(I32, (n, n), 0)
    ci = lax.broadcasted_iota(I32, (n, n), 1)
    lower = jnp.where(ci < ri, 1.0, 0.0).astype(BF16)
    upper = jnp.where(ri < ci, 1.0, 0.0).astype(BF16)
    pre = jnp.dot(lower, gl_len.astype(BF16), preferred_element_type=F32)
    off = jnp.dot(cnt.astype(BF16), upper, preferred_element_type=F32)
    src = ri.astype(F32) * float(region_granules) + off * (1.0 / GRANULE)
    n_g = jnp.sum(gl_len, axis=0, keepdims=True)
    blocks = ((n_g.astype(I32) + (gpb - 1)) >> log_b).astype(F32)
    bstart = jnp.dot(jnp.broadcast_to(blocks, (SUBLANES, n)).astype(BF16), upper,
                     preferred_element_type=F32)[0:1]
    bend = bstart + blocks

    def digits(v, base):
        hi = jnp.floor(v * (1.0 / base))
        return hi.astype(BF16), (v - hi * base).astype(BF16)

    def pick(onehot, table):
        return lax.dot_general(onehot, table, nt_dims, preferred_element_type=F32)

    def pick2(onehot, v, base):
        hi, lo = digits(v, base)
        return pick(onehot, hi) * base + pick(onehot, lo)

    q = pl.program_id(0) * rows + lax.broadcasted_iota(I32, (rows, n), 0)
    bq = (q >> log_b).astype(F32)
    iq = (q & (gpb - 1)).astype(F32)
    own = jnp.where((bstart <= bq) & (bq < bend), 1.0, 0.0).astype(BF16)
    pre_q = pick2(own, pre, 64.0)
    len_q = pick(own, gl_len.astype(BF16))
    src_q = pick2(own, src, 128.0)
    bstart_q = pick(own, jnp.broadcast_to(bstart, (n, n)).astype(BF16))
    gl = (bq - bstart_q) * float(gpb) + iq
    inside = (pre_q <= gl) & (gl < pre_q + len_q)
    hit = jnp.sum(jnp.where(inside, src_q - pre_q + gl + 1.0, 0.0), axis=-1, keepdims=True)
    gsrc_ref[...] = jnp.broadcast_to(hit - 1.0, (rows, n)).astype(I32)

    bb = (pl.program_id(0) * TABLE_BLOCKS + lax.broadcasted_iota(I32, (TABLE_BLOCKS, n), 0)).astype(F32)
    lane = lax.broadcasted_iota(I32, (TABLE_BLOCKS, n), 1).astype(F32)
    mine = (bstart <= bb) & (bb < bend)
    red = lambda v: jnp.sum(jnp.where(mine, v, 0.0), axis=-1, keepdims=True)
    active = red(jnp.ones_like(lane))
    n_left = red(jnp.broadcast_to(n_g, mine.shape)) - (bb[:, 0:1] - red(jnp.broadcast_to(bstart, mine.shape))) * gpb
    ngran_ref[...] = jnp.broadcast_to(jnp.clip(n_left, 0.0, float(gpb)) * active, (TABLE_BLOCKS, n)).astype(I32)

    sub = lax.broadcasted_iota(I32, (SUBLANES, n), 0)
    brange_ref[...] = jnp.where(sub == 0, bstart, jnp.where(sub == 1, bend, 0.0)).astype(I32)


def _block_tables(cnt):
    nt = cnt.shape[0]
    assert nt <= LANES
    max_blocks = (nt * REGION_DATA_GRANULES) // GRANULES_PER_BLOCK + N_EXPERTS
    n_blocks = -(-(max_blocks + 2) // TABLE_BLOCKS) * TABLE_BLOCKS
    gpb = GRANULES_PER_BLOCK
    cnt_sq = jnp.zeros((LANES, LANES), I32).at[:nt].set(cnt[:, 0, :])
    by_step = lambda i: (i, 0)
    brange, ngran, gsrc = pl.pallas_call(
        functools.partial(_tables_kernel, _region_rows(nt) // GRANULE),
        grid=(n_blocks // TABLE_BLOCKS,),
        in_specs=[_full((LANES, LANES))],
        out_specs=[_full((SUBLANES, LANES)), pl.BlockSpec((TABLE_BLOCKS, LANES), by_step),
                   pl.BlockSpec((TABLE_BLOCKS * gpb, LANES), by_step)],
        out_shape=[jax.ShapeDtypeStruct((SUBLANES, LANES), I32), jax.ShapeDtypeStruct((n_blocks, LANES), I32),
                   jax.ShapeDtypeStruct((n_blocks * gpb, LANES), I32)],
        compiler_params=_cparams(1),
        name="moe_block_tables",
    )(cnt_sq)
    return brange[0:2, :N_EXPERTS].reshape(-1), ngran[:, 0], gsrc[:, 0]


def _combine_kernel(x_ref, ys_ref, pos_ref, gate_ref, mod_ref, o_ref):
    d = x_ref.shape[1]
    o_ref[...] = x_ref[...] + mod_ref[0:1, 5 * d:6 * d] * _moe_combine(ys_ref, pos_ref, gate_ref)


def _combine_glu_kernel(x_ref, ys_ref, pos_ref, gate_ref, modp_ref, modn_ref, g_ref, w_ref, b_ref,
                        o_ref, u_ref, wb_ref):
    d = x_ref.shape[1]

    @pl.when(pl.program_id(0) == 0)
    def _():
        wb_ref[...] = w_ref[...].astype(BF16)

    x2 = x_ref[...] + modp_ref[0:1, 5 * d:6 * d] * _moe_combine(ys_ref, pos_ref, gate_ref)
    o_ref[...] = x2
    h = _rms_mod(x2, g_ref[...], modn_ref[0:1, 0:d], modn_ref[0:1, d:2 * d])
    u = jnp.dot(h.astype(BF16), wb_ref[...], preferred_element_type=F32) + b_ref[...]
    u_ref[...] = u[:, :d] * jax.nn.sigmoid(u[:, d:])


def _combine_tile(t):
    return MOE_TILE * (2 if (t // MOE_TILE) % 2 == 0 else 1)


def _combine_in_specs(t, d, tile):
    tok = lambda i: (i, 0)
    region = _region_rows(t // MOE_TILE) * (tile // MOE_TILE)
    return [pl.BlockSpec((tile, d), tok), pl.BlockSpec((region, d // 2), tok),
            pl.BlockSpec((tile, LANES), tok), pl.BlockSpec((tile, LANES), tok)]


def _combine(x1, ys, pos, gates, mod_l):
    t, d = x1.shape
    tile = _combine_tile(t)
    return pl.pallas_call(
        _combine_kernel,
        grid=(t // tile,),
        in_specs=_combine_in_specs(t, d, tile) + [_full(mod_l.shape)],
        out_specs=pl.BlockSpec((tile, d), lambda i: (i, 0)),
        out_shape=jax.ShapeDtypeStruct((t, d), F32),
        compiler_params=_cparams(1),
        name="moe_combine",
    )(x1, ys, pos, gates, mod_l)


def _combine_glu(x1, ys, pos, gates, mod_prev, mod_next, g1n, w_pw1, b_pw1):
    t, d = x1.shape
    tile = _combine_tile(t)
    tok = lambda i: (i, 0)
    return pl.pallas_call(
        _combine_glu_kernel,
        grid=(t // tile,),
        in_specs=_combine_in_specs(t, d, tile)
                 + [_full(mod_prev.shape), _full(mod_next.shape), _full((1, d)),
                    _full(w_pw1.shape), _full((1, 2 * d))],
        out_specs=[pl.BlockSpec((tile, d), tok), pl.BlockSpec((tile, d), tok)],
        out_shape=[jax.ShapeDtypeStruct((t, d), F32), jax.ShapeDtypeStruct((t, d), F32)],
        scratch_shapes=[pltpu.VMEM(w_pw1.shape, BF16)],
        compiler_params=_cparams(1),
        name="combine_pw1_glu",
    )(x1, ys, pos, gates, mod_prev, mod_next, g1n, w_pw1, b_pw1)


def _conv_kernel(x_ref, up_ref, uc_ref, un_ref, mod_ref, wdw_ref, bdw_ref, lg_ref, lb_ref, w2_ref, b2_ref,
                 g2_ref, wr_ref, br_ref, x1_ref, xs_ref, pos_ref, gate_ref, cnt_ref, wb_ref, ubuf, shbuf, cbuf):
    i = pl.program_id(0)
    n = pl.num_programs(0)
    d = x_ref.shape[1]
    tile = x_ref.shape[0]
    halo = up_ref.shape[0]

    @pl.when(i == 0)
    def _():
        wb_ref[...] = w2_ref[...].astype(BF16)

    ubuf[0:halo, :] = jnp.where(i > 0, up_ref[...], 0.0)
    ubuf[halo:halo + tile, :] = uc_ref[...]
    ubuf[halo + tile:, :] = jnp.where(i < n - 1, un_ref[...], 0.0)

    sub = CONV_CHUNK // SUBLANES
    part_rows = shbuf.shape[1] - 2 * halo
    span = part_rows + 2 * halo - SUBLANES
    for base in range(0, tile, part_rows):
        for r in range(1, SUBLANES):
            shbuf[r - 1, 0:span, :] = ubuf[base + r:base + r + span, :]
        for c0 in range(0, part_rows, CONV_CHUNK):
            parts = [jnp.zeros((SUBLANES, d), F32) + bdw_ref[...] for _ in range(sub)]
            for j in range(CONV_WIDTH):
                a, r = divmod(halo + j - CONV_PAD, SUBLANES)
                wj = wdw_ref[j * SUBLANES:(j + 1) * SUBLANES, :]
                for k in range(sub):
                    lo = c0 + (a + k) * SUBLANES
                    rows = (ubuf[base + lo:base + lo + SUBLANES, :] if r == 0
                            else shbuf[r - 1, lo:lo + SUBLANES, :])
                    parts[k] = parts[k] + rows * wj
            for k in range(sub):
                row0 = base + c0 + k * SUBLANES
                cbuf[row0:row0 + SUBLANES, :] = parts[k]
    acc = cbuf[...]

    mu = jnp.mean(acc, axis=-1, keepdims=True)
    cen = acc - mu
    var = jnp.mean(cen * cen, axis=-1, keepdims=True)
    z = _silu(cen * lax.rsqrt(var + EPS) * lg_ref[...] + lb_ref[...])
    y = jnp.dot(z.astype(BF16), wb_ref[...], preferred_element_type=F32) + b2_ref[...]
    x1 = x_ref[...] + mod_ref[0:1, 2 * d:3 * d] * y
    x1_ref[...] = x1
    _dispatch_tiles(x1, g2_ref[...], mod_ref[0:1, 3 * d:4 * d], mod_ref[0:1, 4 * d:5 * d],
                     wr_ref[...], br_ref[...], xs_ref, pos_ref, gate_ref, cnt_ref)


def _conv_router(x2, u, mod_l, w_dw, b_dw, ln_g, ln_b, w_pw2, b_pw2, g2n, wr, br):
    t, d = x2.shape
    tile = _combine_tile(t)
    nt = t // tile
    hb = tile // CONV_HALO
    nh = t // CONV_HALO
    tok = lambda i: (i, 0)
    prv = lambda i: (jnp.maximum(i * hb - 1, 0), 0)
    nxt = lambda i: (jnp.minimum((i + 1) * hb, nh - 1), 0)
    wpad = jnp.repeat(w_dw, SUBLANES, axis=0)
    out_specs, out_shape = _dispatch_out(t, d, tile)
    return pl.pallas_call(
        _conv_kernel,
        grid=(nt,),
        in_specs=[pl.BlockSpec((tile, d), tok), pl.BlockSpec((CONV_HALO, d), prv),
                  pl.BlockSpec((tile, d), tok), pl.BlockSpec((CONV_HALO, d), nxt),
                  _full(mod_l.shape), _full(wpad.shape), _full((1, d)), _full((1, d)), _full((1, d)),
                  _full(w_pw2.shape), _full((1, d)), _full((1, d)), _full(wr.shape), _full(br.shape)],
        out_specs=out_specs,
        out_shape=out_shape,
        scratch_shapes=[pltpu.VMEM(w_pw2.shape, BF16), pltpu.VMEM((tile + 2 * CONV_HALO, d), F32),
                        pltpu.VMEM((SUBLANES - 1, MOE_TILE + 2 * CONV_HALO, d), F32),
                        pltpu.VMEM((tile, d), F32)],
        compiler_params=_cparams(1),
        name="conv_router",
    )(x2, u, u, u, mod_l, wpad, b_dw, ln_g, ln_b, w_pw2, b_pw2, g2n, wr, br)


def _rope_tables(n_rows):
    quarter = HEAD_DIM // 4
    inv = ROPE_BASE ** (-jnp.arange(quarter, dtype=F32) / quarter)

    def tables(npos):
        ang = jnp.arange(npos, dtype=I32).astype(F32)[:, None] * inv[None, :]
        z = jnp.zeros_like(ang)
        c, s = jnp.cos(ang), jnp.sin(ang)
        half = lambda a, b: jnp.concatenate([a, b], axis=1)
        full = lambda h: jnp.concatenate([h, h, h, h], axis=1)
        return jnp.stack([full(half(c, c)), full(half(-s, z)), full(half(z, s))])

    return tables(n_rows), tables(GRID_W)


def _router_params(w_router, b_router):
    d, e = w_router.shape
    wr = jnp.zeros((LANES, d), F32).at[:e].set(w_router.T)
    br = jnp.zeros((LANES, MOE_TILE), F32).at[:e].set(jnp.broadcast_to(b_router[:, None], (e, MOE_TILE)))
    hi = wr.astype(BF16)
    lo = (wr - hi.astype(F32)).astype(BF16)
    return jnp.stack([hi, lo]), br


def kernel(x, c, ctx, c_ctx, w_mod, b_mod, norm1_g, norm2_g, attn_w_qkv, attn_b_qkv, attn_w_o, attn_b_o,
           attn_q_norm, attn_k_norm, attn_sinks, conv_w_pw1, conv_b_pw1, conv_w_dw, conv_b_dw, conv_ln_g,
           conv_ln_b, conv_w_pw2, conv_b_pw2, moe_w_router, moe_b_router, moe_w1, moe_b1, moe_w2, moe_b2):
    bsz, t, d = x.shape
    assert bsz == 1 and w_mod.shape[0] == 2
    n_ctx = ctx.shape[1]
    tile = min(TOKEN_TILE, t)
    assert t % tile == 0 and t % Q_BLOCK == 0 and t % MOE_TILE == 0 and tile % GRID_W == 0
    xt = x.reshape(t, d)
    row = lambda v: v.reshape(1, -1)

    mod = _mod_vectors(c, c_ctx, w_mod, b_mod)
    mod0, mod1 = mod[0], mod[1]

    bd = (jnp.arange(2 * LANES)[:, None] // HEAD_DIM == jnp.arange(2 * LANES)[None, :] // HEAD_DIM).astype(BF16)
    qg = row(jnp.tile(attn_q_norm[0], N_HEADS))
    kg = row(jnp.tile(attn_k_norm[0], N_KV_HEADS))
    rowt, colt = _rope_tables(t // GRID_W)
    g1n = row(norm1_g[0])
    bq = row(attn_b_qkv[0])
    q, kk, vv = _qkv_project(xt, mod0, 0, True, g1n, attn_w_qkv[0], bq, qg, kg, rowt, colt, bd, tile)
    _, kkc, vvc = _qkv_project(ctx.reshape(n_ctx, d), mod0, 1, False, g1n, attn_w_qkv[0], bq, qg, kg,
                               rowt[:, :max(n_ctx // GRID_W, 1)], colt, bd, n_ctx)
    o = _attention(q, kk, vv, kkc, vvc, attn_sinks[0])
    wr0, br0 = _router_params(moe_w_router[0], moe_b_router[0])
    x1, xs, pos, gates, cnt = _oproj_router(xt, o, mod0, attn_w_o[0], row(attn_b_o[0]), row(norm2_g[0]),
                                            wr0, br0)
    ys = _moe_experts(xs, cnt.shape[0], *_block_tables(cnt), 0, moe_w1, moe_b1, moe_w2, moe_b2)

    x2, u = _combine_glu(x1, ys, pos, gates, mod0, mod1, row(norm1_g[1]), conv_w_pw1[0], row(conv_b_pw1[0]))
    wr1, br1 = _router_params(moe_w_router[1], moe_b_router[1])
    x3, xs, pos, gates, cnt = _conv_router(x2, u, mod1, conv_w_dw[0], row(conv_b_dw[0]), row(conv_ln_g[0]),
                                           row(conv_ln_b[0]), conv_w_pw2[0], row(conv_b_pw2[0]),
                                           row(norm2_g[1]), wr1, br1)
    ys = _moe_experts(xs, cnt.shape[0], *_block_tables(cnt), 1, moe_w1, moe_b1, moe_w2, moe_b2)
    out = _combine(x3, ys, pos, gates, mod1)
    return out.reshape(1, t, d)
```

```python
import functools

import jax
import jax.numpy as jnp
from jax import lax
from jax.experimental import pallas as pl
from jax.experimental.pallas import tpu as pltpu

F32 = jnp.float32
BF16 = jnp.bfloat16
I32 = jnp.int32

N_HEADS = 16
N_KV_HEADS = 2
HEAD_DIM = 64
GRID_W = 64
WINDOW = 128
Q_BLOCK = 128
ROPE_BASE = 10000.0
CONV_WIDTH = 31
CONV_PAD = CONV_WIDTH // 2
N_EXPERTS = 32
TOP_K = 4
SWIGLU_LIMIT = 7.0
SWIGLU_ALPHA = 1.702
EPS = 1e-6

LANES = 128
SUBLANES = 8
TOKEN_TILE = 512
MOE_TILE = 256
GRANULE = SUBLANES
EXPERT_ROWS = 512
GRANULES_PER_BLOCK = EXPERT_ROWS // GRANULE
REGION_DATA_GRANULES = (MOE_TILE * TOP_K + N_EXPERTS * (GRANULE - 1)) // GRANULE


def _spare_per_region(nt):
    return -(-GRANULES_PER_BLOCK // nt)


def _region_rows(nt):
    rows = (REGION_DATA_GRANULES + 2 * _spare_per_region(nt) + 1) * GRANULE
    return -(-rows // 256) * 256
CONV_HALO = 16
CONV_CHUNK = 32
NEG_BIG = -1e30
LOG2_E = 1.4426950408889634
VMEM_LIMIT = 56 * 1024 * 1024


def _cparams(n_axes=1, vmem=VMEM_LIMIT):
    return pltpu.CompilerParams(dimension_semantics=("arbitrary",) * n_axes, vmem_limit_bytes=vmem)


def _full(shape):
    nd = len(shape)
    return pl.BlockSpec(shape, lambda *_: (0,) * nd)


def _silu(v):
    return v * jax.nn.sigmoid(v)


def _rms_mod(xv, g, shift, scale):
    ms = jnp.mean(xv * xv, axis=-1, keepdims=True)
    return (xv * lax.rsqrt(ms + EPS)) * g * (1.0 + scale) + shift


def _split_dot(a, b_bf16):
    hi = a.astype(BF16)
    lo = (a - hi.astype(F32)).astype(BF16)
    return (jnp.dot(hi, b_bf16, preferred_element_type=F32)
            + jnp.dot(lo, b_bf16, preferred_element_type=F32))


def _router_dispatch(x_new, g2n, sh2, sc2, wr, br, xs_ref, pos_ref, gate_ref, cnt_ref):
    tt = x_new.shape[0]
    rt = xs_ref.shape[0]
    h2 = _rms_mod(x_new, g2n, sh2, sc2)
    h_hi = h2.astype(BF16)
    h_lo = (h2 - h_hi.astype(F32)).astype(BF16)
    nt_dims = (((1,), (1,)), ((), ()))
    wdot = lambda w, h: lax.dot_general(w, h, nt_dims, preferred_element_type=F32)
    logits = (wdot(wr[0], h_hi) + (wdot(wr[0], h_lo) + wdot(wr[1], h_hi)) + br)[0:N_EXPERTS]
    erow = lax.broadcasted_iota(I32, (N_EXPERTS, tt), 0)
    hits, vals = [], []
    cur = logits
    for k in range(TOP_K):
        m = jnp.max(cur, axis=0, keepdims=True)
        sel = jnp.min(jnp.where(cur == m, erow, N_EXPERTS), axis=0, keepdims=True)
        hit = erow == sel
        hits.append(hit)
        vals.append(m)
        cur = jnp.where(hit, NEG_BIG * 2.0, cur)
    evs = [jnp.exp(v - vals[0]) for v in vals]
    den = (evs[0] + evs[1]) + (evs[2] + evs[3])
    gates = [ev / den for ev in evs]

    onehot = jnp.where((hits[0] | hits[1]) | (hits[2] | hits[3]), 1.0, 0.0).astype(BF16)
    ri = lax.broadcasted_iota(I32, (tt, tt), 0)
    ci = lax.broadcasted_iota(I32, (tt, tt), 1)
    earlier = jnp.where(ri < ci, 1.0, 0.0).astype(BF16)
    rank = jnp.dot(onehot, earlier, preferred_element_type=F32)
    cnt = jnp.dot(onehot, jnp.ones((tt, LANES), BF16), preferred_element_type=F32)
    c8 = jnp.bitwise_and(cnt.astype(I32) + (GRANULE - 1), -GRANULE).astype(F32)
    ue = lax.broadcasted_iota(I32, (N_EXPERTS, N_EXPERTS), 0)
    ve = lax.broadcasted_iota(I32, (N_EXPERTS, N_EXPERTS), 1)
    below = jnp.where(ve < ue, 1.0, 0.0).astype(BF16)
    off = jnp.dot(below, c8.astype(BF16), preferred_element_type=F32)
    slot_of = jnp.concatenate([off] * (tt // LANES), axis=1) + rank

    onehot_pad = jnp.concatenate([onehot, jnp.zeros((LANES - N_EXPERTS, tt), BF16)], axis=0)
    cnt_row = lax.dot_general(jnp.ones((2 * SUBLANES, tt), BF16), onehot_pad, nt_dims,
                              preferred_element_type=F32)[0:SUBLANES]
    cnt_ref[0] = jnp.bitwise_and(cnt_row.astype(I32) + (GRANULE - 1), -GRANULE)

    r_iota = lax.broadcasted_iota(I32, (rt, tt), 0)
    krow = lax.broadcasted_iota(I32, (LANES, tt), 0)
    perm = jnp.zeros((rt, tt), F32)
    pos_t = jnp.zeros((LANES, tt), F32)
    gate_t = jnp.zeros((LANES, tt), F32)
    for k in range(TOP_K):
        pk = jnp.sum(jnp.where(hits[k], slot_of, 0.0), axis=0, keepdims=True)
        perm = jnp.where(r_iota == pk.astype(I32), 1.0, perm)
        pos_t = jnp.where(krow == k, pk, pos_t)
        gate_t = jnp.where(krow == k, gates[k], gate_t)
    pos_ref[...] = pos_t.T.astype(I32)
    gate_ref[...] = gate_t.T
    xs_ref[...] = _pack_halves(jnp.dot(perm.astype(BF16), h_hi, preferred_element_type=F32))


def _pack_halves(v):
    half = v.shape[1] // 2
    lo = lax.bitcast_convert_type(v[:, :half], jnp.uint32) >> 16
    hi = lax.bitcast_convert_type(v[:, half:], jnp.uint32) & jnp.uint32(0xFFFF0000)
    return lo | hi


def _unpack_halves(w):
    return (lax.bitcast_convert_type(w << 16, F32),
            lax.bitcast_convert_type(w & jnp.uint32(0xFFFF0000), F32))


def _moe_combine(ys_ref, pos_ref, gate_ref):
    tt = MOE_TILE
    n_tiles = pos_ref.shape[0] // tt
    rt = ys_ref.shape[0] // n_tiles
    r_iota = lax.broadcasted_iota(I32, (tt, rt), 1)
    outs = []
    for j in range(n_tiles):
        pos = pos_ref[j * tt:(j + 1) * tt, :]
        gate = gate_ref[j * tt:(j + 1) * tt, :]
        g = jnp.zeros((tt, rt), F32)
        for k in range(TOP_K):
            g = jnp.where(r_iota == pos[:, k:k + 1], gate[:, k:k + 1], g)
        g = g.astype(BF16)
        ya, yb = _unpack_halves(ys_ref[j * rt:(j + 1) * rt, :])
        outs.append(jnp.concatenate([jnp.dot(g, ya.astype(BF16), preferred_element_type=F32),
                                     jnp.dot(g, yb.astype(BF16), preferred_element_type=F32)], axis=1))
    return outs[0] if n_tiles == 1 else jnp.concatenate(outs, axis=0)


def _mod_kernel(cc_ref, w_ref, b_ref, o_ref):
    a = _silu(cc_ref[...])
    o_ref[0] = jnp.dot(a, w_ref[0], precision=lax.Precision.HIGHEST,
                       preferred_element_type=F32) + b_ref[0]


def _mod_vectors(c, c_ctx, w_mod, b_mod):
    depth, d, d6 = w_mod.shape
    cc = jnp.zeros((SUBLANES, d), F32).at[0].set(c[0]).at[1].set(c_ctx)
    ncol = 4
    cw = d6 // ncol
    return pl.pallas_call(
        _mod_kernel,
        grid=(depth, ncol),
        in_specs=[pl.BlockSpec((SUBLANES, d), lambda l, j: (0, 0)),
                  pl.BlockSpec((1, d, cw), lambda l, j: (l, 0, j)),
                  pl.BlockSpec((1, 1, cw), lambda l, j: (l, 0, j))],
        out_specs=pl.BlockSpec((1, SUBLANES, cw), lambda l, j: (l, 0, j)),
        out_shape=jax.ShapeDtypeStruct((depth, SUBLANES, d6), F32),
        compiler_params=_cparams(2),
        name="mod_vectors",
    )(cc, w_mod, b_mod.reshape(depth, 1, d6))


def _qkv_kernel(mod_row, use_rope, x_ref, mod_ref, g_ref, w_ref, b_ref, qg_ref, kg_ref, rowt_ref, colt_ref,
                bd_ref, q_ref, k_ref, v_ref, wb_ref):
    d = x_ref.shape[1]
    tile = x_ref.shape[0]
    nq = N_HEADS * HEAD_DIM
    nk = N_KV_HEADS * HEAD_DIM

    @pl.when(pl.program_id(0) == 0)
    def _():
        wb_ref[...] = w_ref[...].astype(BF16)

    sh = mod_ref[mod_row:mod_row + 1, 0:d]
    sc = mod_ref[mod_row:mod_row + 1, d:2 * d]
    h = _rms_mod(x_ref[...], g_ref[...], sh, sc)
    qkv = jnp.dot(h.astype(BF16), wb_ref[...], preferred_element_type=F32) + b_ref[...]
    q = qkv[:, :nq]
    k = qkv[:, nq:nq + nk]
    v = qkv[:, nq + nk:]
    bd = bd_ref[...]

    def head_norm(t, gain):
        tt = t * t
        w = bd.shape[0]
        if t.shape[1] >= w:
            ss = jnp.concatenate([_split_dot(tt[:, c:c + w], bd) for c in range(0, t.shape[1], w)], axis=1)
        else:
            ss = _split_dot(tt, bd[:t.shape[1], :t.shape[1]])
        return t * lax.rsqrt(ss * (1.0 / HEAD_DIM) + EPS) * gain

    qn = head_norm(q, qg_ref[...])
    kn = head_norm(k, kg_ref[...])

    if use_rope:
        lane = lax.broadcasted_iota(I32, (tile, LANES), 1)
        is_row = (lane % HEAD_DIM) < HEAD_DIM // 2
        nrow = tile // GRID_W

        def table(idx):
            rt = jnp.concatenate([jnp.broadcast_to(rowt_ref[idx, r:r + 1, :], (GRID_W, LANES))
                                  for r in range(nrow)], axis=0)
            ct = jnp.concatenate([colt_ref[idx]] * nrow, axis=0)
            return jnp.where(is_row, rt, ct)

        cosv, sav, sbv = table(0), table(1), table(2)

        def rope(t):
            n = t.shape[1]
            reps = n // LANES
            ct = jnp.concatenate([cosv] * reps, axis=1) if reps > 1 else cosv
            at = jnp.concatenate([sav] * reps, axis=1) if reps > 1 else sav
            bt = jnp.concatenate([sbv] * reps, axis=1) if reps > 1 else sbv
            up = pltpu.roll(t, n - HEAD_DIM // 4, 1)
            dn = pltpu.roll(t, HEAD_DIM // 4, 1)
            return t * ct + up * at + dn * bt

        qn = rope(qn)
        kn = rope(kn)

    q_ref[...] = (qn * (HEAD_DIM ** -0.5 * LOG2_E)).astype(BF16)
    klane = lax.broadcasted_iota(I32, kn.shape, 1)
    low = klane < HEAD_DIM

    def variants(t):
        sw = pltpu.roll(t, HEAD_DIM, 1)
        z = jnp.zeros_like(t)
        return jnp.concatenate([jnp.where(low, t, z), jnp.where(low, z, sw),
                                jnp.where(low, sw, z), jnp.where(low, z, t)], axis=1)

    k_ref[...] = variants(kn).astype(BF16)
    v_ref[...] = variants(v).astype(BF16)


def _qkv_project(xt, mod_l, mod_row, use_rope, g1n, w_qkv, b_qkv, qg, kg, rowt, colt, bd, tile):
    t, d = xt.shape
    nqkv = w_qkv.shape[1]
    nq = N_HEADS * HEAD_DIM
    tok = lambda i: (i, 0)
    nrow = max(tile // GRID_W, 1)
    return pl.pallas_call(
        functools.partial(_qkv_kernel, mod_row, use_rope),
        grid=(t // tile,),
        in_specs=[pl.BlockSpec((tile, d), tok), _full(mod_l.shape), _full((1, d)),
                  _full((d, nqkv)), _full((1, nqkv)), _full((1, nq)), _full((1, LANES)),
                  pl.BlockSpec((3, nrow, LANES), lambda i: (0, i, 0)), _full(colt.shape), _full(bd.shape)],
        out_specs=[pl.BlockSpec((tile, nq), tok), pl.BlockSpec((tile, 4 * LANES), tok),
                   pl.BlockSpec((tile, 4 * LANES), tok)],
        out_shape=[jax.ShapeDtypeStruct((t, nq), BF16), jax.ShapeDtypeStruct((t, 4 * LANES), BF16),
                   jax.ShapeDtypeStruct((t, 4 * LANES), BF16)],
        scratch_shapes=[pltpu.VMEM((d, nqkv), BF16)],
        compiler_params=_cparams(1),
        name="qkv_project",
    )(xt, mod_l, g1n, w_qkv, b_qkv, qg, kg, rowt, colt, bd)


def _attn_kernel(sink_ref, q_ref, kp_ref, kc_ref, kn_ref, vp_ref, vc_ref, vn_ref, kx_ref, vx_ref, bias_ref,
                 o_ref):
    i = pl.program_id(0)
    n = pl.num_programs(0)
    qb = Q_BLOCK
    n_sub = q_ref.shape[0] // qb
    pairs_per_group = (N_HEADS // N_KV_HEADS) // 2
    rows = pairs_per_group * qb

    kall = jnp.concatenate([kp_ref[...], kc_ref[...], kn_ref[...]], axis=0)
    vall = jnp.concatenate([vp_ref[...], vc_ref[...], vn_ref[...]], axis=0)
    kctx = kx_ref[...]
    vctx = vx_ref[...]
    dn_t = (((1,), (1,)), ((), ()))
    rsub = lax.broadcasted_iota(I32, (rows, 1), 0) // qb

    for sb in range(n_sub):
        kband = kall[sb * qb:(sb + 3) * qb]
        vband = vall[sb * qb:(sb + 3) * qb]
        first = jnp.where(i == 0, 1, 0) if sb == 0 else 0
        last = jnp.where(i == n - 1, 2, 0) if sb == n_sub - 1 else 0
        bias = bias_ref[first + last]
        q_rows = slice(sb * qb, (sb + 1) * qb)
        for g in range(N_KV_HEADS):
            qg = jnp.concatenate(
                [q_ref[q_rows, (g * pairs_per_group + j) * LANES:(g * pairs_per_group + j + 1) * LANES]
                 for j in range(pairs_per_group)], axis=0)
            acc = jnp.zeros((rows, LANES), F32)
            for par in range(2):
                col = (2 * g + par) * LANES
                s_b = lax.dot_general(qg, kband[:, col:col + LANES], dn_t, preferred_element_type=F32)
                s_c = lax.dot_general(qg, kctx[:, col:col + LANES], dn_t, preferred_element_type=F32)
                s_b = s_b + bias
                sink = jnp.zeros((rows, 1), F32)
                for j in range(pairs_per_group):
                    hd = 2 * (g * pairs_per_group + j) + par
                    sink = jnp.where(rsub == j, sink_ref[hd] * LOG2_E, sink)
                m = jnp.maximum(jnp.maximum(jnp.max(s_b, axis=-1, keepdims=True),
                                            jnp.max(s_c, axis=-1, keepdims=True)), sink)
                p_b = jnp.exp2(s_b - m)
                p_c = jnp.exp2(s_c - m)
                l = (jnp.sum(p_b, axis=-1, keepdims=True) + jnp.sum(p_c, axis=-1, keepdims=True)
                     + jnp.exp2(sink - m))
                o = (jnp.dot(p_b.astype(BF16), vband[:, col:col + LANES], preferred_element_type=F32)
                     + jnp.dot(p_c.astype(BF16), vctx[:, col:col + LANES], preferred_element_type=F32))
                acc = acc + o / l
            for j in range(pairs_per_group):
                pcol = (g * pairs_per_group + j) * LANES
                o_ref[q_rows, pcol:pcol + LANES] = acc[j * qb:(j + 1) * qb].astype(BF16)


def _attention(q, kk, vv, kkc, vvc, sinks):
    t, nq = q.shape
    qb = Q_BLOCK
    nb = t // qb
    c = kkc.shape[0]
    w = kk.shape[1]
    per = 2 if nb % 2 == 0 else 1
    cur = lambda i, s: (i, 0)
    prv = lambda i, s: (jnp.maximum(i * per - 1, 0), 0)
    nxt = lambda i, s: (jnp.minimum((i + 1) * per, nb - 1), 0)
    zero = lambda i, s: (0, 0)
    rows = (N_HEADS // N_KV_HEADS) // 2 * qb
    r = (jnp.arange(rows, dtype=I32) % qb)[:, None]
    cpos = jnp.arange(3 * qb, dtype=I32)[None, :] - qb
    inside = jnp.abs(r - cpos) <= WINDOW
    variants = [inside & ((cpos >= 0) | ((v & 1) == 0)) & ((cpos < qb) | ((v & 2) == 0)) for v in range(4)]
    bias = jnp.where(jnp.stack(variants), 0.0, NEG_BIG).astype(F32)
    gs = pltpu.PrefetchScalarGridSpec(
        num_scalar_prefetch=1,
        grid=(nb // per,),
        in_specs=[pl.BlockSpec((per * qb, nq), cur),
                  pl.BlockSpec((qb, w), prv), pl.BlockSpec((per * qb, w), cur), pl.BlockSpec((qb, w), nxt),
                  pl.BlockSpec((qb, w), prv), pl.BlockSpec((per * qb, w), cur), pl.BlockSpec((qb, w), nxt),
                  pl.BlockSpec((c, w), zero), pl.BlockSpec((c, w), zero),
                  pl.BlockSpec(bias.shape, lambda i, s: (0, 0, 0))],
        out_specs=pl.BlockSpec((per * qb, nq), cur),
    )
    return pl.pallas_call(
        _attn_kernel,
        grid_spec=gs,
        out_shape=jax.ShapeDtypeStruct((t, nq), BF16),
        compiler_params=_cparams(1),
        name="window_attention",
    )(sinks, q, kk, kk, kk, vv, vv, vv, kkc, vvc, bias)


def _dispatch_tiles(x_new, g2n, sh2, sc2, wr, br, xs_ref, pos_ref, gate_ref, cnt_ref):
    n = x_new.shape[0] // MOE_TILE
    rt = xs_ref.shape[0] // n
    for j in range(n):
        rows = pl.ds(j * MOE_TILE, MOE_TILE)
        _router_dispatch(x_new[j * MOE_TILE:(j + 1) * MOE_TILE], g2n, sh2, sc2, wr, br,
                         xs_ref.at[pl.ds(j * rt, rt)], pos_ref.at[rows], gate_ref.at[rows],
                         cnt_ref.at[pl.ds(j, 1)])


def _dispatch_out(t, d, tile):
    nt = t // MOE_TILE
    per_step = tile // MOE_TILE
    tok = lambda i: (i, 0)
    region = _region_rows(nt)
    specs = [pl.BlockSpec((tile, d), tok), pl.BlockSpec((per_step * region, d // 2), tok),
             pl.BlockSpec((tile, LANES), tok), pl.BlockSpec((tile, LANES), tok),
             pl.BlockSpec((per_step, SUBLANES, LANES), lambda i: (i, 0, 0))]
    shapes = [jax.ShapeDtypeStruct((t, d), F32), jax.ShapeDtypeStruct((nt * region, d // 2), jnp.uint32),
              jax.ShapeDtypeStruct((t, LANES), I32), jax.ShapeDtypeStruct((t, LANES), F32),
              jax.ShapeDtypeStruct((nt, SUBLANES, LANES), I32)]
    return specs, shapes


def _oproj_kernel(x_ref, o_ref, mod_ref, wo_ref, bo_ref, g2_ref, wr_ref, br_ref,
                  x1_ref, xs_ref, pos_ref, gate_ref, cnt_ref, wb_ref):
    d = x_ref.shape[1]

    @pl.when(pl.program_id(0) == 0)
    def _():
        wb_ref[...] = wo_ref[...].astype(BF16)

    y = jnp.dot(o_ref[...], wb_ref[...], preferred_element_type=F32) + bo_ref[...]
    x1 = x_ref[...] + mod_ref[0:1, 2 * d:3 * d] * y
    x1_ref[...] = x1
    _dispatch_tiles(x1, g2_ref[...], mod_ref[0:1, 3 * d:4 * d], mod_ref[0:1, 4 * d:5 * d],
                     wr_ref[...], br_ref[...], xs_ref, pos_ref, gate_ref, cnt_ref)


def _oproj_router(xt, o, mod_l, w_o, b_o, g2n, wr, br):
    t, d = xt.shape
    tile = _combine_tile(t)
    tok = lambda i: (i, 0)
    out_specs, out_shape = _dispatch_out(t, d, tile)
    return pl.pallas_call(
        _oproj_kernel,
        grid=(t // tile,),
        in_specs=[pl.BlockSpec((tile, d), tok), pl.BlockSpec((tile, o.shape[1]), tok), _full(mod_l.shape),
                  _full(w_o.shape), _full((1, d)), _full((1, d)), _full(wr.shape), _full(br.shape)],
        out_specs=out_specs,
        out_shape=out_shape,
        scratch_shapes=[pltpu.VMEM(w_o.shape, BF16)],
        compiler_params=_cparams(1),
        name="oproj_router",
    )(xt, o, mod_l, w_o, b_o, g2n, wr, br)


def _moe_kernel(nt, region_granules, brange_ref, nvg_ref, gsrc_ref, xs_hbm, w1_ref, b1_ref, w2_ref, b2_ref,
                ys_hbm, xbuf, obuf, w1b, w2b, gsem, ssem):
    e = pl.program_id(0)
    ne = pl.num_programs(0)
    b_lo = brange_ref[e]
    b_hi = brange_ref[ne + e]
    total = brange_ref[2 * ne - 1]
    gpb = GRANULES_PER_BLOCK
    dff = w2_ref.shape[1]
    spare = _spare_per_region(nt)
    idle_granule = REGION_DATA_GRANULES + 2 * spare

    def rows_of(granule):
        return pl.ds(pl.multiple_of(granule * GRANULE, GRANULE), GRANULE)

    def gather_copy(blk, gi):
        g = gsrc_ref[blk * gpb + gi]
        src = jnp.where(g >= 0, g, idle_granule)
        s = blk % 3
        return pltpu.make_async_copy(xs_hbm.at[rows_of(src)], xbuf.at[s, pl.ds(gi * GRANULE, GRANULE)],
                                     gsem.at[s])

    def scatter_copy(blk, s, gi):
        g = gsrc_ref[blk * gpb + gi]
        own_spare = (gi % nt) * region_granules + REGION_DATA_GRANULES + gi // nt + s * spare
        dst = jnp.where(g >= 0, g, own_spare)
        return pltpu.make_async_copy(obuf.at[s, pl.ds(gi * GRANULE, GRANULE)], ys_hbm.at[rows_of(dst)],
                                     ssem.at[s])

    def start_gather(blk):
        for gi in range(gpb):
            gather_copy(blk, gi).start()

    def wait_gather(s):
        pltpu.make_async_copy(xs_hbm.at[pl.ds(0, EXPERT_ROWS)], xbuf.at[s], gsem.at[s]).wait()

    def wait_scatter(s):
        pltpu.make_async_copy(obuf.at[s], ys_hbm.at[pl.ds(0, EXPERT_ROWS)], ssem.at[s]).wait()

    @pl.when(e == 0)
    def _():
        obuf[...] = jnp.zeros(obuf.shape, jnp.uint32)
        start_gather(0)
        start_gather(1)

    def expert_block(b, rows):
        slot = b % 2
        start_gather(b + 2)
        xb = jnp.concatenate(_unpack_halves(xbuf[b % 3, 0:rows, :]), axis=1).astype(BF16)
        gu = jnp.dot(xb, w1b[...], preferred_element_type=F32) + b1_ref[0]
        gt = jnp.minimum(gu[:, :dff], SWIGLU_LIMIT)
        up = jnp.clip(gu[:, dff:], -SWIGLU_LIMIT, SWIGLU_LIMIT)
        act = gt * jax.nn.sigmoid(SWIGLU_ALPHA * gt) * (up + 1.0)
        out = jnp.dot(act.astype(BF16), w2b[...], preferred_element_type=F32) + b2_ref[0]
        obuf[slot, 0:rows, :] = _pack_halves(out.astype(BF16).astype(F32))
        for gi in range(gpb):
            scatter_copy(b, slot, gi).start()

    def block_body(b, carry):
        wait_gather(b % 3)

        @pl.when(b >= 2)
        def _():
            wait_scatter(b % 2)

        half = EXPERT_ROWS // 2
        rows_needed = nvg_ref[b] * GRANULE

        @pl.when(rows_needed > half)
        def _():
            expert_block(b, EXPERT_ROWS)

        @pl.when(rows_needed <= half)
        def _():
            expert_block(b, half)

        return carry

    @pl.when(b_hi > b_lo)
    def _():
        w1b[...] = w1_ref[0].astype(BF16)
        w2b[...] = w2_ref[0].astype(BF16)
        lax.fori_loop(b_lo, b_hi, block_body, 0)

    @pl.when(e == ne - 1)
    def _():
        wait_gather(total % 3)
        wait_gather((total + 1) % 3)

        @pl.when(total >= 2)
        def _():
            wait_scatter(total % 2)

        @pl.when(total >= 1)
        def _():
            wait_scatter((total + 1) % 2)


def _moe_experts(xs, nt, block_range, n_granules, granule_src, layer, w1, b1, w2, b2):
    dw = xs.shape[1]
    d = 2 * dw
    dff = w2.shape[2]
    n_layers, n_exp = w1.shape[:2]
    by_expert = lambda e, br, ng, gs: (layer, e, 0, 0)
    gs = pltpu.PrefetchScalarGridSpec(
        num_scalar_prefetch=3,
        grid=(n_exp,),
        in_specs=[pl.BlockSpec(memory_space=pl.ANY),
                  pl.BlockSpec((None, 1, d, 2 * dff), by_expert), pl.BlockSpec((None, 1, 1, 2 * dff), by_expert),
                  pl.BlockSpec((None, 1, dff, d), by_expert), pl.BlockSpec((None, 1, 1, d), by_expert)],
        out_specs=pl.BlockSpec(memory_space=pl.ANY),
        scratch_shapes=[pltpu.VMEM((3, EXPERT_ROWS, dw), jnp.uint32), pltpu.VMEM((2, EXPERT_ROWS, dw), jnp.uint32),
                        pltpu.VMEM((d, 2 * dff), BF16), pltpu.VMEM((dff, d), BF16),
                        pltpu.SemaphoreType.DMA((3,)), pltpu.SemaphoreType.DMA((2,))],
    )
    return pl.pallas_call(
        functools.partial(_moe_kernel, nt, _region_rows(nt) // GRANULE),
        grid_spec=gs,
        out_shape=jax.ShapeDtypeStruct(xs.shape, jnp.uint32),
        input_output_aliases={3: 0},
        compiler_params=_cparams(1),
        name="moe_experts",
    )(block_range, n_granules, granule_src, xs, w1, b1.reshape(n_layers, n_exp, 1, 2 * dff),
      w2, b2.reshape(n_layers, n_exp, 1, d))


TABLE_BLOCKS = 8


def _tables_kernel(region_granules, cnt_ref, brange_ref, ngran_ref, gsrc_ref):
    gpb = GRANULES_PER_BLOCK
    log_b = gpb.bit_length() - 1
    n = cnt_ref.shape[0]
    rows = TABLE_BLOCKS * gpb
    nt_dims = (((1,), (1,)), ((), ()))

    cnt = cnt_ref[...].astype(F32)
    gl_len = cnt * (1.0 / GRANULE)
    ri = lax.broadcasted_iota(I32, (n, n), 0)
    ci = lax.broadcasted_iota(I32, (n, n), 1)
    lower = jnp.where(ci < ri, 1.0, 0.0).astype(BF16)
    upper = jnp.where(ri < ci, 1.0, 0.0).astype(BF16)
    pre = jnp.dot(lower, gl_len.astype(BF16), preferred_element_type=F32)
    off = jnp.dot(cnt.astype(BF16), upper, preferred_element_type=F32)
    src = ri.astype(F32) * float(region_granules) + off * (1.0 / GRANULE)
    n_g = jnp.sum(gl_len, axis=0, keepdims=True)
    blocks = ((n_g.astype(I32) + (gpb - 1)) >> log_b).astype(F32)
    bstart = jnp.dot(jnp.broadcast_to(blocks, (SUBLANES, n)).astype(BF16), upper,
                     preferred_element_type=F32)[0:1]
    bend = bstart + blocks

    def digits(v, base):
        hi = jnp.floor(v * (1.0 / base))
        return hi.astype(BF16), (v - hi * base).astype(BF16)

    def pick(onehot, table):
        return lax.dot_general(onehot, table, nt_dims, preferred_element_type=F32)

    def pick2(onehot, v, base):
        hi, lo = digits(v, base)
        return pick(onehot, hi) * base + pick(onehot, lo)

    q = pl.program_id(0) * rows + lax.broadcasted_iota(I32, (rows, n), 0)
    bq = (q >> log_b).astype(F32)
    iq = (q & (gpb - 1)).astype(F32)
    own = jnp.where((bstart <= bq) & (bq < bend), 1.0, 0.0).astype(BF16)
    pre_q = pick2(own, pre, 64.0)
    len_q = pick(own, gl_len.astype(BF16))
    src_q = pick2(own, src, 128.0)
    bstart_q = pick(own, jnp.broadcast_to(bstart, (n, n)).astype(BF16))
    gl = (bq - bstart_q) * float(gpb) + iq
    inside = (pre_q <= gl) & (gl < pre_q + len_q)
    hit = jnp.sum(jnp.where(inside, src_q - pre_q + gl + 1.0, 0.0), axis=-1, keepdims=True)
    gsrc_ref[...] = jnp.broadcast_to(hit - 1.0, (rows, n)).astype(I32)

    bb = (pl.program_id(0) * TABLE_BLOCKS + lax.broadcasted_iota(I32, (TABLE_BLOCKS, n), 0)).astype(F32)
    lane = lax.broadcasted_iota(I32, (TABLE_BLOCKS, n), 1).astype(F32)
    mine = (bstart <= bb) & (bb < bend)
    red = lambda v: jnp.sum(jnp.where(mine, v, 0.0), axis=-1, keepdims=True)
    active = red(jnp.ones_like(lane))
    n_left = red(jnp.broadcast_to(n_g, mine.shape)) - (bb[:, 0:1] - red(jnp.broadcast_to(bstart, mine.shape))) * gpb
    ngran_ref[...] = jnp.broadcast_to(jnp.clip(n_left, 0.0, float(gpb)) * active, (TABLE_BLOCKS, n)).astype(I32)

    sub = lax.broadcasted_iota(I32, (SUBLANES, n), 0)
    brange_ref[...] = jnp.where(sub == 0, bstart, jnp.where(sub == 1, bend, 0.0)).astype(I32)


def _block_tables(cnt):
    nt = cnt.shape[0]
    assert nt <= LANES
    max_blocks = (nt * REGION_DATA_GRANULES) // GRANULES_PER_BLOCK + N_EXPERTS
    n_blocks = -(-(max_blocks + 2) // TABLE_BLOCKS) * TABLE_BLOCKS
    gpb = GRANULES_PER_BLOCK
    cnt_sq = jnp.zeros((LANES, LANES), I32).at[:nt].set(cnt[:, 0, :])
    by_step = lambda i: (i, 0)
    brange, ngran, gsrc = pl.pallas_call(
        functools.partial(_tables_kernel, _region_rows(nt) // GRANULE),
        grid=(n_blocks // TABLE_BLOCKS,),
        in_specs=[_full((LANES, LANES))],
        out_specs=[_full((SUBLANES, LANES)), pl.BlockSpec((TABLE_BLOCKS, LANES), by_step),
                   pl.BlockSpec((TABLE_BLOCKS * gpb, LANES), by_step)],
        out_shape=[jax.ShapeDtypeStruct((SUBLANES, LANES), I32), jax.ShapeDtypeStruct((n_blocks, LANES), I32),
                   jax.ShapeDtypeStruct((n_blocks * gpb, LANES), I32)],
        compiler_params=_cparams(1),
        name="moe_block_tables",
    )(cnt_sq)
    return brange[0:2, :N_EXPERTS].reshape(-1), ngran[:, 0], gsrc[:, 0]


def _combine_kernel(x_ref, ys_ref, pos_ref, gate_ref, mod_ref, o_ref):
    d = x_ref.shape[1]
    o_ref[...] = x_ref[...] + mod_ref[0:1, 5 * d:6 * d] * _moe_combine(ys_ref, pos_ref, gate_ref)


def _combine_glu_kernel(x_ref, ys_ref, pos_ref, gate_ref, modp_ref, modn_ref, g_ref, w_ref, b_ref,
                        o_ref, u_ref, wb_ref):
    d = x_ref.shape[1]

    @pl.when(pl.program_id(0) == 0)
    def _():
        wb_ref[...] = w_ref[...].astype(BF16)

    x2 = x_ref[...] + modp_ref[0:1, 5 * d:6 * d] * _moe_combine(ys_ref, pos_ref, gate_ref)
    o_ref[...] = x2
    h = _rms_mod(x2, g_ref[...], modn_ref[0:1, 0:d], modn_ref[0:1, d:2 * d])
    u = jnp.dot(h.astype(BF16), wb_ref[...], preferred_element_type=F32) + b_ref[...]
    u_ref[...] = u[:, :d] * jax.nn.sigmoid(u[:, d:])


def _combine_tile(t):
    return MOE_TILE * (2 if (t // MOE_TILE) % 2 == 0 else 1)


def _combine_in_specs(t, d, tile):
    tok = lambda i: (i, 0)
    region = _region_rows(t // MOE_TILE) * (tile // MOE_TILE)
    return [pl.BlockSpec((tile, d), tok), pl.BlockSpec((region, d // 2), tok),
            pl.BlockSpec((tile, LANES), tok), pl.BlockSpec((tile, LANES), tok)]


def _combine(x1, ys, pos, gates, mod_l):
    t, d = x1.shape
    tile = _combine_tile(t)
    return pl.pallas_call(
        _combine_kernel,
        grid=(t // tile,),
        in_specs=_combine_in_specs(t, d, tile) + [_full(mod_l.shape)],
        out_specs=pl.BlockSpec((tile, d), lambda i: (i, 0)),
        out_shape=jax.ShapeDtypeStruct((t, d), F32),
        compiler_params=_cparams(1),
        name="moe_combine",
    )(x1, ys, pos, gates, mod_l)


def _combine_glu(x1, ys, pos, gates, mod_prev, mod_next, g1n, w_pw1, b_pw1):
    t, d = x1.shape
    tile = _combine_tile(t)
    tok = lambda i: (i, 0)
    return pl.pallas_call(
        _combine_glu_kernel,
        grid=(t // tile,),
        in_specs=_combine_in_specs(t, d, tile)
                 + [_full(mod_prev.shape), _full(mod_next.shape), _full((1, d)),
                    _full(w_pw1.shape), _full((1, 2 * d))],
        out_specs=[pl.BlockSpec((tile, d), tok), pl.BlockSpec((tile, d), tok)],
        out_shape=[jax.ShapeDtypeStruct((t, d), F32), jax.ShapeDtypeStruct((t, d), F32)],
        scratch_shapes=[pltpu.VMEM(w_pw1.shape, BF16)],
        compiler_params=_cparams(1),
        name="combine_pw1_glu",
    )(x1, ys, pos, gates, mod_prev, mod_next, g1n, w_pw1, b_pw1)


def _conv_kernel(x_ref, up_ref, uc_ref, un_ref, mod_ref, wdw_ref, bdw_ref, lg_ref, lb_ref, w2_ref, b2_ref,
                 g2_ref, wr_ref, br_ref, x1_ref, xs_ref, pos_ref, gate_ref, cnt_ref, wb_ref, ubuf, shbuf, cbuf):
    i = pl.program_id(0)
    n = pl.num_programs(0)
    d = x_ref.shape[1]
    tile = x_ref.shape[0]
    halo = up_ref.shape[0]

    @pl.when(i == 0)
    def _():
        wb_ref[...] = w2_ref[...].astype(BF16)

    ubuf[0:halo, :] = jnp.where(i > 0, up_ref[...], 0.0)
    ubuf[halo:halo + tile, :] = uc_ref[...]
    ubuf[halo + tile:, :] = jnp.where(i < n - 1, un_ref[...], 0.0)

    sub = CONV_CHUNK // SUBLANES
    part_rows = shbuf.shape[1] - 2 * halo
    span = part_rows + 2 * halo - SUBLANES
    for base in range(0, tile, part_rows):
        for r in range(1, SUBLANES):
            shbuf[r - 1, 0:span, :] = ubuf[base + r:base + r + span, :]
        for c0 in range(0, part_rows, CONV_CHUNK):
            parts = [jnp.zeros((SUBLANES, d), F32) + bdw_ref[...] for _ in range(sub)]
            for j in range(CONV_WIDTH):
                a, r = divmod(halo + j - CONV_PAD, SUBLANES)
                wj = wdw_ref[j * SUBLANES:(j + 1) * SUBLANES, :]
                for k in range(sub):
                    lo = c0 + (a + k) * SUBLANES
                    rows = (ubuf[base + lo:base + lo + SUBLANES, :] if r == 0
                            else shbuf[r - 1, lo:lo + SUBLANES, :])
                    parts[k] = parts[k] + rows * wj
            for k in range(sub):
                row0 = base + c0 + k * SUBLANES
                cbuf[row0:row0 + SUBLANES, :] = parts[k]
    acc = cbuf[...]

    mu = jnp.mean(acc, axis=-1, keepdims=True)
    cen = acc - mu
    var = jnp.mean(cen * cen, axis=-1, keepdims=True)
    z = _silu(cen * lax.rsqrt(var + EPS) * lg_ref[...] + lb_ref[...])
    y = jnp.dot(z.astype(BF16), wb_ref[...], preferred_element_type=F32) + b2_ref[...]
    x1 = x_ref[...] + mod_ref[0:1, 2 * d:3 * d] * y
    x1_ref[...] = x1
    _dispatch_tiles(x1, g2_ref[...], mod_ref[0:1, 3 * d:4 * d], mod_ref[0:1, 4 * d:5 * d],
                     wr_ref[...], br_ref[...], xs_ref, pos_ref, gate_ref, cnt_ref)


def _conv_router(x2, u, mod_l, w_dw, b_dw, ln_g, ln_b, w_pw2, b_pw2, g2n, wr, br):
    t, d = x2.shape
    tile = _combine_tile(t)
    nt = t // tile
    hb = tile // CONV_HALO
    nh = t // CONV_HALO
    tok = lambda i: (i, 0)
    prv = lambda i: (jnp.maximum(i * hb - 1, 0), 0)
    nxt = lambda i: (jnp.minimum((i + 1) * hb, nh - 1), 0)
    wpad = jnp.repeat(w_dw, SUBLANES, axis=0)
    out_specs, out_shape = _dispatch_out(t, d, tile)
    return pl.pallas_call(
        _conv_kernel,
        grid=(nt,),
        in_specs=[pl.BlockSpec((tile, d), tok), pl.BlockSpec((CONV_HALO, d), prv),
                  pl.BlockSpec((tile, d), tok), pl.BlockSpec((CONV_HALO, d), nxt),
                  _full(mod_l.shape), _full(wpad.shape), _full((1, d)), _full((1, d)), _full((1, d)),
                  _full(w_pw2.shape), _full((1, d)), _full((1, d)), _full(wr.shape), _full(br.shape)],
        out_specs=out_specs,
        out_shape=out_shape,
        scratch_shapes=[pltpu.VMEM(w_pw2.shape, BF16), pltpu.VMEM((tile + 2 * CONV_HALO, d), F32),
                        pltpu.VMEM((SUBLANES - 1, MOE_TILE + 2 * CONV_HALO, d), F32),
                        pltpu.VMEM((tile, d), F32)],
        compiler_params=_cparams(1),
        name="conv_router",
    )(x2, u, u, u, mod_l, wpad, b_dw, ln_g, ln_b, w_pw2, b_pw2, g2n, wr, br)


def _rope_tables(n_rows):
    quarter = HEAD_DIM // 4
    inv = ROPE_BASE ** (-jnp.arange(quarter, dtype=F32) / quarter)

    def tables(npos):
        ang = jnp.arange(npos, dtype=I32).astype(F32)[:, None] * inv[None, :]
        z = jnp.zeros_like(ang)
        c, s = jnp.cos(ang), jnp.sin(ang)
        half = lambda a, b: jnp.concatenate([a, b], axis=1)
        full = lambda h: jnp.concatenate([h, h, h, h], axis=1)
        return jnp.stack([full(half(c, c)), full(half(-s, z)), full(half(z, s))])

    return tables(n_rows), tables(GRID_W)


def _router_params(w_router, b_router):
    d, e = w_router.shape
    wr = jnp.zeros((LANES, d), F32).at[:e].set(w_router.T)
    br = jnp.zeros((LANES, MOE_TILE), F32).at[:e].set(jnp.broadcast_to(b_router[:, None], (e, MOE_TILE)))
    hi = wr.astype(BF16)
    lo = (wr - hi.astype(F32)).astype(BF16)
    return jnp.stack([hi, lo]), br


def kernel(x, c, ctx, c_ctx, w_mod, b_mod, norm1_g, norm2_g, attn_w_qkv, attn_b_qkv, attn_w_o, attn_b_o,
           attn_q_norm, attn_k_norm, attn_sinks, conv_w_pw1, conv_b_pw1, conv_w_dw, conv_b_dw, conv_ln_g,
           conv_ln_b, conv_w_pw2, conv_b_pw2, moe_w_router, moe_b_router, moe_w1, moe_b1, moe_w2, moe_b2):
    bsz, t, d = x.shape
    assert bsz == 1 and w_mod.shape[0] == 2
    n_ctx = ctx.shape[1]
    tile = min(TOKEN_TILE, t)
    assert t % tile == 0 and t % Q_BLOCK == 0 and t % MOE_TILE == 0 and tile % GRID_W == 0
    xt = x.reshape(t, d)
    row = lambda v: v.reshape(1, -1)

    mod = _mod_vectors(c, c_ctx, w_mod, b_mod)
    mod0, mod1 = mod[0], mod[1]

    bd = (jnp.arange(2 * LANES)[:, None] // HEAD_DIM == jnp.arange(2 * LANES)[None, :] // HEAD_DIM).astype(BF16)
    qg = row(jnp.tile(attn_q_norm[0], N_HEADS))
    kg = row(jnp.tile(attn_k_norm[0], N_KV_HEADS))
    rowt, colt = _rope_tables(t // GRID_W)
    g1n = row(norm1_g[0])
    bq = row(attn_b_qkv[0])
    q, kk, vv = _qkv_project(xt, mod0, 0, True, g1n, attn_w_qkv[0], bq, qg, kg, rowt, colt, bd, tile)
    _, kkc, vvc = _qkv_project(ctx.reshape(n_ctx, d), mod0, 1, False, g1n, attn_w_qkv[0], bq, qg, kg,
                               rowt[:, :max(n_ctx // GRID_W, 1)], colt, bd, n_ctx)
    o = _attention(q, kk, vv, kkc, vvc, attn_sinks[0])
    wr0, br0 = _router_params(moe_w_router[0], moe_b_router[0])
    x1, xs, pos, gates, cnt = _oproj_router(xt, o, mod0, attn_w_o[0], row(attn_b_o[0]), row(norm2_g[0]),
                                            wr0, br0)
    ys = _moe_experts(xs, cnt.shape[0], *_block_tables(cnt), 0, moe_w1, moe_b1, moe_w2, moe_b2)

    x2, u = _combine_glu(x1, ys, pos, gates, mod0, mod1, row(norm1_g[1]), conv_w_pw1[0], row(conv_b_pw1[0]))
    wr1, br1 = _router_params(moe_w_router[1], moe_b_router[1])
    x3, xs, pos, gates, cnt = _conv_router(x2, u, mod1, conv_w_dw[0], row(conv_b_dw[0]), row(conv_ln_g[0]),
                                           row(conv_ln_b[0]), conv_w_pw2[0], row(conv_b_pw2[0]),
                                           row(norm2_g[1]), wr1, br1)
    ys = _moe_experts(xs, cnt.shape[0], *_block_tables(cnt), 1, moe_w1, moe_b1, moe_w2, moe_b2)
    out = _combine(x3, ys, pos, gates, mod1)
    return out.reshape(1, t, d)
```

```python
import functools

import jax
import jax.numpy as jnp
from jax import lax
from jax.experimental import pallas as pl
from jax.experimental.pallas import tpu as pltpu

F32 = jnp.float32
BF16 = jnp.bfloat16
I32 = jnp.int32

N_HEADS = 16
N_KV_HEADS = 2
HEAD_DIM = 64
GRID_W = 64
WINDOW = 128
Q_BLOCK = 128
ROPE_BASE = 10000.0
CONV_WIDTH = 31
CONV_PAD = CONV_WIDTH // 2
N_EXPERTS = 32
TOP_K = 4
SWIGLU_LIMIT = 7.0
SWIGLU_ALPHA = 1.702
EPS = 1e-6

LANES = 128
SUBLANES = 8
TOKEN_TILE = 512
MOE_TILE = 256
GRANULE = SUBLANES
EXPERT_ROWS = 512
GRANULES_PER_BLOCK = EXPERT_ROWS // GRANULE
REGION_DATA_GRANULES = (MOE_TILE * TOP_K + N_EXPERTS * (GRANULE - 1)) // GRANULE


def _spare_per_region(nt):
    return -(-GRANULES_PER_BLOCK // nt)


def _region_rows(nt):
    rows = (REGION_DATA_GRANULES + 2 * _spare_per_region(nt) + 1) * GRANULE
    return -(-rows // 256) * 256
CONV_HALO = 16
CONV_CHUNK = 32
NEG_BIG = -1e30
LOG2_E = 1.4426950408889634
VMEM_LIMIT = 56 * 1024 * 1024


def _cparams(n_axes=1, vmem=VMEM_LIMIT):
    return pltpu.CompilerParams(dimension_semantics=("arbitrary",) * n_axes, vmem_limit_bytes=vmem)


def _full(shape):
    nd = len(shape)
    return pl.BlockSpec(shape, lambda *_: (0,) * nd)


def _silu(v):
    return v * jax.nn.sigmoid(v)


def _rms_mod(xv, g, shift, scale):
    ms = jnp.mean(xv * xv, axis=-1, keepdims=True)
    return (xv * lax.rsqrt(ms + EPS)) * g * (1.0 + scale) + shift


def _split_dot(a, b_bf16):
    hi = a.astype(BF16)
    lo = (a - hi.astype(F32)).astype(BF16)
    return (jnp.dot(hi, b_bf16, preferred_element_type=F32)
            + jnp.dot(lo, b_bf16, preferred_element_type=F32))


def _router_dispatch(x_new, g2n, sh2, sc2, wr, br, xs_ref, pos_ref, gate_ref, cnt_ref):
    tt = x_new.shape[0]
    rt = xs_ref.shape[0]
    h2 = _rms_mod(x_new, g2n, sh2, sc2)
    h_hi = h2.astype(BF16)
    h_lo = (h2 - h_hi.astype(F32)).astype(BF16)
    nt_dims = (((1,), (1,)), ((), ()))
    wdot = lambda w, h: lax.dot_general(w, h, nt_dims, preferred_element_type=F32)
    logits = (wdot(wr[0], h_hi) + (wdot(wr[0], h_lo) + wdot(wr[1], h_hi)) + br)[0:N_EXPERTS]
    erow = lax.broadcasted_iota(I32, (N_EXPERTS, tt), 0)
    hits, vals = [], []
    cur = logits
    for k in range(TOP_K):
        m = jnp.max(cur, axis=0, keepdims=True)
        sel = jnp.min(jnp.where(cur == m, erow, N_EXPERTS), axis=0, keepdims=True)
        hit = erow == sel
        hits.append(hit)
        vals.append(m)
        cur = jnp.where(hit, NEG_BIG * 2.0, cur)
    evs = [jnp.exp(v - vals[0]) for v in vals]
    den = (evs[0] + evs[1]) + (evs[2] + evs[3])
    gates = [ev / den for ev in evs]

    onehot = jnp.where((hits[0] | hits[1]) | (hits[2] | hits[3]), 1.0, 0.0).astype(BF16)
    ri = lax.broadcasted_iota(I32, (tt, tt), 0)
    ci = lax.broadcasted_iota(I32, (tt, tt), 1)
    earlier = jnp.where(ri < ci, 1.0, 0.0).astype(BF16)
    rank = jnp.dot(onehot, earlier, preferred_element_type=F32)
    cnt = jnp.dot(onehot, jnp.ones((tt, LANES), BF16), preferred_element_type=F32)
    c8 = jnp.bitwise_and(cnt.astype(I32) + (GRANULE - 1), -GRANULE).astype(F32)
    ue = lax.broadcasted_iota(I32, (N_EXPERTS, N_EXPERTS), 0)
    ve = lax.broadcasted_iota(I32, (N_EXPERTS, N_EXPERTS), 1)
    below = jnp.where(ve < ue, 1.0, 0.0).astype(BF16)
    off = jnp.dot(below, c8.astype(BF16), preferred_element_type=F32)
    slot_of = jnp.concatenate([off] * (tt // LANES), axis=1) + rank

    onehot_pad = jnp.concatenate([onehot, jnp.zeros((LANES - N_EXPERTS, tt), BF16)], axis=0)
    cnt_row = lax.dot_general(jnp.ones((2 * SUBLANES, tt), BF16), onehot_pad, nt_dims,
                              preferred_element_type=F32)[0:SUBLANES]
    cnt_ref[0] = jnp.bitwise_and(cnt_row.astype(I32) + (GRANULE - 1), -GRANULE)

    r_iota = lax.broadcasted_iota(I32, (rt, tt), 0)
    krow = lax.broadcasted_iota(I32, (LANES, tt), 0)
    perm = jnp.zeros((rt, tt), F32)
    pos_t = jnp.zeros((LANES, tt), F32)
    gate_t = jnp.zeros((LANES, tt), F32)
    for k in range(TOP_K):
        pk = jnp.sum(jnp.where(hits[k], slot_of, 0.0), axis=0, keepdims=True)
        perm = jnp.where(r_iota == pk.astype(I32), 1.0, perm)
        pos_t = jnp.where(krow == k, pk, pos_t)
        gate_t = jnp.where(krow == k, gates[k], gate_t)
    pos_ref[...] = pos_t.T.astype(I32)
    gate_ref[...] = gate_t.T
    xs_ref[...] = _pack_halves(jnp.dot(perm.astype(BF16), h_hi, preferred_element_type=F32))


def _pack_halves(v):
    half = v.shape[1] // 2
    lo = lax.bitcast_convert_type(v[:, :half], jnp.uint32) >> 16
    hi = lax.bitcast_convert_type(v[:, half:], jnp.uint32) & jnp.uint32(0xFFFF0000)
    return lo | hi


def _unpack_halves(w):
    return (lax.bitcast_convert_type(w << 16, F32),
            lax.bitcast_convert_type(w & jnp.uint32(0xFFFF0000), F32))


def _moe_combine(ys_ref, pos_ref, gate_ref):
    tt = MOE_TILE
    n_tiles = pos_ref.shape[0] // tt
    rt = ys_ref.shape[0] // n_tiles
    r_iota = lax.broadcasted_iota(I32, (tt, rt), 1)
    outs = []
    for j in range(n_tiles):
        pos = pos_ref[j * tt:(j + 1) * tt, :]
        gate = gate_ref[j * tt:(j + 1) * tt, :]
        g = jnp.zeros((tt, rt), F32)
        for k in range(TOP_K):
            g = jnp.where(r_iota == pos[:, k:k + 1], gate[:, k:k + 1], g)
        g = g.astype(BF16)
        ya, yb = _unpack_halves(ys_ref[j * rt:(j + 1) * rt, :])
        outs.append(jnp.concatenate([jnp.dot(g, ya.astype(BF16), preferred_element_type=F32),
                                     jnp.dot(g, yb.astype(BF16), preferred_element_type=F32)], axis=1))
    return outs[0] if n_tiles == 1 else jnp.concatenate(outs, axis=0)


def _mod_kernel(cc_ref, w_ref, b_ref, o_ref):
    a = _silu(cc_ref[...])
    o_ref[0] = jnp.dot(a, w_ref[0], precision=lax.Precision.HIGHEST,
                       preferred_element_type=F32) + b_ref[0]


def _mod_vectors(c, c_ctx, w_mod, b_mod):
    depth, d, d6 = w_mod.shape
    cc = jnp.zeros((SUBLANES, d), F32).at[0].set(c[0]).at[1].set(c_ctx)
    ncol = 4
    cw = d6 // ncol
    return pl.pallas_call(
        _mod_kernel,
        grid=(depth, ncol),
        in_specs=[pl.BlockSpec((SUBLANES, d), lambda l, j: (0, 0)),
                  pl.BlockSpec((1, d, cw), lambda l, j: (l, 0, j)),
                  pl.BlockSpec((1, 1, cw), lambda l, j: (l, 0, j))],
        out_specs=pl.BlockSpec((1, SUBLANES, cw), lambda l, j: (l, 0, j)),
        out_shape=jax.ShapeDtypeStruct((depth, SUBLANES, d6), F32),
        compiler_params=_cparams(2),
        name="mod_vectors",
    )(cc, w_mod, b_mod.reshape(depth, 1, d6))


def _qkv_kernel(mod_row, use_rope, x_ref, mod_ref, g_ref, w_ref, b_ref, qg_ref, kg_ref, rowt_ref, colt_ref,
                bd_ref, q_ref, k_ref, v_ref, wb_ref):
    d = x_ref.shape[1]
    tile = x_ref.shape[0]
    nq = N_HEADS * HEAD_DIM
    nk = N_KV_HEADS * HEAD_DIM

    @pl.when(pl.program_id(0) == 0)
    def _():
        wb_ref[...] = w_ref[...].astype(BF16)

    sh = mod_ref[mod_row:mod_row + 1, 0:d]
    sc = mod_ref[mod_row:mod_row + 1, d:2 * d]
    h = _rms_mod(x_ref[...], g_ref[...], sh, sc)
    qkv = jnp.dot(h.astype(BF16), wb_ref[...], preferred_element_type=F32) + b_ref[...]
    q = qkv[:, :nq]
    k = qkv[:, nq:nq + nk]
    v = qkv[:, nq + nk:]
    bd = bd_ref[...]

    def head_norm(t, gain):
        tt = t * t
        w = bd.shape[0]
        if t.shape[1] >= w:
            ss = jnp.concatenate([_split_dot(tt[:, c:c + w], bd) for c in range(0, t.shape[1], w)], axis=1)
        else:
            ss = _split_dot(tt, bd[:t.shape[1], :t.shape[1]])
        return t * lax.rsqrt(ss * (1.0 / HEAD_DIM) + EPS) * gain

    qn = head_norm(q, qg_ref[...])
    kn = head_norm(k, kg_ref[...])

    if use_rope:
        lane = lax.broadcasted_iota(I32, (tile, LANES), 1)
        is_row = (lane % HEAD_DIM) < HEAD_DIM // 2
        nrow = tile // GRID_W

        def table(idx):
            rt = jnp.concatenate([jnp.broadcast_to(rowt_ref[idx, r:r + 1, :], (GRID_W, LANES))
                                  for r in range(nrow)], axis=0)
            ct = jnp.concatenate([colt_ref[idx]] * nrow, axis=0)
            return jnp.where(is_row, rt, ct)

        cosv, sav, sbv = table(0), table(1), table(2)

        def rope(t):
            n = t.shape[1]
            reps = n // LANES
            ct = jnp.concatenate([cosv] * reps, axis=1) if reps > 1 else cosv
            at = jnp.concatenate([sav] * reps, axis=1) if reps > 1 else sav
            bt = jnp.concatenate([sbv] * reps, axis=1) if reps > 1 else sbv
            up = pltpu.roll(t, n - HEAD_DIM // 4, 1)
            dn = pltpu.roll(t, HEAD_DIM // 4, 1)
            return t * ct + up * at + dn * bt

        qn = rope(qn)
        kn = rope(kn)

    q_ref[...] = (qn * (HEAD_DIM ** -0.5 * LOG2_E)).astype(BF16)
    klane = lax.broadcasted_iota(I32, kn.shape, 1)
    low = klane < HEAD_DIM

    def variants(t):
        sw = pltpu.roll(t, HEAD_DIM, 1)
        z = jnp.zeros_like(t)
        return jnp.concatenate([jnp.where(low, t, z), jnp.where(low, z, sw),
                                jnp.where(low, sw, z), jnp.where(low, z, t)], axis=1)

    k_ref[...] = variants(kn).astype(BF16)
    v_ref[...] = variants(v).astype(BF16)


def _qkv_project(xt, mod_l, mod_row, use_rope, g1n, w_qkv, b_qkv, qg, kg, rowt, colt, bd, tile):
    t, d = xt.shape
    nqkv = w_qkv.shape[1]
    nq = N_HEADS * HEAD_DIM
    tok = lambda i: (i, 0)
    nrow = max(tile // GRID_W, 1)
    return pl.pallas_call(
        functools.partial(_qkv_kernel, mod_row, use_rope),
        grid=(t // tile,),
        in_specs=[pl.BlockSpec((tile, d), tok), _full(mod_l.shape), _full((1, d)),
                  _full((d, nqkv)), _full((1, nqkv)), _full((1, nq)), _full((1, LANES)),
                  pl.BlockSpec((3, nrow, LANES), lambda i: (0, i, 0)), _full(colt.shape), _full(bd.shape)],
        out_specs=[pl.BlockSpec((tile, nq), tok), pl.BlockSpec((tile, 4 * LANES), tok),
                   pl.BlockSpec((tile, 4 * LANES), tok)],
        out_shape=[jax.ShapeDtypeStruct((t, nq), BF16), jax.ShapeDtypeStruct((t, 4 * LANES), BF16),
                   jax.ShapeDtypeStruct((t, 4 * LANES), BF16)],
        scratch_shapes=[pltpu.VMEM((d, nqkv), BF16)],
        compiler_params=_cparams(1),
        name="qkv_project",
    )(xt, mod_l, g1n, w_qkv, b_qkv, qg, kg, rowt, colt, bd)


def _attn_kernel(sink_ref, q_ref, kp_ref, kc_ref, kn_ref, vp_ref, vc_ref, vn_ref, kx_ref, vx_ref, bias_ref,
                 o_ref):
    i = pl.program_id(0)
    n = pl.num_programs(0)
    qb = Q_BLOCK
    n_sub = q_ref.shape[0] // qb
    pairs_per_group = (N_HEADS // N_KV_HEADS) // 2
    rows = pairs_per_group * qb

    kall = jnp.concatenate([kp_ref[...], kc_ref[...], kn_ref[...]], axis=0)
    vall = jnp.concatenate([vp_ref[...], vc_ref[...], vn_ref[...]], axis=0)
    kctx = kx_ref[...]
    vctx = vx_ref[...]
    dn_t = (((1,), (1,)), ((), ()))
    rsub = lax.broadcasted_iota(I32, (rows, 1), 0) // qb

    for sb in range(n_sub):
        kband = kall[sb * qb:(sb + 3) * qb]
        vband = vall[sb * qb:(sb + 3) * qb]
        first = jnp.where(i == 0, 1, 0) if sb == 0 else 0
        last = jnp.where(i == n - 1, 2, 0) if sb == n_sub - 1 else 0
        bias = bias_ref[first + last]
        q_rows = slice(sb * qb, (sb + 1) * qb)
        for g in range(N_KV_HEADS):
            qg = jnp.concatenate(
                [q_ref[q_rows, (g * pairs_per_group + j) * LANES:(g * pairs_per_group + j + 1) * LANES]
                 for j in range(pairs_per_group)], axis=0)
            acc = jnp.zeros((rows, LANES), F32)
            for par in range(2):
                col = (2 * g + par) * LANES
                s_b = lax.dot_general(qg, kband[:, col:col + LANES], dn_t, preferred_element_type=F32)
                s_c = lax.dot_general(qg, kctx[:, col:col + LANES], dn_t, preferred_element_type=F32)
                s_b = s_b + bias
                sink = jnp.zeros((rows, 1), F32)
                for j in range(pairs_per_group):
                    hd = 2 * (g * pairs_per_group + j) + par
                    sink = jnp.where(rsub == j, sink_ref[hd] * LOG2_E, sink)
                m = jnp.maximum(jnp.maximum(jnp.max(s_b, axis=-1, keepdims=True),
                                            jnp.max(s_c, axis=-1, keepdims=True)), sink)
                p_b = jnp.exp2(s_b - m)
                p_c = jnp.exp2(s_c - m)
                l = (jnp.sum(p_b, axis=-1, keepdims=True) + jnp.sum(p_c, axis=-1, keepdims=True)
                     + jnp.exp2(sink - m))
                o = (jnp.dot(p_b.astype(BF16), vband[:, col:col + LANES], preferred_element_type=F32)
                     + jnp.dot(p_c.astype(BF16), vctx[:, col:col + LANES], preferred_element_type=F32))
                acc = acc + o / l
            for j in range(pairs_per_group):
                pcol = (g * pairs_per_group + j) * LANES
                o_ref[q_rows, pcol:pcol + LANES] = acc[j * qb:(j + 1) * qb].astype(BF16)


def _attention(q, kk, vv, kkc, vvc, sinks):
    t, nq = q.shape
    qb = Q_BLOCK
    nb = t // qb
    c = kkc.shape[0]
    w = kk.shape[1]
    per = next(p for p in (4, 2, 1) if nb % p == 0)
    cur = lambda i, s: (i, 0)
    prv = lambda i, s: (jnp.maximum(i * per - 1, 0), 0)
    nxt = lambda i, s: (jnp.minimum((i + 1) * per, nb - 1), 0)
    zero = lambda i, s: (0, 0)
    rows = (N_HEADS // N_KV_HEADS) // 2 * qb
    r = (jnp.arange(rows, dtype=I32) % qb)[:, None]
    cpos = jnp.arange(3 * qb, dtype=I32)[None, :] - qb
    inside = jnp.abs(r - cpos) <= WINDOW
    variants = [inside & ((cpos >= 0) | ((v & 1) == 0)) & ((cpos < qb) | ((v & 2) == 0)) for v in range(4)]
    bias = jnp.where(jnp.stack(variants), 0.0, NEG_BIG).astype(F32)
    gs = pltpu.PrefetchScalarGridSpec(
        num_scalar_prefetch=1,
        grid=(nb // per,),
        in_specs=[pl.BlockSpec((per * qb, nq), cur),
                  pl.BlockSpec((qb, w), prv), pl.BlockSpec((per * qb, w), cur), pl.BlockSpec((qb, w), nxt),
                  pl.BlockSpec((qb, w), prv), pl.BlockSpec((per * qb, w), cur), pl.BlockSpec((qb, w), nxt),
                  pl.BlockSpec((c, w), zero), pl.BlockSpec((c, w), zero),
                  pl.BlockSpec(bias.shape, lambda i, s: (0, 0, 0))],
        out_specs=pl.BlockSpec((per * qb, nq), cur),
    )
    return pl.pallas_call(
        _attn_kernel,
        grid_spec=gs,
        out_shape=jax.ShapeDtypeStruct((t, nq), BF16),
        compiler_params=_cparams(1),
        name="window_attention",
    )(sinks, q, kk, kk, kk, vv, vv, vv, kkc, vvc, bias)


def _dispatch_tiles(x_new, g2n, sh2, sc2, wr, br, xs_ref, pos_ref, gate_ref, cnt_ref):
    n = x_new.shape[0] // MOE_TILE
    rt = xs_ref.shape[0] // n
    for j in range(n):
        rows = pl.ds(j * MOE_TILE, MOE_TILE)
        _router_dispatch(x_new[j * MOE_TILE:(j + 1) * MOE_TILE], g2n, sh2, sc2, wr, br,
                         xs_ref.at[pl.ds(j * rt, rt)], pos_ref.at[rows], gate_ref.at[rows],
                         cnt_ref.at[pl.ds(j, 1)])


def _dispatch_out(t, d, tile):
    nt = t // MOE_TILE
    per_step = tile // MOE_TILE
    tok = lambda i: (i, 0)
    region = _region_rows(nt)
    specs = [pl.BlockSpec((tile, d), tok), pl.BlockSpec((per_step * region, d // 2), tok),
             pl.BlockSpec((tile, LANES), tok), pl.BlockSpec((tile, LANES), tok),
             pl.BlockSpec((per_step, SUBLANES, LANES), lambda i: (i, 0, 0))]
    shapes = [jax.ShapeDtypeStruct((t, d), F32), jax.ShapeDtypeStruct((nt * region, d // 2), jnp.uint32),
              jax.ShapeDtypeStruct((t, LANES), I32), jax.ShapeDtypeStruct((t, LANES), F32),
              jax.ShapeDtypeStruct((nt, SUBLANES, LANES), I32)]
    return specs, shapes


def _oproj_kernel(x_ref, o_ref, mod_ref, wo_ref, bo_ref, g2_ref, wr_ref, br_ref,
                  x1_ref, xs_ref, pos_ref, gate_ref, cnt_ref, wb_ref):
    d = x_ref.shape[1]

    @pl.when(pl.program_id(0) == 0)
    def _():
        wb_ref[...] = wo_ref[...].astype(BF16)

    y = jnp.dot(o_ref[...], wb_ref[...], preferred_element_type=F32) + bo_ref[...]
    x1 = x_ref[...] + mod_ref[0:1, 2 * d:3 * d] * y
    x1_ref[...] = x1
    _dispatch_tiles(x1, g2_ref[...], mod_ref[0:1, 3 * d:4 * d], mod_ref[0:1, 4 * d:5 * d],
                     wr_ref[...], br_ref[...], xs_ref, pos_ref, gate_ref, cnt_ref)


def _oproj_router(xt, o, mod_l, w_o, b_o, g2n, wr, br):
    t, d = xt.shape
    tile = _combine_tile(t)
    tok = lambda i: (i, 0)
    out_specs, out_shape = _dispatch_out(t, d, tile)
    return pl.pallas_call(
        _oproj_kernel,
        grid=(t // tile,),
        in_specs=[pl.BlockSpec((tile, d), tok), pl.BlockSpec((tile, o.shape[1]), tok), _full(mod_l.shape),
                  _full(w_o.shape), _full((1, d)), _full((1, d)), _full(wr.shape), _full(br.shape)],
        out_specs=out_specs,
        out_shape=out_shape,
        scratch_shapes=[pltpu.VMEM(w_o.shape, BF16)],
        compiler_params=_cparams(1),
        name="oproj_router",
    )(xt, o, mod_l, w_o, b_o, g2n, wr, br)


def _moe_kernel(nt, region_granules, brange_ref, nvg_ref, gsrc_ref, xs_hbm, w1_ref, b1_ref, w2_ref, b2_ref,
                ys_hbm, xbuf, obuf, w1b, w2b, gsem, ssem):
    e = pl.program_id(0)
    ne = pl.num_programs(0)
    b_lo = brange_ref[e]
    b_hi = brange_ref[ne + e]
    total = brange_ref[2 * ne - 1]
    gpb = GRANULES_PER_BLOCK
    dff = w2_ref.shape[1]
    spare = _spare_per_region(nt)
    idle_granule = REGION_DATA_GRANULES + 2 * spare

    def rows_of(granule):
        return pl.ds(pl.multiple_of(granule * GRANULE, GRANULE), GRANULE)

    def gather_copy(blk, gi):
        g = gsrc_ref[blk * gpb + gi]
        src = jnp.where(g >= 0, g, idle_granule)
        s = blk % 3
        return pltpu.make_async_copy(xs_hbm.at[rows_of(src)], xbuf.at[s, pl.ds(gi * GRANULE, GRANULE)],
                                     gsem.at[s])

    def scatter_copy(blk, s, gi):
        g = gsrc_ref[blk * gpb + gi]
        own_spare = (gi % nt) * region_granules + REGION_DATA_GRANULES + gi // nt + s * spare
        dst = jnp.where(g >= 0, g, own_spare)
        return pltpu.make_async_copy(obuf.at[s, pl.ds(gi * GRANULE, GRANULE)], ys_hbm.at[rows_of(dst)],
                                     ssem.at[s])

    def start_gather(blk):
        for gi in range(gpb):
            gather_copy(blk, gi).start()

    def wait_gather(s):
        pltpu.make_async_copy(xs_hbm.at[pl.ds(0, EXPERT_ROWS)], xbuf.at[s], gsem.at[s]).wait()

    def wait_scatter(s):
        pltpu.make_async_copy(obuf.at[s], ys_hbm.at[pl.ds(0, EXPERT_ROWS)], ssem.at[s]).wait()

    @pl.when(e == 0)
    def _():
        obuf[...] = jnp.zeros(obuf.shape, jnp.uint32)
        start_gather(0)
        start_gather(1)

    def expert_block(b, rows):
        slot = b % 2
        start_gather(b + 2)
        xb = jnp.concatenate(_unpack_halves(xbuf[b % 3, 0:rows, :]), axis=1).astype(BF16)
        gu = jnp.dot(xb, w1b[...], preferred_element_type=F32) + b1_ref[0]
        gt = jnp.minimum(gu[:, :dff], SWIGLU_LIMIT)
        up = jnp.clip(gu[:, dff:], -SWIGLU_LIMIT, SWIGLU_LIMIT)
        act = gt * jax.nn.sigmoid(SWIGLU_ALPHA * gt) * (up + 1.0)
        out = jnp.dot(act.astype(BF16), w2b[...], preferred_element_type=F32) + b2_ref[0]
        obuf[slot, 0:rows, :] = _pack_halves(out.astype(BF16).astype(F32))
        for gi in range(gpb):
            scatter_copy(b, slot, gi).start()

    def block_body(b, carry):
        wait_gather(b % 3)

        @pl.when(b >= 2)
        def _():
            wait_scatter(b % 2)

        half = EXPERT_ROWS // 2
        rows_needed = nvg_ref[b] * GRANULE

        @pl.when(rows_needed > half)
        def _():
            expert_block(b, EXPERT_ROWS)

        @pl.when(rows_needed <= half)
        def _():
            expert_block(b, half)

        return carry

    @pl.when(b_hi > b_lo)
    def _():
        w1b[...] = w1_ref[0].astype(BF16)
        w2b[...] = w2_ref[0].astype(BF16)
        lax.fori_loop(b_lo, b_hi, block_body, 0)

    @pl.when(e == ne - 1)
    def _():
        wait_gather(total % 3)
        wait_gather((total + 1) % 3)

        @pl.when(total >= 2)
        def _():
            wait_scatter(total % 2)

        @pl.when(total >= 1)
        def _():
            wait_scatter((total + 1) % 2)


def _moe_experts(xs, nt, block_range, n_granules, granule_src, layer, w1, b1, w2, b2):
    dw = xs.shape[1]
    d = 2 * dw
    dff = w2.shape[2]
    n_layers, n_exp = w1.shape[:2]
    by_expert = lambda e, br, ng, gs: (layer, e, 0, 0)
    gs = pltpu.PrefetchScalarGridSpec(
        num_scalar_prefetch=3,
        grid=(n_exp,),
        in_specs=[pl.BlockSpec(memory_space=pl.ANY),
                  pl.BlockSpec((None, 1, d, 2 * dff), by_expert), pl.BlockSpec((None, 1, 1, 2 * dff), by_expert),
                  pl.BlockSpec((None, 1, dff, d), by_expert), pl.BlockSpec((None, 1, 1, d), by_expert)],
        out_specs=pl.BlockSpec(memory_space=pl.ANY),
        scratch_shapes=[pltpu.VMEM((3, EXPERT_ROWS, dw), jnp.uint32), pltpu.VMEM((2, EXPERT_ROWS, dw), jnp.uint32),
                        pltpu.VMEM((d, 2 * dff), BF16), pltpu.VMEM((dff, d), BF16),
                        pltpu.SemaphoreType.DMA((3,)), pltpu.SemaphoreType.DMA((2,))],
    )
    return pl.pallas_call(
        functools.partial(_moe_kernel, nt, _region_rows(nt) // GRANULE),
        grid_spec=gs,
        out_shape=jax.ShapeDtypeStruct(xs.shape, jnp.uint32),
        input_output_aliases={3: 0},
        compiler_params=_cparams(1),
        name="moe_experts",
    )(block_range, n_granules, granule_src, xs, w1, b1.reshape(n_layers, n_exp, 1, 2 * dff),
      w2, b2.reshape(n_layers, n_exp, 1, d))


TABLE_BLOCKS = 8


def _tables_kernel(region_granules, cnt_ref, brange_ref, ngran_ref, gsrc_ref):
    gpb = GRANULES_PER_BLOCK
    log_b = gpb.bit_length() - 1
    n = cnt_ref.shape[0]
    rows = TABLE_BLOCKS * gpb
    nt_dims = (((1,), (1,)), ((), ()))

    cnt = cnt_ref[...].astype(F32)
    gl_len = cnt * (1.0 / GRANULE)
    ri = lax.broadcasted_iota(I32, (n, n), 0)
    ci = lax.broadcasted_iota(I32, (n, n), 1)
    lower = jnp.where(ci < ri, 1.0, 0.0).astype(BF16)
    upper = jnp.where(ri < ci, 1.0, 0.0).astype(BF16)
    pre = jnp.dot(lower, gl_len.astype(BF16), preferred_element_type=F32)
    off = jnp.dot(cnt.astype(BF16), upper, preferred_element_type=F32)
    src = ri.astype(F32) * float(region_granules) + off * (1.0 / GRANULE)
    n_g = jnp.sum(gl_len, axis=0, keepdims=True)
    blocks = ((n_g.astype(I32) + (gpb - 1)) >> log_b).astype(F32)
    bstart = jnp.dot(jnp.broadcast_to(blocks, (SUBLANES, n)).astype(BF16), upper,
                     preferred_element_type=F32)[0:1]
    bend = bstart + blocks

    def digits(v, base):
        hi = jnp.floor(v * (1.0 / base))
        return hi.astype(BF16), (v - hi * base).astype(BF16)

    def pick(onehot, table):
        return lax.dot_general(onehot, table, nt_dims, preferred_element_type=F32)

    def pick2(onehot, v, base):
        hi, lo = digits(v, base)
        return pick(onehot, hi) * base + pick(onehot, lo)

    q = pl.program_id(0) * rows + lax.broadcasted_iota(I32, (rows, n), 0)
    bq = (q >> log_b).astype(F32)
    iq = (q & (gpb - 1)).astype(F32)
    own = jnp.where((bstart <= bq) & (bq < bend), 1.0, 0.0).astype(BF16)
    pre_q = pick2(own, pre, 64.0)
    len_q = pick(own, gl_len.astype(BF16))
    src_q = pick2(own, src, 128.0)
    bstart_q = pick(own, jnp.broadcast_to(bstart, (n, n)).astype(BF16))
    gl = (bq - bstart_q) * float(gpb) + iq
    inside = (pre_q <= gl) & (gl < pre_q + len_q)
    hit = jnp.sum(jnp.where(inside, src_q - pre_q + gl + 1.0, 0.0), axis=-1, keepdims=True)
    gsrc_ref[...] = jnp.broadcast_to(hit - 1.0, (rows, n)).astype(I32)

    bb = (pl.program_id(0) * TABLE_BLOCKS + lax.broadcasted_iota(I32, (TABLE_BLOCKS, n), 0)).astype(F32)
    lane = lax.broadcasted_iota(I32, (TABLE_BLOCKS, n), 1).astype(F32)
    mine = (bstart <= bb) & (bb < bend)
    red = lambda v: jnp.sum(jnp.where(mine, v, 0.0), axis=-1, keepdims=True)
    active = red(jnp.ones_like(lane))
    n_left = red(jnp.broadcast_to(n_g, mine.shape)) - (bb[:, 0:1] - red(jnp.broadcast_to(bstart, mine.shape))) * gpb
    ngran_ref[...] = jnp.broadcast_to(jnp.clip(n_left, 0.0, float(gpb)) * active, (TABLE_BLOCKS, n)).astype(I32)

    sub = lax.broadcasted_iota(I32, (SUBLANES, n), 0)
    brange_ref[...] = jnp.where(sub == 0, bstart, jnp.where(sub == 1, bend, 0.0)).astype(I32)


def _block_tables(cnt):
    nt = cnt.shape[0]
    assert nt <= LANES
    max_blocks = (nt * REGION_DATA_GRANULES) // GRANULES_PER_BLOCK + N_EXPERTS
    n_blocks = -(-(max_blocks + 2) // TABLE_BLOCKS) * TABLE_BLOCKS
    gpb = GRANULES_PER_BLOCK
    cnt_sq = jnp.zeros((LANES, LANES), I32).at[:nt].set(cnt[:, 0, :])
    by_step = lambda i: (i, 0)
    brange, ngran, gsrc = pl.pallas_call(
        functools.partial(_tables_kernel, _region_rows(nt) // GRANULE),
        grid=(n_blocks // TABLE_BLOCKS,),
        in_specs=[_full((LANES, LANES))],
        out_specs=[_full((SUBLANES, LANES)), pl.BlockSpec((TABLE_BLOCKS, LANES), by_step),
                   pl.BlockSpec((TABLE_BLOCKS * gpb, LANES), by_step)],
        out_shape=[jax.ShapeDtypeStruct((SUBLANES, LANES), I32), jax.ShapeDtypeStruct((n_blocks, LANES), I32),
                   jax.ShapeDtypeStruct((n_blocks * gpb, LANES), I32)],
        compiler_params=_cparams(1),
        name="moe_block_tables",
    )(cnt_sq)
    return brange[0:2, :N_EXPERTS].reshape(-1), ngran[:, 0], gsrc[:, 0]


def _combine_kernel(x_ref, ys_ref, pos_ref, gate_ref, mod_ref, o_ref):
    d = x_ref.shape[1]
    o_ref[...] = x_ref[...] + mod_ref[0:1, 5 * d:6 * d] * _moe_combine(ys_ref, pos_ref, gate_ref)


def _combine_glu_kernel(x_ref, ys_ref, pos_ref, gate_ref, modp_ref, modn_ref, g_ref, w_ref, b_ref,
                        o_ref, u_ref, wb_ref):
    d = x_ref.shape[1]

    @pl.when(pl.program_id(0) == 0)
    def _():
        wb_ref[...] = w_ref[...].astype(BF16)

    x2 = x_ref[...] + modp_ref[0:1, 5 * d:6 * d] * _moe_combine(ys_ref, pos_ref, gate_ref)
    o_ref[...] = x2
    h = _rms_mod(x2, g_ref[...], modn_ref[0:1, 0:d], modn_ref[0:1, d:2 * d])
    u = jnp.dot(h.astype(BF16), wb_ref[...], preferred_element_type=F32) + b_ref[...]
    u_ref[...] = u[:, :d] * jax.nn.sigmoid(u[:, d:])


def _combine_tile(t):
    return MOE_TILE * (2 if (t // MOE_TILE) % 2 == 0 else 1)


def _combine_in_specs(t, d, tile):
    tok = lambda i: (i, 0)
    region = _region_rows(t // MOE_TILE) * (tile // MOE_TILE)
    return [pl.BlockSpec((tile, d), tok), pl.BlockSpec((region, d // 2), tok),
            pl.BlockSpec((tile, LANES), tok), pl.BlockSpec((tile, LANES), tok)]


def _combine(x1, ys, pos, gates, mod_l):
    t, d = x1.shape
    tile = _combine_tile(t)
    return pl.pallas_call(
        _combine_kernel,
        grid=(t // tile,),
        in_specs=_combine_in_specs(t, d, tile) + [_full(mod_l.shape)],
        out_specs=pl.BlockSpec((tile, d), lambda i: (i, 0)),
        out_shape=jax.ShapeDtypeStruct((t, d), F32),
        compiler_params=_cparams(1),
        name="moe_combine",
    )(x1, ys, pos, gates, mod_l)


def _combine_glu(x1, ys, pos, gates, mod_prev, mod_next, g1n, w_pw1, b_pw1):
    t, d = x1.shape
    tile = _combine_tile(t)
    tok = lambda i: (i, 0)
    return pl.pallas_call(
        _combine_glu_kernel,
        grid=(t // tile,),
        in_specs=_combine_in_specs(t, d, tile)
                 + [_full(mod_prev.shape), _full(mod_next.shape), _full((1, d)),
                    _full(w_pw1.shape), _full((1, 2 * d))],
        out_specs=[pl.BlockSpec((tile, d), tok), pl.BlockSpec((tile, d), tok)],
        out_shape=[jax.ShapeDtypeStruct((t, d), F32), jax.ShapeDtypeStruct((t, d), F32)],
        scratch_shapes=[pltpu.VMEM(w_pw1.shape, BF16)],
        compiler_params=_cparams(1),
        name="combine_pw1_glu",
    )(x1, ys, pos, gates, mod_prev, mod_next, g1n, w_pw1, b_pw1)


def _conv_kernel(x_ref, up_ref, uc_ref, un_ref, mod_ref, wdw_ref, bdw_ref, lg_ref, lb_ref, w2_ref, b2_ref,
                 g2_ref, wr_ref, br_ref, x1_ref, xs_ref, pos_ref, gate_ref, cnt_ref, wb_ref, ubuf, shbuf, cbuf):
    i = pl.program_id(0)
    n = pl.num_programs(0)
    d = x_ref.shape[1]
    tile = x_ref.shape[0]
    halo = up_ref.shape[0]

    @pl.when(i == 0)
    def _():
        wb_ref[...] = w2_ref[...].astype(BF16)

    ubuf[0:halo, :] = jnp.where(i > 0, up_ref[...], 0.0)
    ubuf[halo:halo + tile, :] = uc_ref[...]
    ubuf[halo + tile:, :] = jnp.where(i < n - 1, un_ref[...], 0.0)

    sub = CONV_CHUNK // SUBLANES
    part_rows = shbuf.shape[1] - 2 * halo
    span = part_rows + 2 * halo - SUBLANES
    for base in range(0, tile, part_rows):
        for r in range(1, SUBLANES):
            shbuf[r - 1, 0:span, :] = ubuf[base + r:base + r + span, :]
        for c0 in range(0, part_rows, CONV_CHUNK):
            parts = [jnp.zeros((SUBLANES, d), F32) + bdw_ref[...] for _ in range(sub)]
            for j in range(CONV_WIDTH):
                a, r = divmod(halo + j - CONV_PAD, SUBLANES)
                wj = wdw_ref[j * SUBLANES:(j + 1) * SUBLANES, :]
                for k in range(sub):
                    lo = c0 + (a + k) * SUBLANES
                    rows = (ubuf[base + lo:base + lo + SUBLANES, :] if r == 0
                            else shbuf[r - 1, lo:lo + SUBLANES, :])
                    parts[k] = parts[k] + rows * wj
            for k in range(sub):
                row0 = base + c0 + k * SUBLANES
                cbuf[row0:row0 + SUBLANES, :] = parts[k]
    acc = cbuf[...]

    mu = jnp.mean(acc, axis=-1, keepdims=True)
    cen = acc - mu
    var = jnp.mean(cen * cen, axis=-1, keepdims=True)
    z = _silu(cen * lax.rsqrt(var + EPS) * lg_ref[...] + lb_ref[...])
    y = jnp.dot(z.astype(BF16), wb_ref[...], preferred_element_type=F32) + b2_ref[...]
    x1 = x_ref[...] + mod_ref[0:1, 2 * d:3 * d] * y
    x1_ref[...] = x1
    _dispatch_tiles(x1, g2_ref[...], mod_ref[0:1, 3 * d:4 * d], mod_ref[0:1, 4 * d:5 * d],
                     wr_ref[...], br_ref[...], xs_ref, pos_ref, gate_ref, cnt_ref)


def _conv_router(x2, u, mod_l, w_dw, b_dw, ln_g, ln_b, w_pw2, b_pw2, g2n, wr, br):
    t, d = x2.shape
    tile = _combine_tile(t)
    nt = t // tile
    hb = tile // CONV_HALO
    nh = t // CONV_HALO
    tok = lambda i: (i, 0)
    prv = lambda i: (jnp.maximum(i * hb - 1, 0), 0)
    nxt = lambda i: (jnp.minimum((i + 1) * hb, nh - 1), 0)
    wpad = jnp.repeat(w_dw, SUBLANES, axis=0)
    out_specs, out_shape = _dispatch_out(t, d, tile)
    return pl.pallas_call(
        _conv_kernel,
        grid=(nt,),
        in_specs=[pl.BlockSpec((tile, d), tok), pl.BlockSpec((CONV_HALO, d), prv),
                  pl.BlockSpec((tile, d), tok), pl.BlockSpec((CONV_HALO, d), nxt),
                  _full(mod_l.shape), _full(wpad.shape), _full((1, d)), _full((1, d)), _full((1, d)),
                  _full(w_pw2.shape), _full((1, d)), _full((1, d)), _full(wr.shape), _full(br.shape)],
        out_specs=out_specs,
        out_shape=out_shape,
        scratch_shapes=[pltpu.VMEM(w_pw2.shape, BF16), pltpu.VMEM((tile + 2 * CONV_HALO, d), F32),
                        pltpu.VMEM((SUBLANES - 1, MOE_TILE + 2 * CONV_HALO, d), F32),
                        pltpu.VMEM((tile, d), F32)],
        compiler_params=_cparams(1),
        name="conv_router",
    )(x2, u, u, u, mod_l, wpad, b_dw, ln_g, ln_b, w_pw2, b_pw2, g2n, wr, br)


def _rope_tables(n_rows):
    quarter = HEAD_DIM // 4
    inv = ROPE_BASE ** (-jnp.arange(quarter, dtype=F32) / quarter)

    def tables(npos):
        ang = jnp.arange(npos, dtype=I32).astype(F32)[:, None] * inv[None, :]
        z = jnp.zeros_like(ang)
        c, s = jnp.cos(ang), jnp.sin(ang)
        half = lambda a, b: jnp.concatenate([a, b], axis=1)
        full = lambda h: jnp.concatenate([h, h, h, h], axis=1)
        return jnp.stack([full(half(c, c)), full(half(-s, z)), full(half(z, s))])

    return tables(n_rows), tables(GRID_W)


def _router_params(w_router, b_router):
    d, e = w_router.shape
    wr = jnp.zeros((LANES, d), F32).at[:e].set(w_router.T)
    br = jnp.zeros((LANES, MOE_TILE), F32).at[:e].set(jnp.broadcast_to(b_router[:, None], (e, MOE_TILE)))
    hi = wr.astype(BF16)
    lo = (wr - hi.astype(F32)).astype(BF16)
    return jnp.stack([hi, lo]), br


def kernel(x, c, ctx, c_ctx, w_mod, b_mod, norm1_g, norm2_g, attn_w_qkv, attn_b_qkv, attn_w_o, attn_b_o,
           attn_q_norm, attn_k_norm, attn_sinks, conv_w_pw1, conv_b_pw1, conv_w_dw, conv_b_dw, conv_ln_g,
           conv_ln_b, conv_w_pw2, conv_b_pw2, moe_w_router, moe_b_router, moe_w1, moe_b1, moe_w2, moe_b2):
    bsz, t, d = x.shape
    assert bsz == 1 and w_mod.shape[0] == 2
    n_ctx = ctx.shape[1]
    tile = min(TOKEN_TILE, t)
    assert t % tile == 0 and t % Q_BLOCK == 0 and t % MOE_TILE == 0 and tile % GRID_W == 0
    xt = x.reshape(t, d)
    row = lambda v: v.reshape(1, -1)

    mod = _mod_vectors(c, c_ctx, w_mod, b_mod)
    mod0, mod1 = mod[0], mod[1]

    bd = (jnp.arange(2 * LANES)[:, None] // HEAD_DIM == jnp.arange(2 * LANES)[None, :] // HEAD_DIM).astype(BF16)
    qg = row(jnp.tile(attn_q_norm[0], N_HEADS))
    kg = row(jnp.tile(attn_k_norm[0], N_KV_HEADS))
    rowt, colt = _rope_tables(t // GRID_W)
    g1n = row(norm1_g[0])
    bq = row(attn_b_qkv[0])
    q, kk, vv = _qkv_project(xt, mod0, 0, True, g1n, attn_w_qkv[0], bq, qg, kg, rowt, colt, bd, tile)
    _, kkc, vvc = _qkv_project(ctx.reshape(n_ctx, d), mod0, 1, False, g1n, attn_w_qkv[0], bq, qg, kg,
                               rowt[:, :max(n_ctx // GRID_W, 1)], colt, bd, n_ctx)
    o = _attention(q, kk, vv, kkc, vvc, attn_sinks[0])
    wr0, br0 = _router_params(moe_w_router[0], moe_b_router[0])
    x1, xs, pos, gates, cnt = _oproj_router(xt, o, mod0, attn_w_o[0], row(attn_b_o[0]), row(norm2_g[0]),
                                            wr0, br0)
    ys = _moe_experts(xs, cnt.shape[0], *_block_tables(cnt), 0, moe_w1, moe_b1, moe_w2, moe_b2)

    x2, u = _combine_glu(x1, ys, pos, gates, mod0, mod1, row(norm1_g[1]), conv_w_pw1[0], row(conv_b_pw1[0]))
    wr1, br1 = _router_params(moe_w_router[1], moe_b_router[1])
    x3, xs, pos, gates, cnt = _conv_router(x2, u, mod1, conv_w_dw[0], row(conv_b_dw[0]), row(conv_ln_g[0]),
                                           row(conv_ln_b[0]), conv_w_pw2[0], row(conv_b_pw2[0]),
                                           row(norm2_g[1]), wr1, br1)
    ys = _moe_experts(xs, cnt.shape[0], *_block_tables(cnt), 1, moe_w1, moe_b1, moe_w2, moe_b2)
    out = _combine(x3, ys, pos, gates, mod1)
    return out.reshape(1, t, d)
```

```python
import functools

import jax
import jax.numpy as jnp
from jax import lax
from jax.experimental import pallas as pl
from jax.experimental.pallas import tpu as pltpu

F32 = jnp.float32
BF16 = jnp.bfloat16
I32 = jnp.int32

N_HEADS = 16
N_KV_HEADS = 2
HEAD_DIM = 64
GRID_W = 64
WINDOW = 128
Q_BLOCK = 128
ROPE_BASE = 10000.0
CONV_WIDTH = 31
CONV_PAD = CONV_WIDTH // 2
N_EXPERTS = 32
TOP_K = 4
SWIGLU_LIMIT = 7.0
SWIGLU_ALPHA = 1.702
EPS = 1e-6

LANES = 128
SUBLANES = 8
TOKEN_TILE = 1024
MOE_TILE = 256
GRANULE = SUBLANES
EXPERT_ROWS = 512
GRANULES_PER_BLOCK = EXPERT_ROWS // GRANULE
REGION_DATA_GRANULES = (MOE_TILE * TOP_K + N_EXPERTS * (GRANULE - 1)) // GRANULE


def _spare_per_region(nt):
    return -(-GRANULES_PER_BLOCK // nt)


def _region_rows(nt):
    rows = (REGION_DATA_GRANULES + 2 * _spare_per_region(nt) + 1) * GRANULE
    return -(-rows // 256) * 256
CONV_HALO = 16
CONV_CHUNK = 32
NEG_BIG = -1e30
LOG2_E = 1.4426950408889634
VMEM_LIMIT = 56 * 1024 * 1024


def _cparams(n_axes=1, vmem=VMEM_LIMIT):
    return pltpu.CompilerParams(dimension_semantics=("arbitrary",) * n_axes, vmem_limit_bytes=vmem)


def _full(shape):
    nd = len(shape)
    return pl.BlockSpec(shape, lambda *_: (0,) * nd)


def _silu(v):
    return v * jax.nn.sigmoid(v)


def _rms_mod(xv, g, shift, scale):
    ms = jnp.mean(xv * xv, axis=-1, keepdims=True)
    return (xv * lax.rsqrt(ms + EPS)) * g * (1.0 + scale) + shift


def _split_dot(a, b_bf16):
    hi = a.astype(BF16)
    lo = (a - hi.astype(F32)).astype(BF16)
    return (jnp.dot(hi, b_bf16, preferred_element_type=F32)
            + jnp.dot(lo, b_bf16, preferred_element_type=F32))


def _router_dispatch(x_new, g2n, sh2, sc2, wr, br, xs_ref, pos_ref, gate_ref, cnt_ref):
    tt = x_new.shape[0]
    rt = xs_ref.shape[0]
    h2 = _rms_mod(x_new, g2n, sh2, sc2)
    h_hi = h2.astype(BF16)
    h_lo = (h2 - h_hi.astype(F32)).astype(BF16)
    nt_dims = (((1,), (1,)), ((), ()))
    wdot = lambda w, h: lax.dot_general(w, h, nt_dims, preferred_element_type=F32)
    logits = (wdot(wr[0], h_hi) + (wdot(wr[0], h_lo) + wdot(wr[1], h_hi)) + br)[0:N_EXPERTS]
    erow = lax.broadcasted_iota(I32, (N_EXPERTS, tt), 0)
    hits, vals = [], []
    cur = logits
    for k in range(TOP_K):
        m = jnp.max(cur, axis=0, keepdims=True)
        sel = jnp.min(jnp.where(cur == m, erow, N_EXPERTS), axis=0, keepdims=True)
        hit = erow == sel
        hits.append(hit)
        vals.append(m)
        cur = jnp.where(hit, NEG_BIG * 2.0, cur)
    evs = [jnp.exp(v - vals[0]) for v in vals]
    den = (evs[0] + evs[1]) + (evs[2] + evs[3])
    gates = [ev / den for ev in evs]

    onehot = jnp.where((hits[0] | hits[1]) | (hits[2] | hits[3]), 1.0, 0.0).astype(BF16)
    ri = lax.broadcasted_iota(I32, (tt, tt), 0)
    ci = lax.broadcasted_iota(I32, (tt, tt), 1)
    earlier = jnp.where(ri < ci, 1.0, 0.0).astype(BF16)
    rank = jnp.dot(onehot, earlier, preferred_element_type=F32)
    cnt = jnp.dot(onehot, jnp.ones((tt, LANES), BF16), preferred_element_type=F32)
    c8 = jnp.bitwise_and(cnt.astype(I32) + (GRANULE - 1), -GRANULE).astype(F32)
    ue = lax.broadcasted_iota(I32, (N_EXPERTS, N_EXPERTS), 0)
    ve = lax.broadcasted_iota(I32, (N_EXPERTS, N_EXPERTS), 1)
    below = jnp.where(ve < ue, 1.0, 0.0).astype(BF16)
    off = jnp.dot(below, c8.astype(BF16), preferred_element_type=F32)
    slot_of = jnp.concatenate([off] * (tt // LANES), axis=1) + rank

    onehot_pad = jnp.concatenate([onehot, jnp.zeros((LANES - N_EXPERTS, tt), BF16)], axis=0)
    cnt_row = lax.dot_general(jnp.ones((2 * SUBLANES, tt), BF16), onehot_pad, nt_dims,
                              preferred_element_type=F32)[0:SUBLANES]
    cnt_ref[0] = jnp.bitwise_and(cnt_row.astype(I32) + (GRANULE - 1), -GRANULE)

    r_iota = lax.broadcasted_iota(I32, (rt, tt), 0)
    krow = lax.broadcasted_iota(I32, (LANES, tt), 0)
    perm = jnp.zeros((rt, tt), F32)
    pos_t = jnp.zeros((LANES, tt), F32)
    gate_t = jnp.zeros((LANES, tt), F32)
    for k in range(TOP_K):
        pk = jnp.sum(jnp.where(hits[k], slot_of, 0.0), axis=0, keepdims=True)
        perm = jnp.where(r_iota == pk.astype(I32), 1.0, perm)
        pos_t = jnp.where(krow == k, pk, pos_t)
        gate_t = jnp.where(krow == k, gates[k], gate_t)
    pos_ref[...] = pos_t.T.astype(I32)
    gate_ref[...] = gate_t.T
    xs_ref[...] = _pack_halves(jnp.dot(perm.astype(BF16), h_hi, preferred_element_type=F32))


def _pack_halves(v):
    half = v.shape[1] // 2
    lo = lax.bitcast_convert_type(v[:, :half], jnp.uint32) >> 16
    hi = lax.bitcast_convert_type(v[:, half:], jnp.uint32) & jnp.uint32(0xFFFF0000)
    return lo | hi


def _unpack_halves(w):
    return (lax.bitcast_convert_type(w << 16, F32),
            lax.bitcast_convert_type(w & jnp.uint32(0xFFFF0000), F32))


def _moe_combine(ys_ref, pos_ref, gate_ref):
    tt = MOE_TILE
    n_tiles = pos_ref.shape[0] // tt
    rt = ys_ref.shape[0] // n_tiles
    r_iota = lax.broadcasted_iota(I32, (tt, rt), 1)
    outs = []
    for j in range(n_tiles):
        pos = pos_ref[j * tt:(j + 1) * tt, :]
        gate = gate_ref[j * tt:(j + 1) * tt, :]
        g = jnp.zeros((tt, rt), F32)
        for k in range(TOP_K):
            g = jnp.where(r_iota == pos[:, k:k + 1], gate[:, k:k + 1], g)
        g = g.astype(BF16)
        ya, yb = _unpack_halves(ys_ref[j * rt:(j + 1) * rt, :])
        outs.append(jnp.concatenate([jnp.dot(g, ya.astype(BF16), preferred_element_type=F32),
                                     jnp.dot(g, yb.astype(BF16), preferred_element_type=F32)], axis=1))
    return outs[0] if n_tiles == 1 else jnp.concatenate(outs, axis=0)


def _mod_kernel(cc_ref, w_ref, b_ref, o_ref):
    a = _silu(cc_ref[...])
    o_ref[0] = jnp.dot(a, w_ref[0], precision=lax.Precision.HIGHEST,
                       preferred_element_type=F32) + b_ref[0]


def _mod_vectors(c, c_ctx, w_mod, b_mod):
    depth, d, d6 = w_mod.shape
    cc = jnp.zeros((SUBLANES, d), F32).at[0].set(c[0]).at[1].set(c_ctx)
    ncol = 4
    cw = d6 // ncol
    return pl.pallas_call(
        _mod_kernel,
        grid=(depth, ncol),
        in_specs=[pl.BlockSpec((SUBLANES, d), lambda l, j: (0, 0)),
                  pl.BlockSpec((1, d, cw), lambda l, j: (l, 0, j)),
                  pl.BlockSpec((1, 1, cw), lambda l, j: (l, 0, j))],
        out_specs=pl.BlockSpec((1, SUBLANES, cw), lambda l, j: (l, 0, j)),
        out_shape=jax.ShapeDtypeStruct((depth, SUBLANES, d6), F32),
        compiler_params=_cparams(2),
        name="mod_vectors",
    )(cc, w_mod, b_mod.reshape(depth, 1, d6))


def _qkv_kernel(mod_row, use_rope, x_ref, mod_ref, g_ref, w_ref, b_ref, qg_ref, kg_ref, rowt_ref, colt_ref,
                bd_ref, q_ref, k_ref, v_ref, wb_ref):
    d = x_ref.shape[1]
    tile = x_ref.shape[0]
    nq = N_HEADS * HEAD_DIM
    nk = N_KV_HEADS * HEAD_DIM

    @pl.when(pl.program_id(0) == 0)
    def _():
        wb_ref[...] = w_ref[...].astype(BF16)

    sh = mod_ref[mod_row:mod_row + 1, 0:d]
    sc = mod_ref[mod_row:mod_row + 1, d:2 * d]
    h = _rms_mod(x_ref[...], g_ref[...], sh, sc)
    qkv = jnp.dot(h.astype(BF16), wb_ref[...], preferred_element_type=F32) + b_ref[...]
    q = qkv[:, :nq]
    k = qkv[:, nq:nq + nk]
    v = qkv[:, nq + nk:]
    bd = bd_ref[...]

    def head_norm(t, gain):
        tt = t * t
        w = bd.shape[0]
        if t.shape[1] >= w:
            ss = jnp.concatenate([_split_dot(tt[:, c:c + w], bd) for c in range(0, t.shape[1], w)], axis=1)
        else:
            ss = _split_dot(tt, bd[:t.shape[1], :t.shape[1]])
        return t * lax.rsqrt(ss * (1.0 / HEAD_DIM) + EPS) * gain

    qn = head_norm(q, qg_ref[...])
    kn = head_norm(k, kg_ref[...])

    if use_rope:
        lane = lax.broadcasted_iota(I32, (tile, LANES), 1)
        is_row = (lane % HEAD_DIM) < HEAD_DIM // 2
        nrow = tile // GRID_W

        def table(idx):
            rt = jnp.concatenate([jnp.broadcast_to(rowt_ref[idx, r:r + 1, :], (GRID_W, LANES))
                                  for r in range(nrow)], axis=0)
            ct = jnp.concatenate([colt_ref[idx]] * nrow, axis=0)
            return jnp.where(is_row, rt, ct)

        cosv, sav, sbv = table(0), table(1), table(2)

        def rope(t):
            n = t.shape[1]
            reps = n // LANES
            ct = jnp.concatenate([cosv] * reps, axis=1) if reps > 1 else cosv
            at = jnp.concatenate([sav] * reps, axis=1) if reps > 1 else sav
            bt = jnp.concatenate([sbv] * reps, axis=1) if reps > 1 else sbv
            up = pltpu.roll(t, n - HEAD_DIM // 4, 1)
            dn = pltpu.roll(t, HEAD_DIM // 4, 1)
            return t * ct + up * at + dn * bt

        qn = rope(qn)
        kn = rope(kn)

    q_ref[...] = (qn * (HEAD_DIM ** -0.5 * LOG2_E)).astype(BF16)
    klane = lax.broadcasted_iota(I32, kn.shape, 1)
    low = klane < HEAD_DIM

    def variants(t):
        sw = pltpu.roll(t, HEAD_DIM, 1)
        z = jnp.zeros_like(t)
        return jnp.concatenate([jnp.where(low, t, z), jnp.where(low, z, sw),
                                jnp.where(low, sw, z), jnp.where(low, z, t)], axis=1)

    k_ref[...] = variants(kn).astype(BF16)
    v_ref[...] = variants(v).astype(BF16)


def _qkv_project(xt, mod_l, mod_row, use_rope, g1n, w_qkv, b_qkv, qg, kg, rowt, colt, bd, tile):
    t, d = xt.shape
    nqkv = w_qkv.shape[1]
    nq = N_HEADS * HEAD_DIM
    tok = lambda i: (i, 0)
    nrow = max(tile // GRID_W, 1)
    return pl.pallas_call(
        functools.partial(_qkv_kernel, mod_row, use_rope),
        grid=(t // tile,),
        in_specs=[pl.BlockSpec((tile, d), tok), _full(mod_l.shape), _full((1, d)),
                  _full((d, nqkv)), _full((1, nqkv)), _full((1, nq)), _full((1, LANES)),
                  pl.BlockSpec((3, nrow, LANES), lambda i: (0, i, 0)), _full(colt.shape), _full(bd.shape)],
        out_specs=[pl.BlockSpec((tile, nq), tok), pl.BlockSpec((tile, 4 * LANES), tok),
                   pl.BlockSpec((tile, 4 * LANES), tok)],
        out_shape=[jax.ShapeDtypeStruct((t, nq), BF16), jax.ShapeDtypeStruct((t, 4 * LANES), BF16),
                   jax.ShapeDtypeStruct((t, 4 * LANES), BF16)],
        scratch_shapes=[pltpu.VMEM((d, nqkv), BF16)],
        compiler_params=_cparams(1),
        name="qkv_project",
    )(xt, mod_l, g1n, w_qkv, b_qkv, qg, kg, rowt, colt, bd)


def _attn_kernel(sink_ref, q_ref, kp_ref, kc_ref, kn_ref, vp_ref, vc_ref, vn_ref, kx_ref, vx_ref, bias_ref,
                 o_ref):
    i = pl.program_id(0)
    n = pl.num_programs(0)
    qb = Q_BLOCK
    n_sub = q_ref.shape[0] // qb
    pairs_per_group = (N_HEADS // N_KV_HEADS) // 2
    rows = pairs_per_group * qb

    kall = jnp.concatenate([kp_ref[...], kc_ref[...], kn_ref[...]], axis=0)
    vall = jnp.concatenate([vp_ref[...], vc_ref[...], vn_ref[...]], axis=0)
    kctx = kx_ref[...]
    vctx = vx_ref[...]
    dn_t = (((1,), (1,)), ((), ()))
    rsub = lax.broadcasted_iota(I32, (rows, 1), 0) // qb

    for sb in range(n_sub):
        kband = kall[sb * qb:(sb + 3) * qb]
        vband = vall[sb * qb:(sb + 3) * qb]
        first = jnp.where(i == 0, 1, 0) if sb == 0 else 0
        last = jnp.where(i == n - 1, 2, 0) if sb == n_sub - 1 else 0
        bias = bias_ref[first + last]
        q_rows = slice(sb * qb, (sb + 1) * qb)
        for g in range(N_KV_HEADS):
            qg = jnp.concatenate(
                [q_ref[q_rows, (g * pairs_per_group + j) * LANES:(g * pairs_per_group + j + 1) * LANES]
                 for j in range(pairs_per_group)], axis=0)
            acc = jnp.zeros((rows, LANES), F32)
            for par in range(2):
                col = (2 * g + par) * LANES
                s_b = lax.dot_general(qg, kband[:, col:col + LANES], dn_t, preferred_element_type=F32)
                s_c = lax.dot_general(qg, kctx[:, col:col + LANES], dn_t, preferred_element_type=F32)
                s_b = s_b + bias
                sink = jnp.zeros((rows, 1), F32)
                for j in range(pairs_per_group):
                    hd = 2 * (g * pairs_per_group + j) + par
                    sink = jnp.where(rsub == j, sink_ref[hd] * LOG2_E, sink)
                m = jnp.maximum(jnp.maximum(jnp.max(s_b, axis=-1, keepdims=True),
                                            jnp.max(s_c, axis=-1, keepdims=True)), sink)
                p_b = jnp.exp2(s_b - m)
                p_c = jnp.exp2(s_c - m)
                l = (jnp.sum(p_b, axis=-1, keepdims=True) + jnp.sum(p_c, axis=-1, keepdims=True)
                     + jnp.exp2(sink - m))
                o = (jnp.dot(p_b.astype(BF16), vband[:, col:col + LANES], preferred_element_type=F32)
                     + jnp.dot(p_c.astype(BF16), vctx[:, col:col + LANES], preferred_element_type=F32))
                acc = acc + o / l
            for j in range(pairs_per_group):
                pcol = (g * pairs_per_group + j) * LANES
                o_ref[q_rows, pcol:pcol + LANES] = acc[j * qb:(j + 1) * qb].astype(BF16)


def _attention(q, kk, vv, kkc, vvc, sinks):
    t, nq = q.shape
    qb = Q_BLOCK
    nb = t // qb
    c = kkc.shape[0]
    w = kk.shape[1]
    per = next(p for p in (4, 2, 1) if nb % p == 0)
    cur = lambda i, s: (i, 0)
    prv = lambda i, s: (jnp.maximum(i * per - 1, 0), 0)
    nxt = lambda i, s: (jnp.minimum((i + 1) * per, nb - 1), 0)
    zero = lambda i, s: (0, 0)
    rows = (N_HEADS // N_KV_HEADS) // 2 * qb
    r = (jnp.arange(rows, dtype=I32) % qb)[:, None]
    cpos = jnp.arange(3 * qb, dtype=I32)[None, :] - qb
    inside = jnp.abs(r - cpos) <= WINDOW
    variants = [inside & ((cpos >= 0) | ((v & 1) == 0)) & ((cpos < qb) | ((v & 2) == 0)) for v in range(4)]
    bias = jnp.where(jnp.stack(variants), 0.0, NEG_BIG).astype(F32)
    gs = pltpu.PrefetchScalarGridSpec(
        num_scalar_prefetch=1,
        grid=(nb // per,),
        in_specs=[pl.BlockSpec((per * qb, nq), cur),
                  pl.BlockSpec((qb, w), prv), pl.BlockSpec((per * qb, w), cur), pl.BlockSpec((qb, w), nxt),
                  pl.BlockSpec((qb, w), prv), pl.BlockSpec((per * qb, w), cur), pl.BlockSpec((qb, w), nxt),
                  pl.BlockSpec((c, w), zero), pl.BlockSpec((c, w), zero),
                  pl.BlockSpec(bias.shape, lambda i, s: (0, 0, 0))],
        out_specs=pl.BlockSpec((per * qb, nq), cur),
    )
    return pl.pallas_call(
        _attn_kernel,
        grid_spec=gs,
        out_shape=jax.ShapeDtypeStruct((t, nq), BF16),
        compiler_params=_cparams(1),
        name="window_attention",
    )(sinks, q, kk, kk, kk, vv, vv, vv, kkc, vvc, bias)


def _dispatch_tiles(x_new, g2n, sh2, sc2, wr, br, xs_ref, pos_ref, gate_ref, cnt_ref):
    n = x_new.shape[0] // MOE_TILE
    rt = xs_ref.shape[0] // n
    for j in range(n):
        rows = pl.ds(j * MOE_TILE, MOE_TILE)
        _router_dispatch(x_new[j * MOE_TILE:(j + 1) * MOE_TILE], g2n, sh2, sc2, wr, br,
                         xs_ref.at[pl.ds(j * rt, rt)], pos_ref.at[rows], gate_ref.at[rows],
                         cnt_ref.at[pl.ds(j, 1)])


def _dispatch_out(t, d, tile):
    nt = t // MOE_TILE
    per_step = tile // MOE_TILE
    tok = lambda i: (i, 0)
    region = _region_rows(nt)
    specs = [pl.BlockSpec((tile, d), tok), pl.BlockSpec((per_step * region, d // 2), tok),
             pl.BlockSpec((tile, LANES), tok), pl.BlockSpec((tile, LANES), tok),
             pl.BlockSpec((per_step, SUBLANES, LANES), lambda i: (i, 0, 0))]
    shapes = [jax.ShapeDtypeStruct((t, d), F32), jax.ShapeDtypeStruct((nt * region, d // 2), jnp.uint32),
              jax.ShapeDtypeStruct((t, LANES), I32), jax.ShapeDtypeStruct((t, LANES), F32),
              jax.ShapeDtypeStruct((nt, SUBLANES, LANES), I32)]
    return specs, shapes


def _oproj_kernel(x_ref, o_ref, mod_ref, wo_ref, bo_ref, g2_ref, wr_ref, br_ref,
                  x1_ref, xs_ref, pos_ref, gate_ref, cnt_ref, wb_ref):
    d = x_ref.shape[1]

    @pl.when(pl.program_id(0) == 0)
    def _():
        wb_ref[...] = wo_ref[...].astype(BF16)

    y = jnp.dot(o_ref[...], wb_ref[...], preferred_element_type=F32) + bo_ref[...]
    x1 = x_ref[...] + mod_ref[0:1, 2 * d:3 * d] * y
    x1_ref[...] = x1
    _dispatch_tiles(x1, g2_ref[...], mod_ref[0:1, 3 * d:4 * d], mod_ref[0:1, 4 * d:5 * d],
                     wr_ref[...], br_ref[...], xs_ref, pos_ref, gate_ref, cnt_ref)


def _oproj_router(xt, o, mod_l, w_o, b_o, g2n, wr, br):
    t, d = xt.shape
    tile = _combine_tile(t)
    tok = lambda i: (i, 0)
    out_specs, out_shape = _dispatch_out(t, d, tile)
    return pl.pallas_call(
        _oproj_kernel,
        grid=(t // tile,),
        in_specs=[pl.BlockSpec((tile, d), tok), pl.BlockSpec((tile, o.shape[1]), tok), _full(mod_l.shape),
                  _full(w_o.shape), _full((1, d)), _full((1, d)), _full(wr.shape), _full(br.shape)],
        out_specs=out_specs,
        out_shape=out_shape,
        scratch_shapes=[pltpu.VMEM(w_o.shape, BF16)],
        compiler_params=_cparams(1),
        name="oproj_router",
    )(xt, o, mod_l, w_o, b_o, g2n, wr, br)


def _moe_kernel(nt, region_granules, brange_ref, nvg_ref, gsrc_ref, xs_hbm, w1_ref, b1_ref, w2_ref, b2_ref,
                ys_hbm, xbuf, obuf, w1b, w2b, gsem, ssem):
    e = pl.program_id(0)
    ne = pl.num_programs(0)
    b_lo = brange_ref[e]
    b_hi = brange_ref[ne + e]
    total = brange_ref[2 * ne - 1]
    gpb = GRANULES_PER_BLOCK
    dff = w2_ref.shape[1]
    spare = _spare_per_region(nt)
    idle_granule = REGION_DATA_GRANULES + 2 * spare

    def rows_of(granule):
        return pl.ds(pl.multiple_of(granule * GRANULE, GRANULE), GRANULE)

    def gather_copy(blk, gi):
        g = gsrc_ref[blk * gpb + gi]
        src = jnp.where(g >= 0, g, idle_granule)
        s = blk % 3
        return pltpu.make_async_copy(xs_hbm.at[rows_of(src)], xbuf.at[s, pl.ds(gi * GRANULE, GRANULE)],
                                     gsem.at[s])

    def scatter_copy(blk, s, gi):
        g = gsrc_ref[blk * gpb + gi]
        own_spare = (gi % nt) * region_granules + REGION_DATA_GRANULES + gi // nt + s * spare
        dst = jnp.where(g >= 0, g, own_spare)
        return pltpu.make_async_copy(obuf.at[s, pl.ds(gi * GRANULE, GRANULE)], ys_hbm.at[rows_of(dst)],
                                     ssem.at[s])

    def start_gather(blk):
        for gi in range(gpb):
            gather_copy(blk, gi).start()

    def wait_gather(s):
        pltpu.make_async_copy(xs_hbm.at[pl.ds(0, EXPERT_ROWS)], xbuf.at[s], gsem.at[s]).wait()

    def wait_scatter(s):
        pltpu.make_async_copy(obuf.at[s], ys_hbm.at[pl.ds(0, EXPERT_ROWS)], ssem.at[s]).wait()

    @pl.when(e == 0)
    def _():
        obuf[...] = jnp.zeros(obuf.shape, jnp.uint32)
        start_gather(0)
        start_gather(1)

    def expert_block(b, rows):
        slot = b % 2
        start_gather(b + 2)
        xb = jnp.concatenate(_unpack_halves(xbuf[b % 3, 0:rows, :]), axis=1).astype(BF16)
        gu = jnp.dot(xb, w1b[...], preferred_element_type=F32) + b1_ref[0]
        gt = jnp.minimum(gu[:, :dff], SWIGLU_LIMIT)
        up = jnp.clip(gu[:, dff:], -SWIGLU_LIMIT, SWIGLU_LIMIT)
        act = gt * jax.nn.sigmoid(SWIGLU_ALPHA * gt) * (up + 1.0)
        out = jnp.dot(act.astype(BF16), w2b[...], preferred_element_type=F32) + b2_ref[0]
        obuf[slot, 0:rows, :] = _pack_halves(out.astype(BF16).astype(F32))
        for gi in range(gpb):
            scatter_copy(b, slot, gi).start()

    def block_body(b, carry):
        wait_gather(b % 3)

        @pl.when(b >= 2)
        def _():
            wait_scatter(b % 2)

        half = EXPERT_ROWS // 2
        rows_needed = nvg_ref[b] * GRANULE

        @pl.when(rows_needed > half)
        def _():
            expert_block(b, EXPERT_ROWS)

        @pl.when(rows_needed <= half)
        def _():
            expert_block(b, half)

        return carry

    @pl.when(b_hi > b_lo)
    def _():
        w1b[...] = w1_ref[0].astype(BF16)
        w2b[...] = w2_ref[0].astype(BF16)
        lax.fori_loop(b_lo, b_hi, block_body, 0)

    @pl.when(e == ne - 1)
    def _():
        wait_gather(total % 3)
        wait_gather((total + 1) % 3)

        @pl.when(total >= 2)
        def _():
            wait_scatter(total % 2)

        @pl.when(total >= 1)
        def _():
            wait_scatter((total + 1) % 2)


def _moe_experts(xs, nt, block_range, n_granules, granule_src, layer, w1, b1, w2, b2):
    dw = xs.shape[1]
    d = 2 * dw
    dff = w2.shape[2]
    n_layers, n_exp = w1.shape[:2]
    by_expert = lambda e, br, ng, gs: (layer, e, 0, 0)
    gs = pltpu.PrefetchScalarGridSpec(
        num_scalar_prefetch=3,
        grid=(n_exp,),
        in_specs=[pl.BlockSpec(memory_space=pl.ANY),
                  pl.BlockSpec((None, 1, d, 2 * dff), by_expert), pl.BlockSpec((None, 1, 1, 2 * dff), by_expert),
                  pl.BlockSpec((None, 1, dff, d), by_expert), pl.BlockSpec((None, 1, 1, d), by_expert)],
        out_specs=pl.BlockSpec(memory_space=pl.ANY),
        scratch_shapes=[pltpu.VMEM((3, EXPERT_ROWS, dw), jnp.uint32), pltpu.VMEM((2, EXPERT_ROWS, dw), jnp.uint32),
                        pltpu.VMEM((d, 2 * dff), BF16), pltpu.VMEM((dff, d), BF16),
                        pltpu.SemaphoreType.DMA((3,)), pltpu.SemaphoreType.DMA((2,))],
    )
    return pl.pallas_call(
        functools.partial(_moe_kernel, nt, _region_rows(nt) // GRANULE),
        grid_spec=gs,
        out_shape=jax.ShapeDtypeStruct(xs.shape, jnp.uint32),
        input_output_aliases={3: 0},
        compiler_params=_cparams(1),
        name="moe_experts",
    )(block_range, n_granules, granule_src, xs, w1, b1.reshape(n_layers, n_exp, 1, 2 * dff),
      w2, b2.reshape(n_layers, n_exp, 1, d))


TABLE_BLOCKS = 8


def _tables_kernel(region_granules, cnt_ref, brange_ref, ngran_ref, gsrc_ref):
    gpb = GRANULES_PER_BLOCK
    log_b = gpb.bit_length() - 1
    n = cnt_ref.shape[0]
    rows = TABLE_BLOCKS * gpb
    nt_dims = (((1,), (1,)), ((), ()))

    cnt = cnt_ref[...].astype(F32)
    gl_len = cnt * (1.0 / GRANULE)
    ri = lax.broadcasted_iota(I32, (n, n), 0)
    ci = lax.broadcasted_iota(I32, (n, n), 1)
    lower = jnp.where(ci < ri, 1.0, 0.0).astype(BF16)
    upper = jnp.where(ri < ci, 1.0, 0.0).astype(BF16)
    pre = jnp.dot(lower, gl_len.astype(BF16), preferred_element_type=F32)
    off = jnp.dot(cnt.astype(BF16), upper, preferred_element_type=F32)
    src = ri.astype(F32) * float(region_granules) + off * (1.0 / GRANULE)
    n_g = jnp.sum(gl_len, axis=0, keepdims=True)
    blocks = ((n_g.astype(I32) + (gpb - 1)) >> log_b).astype(F32)
    bstart = jnp.dot(jnp.broadcast_to(blocks, (SUBLANES, n)).astype(BF16), upper,
                     preferred_element_type=F32)[0:1]
    bend = bstart + blocks

    def digits(v, base):
        hi = jnp.floor(v * (1.0 / base))
        return hi.astype(BF16), (v - hi * base).astype(BF16)

    def pick(onehot, table):
        return lax.dot_general(onehot, table, nt_dims, preferred_element_type=F32)

    def pick2(onehot, v, base):
        hi, lo = digits(v, base)
        return pick(onehot, hi) * base + pick(onehot, lo)

    q = pl.program_id(0) * rows + lax.broadcasted_iota(I32, (rows, n), 0)
    bq = (q >> log_b).astype(F32)
    iq = (q & (gpb - 1)).astype(F32)
    own = jnp.where((bstart <= bq) & (bq < bend), 1.0, 0.0).astype(BF16)
    pre_q = pick2(own, pre, 64.0)
    len_q = pick(own, gl_len.astype(BF16))
    src_q = pick2(own, src, 128.0)
    bstart_q = pick(own, jnp.broadcast_to(bstart, (n, n)).astype(BF16))
    gl = (bq - bstart_q) * float(gpb) + iq
    inside = (pre_q <= gl) & (gl < pre_q + len_q)
    hit = jnp.sum(jnp.where(inside, src_q - pre_q + gl + 1.0, 0.0), axis=-1, keepdims=True)
    gsrc_ref[...] = jnp.broadcast_to(hit - 1.0, (rows, n)).astype(I32)

    bb = (pl.program_id(0) * TABLE_BLOCKS + lax.broadcasted_iota(I32, (TABLE_BLOCKS, n), 0)).astype(F32)
    lane = lax.broadcasted_iota(I32, (TABLE_BLOCKS, n), 1).astype(F32)
    mine = (bstart <= bb) & (bb < bend)
    red = lambda v: jnp.sum(jnp.where(mine, v, 0.0), axis=-1, keepdims=True)
    active = red(jnp.ones_like(lane))
    n_left = red(jnp.broadcast_to(n_g, mine.shape)) - (bb[:, 0:1] - red(jnp.broadcast_to(bstart, mine.shape))) * gpb
    ngran_ref[...] = jnp.broadcast_to(jnp.clip(n_left, 0.0, float(gpb)) * active, (TABLE_BLOCKS, n)).astype(I32)

    sub = lax.broadcasted_iota(I32, (SUBLANES, n), 0)
    brange_ref[...] = jnp.where(sub == 0, bstart, jnp.where(sub == 1, bend, 0.0)).astype(I32)


def _block_tables(cnt):
    nt = cnt.shape[0]
    assert nt <= LANES
    max_blocks = (nt * REGION_DATA_GRANULES) // GRANULES_PER_BLOCK + N_EXPERTS
    n_blocks = -(-(max_blocks + 2) // TABLE_BLOCKS) * TABLE_BLOCKS
    gpb = GRANULES_PER_BLOCK
    cnt_sq = jnp.zeros((LANES, LANES), I32).at[:nt].set(cnt[:, 0, :])
    by_step = lambda i: (i, 0)
    brange, ngran, gsrc = pl.pallas_call(
        functools.partial(_tables_kernel, _region_rows(nt) // GRANULE),
        grid=(n_blocks // TABLE_BLOCKS,),
        in_specs=[_full((LANES, LANES))],
        out_specs=[_full((SUBLANES, LANES)), pl.BlockSpec((TABLE_BLOCKS, LANES), by_step),
                   pl.BlockSpec((TABLE_BLOCKS * gpb, LANES), by_step)],
        out_shape=[jax.ShapeDtypeStruct((SUBLANES, LANES), I32), jax.ShapeDtypeStruct((n_blocks, LANES), I32),
                   jax.ShapeDtypeStruct((n_blocks * gpb, LANES), I32)],
        compiler_params=_cparams(1),
        name="moe_block_tables",
    )(cnt_sq)
    return brange[0:2, :N_EXPERTS].reshape(-1), ngran[:, 0], gsrc[:, 0]


def _combine_kernel(x_ref, ys_ref, pos_ref, gate_ref, mod_ref, o_ref):
    d = x_ref.shape[1]
    o_ref[...] = x_ref[...] + mod_ref[0:1, 5 * d:6 * d] * _moe_combine(ys_ref, pos_ref, gate_ref)


def _combine_glu_kernel(x_ref, ys_ref, pos_ref, gate_ref, modp_ref, modn_ref, g_ref, w_ref, b_ref,
                        o_ref, u_ref, wb_ref):
    d = x_ref.shape[1]

    @pl.when(pl.program_id(0) == 0)
    def _():
        wb_ref[...] = w_ref[...].astype(BF16)

    x2 = x_ref[...] + modp_ref[0:1, 5 * d:6 * d] * _moe_combine(ys_ref, pos_ref, gate_ref)
    o_ref[...] = x2
    h = _rms_mod(x2, g_ref[...], modn_ref[0:1, 0:d], modn_ref[0:1, d:2 * d])
    u = jnp.dot(h.astype(BF16), wb_ref[...], preferred_element_type=F32) + b_ref[...]
    u_ref[...] = u[:, :d] * jax.nn.sigmoid(u[:, d:])


def _combine_tile(t):
    return MOE_TILE * (2 if (t // MOE_TILE) % 2 == 0 else 1)


def _combine_in_specs(t, d, tile):
    tok = lambda i: (i, 0)
    region = _region_rows(t // MOE_TILE) * (tile // MOE_TILE)
    return [pl.BlockSpec((tile, d), tok), pl.BlockSpec((region, d // 2), tok),
            pl.BlockSpec((tile, LANES), tok), pl.BlockSpec((tile, LANES), tok)]


def _combine(x1, ys, pos, gates, mod_l):
    t, d = x1.shape
    tile = _combine_tile(t)
    return pl.pallas_call(
        _combine_kernel,
        grid=(t // tile,),
        in_specs=_combine_in_specs(t, d, tile) + [_full(mod_l.shape)],
        out_specs=pl.BlockSpec((tile, d), lambda i: (i, 0)),
        out_shape=jax.ShapeDtypeStruct((t, d), F32),
        compiler_params=_cparams(1),
        name="moe_combine",
    )(x1, ys, pos, gates, mod_l)


def _combine_glu(x1, ys, pos, gates, mod_prev, mod_next, g1n, w_pw1, b_pw1):
    t, d = x1.shape
    tile = _combine_tile(t)
    tok = lambda i: (i, 0)
    return pl.pallas_call(
        _combine_glu_kernel,
        grid=(t // tile,),
        in_specs=_combine_in_specs(t, d, tile)
                 + [_full(mod_prev.shape), _full(mod_next.shape), _full((1, d)),
                    _full(w_pw1.shape), _full((1, 2 * d))],
        out_specs=[pl.BlockSpec((tile, d), tok), pl.BlockSpec((tile, d), tok)],
        out_shape=[jax.ShapeDtypeStruct((t, d), F32), jax.ShapeDtypeStruct((t, d), F32)],
        scratch_shapes=[pltpu.VMEM(w_pw1.shape, BF16)],
        compiler_params=_cparams(1),
        name="combine_pw1_glu",
    )(x1, ys, pos, gates, mod_prev, mod_next, g1n, w_pw1, b_pw1)


def _conv_kernel(x_ref, up_ref, uc_ref, un_ref, mod_ref, wdw_ref, bdw_ref, lg_ref, lb_ref, w2_ref, b2_ref,
                 g2_ref, wr_ref, br_ref, x1_ref, xs_ref, pos_ref, gate_ref, cnt_ref, wb_ref, ubuf, shbuf, cbuf):
    i = pl.program_id(0)
    n = pl.num_programs(0)
    d = x_ref.shape[1]
    tile = x_ref.shape[0]
    halo = up_ref.shape[0]

    @pl.when(i == 0)
    def _():
        wb_ref[...] = w2_ref[...].astype(BF16)

    ubuf[0:halo, :] = jnp.where(i > 0, up_ref[...], 0.0)
    ubuf[halo:halo + tile, :] = uc_ref[...]
    ubuf[halo + tile:, :] = jnp.where(i < n - 1, un_ref[...], 0.0)

    sub = CONV_CHUNK // SUBLANES
    part_rows = shbuf.shape[1] - 2 * halo
    span = part_rows + 2 * halo - SUBLANES
    for base in range(0, tile, part_rows):
        for r in range(1, SUBLANES):
            shbuf[r - 1, 0:span, :] = ubuf[base + r:base + r + span, :]
        for c0 in range(0, part_rows, CONV_CHUNK):
            parts = [jnp.zeros((SUBLANES, d), F32) + bdw_ref[...] for _ in range(sub)]
            for j in range(CONV_WIDTH):
                a, r = divmod(halo + j - CONV_PAD, SUBLANES)
                wj = wdw_ref[j * SUBLANES:(j + 1) * SUBLANES, :]
                for k in range(sub):
                    lo = c0 + (a + k) * SUBLANES
                    rows = (ubuf[base + lo:base + lo + SUBLANES, :] if r == 0
                            else shbuf[r - 1, lo:lo + SUBLANES, :])
                    parts[k] = parts[k] + rows * wj
            for k in range(sub):
                row0 = base + c0 + k * SUBLANES
                cbuf[row0:row0 + SUBLANES, :] = parts[k]
    acc = cbuf[...]

    mu = jnp.mean(acc, axis=-1, keepdims=True)
    cen = acc - mu
    var = jnp.mean(cen * cen, axis=-1, keepdims=True)
    z = _silu(cen * lax.rsqrt(var + EPS) * lg_ref[...] + lb_ref[...])
    y = jnp.dot(z.astype(BF16), wb_ref[...], preferred_element_type=F32) + b2_ref[...]
    x1 = x_ref[...] + mod_ref[0:1, 2 * d:3 * d] * y
    x1_ref[...] = x1
    _dispatch_tiles(x1, g2_ref[...], mod_ref[0:1, 3 * d:4 * d], mod_ref[0:1, 4 * d:5 * d],
                     wr_ref[...], br_ref[...], xs_ref, pos_ref, gate_ref, cnt_ref)


def _conv_router(x2, u, mod_l, w_dw, b_dw, ln_g, ln_b, w_pw2, b_pw2, g2n, wr, br):
    t, d = x2.shape
    tile = _combine_tile(t)
    nt = t // tile
    hb = tile // CONV_HALO
    nh = t // CONV_HALO
    tok = lambda i: (i, 0)
    prv = lambda i: (jnp.maximum(i * hb - 1, 0), 0)
    nxt = lambda i: (jnp.minimum((i + 1) * hb, nh - 1), 0)
    wpad = jnp.repeat(w_dw, SUBLANES, axis=0)
    out_specs, out_shape = _dispatch_out(t, d, tile)
    return pl.pallas_call(
        _conv_kernel,
        grid=(nt,),
        in_specs=[pl.BlockSpec((tile, d), tok), pl.BlockSpec((CONV_HALO, d), prv),
                  pl.BlockSpec((tile, d), tok), pl.BlockSpec((CONV_HALO, d), nxt),
                  _full(mod_l.shape), _full(wpad.shape), _full((1, d)), _full((1, d)), _full((1, d)),
                  _full(w_pw2.shape), _full((1, d)), _full((1, d)), _full(wr.shape), _full(br.shape)],
        out_specs=out_specs,
        out_shape=out_shape,
        scratch_shapes=[pltpu.VMEM(w_pw2.shape, BF16), pltpu.VMEM((tile + 2 * CONV_HALO, d), F32),
                        pltpu.VMEM((SUBLANES - 1, MOE_TILE + 2 * CONV_HALO, d), F32),
                        pltpu.VMEM((tile, d), F32)],
        compiler_params=_cparams(1),
        name="conv_router",
    )(x2, u, u, u, mod_l, wpad, b_dw, ln_g, ln_b, w_pw2, b_pw2, g2n, wr, br)


def _rope_tables(n_rows):
    quarter = HEAD_DIM // 4
    inv = ROPE_BASE ** (-jnp.arange(quarter, dtype=F32) / quarter)

    def tables(npos):
        ang = jnp.arange(npos, dtype=I32).astype(F32)[:, None] * inv[None, :]
        z = jnp.zeros_like(ang)
        c, s = jnp.cos(ang), jnp.sin(ang)
        half = lambda a, b: jnp.concatenate([a, b], axis=1)
        full = lambda h: jnp.concatenate([h, h, h, h], axis=1)
        return jnp.stack([full(half(c, c)), full(half(-s, z)), full(half(z, s))])

    return tables(n_rows), tables(GRID_W)


def _router_params(w_router, b_router):
    d, e = w_router.shape
    wr = jnp.zeros((LANES, d), F32).at[:e].set(w_router.T)
    br = jnp.zeros((LANES, MOE_TILE), F32).at[:e].set(jnp.broadcast_to(b_router[:, None], (e, MOE_TILE)))
    hi = wr.astype(BF16)
    lo = (wr - hi.astype(F32)).astype(BF16)
    return jnp.stack([hi, lo]), br


def kernel(x, c, ctx, c_ctx, w_mod, b_mod, norm1_g, norm2_g, attn_w_qkv, attn_b_qkv, attn_w_o, attn_b_o,
           attn_q_norm, attn_k_norm, attn_sinks, conv_w_pw1, conv_b_pw1, conv_w_dw, conv_b_dw, conv_ln_g,
           conv_ln_b, conv_w_pw2, conv_b_pw2, moe_w_router, moe_b_router, moe_w1, moe_b1, moe_w2, moe_b2):
    bsz, t, d = x.shape
    assert bsz == 1 and w_mod.shape[0] == 2
    n_ctx = ctx.shape[1]
    tile = min(TOKEN_TILE, t)
    assert t % tile == 0 and t % Q_BLOCK == 0 and t % MOE_TILE == 0 and tile % GRID_W == 0
    xt = x.reshape(t, d)
    row = lambda v: v.reshape(1, -1)

    mod = _mod_vectors(c, c_ctx, w_mod, b_mod)
    mod0, mod1 = mod[0], mod[1]

    bd = (jnp.arange(2 * LANES)[:, None] // HEAD_DIM == jnp.arange(2 * LANES)[None, :] // HEAD_DIM).astype(BF16)
    qg = row(jnp.tile(attn_q_norm[0], N_HEADS))
    kg = row(jnp.tile(attn_k_norm[0], N_KV_HEADS))
    rowt, colt = _rope_tables(t // GRID_W)
    g1n = row(norm1_g[0])
    bq = row(attn_b_qkv[0])
    q, kk, vv = _qkv_project(xt, mod0, 0, True, g1n, attn_w_qkv[0], bq, qg, kg, rowt, colt, bd, tile)
    _, kkc, vvc = _qkv_project(ctx.reshape(n_ctx, d), mod0, 1, False, g1n, attn_w_qkv[0], bq, qg, kg,
                               rowt[:, :max(n_ctx // GRID_W, 1)], colt, bd, n_ctx)
    o = _attention(q, kk, vv, kkc, vvc, attn_sinks[0])
    wr0, br0 = _router_params(moe_w_router[0], moe_b_router[0])
    x1, xs, pos, gates, cnt = _oproj_router(xt, o, mod0, attn_w_o[0], row(attn_b_o[0]), row(norm2_g[0]),
                                            wr0, br0)
    ys = _moe_experts(xs, cnt.shape[0], *_block_tables(cnt), 0, moe_w1, moe_b1, moe_w2, moe_b2)

    x2, u = _combine_glu(x1, ys, pos, gates, mod0, mod1, row(norm1_g[1]), conv_w_pw1[0], row(conv_b_pw1[0]))
    wr1, br1 = _router_params(moe_w_router[1], moe_b_router[1])
    x3, xs, pos, gates, cnt = _conv_router(x2, u, mod1, conv_w_dw[0], row(conv_b_dw[0]), row(conv_ln_g[0]),
                                           row(conv_ln_b[0]), conv_w_pw2[0], row(conv_b_pw2[0]),
                                           row(norm2_g[1]), wr1, br1)
    ys = _moe_experts(xs, cnt.shape[0], *_block_tables(cnt), 1, moe_w1, moe_b1, moe_w2, moe_b2)
    out = _combine(x3, ys, pos, gates, mod1)
    return out.reshape(1, t, d)
```
